```python
import math
import jax, jax.numpy as jnp
from jax import lax
import numpy as np

D_MODEL = 1024
BATCH = 16
SEQ = 4096
DEPTH = 1
DEC_BATCH = 8
DEC_SEQ = 64
PAST_LEN = 4096

CHUNK = 64
Q_BLOCK = 128
EPS = 1e-6
R_HEADS = 4
R_DK = 128
R_DV = 2 * R_DK
ROPE_BASE = 10000.0
D_HEADS = 4
D_DH = 128
D_DV = 2 * D_DH
R_QK_W = R_HEADS * R_DK
R_V_W = R_HEADS * R_DV
D_QK_W = D_HEADS * 2 * D_DH
D_V_W = D_HEADS * D_DV
SPLITS = (R_QK_W, R_QK_W, R_V_W, R_V_W, D_QK_W, D_QK_W, D_V_W, D_MODEL, D_MODEL)
SPLIT_POINTS = tuple(int(s) for s in np.cumsum(SPLITS)[:-1])
IN_W = int(sum(SPLITS))
N_GROUPS = 4
EXP_PER_GROUP = 4
N_EXPERTS = N_GROUPS * EXP_PER_GROUP
TOP_K = 2
D_FF = 512

kernel_name = "hybrid_retention_diffattn_hmoe_stream_step"

F32 = jnp.float32


def rmsnorm(x, g):
    xf = x.astype(F32)
    y = xf * lax.rsqrt(jnp.mean(xf * xf, axis=-1, keepdims=True) + EPS)
    return (y * g.astype(F32)).astype(x.dtype)


def retention_log_gamma():
    return jnp.log1p(-jnp.exp2(-5.0 - jnp.arange(R_HEADS, dtype=F32)))


def rotary(x, pos):
    half = x.shape[-1] // 2
    inv = 1.0 / (ROPE_BASE ** jnp.linspace(0.0, 1.0, half, dtype=F32))
    ang = pos.astype(F32)[:, None] * inv[None, :]
    cos = jnp.cos(ang)[None, :, None, :]
    sin = jnp.sin(ang)[None, :, None, :]
    x1, x2 = x[..., :half], x[..., half:]
    return jnp.concatenate([x1 * cos - x2 * sin, x1 * sin + x2 * cos], axis=-1)


def project(h, w_in, pos, g_q, g_k):
    B, T, _ = h.shape
    z = jnp.einsum('btd,de->bte', h, w_in)
    rq, rk, rv, rg, dq, dk, dv, gr, gd = jnp.split(z, SPLIT_POINTS, axis=-1)
    rq = rotary(rq.reshape(B, T, R_HEADS, R_DK).astype(F32), pos)
    rk = rotary(rk.reshape(B, T, R_HEADS, R_DK).astype(F32), pos) * (R_DK ** -0.5)
    rv = rv.reshape(B, T, R_HEADS, R_DV).astype(F32)
    rq, rk, rv = (a.transpose(0, 2, 1, 3) for a in (rq, rk, rv))
    dq = rmsnorm(dq.reshape(B, T, D_HEADS, 2, D_DH), g_q)
    dk = rmsnorm(dk.reshape(B, T, D_HEADS, 2, D_DH), g_k)
    dv = dv.reshape(B, T, D_HEADS, D_DV)
    return rq, rk, rv, rg, dq, dk, dv, gr, gd


def retention_chunk(state, q, k, v, log_gamma):
    L = q.shape[2]
    idx = jnp.arange(L, dtype=F32)
    intra = jnp.exp(log_gamma[:, None, None] * jnp.abs(idx[:, None] - idx[None, :]))
    s = jnp.einsum('bhld,bhmd->bhlm', q, k) * intra[None]
    o = jnp.einsum('bhlm,bhmv->bhlv', s, v)
    inter = jnp.exp(log_gamma[:, None] * (idx[None, :] + 1.0))
    o = o + jnp.einsum('bhld,bhdv->bhlv', q, state) * inter[None, :, :, None]
    wk = jnp.exp(log_gamma[:, None] * (L - 1.0 - idx[None, :]))
    new_state = (state * jnp.exp(log_gamma * L)[None, :, None, None]
                 + jnp.einsum('bhld,bhlv->bhdv', k * wk[None, :, :, None], v))
    return new_state, o


def retention_prompt(q, k, v, log_gamma):
    B, H, T, _ = q.shape
    n_chunks = T // CHUNK

    def to_chunks(a):
        return a.reshape(B, H, n_chunks, CHUNK, a.shape[-1]).transpose(2, 0, 1, 3, 4)

    def step(s, qkv):
        qc, kc, vc = qkv
        return retention_chunk(s, qc, kc, vc, log_gamma)

    s0 = jnp.zeros((B, H, R_DK, R_DV), F32)
    s_fin, o = lax.scan(step, s0, (to_chunks(q), to_chunks(k), to_chunks(v)))
    o = o.transpose(1, 2, 0, 3, 4).reshape(B, H, T, R_DV)
    return o, s_fin


def retention_out(o, rg, g_ret, w_ret_o):
    B, H, T, _ = o.shape
    o = rmsnorm(o, g_ret).transpose(0, 2, 1, 3).reshape(B, T, R_V_W).astype(rg.dtype)
    return jnp.einsum('btv,vd->btd', jax.nn.silu(rg) * o, w_ret_o)


def diff_attend(q, k, v, lam, mask):
    s = jnp.einsum('bqhmd,bkhmd->bhmqk', q, k).astype(F32) * (D_DH ** -0.5)
    if mask is not None:
        s = jnp.where(mask[None, None, None], s, -jnp.inf)
    p = jax.nn.softmax(s, axis=-1)
    a = p[:, :, 0] - lam * p[:, :, 1]
    return jnp.einsum('bhqk,bkhv->bhqv', a, v.astype(F32))


def diff_attn_prompt(q, k, v, lam):
    B, T = q.shape[:2]
    n_blocks = T // Q_BLOCK
    qb = q.reshape(B, n_blocks, Q_BLOCK, D_HEADS, 2, D_DH).swapaxes(0, 1)
    key_chunk = jnp.arange(T) // CHUNK

    def block(args):
        qi, i = args
        q_chunk = (i * Q_BLOCK + jnp.arange(Q_BLOCK)) // CHUNK
        mask = key_chunk[None, :] <= q_chunk[:, None]
        return diff_attend(qi, k, v, lam, mask)

    o = lax.map(block, (qb, jnp.arange(n_blocks)))
    return o.transpose(1, 2, 0, 3, 4).reshape(B, D_HEADS, T, D_DV)


def diff_out(o, g_sub, lam_init, w_diff_o, dtype):
    B, H, T, _ = o.shape
    o = rmsnorm(o, g_sub) * (1.0 - lam_init)
    o = o.transpose(0, 2, 1, 3).reshape(B, T, D_V_W).astype(dtype)
    return jnp.einsum('btv,vd->btd', o, w_diff_o)


def gated_merge(x, y_r, y_d, gr, gd, w_out):
    m = jax.nn.sigmoid(gr) * y_r + jax.nn.sigmoid(gd) * y_d
    return x + jnp.einsum('btd,de->bte', m, w_out)


def hier_moe(h, w_group, b_group, w_expert, b_expert, w_gate, w_up, w_down):
    B, T, D = h.shape
    xt = h.reshape(B * T, D)
    xf = xt.astype(F32)
    g_logits = xf @ w_group.astype(F32) + b_group.astype(F32)
    g_prob = jax.nn.softmax(g_logits, axis=-1)
    g_idx = jnp.argmax(g_logits, axis=-1)
    g_w = jnp.take_along_axis(g_prob, g_idx[:, None], axis=-1)
    e_logits = (xf @ w_expert.astype(F32) + b_expert.astype(F32)).reshape(-1, N_GROUPS, EXP_PER_GROUP)
    e_sel = jnp.take_along_axis(e_logits, g_idx[:, None, None], axis=1)[:, 0]
    e_prob = jax.nn.softmax(e_sel, axis=-1)
    top_v, top_i = lax.top_k(e_prob, TOP_K)
    top_v = top_v / jnp.sum(top_v, axis=-1, keepdims=True)
    global_idx = g_idx[:, None] * EXP_PER_GROUP + top_i
    combine = jnp.sum(jax.nn.one_hot(global_idx, N_EXPERTS, dtype=F32) * (g_w * top_v)[..., None], axis=1)
    y = jnp.zeros((B * T, D), F32)
    for e in range(N_EXPERTS):
        he = jax.nn.silu(xt @ w_gate[e]) * (xt @ w_up[e])
        y = y + combine[:, e:e + 1] * (he @ w_down[e]).astype(F32)
    return y.reshape(B, T, D).astype(h.dtype)


def setup_inputs(seed: int = 0) -> dict:
    key = jax.random.key(seed)
    ks = jax.random.split(key, 32)
    nrm = lambda k, shape, scale: jax.random.normal(k, shape, F32) * scale
    gain = lambda k, shape: 1.0 + 0.02 * jax.random.normal(k, shape, F32)
    L = DEPTH
    return {
        "x_prompt": nrm(ks[0], (BATCH, SEQ, D_MODEL), 1.0),
        "x_sample": nrm(ks[1], (DEC_BATCH, DEC_SEQ, D_MODEL), 1.0),
        "cache_k": nrm(ks[2], (L, DEC_BATCH, PAST_LEN, D_HEADS, 2 * D_DH), 1.0),
        "cache_v": nrm(ks[3], (L, DEC_BATCH, PAST_LEN, D_HEADS, D_DV), 1.0),
        "state_ret": nrm(ks[4], (L, DEC_BATCH, R_HEADS, R_DK, R_DV), 1.0),
        "g_mix": gain(ks[5], (L, D_MODEL)),
        "w_in": nrm(ks[6], (L, D_MODEL, IN_W), D_MODEL ** -0.5),
        "g_q": gain(ks[7], (L, D_DH)),
        "g_k": gain(ks[8], (L, D_DH)),
        "lambda_q1": nrm(ks[9], (L, D_DH), 0.1),
        "lambda_k1": nrm(ks[10], (L, D_DH), 0.1),
        "lambda_q2": nrm(ks[11], (L, D_DH), 0.1),
        "lambda_k2": nrm(ks[12], (L, D_DH), 0.1),
        "g_ret": gain(ks[13], (L, R_DV)),
        "w_ret_o": nrm(ks[14], (L, R_V_W, D_MODEL), R_V_W ** -0.5),
        "g_sub": gain(ks[15], (L, D_DV)),
        "w_diff_o": nrm(ks[16], (L, D_V_W, D_MODEL), D_V_W ** -0.5),
        "w_out": nrm(ks[17], (L, D_MODEL, D_MODEL), D_MODEL ** -0.5),
        "g_ffn": gain(ks[18], (L, D_MODEL)),
        "w_group": nrm(ks[19], (L, D_MODEL, N_GROUPS), D_MODEL ** -0.5),
        "b_group": nrm(ks[20], (L, N_GROUPS), 0.01),
        "w_expert": nrm(ks[21], (L, D_MODEL, N_EXPERTS), D_MODEL ** -0.5),
        "b_expert": nrm(ks[22], (L, N_EXPERTS), 0.01),
        "w_gate": nrm(ks[23], (L, N_EXPERTS, D_MODEL, D_FF), D_MODEL ** -0.5),
        "w_up": nrm(ks[24], (L, N_EXPERTS, D_MODEL, D_FF), D_MODEL ** -0.5),
        "w_down": nrm(ks[25], (L, N_EXPERTS, D_FF, D_MODEL), D_FF ** -0.5),
    }


def reference(x_prompt, x_sample, cache_k, cache_v, state_ret, g_mix, w_in, g_q, g_k,
              lambda_q1, lambda_k1, lambda_q2, lambda_k2, g_ret, w_ret_o, g_sub, w_diff_o,
              w_out, g_ffn, w_group, b_group, w_expert, b_expert, w_gate, w_up, w_down):
    log_gamma = retention_log_gamma()
    xp, xs = x_prompt, x_sample
    Bp, Tp = xp.shape[:2]
    Bs, Ts = xs.shape[:2]
    pos_p = jnp.arange(Tp, dtype=jnp.int32)
    pos_s = PAST_LEN + jnp.arange(Ts, dtype=jnp.int32)
    kp_l, vp_l, sp_l, ks_l, vs_l, ss_l = [], [], [], [], [], []
    for l in range(DEPTH):
        lam_init = 0.8 - 0.6 * math.exp(-0.3 * l)
        lam = (jnp.exp(jnp.sum(lambda_q1[l].astype(F32) * lambda_k1[l].astype(F32)))
               - jnp.exp(jnp.sum(lambda_q2[l].astype(F32) * lambda_k2[l].astype(F32))) + lam_init)

        h = rmsnorm(xp, g_mix[l])
        rq, rk, rv, rg, dq, dk, dv, gr, gd = project(h, w_in[l], pos_p, g_q[l], g_k[l])
        o_r, s_fin = retention_prompt(rq, rk, rv, log_gamma)
        o_d = diff_attn_prompt(dq, dk, dv, lam)
        xp = gated_merge(xp, retention_out(o_r, rg, g_ret[l], w_ret_o[l]),
                         diff_out(o_d, g_sub[l], lam_init, w_diff_o[l], xp.dtype), gr, gd, w_out[l])
        xp = xp + hier_moe(rmsnorm(xp, g_ffn[l]), w_group[l], b_group[l], w_expert[l], b_expert[l],
                           w_gate[l], w_up[l], w_down[l])
        kp_l.append(dk.reshape(Bp, Tp, D_HEADS, 2 * D_DH).astype(x_prompt.dtype))
        vp_l.append(dv.astype(x_prompt.dtype))
        sp_l.append(s_fin.astype(x_prompt.dtype))

        h = rmsnorm(xs, g_mix[l])
        rq, rk, rv, rg, dq, dk, dv, gr, gd = project(h, w_in[l], pos_s, g_q[l], g_k[l])
        s_new, o_r = retention_chunk(state_ret[l].astype(F32), rq, rk, rv, log_gamma)
        k_all = jnp.concatenate([cache_k[l].reshape(Bs, -1, D_HEADS, 2, D_DH).astype(dk.dtype), dk], axis=1)
        v_all = jnp.concatenate([cache_v[l].astype(dv.dtype), dv], axis=1)
        o_d = diff_attend(dq, k_all, v_all, lam, None)
        xs = gated_merge(xs, retention_out(o_r, rg, g_ret[l], w_ret_o[l]),
                         diff_out(o_d, g_sub[l], lam_init, w_diff_o[l], xs.dtype), gr, gd, w_out[l])
        xs = xs + hier_moe(rmsnorm(xs, g_ffn[l]), w_group[l], b_group[l], w_expert[l], b_expert[l],
                           w_gate[l], w_up[l], w_down[l])
        ks_l.append(dk.reshape(Bs, Ts, D_HEADS, 2 * D_DH).astype(x_sample.dtype))
        vs_l.append(dv.astype(x_sample.dtype))
        ss_l.append(s_new.astype(x_sample.dtype))

    return (xp, xs, jnp.stack(kp_l), jnp.stack(vp_l), jnp.stack(sp_l),
            jnp.stack(ks_l), jnp.stack(vs_l), jnp.stack(ss_l))
```

```python
import functools
import math

import jax
import jax.numpy as jnp
from jax import lax
from jax.experimental import pallas as pl
from jax.experimental.pallas import tpu as pltpu

F32 = jnp.float32
BF16 = jnp.bfloat16
I32 = jnp.int32

CHUNK = 64
EPS = 1e-6
R_HEADS = 4
R_DK = 128
R_DV = 256
ROPE_BASE = 10000.0
D_HEADS = 4
D_DH = 128
D_DV = 256
N_GROUPS = 4
EXP_PER_GROUP = 4
N_EXPERTS = N_GROUPS * EXP_PER_GROUP
N_PAIRS = EXP_PER_GROUP * (EXP_PER_GROUP - 1) // 2
N_CLASSES = N_GROUPS * N_PAIRS
D_FF = 512
SEG = 1024
N_SEG = 8

V7X_LANES = 128
V7X_SUBLANES = 8
V7X_VMEM_LIMIT_BYTES = 56 * 1024 * 1024

ROW_TILE = 512
RET_SUPER = 512
ATT_TILE = 512
MOE_TILE = 256
CLS_PAD = 32
NEG_BIG = -1e30
PAY = V7X_LANES


def _row_tile(n):
    return min(ROW_TILE, n)


def _cparams(sem, vmem_mb=None):
    kw = dict(dimension_semantics=sem)
    if vmem_mb is not None:
        kw["vmem_limit_bytes"] = min(vmem_mb * 1024 * 1024, V7X_VMEM_LIMIT_BYTES)
    return pltpu.CompilerParams(**kw)


def _resident(shape):
    nd = len(shape)
    return pl.BlockSpec(shape, lambda *_: (0,) * nd, pipeline_mode=pl.Buffered(1))


def _rms(x, eps=EPS):
    return x * lax.rsqrt(jnp.mean(x * x, axis=-1, keepdims=True) + eps)


def _sigmoid(x):
    return 1.0 / (1.0 + jnp.exp(-x))


def _dot(a, b):
    return jnp.dot(a, b, preferred_element_type=F32)


def _dot_nt(a, b):
    return lax.dot_general(a, b, (((1,), (1,)), ((), ())), preferred_element_type=F32)


def _dot_tn(a, b):
    return lax.dot_general(a, b, (((0,), (0,)), ((), ())), preferred_element_type=F32)


def _inproj_kernel(x_ref, g_ref, w_ref, cos_ref, sin_ref, gq_ref, gk_ref,
                   rqk_ref, rv_ref, rgs_ref, dq_ref, dk_ref, dv_ref, sgr_ref, sgd_ref):
    x = x_ref[...]
    hb = (_rms(x) * g_ref[...]).astype(BF16)

    def seg(s):
        return _dot(hb, w_ref[:, s * SEG:(s + 1) * SEG])

    cos = cos_ref[...]
    sin = sin_ref[...]
    z = seg(0)
    for j in range(2 * R_HEADS):
        v = z[:, j * R_DK:(j + 1) * R_DK]
        r = v * cos + pltpu.roll(v, R_DK // 2, 1) * sin
        if j >= R_HEADS:
            r = r * (R_DK ** -0.5)
        rqk_ref[:, j * R_DK:(j + 1) * R_DK] = r.astype(BF16)

    rv_ref[...] = seg(1).astype(BF16)
    z = seg(2)
    rgs_ref[...] = (z * _sigmoid(z)).astype(BF16)

    z = seg(3)
    gq = gq_ref[...] * (D_DH ** -0.5)
    for j in range(2 * D_HEADS):
        v = z[:, j * D_DH:(j + 1) * D_DH]
        dq_ref[:, j * D_DH:(j + 1) * D_DH] = (_rms(v) * gq).astype(BF16)
    z = seg(4)
    gk = gk_ref[...]
    for j in range(2 * D_HEADS):
        v = z[:, j * D_DH:(j + 1) * D_DH]
        dk_ref[:, j * D_DH:(j + 1) * D_DH] = _rms(v) * gk
    dv_ref[...] = seg(5)
    sgr_ref[...] = _sigmoid(seg(6)).astype(BF16)
    sgd_ref[...] = _sigmoid(seg(7)).astype(BF16)


def _in_proj(x2d, g_mix, w_in_bf, cos_tab, sin_tab, g_q, g_k):
    n, d = x2d.shape
    tm = _row_tile(n)
    nt = n // tm
    ntab = cos_tab.shape[0] // tm
    row = lambda w: pl.BlockSpec((tm, w), lambda i: (i, 0))
    tab = pl.BlockSpec((tm, R_DK), lambda i: (i % ntab, 0))
    outs = [jax.ShapeDtypeStruct((n, SEG), dt)
            for dt in (BF16, BF16, BF16, BF16, F32, F32, BF16, BF16)]
    return pl.pallas_call(
        _inproj_kernel,
        out_shape=outs,
        grid=(nt,),
        in_specs=[row(d), _resident((1, d)), _resident(w_in_bf.shape), tab, tab,
                  _resident((1, D_DH)), _resident((1, D_DH))],
        out_specs=[row(SEG)] * 8,
        compiler_params=_cparams(("arbitrary",), 56),
        name="in_proj",
    )(x2d, g_mix.reshape(1, d), w_in_bf, cos_tab, sin_tab,
      g_q.reshape(1, D_DH), g_k.reshape(1, D_DH))


def _ret_kernel(lg_ref, q_ref, k_ref, v_ref, rgs_ref, s0_ref, g_ref,
                o_ref, sfin_ref, s_scr, d_scr, *, c_len):
    h = pl.program_id(1)
    c = pl.program_id(2)
    lg = lg_ref[h]

    @pl.when(c == 0)
    def _():
        s_scr[...] = s0_ref[0, 0]
        t = lax.broadcasted_iota(I32, (c_len, c_len), 0)
        s = lax.broadcasted_iota(I32, (c_len, c_len), 1)
        dist = jnp.abs(t - s).astype(F32)
        vis = (s // CHUNK) <= (t // CHUNK)
        d_scr[...] = jnp.where(vis, jnp.exp(lg * dist), 0.0)

    q = q_ref[0]
    k = k_ref[0]
    v = v_ref[0]
    pos = lax.broadcasted_iota(I32, (c_len, 1), 0).astype(F32)
    state = s_scr[...]
    s = _dot_nt(q, k) * d_scr[...]
    o = _dot(s.astype(BF16), v)
    o = o + jnp.exp(lg * (pos + 1.0)) * _dot(q, state.astype(BF16))
    kw = (k.astype(F32) * jnp.exp(lg * (c_len - 1.0 - pos))).astype(BF16)
    decay = jnp.exp(lg * jnp.full((1, R_DV), float(c_len), F32))
    new_state = state * decay + _dot_tn(kw, v)
    s_scr[...] = new_state

    o_ref[0] = (_rms(o) * g_ref[...] * rgs_ref[0].astype(F32)).astype(BF16)

    @pl.when(c == pl.num_programs(2) - 1)
    def _():
        sfin_ref[0, 0] = new_state


def _retention(rqk, rv, rgs, state0, g_ret, log_gamma, bsz, t_len):
    c_len = min(RET_SUPER, t_len)
    nc = t_len // c_len
    rqk3 = rqk.reshape(bsz, t_len, SEG)
    rv3 = rv.reshape(bsz, t_len, SEG)
    rgs3 = rgs.reshape(bsz, t_len, SEG)
    qspec = pl.BlockSpec((1, c_len, R_DK), lambda b, h, c, lg: (b, c, h))
    kspec = pl.BlockSpec((1, c_len, R_DK), lambda b, h, c, lg: (b, c, R_HEADS + h))
    vspec = pl.BlockSpec((1, c_len, R_DV), lambda b, h, c, lg: (b, c, h))
    sspec = pl.BlockSpec((1, 1, R_DK, R_DV), lambda b, h, c, lg: (b, h, 0, 0))
    gspec = pl.BlockSpec((1, R_DV), lambda b, h, c, lg: (0, 0))
    o, sfin = pl.pallas_call(
        functools.partial(_ret_kernel, c_len=c_len),
        out_shape=[jax.ShapeDtypeStruct((bsz, t_len, SEG), BF16),
                   jax.ShapeDtypeStruct((bsz, R_HEADS, R_DK, R_DV), F32)],
        grid_spec=pltpu.PrefetchScalarGridSpec(
            num_scalar_prefetch=1,
            grid=(bsz, R_HEADS, nc),
            in_specs=[qspec, kspec, vspec, vspec, sspec, gspec],
            out_specs=[vspec, sspec],
            scratch_shapes=[pltpu.VMEM((R_DK, R_DV), F32), pltpu.VMEM((c_len, c_len), F32)],
        ),
        compiler_params=_cparams(("arbitrary",) * 3),
        name="retention",
    )(log_gamma, rqk3, rqk3, rv3, rgs3, state0, g_ret.reshape(1, R_DV))
    return o.reshape(bsz * t_len, SEG), sfin


def _lambda_full(lp_ref, lam_init):
    lp = lp_ref[...]
    a = jnp.sum(lp[0:1] * lp[1:2], axis=-1, keepdims=True)
    b = jnp.sum(lp[2:3] * lp[3:4], axis=-1, keepdims=True)
    return jnp.exp(a) - jnp.exp(b) + lam_init


def _attn_finish(o1, l1, o2, l2, lam, g, lam_init):
    o = o1 / l1 - lam * (o2 / l2)
    return (_rms(o) * g * (1.0 - lam_init)).astype(BF16)


def _attn_prompt_kernel(q_ref, k_ref, v_ref, lp_ref, g_ref, o_ref,
                        kb, vb, m1, l1, a1, m2, l2, a2, *, tile, lam_init):
    i = pl.program_id(2)

    @pl.when(i == 0)
    def _():
        kb[...] = k_ref[0].astype(BF16)
        vb[...] = v_ref[0].astype(BF16)

    q = q_ref[0]
    q1 = q[:, :D_DH]
    q2 = q[:, D_DH:]
    for m, l, a in ((m1, l1, a1), (m2, l2, a2)):
        m[...] = jnp.full(m.shape, NEG_BIG, F32)
        l[...] = jnp.zeros(l.shape, F32)
        a[...] = jnp.zeros(a.shape, F32)

    def update(s, vs, m, l, a):
        m_old = m[...]
        m_new = jnp.maximum(m_old, jnp.max(s, axis=-1, keepdims=True))
        alpha = jnp.exp(m_old - m_new)
        p = jnp.exp(s - m_new)
        l[...] = alpha * l[...] + jnp.sum(p, axis=-1, keepdims=True)
        a[...] = alpha * a[...] + _dot(p.astype(BF16), vs)
        m[...] = m_new

    def block(j, mask):
        off = pl.multiple_of(j * tile, tile)
        ks = kb[pl.ds(off, tile), :]
        vs = vb[pl.ds(off, tile), :]
        s1 = _dot_nt(q1, ks[:, :D_DH])
        s2 = _dot_nt(q2, ks[:, D_DH:])
        if mask is not None:
            s1 = jnp.where(mask, s1, NEG_BIG)
            s2 = jnp.where(mask, s2, NEG_BIG)
        update(s1, vs, m1, l1, a1)
        update(s2, vs, m2, l2, a2)

    def body(j, carry):
        block(j, None)
        return carry

    lax.fori_loop(0, i, body, 0)
    r = lax.broadcasted_iota(I32, (tile, tile), 0)
    c = lax.broadcasted_iota(I32, (tile, tile), 1)
    block(i, (c // CHUNK) <= (r // CHUNK))

    lam = _lambda_full(lp_ref, lam_init)
    o_ref[0] = _attn_finish(a1[...], l1[...], a2[...], l2[...], lam, g_ref[...], lam_init)


def _attn_prompt(dq, dk, dv, lam_p, g_sub, lam_init, bsz, t_len):
    tile = min(ATT_TILE, t_len)
    nq = t_len // tile
    w = 2 * D_DH
    q3 = dq.reshape(bsz, t_len, SEG)
    k3 = dk.reshape(bsz, t_len, SEG)
    v3 = dv.reshape(bsz, t_len, SEG)
    qspec = pl.BlockSpec((1, tile, w), lambda b, h, i: (b, i, h))
    kvspec = pl.BlockSpec((1, t_len, w), lambda b, h, i: (b, 0, h))
    col = lambda: pltpu.VMEM((tile, 1), F32)
    acc = lambda: pltpu.VMEM((tile, D_DV), F32)
    out = pl.pallas_call(
        functools.partial(_attn_prompt_kernel, tile=tile, lam_init=lam_init),
        out_shape=jax.ShapeDtypeStruct((bsz, t_len, SEG), BF16),
        grid=(bsz, D_HEADS, nq),
        in_specs=[qspec, kvspec, kvspec,
                  pl.BlockSpec((4, D_DH), lambda b, h, i: (0, 0)),
                  pl.BlockSpec((1, D_DV), lambda b, h, i: (0, 0))],
        out_specs=qspec,
        scratch_shapes=[pltpu.VMEM((t_len, w), BF16), pltpu.VMEM((t_len, D_DV), BF16),
                        col(), col(), acc(), col(), col(), acc()],
        compiler_params=_cparams(("arbitrary",) * 3, 48),
        name="attn_prompt",
    )(q3, k3, v3, lam_p, g_sub.reshape(1, D_DV))
    return out.reshape(bsz * t_len, SEG)


def _attn_sample_kernel(q_ref, kc_ref, vc_ref, kn_ref, vn_ref, lp_ref, g_ref, o_ref, *, lam_init):
    q = q_ref[0]
    kc = kc_ref[0].astype(BF16)
    vc = vc_ref[0].astype(BF16)
    kn = kn_ref[0].astype(BF16)
    vn = vn_ref[0].astype(BF16)

    def one(qm, lo):
        sc = _dot_nt(qm, kc[:, lo:lo + D_DH])
        sn = _dot_nt(qm, kn[:, lo:lo + D_DH])
        m = jnp.maximum(jnp.max(sc, axis=-1, keepdims=True), jnp.max(sn, axis=-1, keepdims=True))
        pc = jnp.exp(sc - m)
        pn = jnp.exp(sn - m)
        l = jnp.sum(pc, axis=-1, keepdims=True) + jnp.sum(pn, axis=-1, keepdims=True)
        return _dot(pc.astype(BF16), vc) + _dot(pn.astype(BF16), vn), l

    o1, l1 = one(q[:, :D_DH], 0)
    o2, l2 = one(q[:, D_DH:], D_DH)
    lam = _lambda_full(lp_ref, lam_init)
    o_ref[0] = _attn_finish(o1, l1, o2, l2, lam, g_ref[...], lam_init)


def _attn_sample(dq, dk, dv, cache_k, cache_v, lam_p, g_sub, lam_init, bsz, t_len):
    past = cache_k.shape[1]
    w = 2 * D_DH
    q3 = dq.reshape(bsz, t_len, SEG)
    kn3 = dk.reshape(bsz, t_len, SEG)
    vn3 = dv.reshape(bsz, t_len, SEG)
    kc3 = cache_k.reshape(bsz, past, SEG)
    vc3 = cache_v.reshape(bsz, past, SEG)
    nspec = pl.BlockSpec((1, t_len, w), lambda b, h: (b, 0, h))
    cspec = pl.BlockSpec((1, past, w), lambda b, h: (b, 0, h))
    out = pl.pallas_call(
        functools.partial(_attn_sample_kernel, lam_init=lam_init),
        out_shape=jax.ShapeDtypeStruct((bsz, t_len, SEG), BF16),
        grid=(bsz, D_HEADS),
        in_specs=[nspec, cspec, cspec, nspec, nspec,
                  pl.BlockSpec((4, D_DH), lambda b, h: (0, 0)),
                  pl.BlockSpec((1, D_DV), lambda b, h: (0, 0))],
        out_specs=nspec,
        compiler_params=_cparams(("arbitrary",) * 2, 48),
        name="attn_sample",
    )(q3, kc3, vc3, kn3, vn3, lam_p, g_sub.reshape(1, D_DV))
    return out.reshape(bsz * t_len, SEG)


def _split_hi_lo(x):
    hi = x.astype(BF16)
    lo = (x - hi.astype(F32)).astype(BF16)
    return hi, lo


def _merge_kernel(ogr_ref, od_ref, sgr_ref, sgd_ref, x_ref, wr_ref, wd_ref, wo_ref,
                  gf_ref, rw_hi_ref, rw_lo_ref, rb_ref, x1e_ref, cls_ref):
    tm = x_ref.shape[0]
    y_r = _dot(ogr_ref[...], wr_ref[...])
    y_d = _dot(od_ref[...], wd_ref[...])
    m = sgr_ref[...].astype(F32) * y_r + sgd_ref[...].astype(F32) * y_d
    x1 = x_ref[...] + _dot(m.astype(BF16), wo_ref[...])

    hn = _rms(x1) * gf_ref[...]
    h_hi, h_lo = _split_hi_lo(hn)
    w_hi = rw_hi_ref[...]
    lt = _dot_nt(w_hi, h_hi) + _dot_nt(rw_lo_ref[...], h_hi) + _dot_nt(w_hi, h_lo)
    lt = lt + rb_ref[...][:, 0:1]
    g = [lt[i:i + 1, :] for i in range(N_GROUPS)]
    e = [lt[N_GROUPS + i:N_GROUPS + i + 1, :] for i in range(N_EXPERTS)]

    gmax = functools.reduce(jnp.maximum, g)
    gid = jnp.full(g[0].shape, N_GROUPS - 1, I32)
    for i in range(N_GROUPS - 2, -1, -1):
        gid = jnp.where(g[i] == gmax, i, gid)
    g_w = 1.0 / functools.reduce(lambda a, b: a + b, [jnp.exp(v - gmax) for v in g])

    es = []
    for j in range(EXP_PER_GROUP):
        v = e[(N_GROUPS - 1) * EXP_PER_GROUP + j]
        for i in range(N_GROUPS - 2, -1, -1):
            v = jnp.where(gid == i, e[i * EXP_PER_GROUP + j], v)
        es.append(v)

    def first_argmax(vals):
        mx = functools.reduce(jnp.maximum, vals)
        idx = jnp.full(mx.shape, len(vals) - 1, I32)
        for i in range(len(vals) - 2, -1, -1):
            idx = jnp.where(vals[i] == mx, i, idx)
        return mx, idx

    l1, i1 = first_argmax(es)
    rest = [jnp.where(i1 == j, -jnp.inf, es[j]) for j in range(EXP_PER_GROUP)]
    l2, i2 = first_argmax(rest)
    t = jnp.exp(l2 - l1)
    c1 = g_w / (1.0 + t)
    c2 = g_w * t / (1.0 + t)
    lo = jnp.minimum(i1, i2)
    hi = jnp.maximum(i1, i2)
    base = jnp.where(lo == 0, 0, jnp.where(lo == 1, EXP_PER_GROUP - 1, 2 * EXP_PER_GROUP - 3))
    cls = gid * N_PAIRS + base + hi - lo - 1
    wa = jnp.where(i1 < i2, c1, c2)
    wb = jnp.where(i1 < i2, c2, c1)

    cls_ref[0] = cls
    rid = lax.broadcasted_iota(I32, (PAY, tm), 0)
    pay = jnp.where(rid == 0, wa, jnp.where(rid == 1, wb, 0.0))
    d = x_ref.shape[1]
    x1e_ref[:, :d] = x1
    x1e_ref[:, d:] = pay.T


def _merge(ogr, od, sgr, sgd, x2d, w_ret_o, w_diff_o, w_out, g_ffn, rw_hi, rw_lo, rb):
    n, d = x2d.shape
    tm = _row_tile(n)
    nt = n // tm
    row = lambda w: pl.BlockSpec((tm, w), lambda i: (i, 0))
    return pl.pallas_call(
        _merge_kernel,
        out_shape=[jax.ShapeDtypeStruct((n, d + PAY), F32),
                   jax.ShapeDtypeStruct((nt, 1, tm), I32)],
        grid=(nt,),
        in_specs=[row(SEG), row(SEG), row(SEG), row(SEG), row(d),
                  _resident(w_ret_o.shape), _resident(w_diff_o.shape), _resident(w_out.shape),
                  _resident((1, d)), _resident(rw_hi.shape), _resident(rw_lo.shape),
                  _resident(rb.shape)],
        out_specs=[row(d + PAY), pl.BlockSpec((1, 1, tm), lambda i: (i, 0, 0))],
        compiler_params=_cparams(("arbitrary",), 48),
        name="merge_route",
    )(ogr, od, sgr, sgd, x2d, w_ret_o, w_diff_o, w_out, g_ffn.reshape(1, d), rw_hi, rw_lo, rb)


def _prefix_excl(x):
    rid = lax.broadcasted_iota(I32, x.shape, 0)
    inc = x
    s = 1
    while s < x.shape[0]:
        inc = inc + jnp.where(rid >= s, pltpu.roll(inc, s, 0), 0.0)
        s *= 2
    return inc - x


def _sort_kernel(cls_ref, pos_ref, cnt_ref, off_ref, tiles_ref, cnt_scr, run_scr, off_scr,
                 *, moe_tile, n_tiles_pad):
    ph = pl.program_id(0)
    t = pl.program_id(1)
    tm = cls_ref.shape[2]
    cls = cls_ref[0]
    cid = lax.broadcasted_iota(I32, (CLS_PAD, tm), 0)
    onehot = (cid == cls)

    @pl.when(jnp.logical_and(ph == 0, t == 0))
    def _():
        cnt_scr[...] = jnp.zeros(cnt_scr.shape, F32)

    @pl.when(ph == 0)
    def _():
        cnt_scr[...] += jnp.sum(onehot.astype(F32), axis=-1, keepdims=True)

    @pl.when(jnp.logical_and(ph == 1, t == 0))
    def _():
        cnt = cnt_scr[...]
        padded = jnp.ceil(cnt / moe_tile) * moe_tile
        off = _prefix_excl(padded)
        off_scr[...] = off
        run_scr[...] = jnp.zeros(run_scr.shape, F32)
        cnt_ref[...] = cnt.astype(I32)
        off_ref[...] = off.astype(I32)
        end = (off + padded)[:, 0:1]
        total = jnp.max(end, axis=0, keepdims=True)
        n_used = total / moe_tile
        p = lax.broadcasted_iota(I32, (1, n_tiles_pad), 1).astype(F32)
        start = jnp.minimum(p, n_used - 1.0) * moe_tile
        cid2 = lax.broadcasted_iota(I32, (CLS_PAD, n_tiles_pad), 0)
        before = jnp.logical_and(end <= start, cid2 < N_CLASSES)
        tcls = jnp.sum(before.astype(F32), axis=0, keepdims=True).astype(I32)
        grp = tcls // N_PAIRS
        pr = tcls - grp * N_PAIRS
        lo = (pr >= EXP_PER_GROUP - 1).astype(I32) + (pr >= 2 * EXP_PER_GROUP - 3).astype(I32)
        base = jnp.where(lo == 0, 0, jnp.where(lo == 1, EXP_PER_GROUP - 1, 2 * EXP_PER_GROUP - 3))
        hi = pr - base + lo + 1
        rid = lax.broadcasted_iota(I32, (V7X_SUBLANES, n_tiles_pad), 0)
        ea = grp * EXP_PER_GROUP + lo
        eb = grp * EXP_PER_GROUP + hi
        nu = jnp.broadcast_to(n_used.astype(I32), (1, n_tiles_pad))
        tiles_ref[...] = jnp.where(rid == 0, ea, jnp.where(rid == 1, eb, jnp.where(rid == 2, nu, 0)))

    @pl.when(ph == 1)
    def _():
        r = lax.broadcasted_iota(I32, (tm, tm), 0)
        c = lax.broadcasted_iota(I32, (tm, tm), 1)
        upper = jnp.where(r <= c, 1.0, 0.0).astype(BF16)
        oh = jnp.where(onehot, 1.0, 0.0)
        incl = _dot(oh.astype(BF16), upper)
        slot = off_scr[...][:, 0:1] + run_scr[...][:, 0:1] + incl - 1.0
        pos_ref[0] = jnp.sum(oh * slot, axis=0, keepdims=True).astype(I32)
        run_scr[...] += jnp.sum(oh, axis=-1, keepdims=True)


def _sort(cls, moe_tile, n_tiles_pad):
    nt, _, tm = cls.shape
    blk = pl.BlockSpec((1, 1, tm), lambda ph, t: (t, 0, 0))
    oblk = pl.BlockSpec((1, 1, tm), lambda ph, t: (t * ph, 0, 0))
    whole = lambda shape: pl.BlockSpec(shape, lambda ph, t: (0, 0))
    return pl.pallas_call(
        functools.partial(_sort_kernel, moe_tile=moe_tile, n_tiles_pad=n_tiles_pad),
        out_shape=[jax.ShapeDtypeStruct((nt, 1, tm), I32),
                   jax.ShapeDtypeStruct((CLS_PAD, V7X_LANES), I32),
                   jax.ShapeDtypeStruct((CLS_PAD, V7X_LANES), I32),
                   jax.ShapeDtypeStruct((V7X_SUBLANES, n_tiles_pad), I32)],
        grid=(2, nt),
        in_specs=[blk],
        out_specs=[oblk, whole((CLS_PAD, V7X_LANES)), whole((CLS_PAD, V7X_LANES)),
                   whole((V7X_SUBLANES, n_tiles_pad))],
        scratch_shapes=[pltpu.VMEM((CLS_PAD, V7X_LANES), F32)] * 3,
        compiler_params=_cparams(("arbitrary",) * 2),
        name="class_sort",
    )(cls)


def _permute_kernel(pos_ref, cnt_ref, off_ref, src_ref, dst_ref, zblk, sem, zsem, *, moe_tile):
    t = pl.program_id(0)
    tm = pos_ref.shape[2]
    base = t * tm

    def row_copy(r):
        return pltpu.make_async_copy(src_ref.at[pl.ds(base + r, 1)],
                                     dst_ref.at[pl.ds(pos_ref[0, 0, r], 1)], sem)

    def start(r, carry):
        row_copy(r).start()
        return carry

    def wait(r, carry):
        row_copy(r).wait()
        return carry

    lax.fori_loop(0, tm, start, 0)

    @pl.when(t == 0)
    def _():
        zblk[...] = jnp.zeros(zblk.shape, zblk.dtype)
        used = 0
        for c in range(N_CLASSES):
            cnt = cnt_ref[c, 0]
            off = off_ref[c, 0]
            padded = ((cnt + moe_tile - 1) // moe_tile) * moe_tile
            used = off + padded

            def fill(r, carry):
                cp = pltpu.make_async_copy(zblk.at[pl.ds(0, 1)], dst_ref.at[pl.ds(off + r, 1)], zsem)
                cp.start()
                cp.wait()
                return carry

            lax.fori_loop(cnt, padded, fill, 0)

        def fill_tile(p, carry):
            cp = pltpu.make_async_copy(zblk, dst_ref.at[pl.ds(p * moe_tile, moe_tile)], zsem)
            cp.start()
            cp.wait()
            return carry

        lax.fori_loop(used // moe_tile, dst_ref.shape[0] // moe_tile, fill_tile, 0)

    lax.fori_loop(0, tm, wait, 0)


def _permute(pos, cnt, off, src, n_rows_out, moe_tile):
    nt, _, tm = pos.shape
    width = src.shape[1]
    smem = lambda shape, imap: pl.BlockSpec(shape, imap, memory_space=pltpu.SMEM)
    return pl.pallas_call(
        functools.partial(_permute_kernel, moe_tile=moe_tile),
        out_shape=jax.ShapeDtypeStruct((n_rows_out, width), src.dtype),
        grid=(nt,),
        in_specs=[smem((1, 1, tm), lambda t: (t, 0, 0)),
                  smem(cnt.shape, lambda t: (0, 0)),
                  smem(off.shape, lambda t: (0, 0)),
                  pl.BlockSpec(memory_space=pl.ANY)],
        out_specs=pl.BlockSpec(memory_space=pl.ANY),
        scratch_shapes=[pltpu.VMEM((moe_tile, width), src.dtype),
                        pltpu.SemaphoreType.DMA, pltpu.SemaphoreType.DMA],
        compiler_params=_cparams(("arbitrary",)),
        name="permute_rows",
    )(pos, cnt, off, src)


def _unpermute_kernel(pos_ref, src_ref, dst_ref, sem):
    t = pl.program_id(0)
    tm = pos_ref.shape[2]
    base = t * tm

    def row_copy(r):
        return pltpu.make_async_copy(src_ref.at[pl.ds(pos_ref[0, 0, r], 1)],
                                     dst_ref.at[pl.ds(base + r, 1)], sem)

    def start(r, carry):
        row_copy(r).start()
        return carry

    def wait(r, carry):
        row_copy(r).wait()
        return carry

    lax.fori_loop(0, tm, start, 0)
    lax.fori_loop(0, tm, wait, 0)


def _unpermute(pos, src, n_rows_out):
    nt, _, tm = pos.shape
    width = src.shape[1]
    return pl.pallas_call(
        _unpermute_kernel,
        out_shape=jax.ShapeDtypeStruct((n_rows_out, width), src.dtype),
        grid=(nt,),
        in_specs=[pl.BlockSpec((1, 1, tm), lambda t: (t, 0, 0), memory_space=pltpu.SMEM),
                  pl.BlockSpec(memory_space=pl.ANY)],
        out_specs=pl.BlockSpec(memory_space=pl.ANY),
        scratch_shapes=[pltpu.SemaphoreType.DMA],
        compiler_params=_cparams(("arbitrary",)),
        name="unpermute_rows",
    )(pos, src)


def _moe_kernel(ea_ref, eb_ref, nu_ref, xs_ref, gf_ref, wgu_a, wdn_a, wgu_b, wdn_b, y_ref):
    p = pl.program_id(0)
    d = y_ref.shape[1]

    @pl.when(p < nu_ref[0])
    def _():
        x = xs_ref[:, :d]
        wa = xs_ref[:, d:d + 1]
        wb = xs_ref[:, d + 1:d + 2]
        hn = (_rms(x) * gf_ref[...]).astype(BF16)

        def expert(wgu, wdn):
            gu = _dot(hn, wgu[0])
            gate = gu[:, :D_FF]
            he = gate * _sigmoid(gate) * gu[:, D_FF:]
            return _dot(he.astype(BF16), wdn[0])

        y_ref[...] = x + wa * expert(wgu_a, wdn_a) + wb * expert(wgu_b, wdn_b)

    @pl.when(p >= nu_ref[0])
    def _():
        y_ref[...] = jnp.zeros(y_ref.shape, y_ref.dtype)


def _moe(ea, eb, nu, xs, g_ffn, w_gu, w_dn, moe_tile):
    n_rows, width = xs.shape
    d = width - PAY
    n_tiles = n_rows // moe_tile
    used = lambda p, ea, eb, nu: (jnp.minimum(p, nu[0] - 1), 0)
    return pl.pallas_call(
        _moe_kernel,
        out_shape=jax.ShapeDtypeStruct((n_rows, d), F32),
        grid_spec=pltpu.PrefetchScalarGridSpec(
            num_scalar_prefetch=3,
            grid=(n_tiles,),
            in_specs=[pl.BlockSpec((moe_tile, width), used),
                      pl.BlockSpec((1, d), lambda p, ea, eb, nu: (0, 0)),
                      pl.BlockSpec((1, d, 2 * D_FF), lambda p, ea, eb, nu: (ea[p], 0, 0)),
                      pl.BlockSpec((1, D_FF, d), lambda p, ea, eb, nu: (ea[p], 0, 0)),
                      pl.BlockSpec((1, d, 2 * D_FF), lambda p, ea, eb, nu: (eb[p], 0, 0)),
                      pl.BlockSpec((1, D_FF, d), lambda p, ea, eb, nu: (eb[p], 0, 0))],
            out_specs=pl.BlockSpec((moe_tile, d), lambda p, ea, eb, nu: (p, 0)),
        ),
        compiler_params=_cparams(("arbitrary",), 48),
        name="moe_sorted",
    )(ea, eb, nu, xs, g_ffn.reshape(1, d), w_gu, w_dn, w_gu, w_dn)


def _hier_moe_residual(x1e, cls, g_ffn, w_gu, w_dn):
    n = x1e.shape[0]
    moe_tile = min(MOE_TILE, n)
    n_tiles = n // moe_tile + N_CLASSES
    n_tiles_pad = -(-n_tiles // V7X_LANES) * V7X_LANES
    pos, cnt, off, tiles = _sort(cls, moe_tile, n_tiles_pad)
    xs = _permute(pos, cnt, off, x1e, n_tiles * moe_tile, moe_tile)
    ys = _moe(tiles[0, :n_tiles], tiles[1, :n_tiles], tiles[2, :1], xs, g_ffn, w_gu, w_dn, moe_tile)
    return _unpermute(pos, ys, n)


def _rotary_tables(pos):
    half = R_DK // 2
    inv = 1.0 / (ROPE_BASE ** jnp.linspace(0.0, 1.0, half, dtype=F32))
    ang = pos.astype(F32)[:, None] * inv[None, :]
    cos = jnp.cos(ang)
    sin = jnp.sin(ang)
    return jnp.concatenate([cos, cos], axis=-1), jnp.concatenate([-sin, sin], axis=-1)


def _token_group(x, pos, lw, lam_init, log_gamma, state0, cache):
    bsz, t_len, d = x.shape
    x2d = x.reshape(bsz * t_len, d)
    cos_tab, sin_tab = _rotary_tables(pos)
    tm = _row_tile(bsz * t_len)
    if t_len < tm:
        reps = tm // t_len
        cos_tab = jnp.tile(cos_tab, (reps, 1))
        sin_tab = jnp.tile(sin_tab, (reps, 1))
    rqk, rv, rgs, dq, dk, dv, sgr, sgd = _in_proj(
        x2d, lw["g_mix"], lw["w_in"], cos_tab, sin_tab, lw["g_q"], lw["g_k"])
    ogr, s_fin = _retention(rqk, rv, rgs, state0, lw["g_ret"], log_gamma, bsz, t_len)
    if cache is None:
        od = _attn_prompt(dq, dk, dv, lw["lam_p"], lw["g_sub"], lam_init, bsz, t_len)
    else:
        od = _attn_sample(dq, dk, dv, cache[0], cache[1], lw["lam_p"], lw["g_sub"], lam_init,
                          bsz, t_len)
    x1e, cls = _merge(ogr, od, sgr, sgd, x2d, lw["w_ret_o"], lw["w_diff_o"], lw["w_out"],
                      lw["g_ffn"], lw["rw_hi"], lw["rw_lo"], lw["rb"])
    y = _hier_moe_residual(x1e, cls, lw["g_ffn"], lw["w_gu"], lw["w_dn"])
    k_new = dk.reshape(bsz, t_len, D_HEADS, 2 * D_DH)
    v_new = dv.reshape(bsz, t_len, D_HEADS, D_DV)
    return y.reshape(bsz, t_len, d), k_new, v_new, s_fin


def _layer_weights(l, g_mix, w_in, g_q, g_k, lambda_q1, lambda_k1, lambda_q2, lambda_k2, g_ret,
                   w_ret_o, g_sub, w_diff_o, w_out, g_ffn, w_group, b_group, w_expert, b_expert,
                   w_gate, w_up, w_down):
    d = w_in.shape[1]
    n_r = N_GROUPS + N_EXPERTS
    rw = jnp.concatenate([w_group[l], w_expert[l]], axis=1).astype(F32).T
    rw = jnp.zeros((CLS_PAD, d), F32).at[:n_r].set(rw)
    rw_hi = rw.astype(BF16)
    rw_lo = (rw - rw_hi.astype(F32)).astype(BF16)
    rb = jnp.concatenate([b_group[l], b_expert[l]]).astype(F32)
    rb = jnp.zeros((CLS_PAD,), F32).at[:n_r].set(rb)
    rb = jnp.broadcast_to(rb[:, None], (CLS_PAD, V7X_LANES))
    return dict(
        g_mix=g_mix[l], w_in=w_in[l].astype(BF16), g_q=g_q[l], g_k=g_k[l],
        lam_p=jnp.stack([lambda_q1[l], lambda_k1[l], lambda_q2[l], lambda_k2[l]]).astype(F32),
        g_ret=g_ret[l], w_ret_o=w_ret_o[l].astype(BF16), g_sub=g_sub[l],
        w_diff_o=w_diff_o[l].astype(BF16), w_out=w_out[l].astype(BF16), g_ffn=g_ffn[l],
        rw_hi=rw_hi, rw_lo=rw_lo, rb=rb,
        w_gu=jnp.concatenate([w_gate[l], w_up[l]], axis=-1).astype(BF16),
        w_dn=w_down[l].astype(BF16))


def kernel(x_prompt, x_sample, cache_k, cache_v, state_ret, g_mix, w_in, g_q, g_k, lambda_q1, lambda_k1, lambda_q2, lambda_k2, g_ret, w_ret_o, g_sub, w_diff_o, w_out, g_ffn, w_group, b_group, w_expert, b_expert, w_gate, w_up, w_down):
    depth = w_in.shape[0]
    bp, tp, _ = x_prompt.shape
    bs, ts, _ = x_sample.shape
    past = cache_k.shape[2]
    log_gamma = jnp.log1p(-jnp.exp2(-5.0 - jnp.arange(R_HEADS, dtype=F32)))
    pos_p = jnp.arange(tp, dtype=jnp.int32)
    pos_s = past + jnp.arange(ts, dtype=jnp.int32)
    zero_state = jnp.zeros((bp, R_HEADS, R_DK, R_DV), F32)
    xp, xs = x_prompt, x_sample
    outs = [[] for _ in range(6)]
    for l in range(depth):
        lam_init = 0.8 - 0.6 * math.exp(-0.3 * l)
        lw = _layer_weights(l, g_mix, w_in, g_q, g_k, lambda_q1, lambda_k1, lambda_q2, lambda_k2,
                            g_ret, w_ret_o, g_sub, w_diff_o, w_out, g_ffn, w_group, b_group,
                            w_expert, b_expert, w_gate, w_up, w_down)
        xp, kp, vp, sp = _token_group(xp, pos_p, lw, lam_init, log_gamma, zero_state, None)
        cache = (cache_k[l].reshape(bs, past, D_HEADS * 2 * D_DH), cache_v[l].reshape(bs, past, D_HEADS * D_DV))
        xs, ks, vs, ss = _token_group(xs, pos_s, lw, lam_init, log_gamma,
                                      state_ret[l].astype(F32), cache)
        for lst, val in zip(outs, (kp, vp, sp, ks, vs, ss)):
            lst.append(val)
    return (xp, xs) + tuple(jnp.stack(o) for o in outs)
```

```python
import functools
import math

import jax
import jax.numpy as jnp
from jax import lax
from jax.experimental import pallas as pl
from jax.experimental.pallas import tpu as pltpu

F32 = jnp.float32
BF16 = jnp.bfloat16
I32 = jnp.int32

CHUNK = 64
EPS = 1e-6
R_HEADS = 4
R_DK = 128
R_DV = 256
ROPE_BASE = 10000.0
D_HEADS = 4
D_DH = 128
D_DV = 256
N_GROUPS = 4
EXP_PER_GROUP = 4
N_EXPERTS = N_GROUPS * EXP_PER_GROUP
N_PAIRS = EXP_PER_GROUP * (EXP_PER_GROUP - 1) // 2
N_CLASSES = N_GROUPS * N_PAIRS
D_FF = 512
SEG = 1024
N_SEG = 8

V7X_LANES = 128
V7X_SUBLANES = 8
V7X_VMEM_LIMIT_BYTES = 56 * 1024 * 1024

ROW_TILE = 512
RET_SUPER = 512
ATT_TILE = 512
MOE_TILE = 256
CLS_PAD = 32
NEG_BIG = -1e30
PAY = V7X_LANES


def _row_tile(n):
    return min(ROW_TILE, n)


def _cparams(sem, vmem_mb=None):
    kw = dict(dimension_semantics=sem)
    if vmem_mb is not None:
        kw["vmem_limit_bytes"] = min(vmem_mb * 1024 * 1024, V7X_VMEM_LIMIT_BYTES)
    return pltpu.CompilerParams(**kw)


def _resident(shape):
    nd = len(shape)
    return pl.BlockSpec(shape, lambda *_: (0,) * nd, pipeline_mode=pl.Buffered(1))


def _rms(x, eps=EPS):
    return x * lax.rsqrt(jnp.mean(x * x, axis=-1, keepdims=True) + eps)


def _sigmoid(x):
    return 1.0 / (1.0 + jnp.exp(-x))


def _dot(a, b):
    return jnp.dot(a, b, preferred_element_type=F32)


def _dot_nt(a, b):
    return lax.dot_general(a, b, (((1,), (1,)), ((), ())), preferred_element_type=F32)


def _dot_tn(a, b):
    return lax.dot_general(a, b, (((0,), (0,)), ((), ())), preferred_element_type=F32)


def _inproj_kernel(x_ref, g_ref, w_ref, cos_ref, sin_ref, gq_ref, gk_ref,
                   rqk_ref, rv_ref, rgs_ref, dq_ref, dk_ref, dv_ref, sgr_ref, sgd_ref):
    x = x_ref[...]
    hb = (_rms(x) * g_ref[...]).astype(BF16)

    def seg(s):
        return _dot(hb, w_ref[:, s * SEG:(s + 1) * SEG])

    cos = cos_ref[...]
    sin = sin_ref[...]
    z = seg(0)
    for j in range(2 * R_HEADS):
        v = z[:, j * R_DK:(j + 1) * R_DK]
        r = v * cos + pltpu.roll(v, R_DK // 2, 1) * sin
        if j >= R_HEADS:
            r = r * (R_DK ** -0.5)
        rqk_ref[:, j * R_DK:(j + 1) * R_DK] = r.astype(BF16)

    rv_ref[...] = seg(1).astype(BF16)
    z = seg(2)
    rgs_ref[...] = (z * _sigmoid(z)).astype(BF16)

    z = seg(3)
    gq = gq_ref[...] * (D_DH ** -0.5)
    for j in range(2 * D_HEADS):
        v = z[:, j * D_DH:(j + 1) * D_DH]
        dq_ref[:, j * D_DH:(j + 1) * D_DH] = (_rms(v) * gq).astype(BF16)
    z = seg(4)
    gk = gk_ref[...]
    for j in range(2 * D_HEADS):
        v = z[:, j * D_DH:(j + 1) * D_DH]
        dk_ref[:, j * D_DH:(j + 1) * D_DH] = _rms(v) * gk
    dv_ref[...] = seg(5)
    sgr_ref[...] = _sigmoid(seg(6)).astype(BF16)
    sgd_ref[...] = _sigmoid(seg(7)).astype(BF16)


def _in_proj(x2d, g_mix, w_in_bf, cos_tab, sin_tab, g_q, g_k):
    n, d = x2d.shape
    tm = _row_tile(n)
    nt = n // tm
    ntab = cos_tab.shape[0] // tm
    row = lambda w: pl.BlockSpec((tm, w), lambda i: (i, 0))
    tab = pl.BlockSpec((tm, R_DK), lambda i: (i % ntab, 0))
    outs = [jax.ShapeDtypeStruct((n, SEG), dt)
            for dt in (BF16, BF16, BF16, BF16, F32, F32, BF16, BF16)]
    return pl.pallas_call(
        _inproj_kernel,
        out_shape=outs,
        grid=(nt,),
        in_specs=[row(d), _resident((1, d)), _resident(w_in_bf.shape), tab, tab,
                  _resident((1, D_DH)), _resident((1, D_DH))],
        out_specs=[row(SEG)] * 8,
        compiler_params=_cparams(("arbitrary",), 56),
        name="in_proj",
    )(x2d, g_mix.reshape(1, d), w_in_bf, cos_tab, sin_tab,
      g_q.reshape(1, D_DH), g_k.reshape(1, D_DH))


def _ret_kernel(lg_ref, q_ref, k_ref, v_ref, rgs_ref, s0_ref, g_ref,
                o_ref, sfin_ref, s_scr, d_scr, *, c_len):
    h = pl.program_id(1)
    c = pl.program_id(2)
    lg = lg_ref[h]

    @pl.when(c == 0)
    def _():
        s_scr[...] = s0_ref[0, 0]
        t = lax.broadcasted_iota(I32, (c_len, c_len), 0)
        s = lax.broadcasted_iota(I32, (c_len, c_len), 1)
        dist = jnp.abs(t - s).astype(F32)
        vis = (s // CHUNK) <= (t // CHUNK)
        d_scr[...] = jnp.where(vis, jnp.exp(lg * dist), 0.0)

    q = q_ref[0]
    k = k_ref[0]
    v = v_ref[0]
    pos = lax.broadcasted_iota(I32, (c_len, 1), 0).astype(F32)
    state = s_scr[...]
    s = _dot_nt(q, k) * d_scr[...]
    o = _dot(s.astype(BF16), v)
    o = o + jnp.exp(lg * (pos + 1.0)) * _dot(q, state.astype(BF16))
    kw = (k.astype(F32) * jnp.exp(lg * (c_len - 1.0 - pos))).astype(BF16)
    decay = jnp.exp(lg * jnp.full((1, R_DV), float(c_len), F32))
    new_state = state * decay + _dot_tn(kw, v)
    s_scr[...] = new_state

    o_ref[0] = (_rms(o) * g_ref[...] * rgs_ref[0].astype(F32)).astype(BF16)

    @pl.when(c == pl.num_programs(2) - 1)
    def _():
        sfin_ref[0, 0] = new_state


def _retention(rqk, rv, rgs, state0, g_ret, log_gamma, bsz, t_len):
    c_len = min(RET_SUPER, t_len)
    nc = t_len // c_len
    rqk3 = rqk.reshape(bsz, t_len, SEG)
    rv3 = rv.reshape(bsz, t_len, SEG)
    rgs3 = rgs.reshape(bsz, t_len, SEG)
    qspec = pl.BlockSpec((1, c_len, R_DK), lambda b, h, c, lg: (b, c, h))
    kspec = pl.BlockSpec((1, c_len, R_DK), lambda b, h, c, lg: (b, c, R_HEADS + h))
    vspec = pl.BlockSpec((1, c_len, R_DV), lambda b, h, c, lg: (b, c, h))
    sspec = pl.BlockSpec((1, 1, R_DK, R_DV), lambda b, h, c, lg: (b, h, 0, 0))
    gspec = pl.BlockSpec((1, R_DV), lambda b, h, c, lg: (0, 0))
    o, sfin = pl.pallas_call(
        functools.partial(_ret_kernel, c_len=c_len),
        out_shape=[jax.ShapeDtypeStruct((bsz, t_len, SEG), BF16),
                   jax.ShapeDtypeStruct((bsz, R_HEADS, R_DK, R_DV), F32)],
        grid_spec=pltpu.PrefetchScalarGridSpec(
            num_scalar_prefetch=1,
            grid=(bsz, R_HEADS, nc),
            in_specs=[qspec, kspec, vspec, vspec, sspec, gspec],
            out_specs=[vspec, sspec],
            scratch_shapes=[pltpu.VMEM((R_DK, R_DV), F32), pltpu.VMEM((c_len, c_len), F32)],
        ),
        compiler_params=_cparams(("arbitrary",) * 3),
        name="retention",
    )(log_gamma, rqk3, rqk3, rv3, rgs3, state0, g_ret.reshape(1, R_DV))
    return o.reshape(bsz * t_len, SEG), sfin


def _lambda_full(lp_ref, lam_init):
    lp = lp_ref[...]
    a = jnp.sum(lp[0:1] * lp[1:2], axis=-1, keepdims=True)
    b = jnp.sum(lp[2:3] * lp[3:4], axis=-1, keepdims=True)
    return jnp.exp(a) - jnp.exp(b) + lam_init


def _attn_finish(o1, l1, o2, l2, lam, g, lam_init):
    o = o1 / l1 - lam * (o2 / l2)
    return (_rms(o) * g * (1.0 - lam_init)).astype(BF16)


def _attn_prompt_kernel(q_ref, k_ref, v_ref, lp_ref, g_ref, o_ref,
                        kb, vb, m1, l1, a1, m2, l2, a2, *, tile, lam_init):
    i = pl.program_id(2)

    @pl.when(i == 0)
    def _():
        kb[...] = k_ref[0].astype(BF16)
        vb[...] = v_ref[0].astype(BF16)

    q = q_ref[0]
    q1 = q[:, :D_DH]
    q2 = q[:, D_DH:]
    for m, l, a in ((m1, l1, a1), (m2, l2, a2)):
        m[...] = jnp.full(m.shape, NEG_BIG, F32)
        l[...] = jnp.zeros(l.shape, F32)
        a[...] = jnp.zeros(a.shape, F32)

    def update(s, vs, m, l, a):
        m_old = m[...]
        m_new = jnp.maximum(m_old, jnp.max(s, axis=-1, keepdims=True))
        alpha = jnp.exp(m_old - m_new)
        p = jnp.exp(s - m_new)
        l[...] = alpha * l[...] + jnp.sum(p, axis=-1, keepdims=True)
        a[...] = alpha * a[...] + _dot(p.astype(BF16), vs)
        m[...] = m_new

    def block(j, mask):
        off = pl.multiple_of(j * tile, tile)
        ks = kb[pl.ds(off, tile), :]
        vs = vb[pl.ds(off, tile), :]
        s1 = _dot_nt(q1, ks[:, :D_DH])
        s2 = _dot_nt(q2, ks[:, D_DH:])
        if mask is not None:
            s1 = jnp.where(mask, s1, NEG_BIG)
            s2 = jnp.where(mask, s2, NEG_BIG)
        update(s1, vs, m1, l1, a1)
        update(s2, vs, m2, l2, a2)

    def body(j, carry):
        block(j, None)
        return carry

    lax.fori_loop(0, i, body, 0)
    r = lax.broadcasted_iota(I32, (tile, tile), 0)
    c = lax.broadcasted_iota(I32, (tile, tile), 1)
    block(i, (c // CHUNK) <= (r // CHUNK))

    lam = _lambda_full(lp_ref, lam_init)
    o_ref[0] = _attn_finish(a1[...], l1[...], a2[...], l2[...], lam, g_ref[...], lam_init)


def _attn_prompt(dq, dk, dv, lam_p, g_sub, lam_init, bsz, t_len):
    tile = min(ATT_TILE, t_len)
    nq = t_len // tile
    w = 2 * D_DH
    q3 = dq.reshape(bsz, t_len, SEG)
    k3 = dk.reshape(bsz, t_len, SEG)
    v3 = dv.reshape(bsz, t_len, SEG)
    qspec = pl.BlockSpec((1, tile, w), lambda b, h, i: (b, i, h))
    kvspec = pl.BlockSpec((1, t_len, w), lambda b, h, i: (b, 0, h))
    col = lambda: pltpu.VMEM((tile, 1), F32)
    acc = lambda: pltpu.VMEM((tile, D_DV), F32)
    out = pl.pallas_call(
        functools.partial(_attn_prompt_kernel, tile=tile, lam_init=lam_init),
        out_shape=jax.ShapeDtypeStruct((bsz, t_len, SEG), BF16),
        grid=(bsz, D_HEADS, nq),
        in_specs=[qspec, kvspec, kvspec,
                  pl.BlockSpec((4, D_DH), lambda b, h, i: (0, 0)),
                  pl.BlockSpec((1, D_DV), lambda b, h, i: (0, 0))],
        out_specs=qspec,
        scratch_shapes=[pltpu.VMEM((t_len, w), BF16), pltpu.VMEM((t_len, D_DV), BF16),
                        col(), col(), acc(), col(), col(), acc()],
        compiler_params=_cparams(("arbitrary",) * 3, 48),
        name="attn_prompt",
    )(q3, k3, v3, lam_p, g_sub.reshape(1, D_DV))
    return out.reshape(bsz * t_len, SEG)


def _attn_sample_kernel(q_ref, kc_ref, vc_ref, kn_ref, vn_ref, lp_ref, g_ref, o_ref, *, lam_init):
    q = q_ref[0]
    kc = kc_ref[0].astype(BF16)
    vc = vc_ref[0].astype(BF16)
    kn = kn_ref[0].astype(BF16)
    vn = vn_ref[0].astype(BF16)

    def one(qm, lo):
        sc = _dot_nt(qm, kc[:, lo:lo + D_DH])
        sn = _dot_nt(qm, kn[:, lo:lo + D_DH])
        m = jnp.maximum(jnp.max(sc, axis=-1, keepdims=True), jnp.max(sn, axis=-1, keepdims=True))
        pc = jnp.exp(sc - m)
        pn = jnp.exp(sn - m)
        l = jnp.sum(pc, axis=-1, keepdims=True) + jnp.sum(pn, axis=-1, keepdims=True)
        return _dot(pc.astype(BF16), vc) + _dot(pn.astype(BF16), vn), l

    o1, l1 = one(q[:, :D_DH], 0)
    o2, l2 = one(q[:, D_DH:], D_DH)
    lam = _lambda_full(lp_ref, lam_init)
    o_ref[0] = _attn_finish(o1, l1, o2, l2, lam, g_ref[...], lam_init)


def _attn_sample(dq, dk, dv, cache_k, cache_v, lam_p, g_sub, lam_init, bsz, t_len):
    past = cache_k.shape[1]
    w = 2 * D_DH
    q3 = dq.reshape(bsz, t_len, SEG)
    kn3 = dk.reshape(bsz, t_len, SEG)
    vn3 = dv.reshape(bsz, t_len, SEG)
    kc3 = cache_k.reshape(bsz, past, SEG)
    vc3 = cache_v.reshape(bsz, past, SEG)
    nspec = pl.BlockSpec((1, t_len, w), lambda b, h: (b, 0, h))
    cspec = pl.BlockSpec((1, past, w), lambda b, h: (b, 0, h))
    out = pl.pallas_call(
        functools.partial(_attn_sample_kernel, lam_init=lam_init),
        out_shape=jax.ShapeDtypeStruct((bsz, t_len, SEG), BF16),
        grid=(bsz, D_HEADS),
        in_specs=[nspec, cspec, cspec, nspec, nspec,
                  pl.BlockSpec((4, D_DH), lambda b, h: (0, 0)),
                  pl.BlockSpec((1, D_DV), lambda b, h: (0, 0))],
        out_specs=nspec,
        compiler_params=_cparams(("arbitrary",) * 2, 48),
        name="attn_sample",
    )(q3, kc3, vc3, kn3, vn3, lam_p, g_sub.reshape(1, D_DV))
    return out.reshape(bsz * t_len, SEG)


def _split_hi_lo(x):
    hi = x.astype(BF16)
    lo = (x - hi.astype(F32)).astype(BF16)
    return hi, lo


def _merge_kernel(ogr_ref, od_ref, sgr_ref, sgd_ref, x_ref, wr_ref, wd_ref, wo_ref,
                  gf_ref, rw_hi_ref, rw_lo_ref, rb_ref, x1e_ref, cls_ref):
    tm = x_ref.shape[0]
    y_r = _dot(ogr_ref[...], wr_ref[...])
    y_d = _dot(od_ref[...], wd_ref[...])
    m = sgr_ref[...].astype(F32) * y_r + sgd_ref[...].astype(F32) * y_d
    x1 = x_ref[...] + _dot(m.astype(BF16), wo_ref[...])

    hn = _rms(x1) * gf_ref[...]
    h_hi, h_lo = _split_hi_lo(hn)
    w_hi = rw_hi_ref[...]
    lt = _dot_nt(w_hi, h_hi) + _dot_nt(rw_lo_ref[...], h_hi) + _dot_nt(w_hi, h_lo)
    lt = lt + rb_ref[...][:, 0:1]
    g = [lt[i:i + 1, :] for i in range(N_GROUPS)]
    e = [lt[N_GROUPS + i:N_GROUPS + i + 1, :] for i in range(N_EXPERTS)]

    gmax = functools.reduce(jnp.maximum, g)
    gid = jnp.full(g[0].shape, N_GROUPS - 1, I32)
    for i in range(N_GROUPS - 2, -1, -1):
        gid = jnp.where(g[i] == gmax, i, gid)
    g_w = 1.0 / functools.reduce(lambda a, b: a + b, [jnp.exp(v - gmax) for v in g])

    es = []
    for j in range(EXP_PER_GROUP):
        v = e[(N_GROUPS - 1) * EXP_PER_GROUP + j]
        for i in range(N_GROUPS - 2, -1, -1):
            v = jnp.where(gid == i, e[i * EXP_PER_GROUP + j], v)
        es.append(v)

    def first_argmax(vals):
        mx = functools.reduce(jnp.maximum, vals)
        idx = jnp.full(mx.shape, len(vals) - 1, I32)
        for i in range(len(vals) - 2, -1, -1):
            idx = jnp.where(vals[i] == mx, i, idx)
        return mx, idx

    l1, i1 = first_argmax(es)
    rest = [jnp.where(i1 == j, -jnp.inf, es[j]) for j in range(EXP_PER_GROUP)]
    l2, i2 = first_argmax(rest)
    t = jnp.exp(l2 - l1)
    c1 = g_w / (1.0 + t)
    c2 = g_w * t / (1.0 + t)
    lo = jnp.minimum(i1, i2)
    hi = jnp.maximum(i1, i2)
    base = jnp.where(lo == 0, 0, jnp.where(lo == 1, EXP_PER_GROUP - 1, 2 * EXP_PER_GROUP - 3))
    cls = gid * N_PAIRS + base + hi - lo - 1
    wa = jnp.where(i1 < i2, c1, c2)
    wb = jnp.where(i1 < i2, c2, c1)

    cls_ref[0] = cls
    rid = lax.broadcasted_iota(I32, (PAY, tm), 0)
    pay = jnp.where(rid == 0, wa, jnp.where(rid == 1, wb, 0.0))
    d = x_ref.shape[1]
    x1e_ref[:, :d] = x1
    x1e_ref[:, d:] = pay.T


def _merge(ogr, od, sgr, sgd, x2d, w_ret_o, w_diff_o, w_out, g_ffn, rw_hi, rw_lo, rb):
    n, d = x2d.shape
    tm = _row_tile(n)
    nt = n // tm
    row = lambda w: pl.BlockSpec((tm, w), lambda i: (i, 0))
    return pl.pallas_call(
        _merge_kernel,
        out_shape=[jax.ShapeDtypeStruct((n, d + PAY), F32),
                   jax.ShapeDtypeStruct((nt, 1, tm), I32)],
        grid=(nt,),
        in_specs=[row(SEG), row(SEG), row(SEG), row(SEG), row(d),
                  _resident(w_ret_o.shape), _resident(w_diff_o.shape), _resident(w_out.shape),
                  _resident((1, d)), _resident(rw_hi.shape), _resident(rw_lo.shape),
                  _resident(rb.shape)],
        out_specs=[row(d + PAY), pl.BlockSpec((1, 1, tm), lambda i: (i, 0, 0))],
        compiler_params=_cparams(("arbitrary",), 48),
        name="merge_route",
    )(ogr, od, sgr, sgd, x2d, w_ret_o, w_diff_o, w_out, g_ffn.reshape(1, d), rw_hi, rw_lo, rb)


def _prefix_excl(x):
    rid = lax.broadcasted_iota(I32, x.shape, 0)
    inc = x
    s = 1
    while s < x.shape[0]:
        inc = inc + jnp.where(rid >= s, pltpu.roll(inc, s, 0), 0.0)
        s *= 2
    return inc - x


def _sort_kernel(cls_ref, pos_ref, cnt_ref, off_ref, tiles_ref, cnt_scr, run_scr, off_scr,
                 *, moe_tile, n_tiles_pad):
    ph = pl.program_id(0)
    t = pl.program_id(1)
    tm = cls_ref.shape[2]
    cls = cls_ref[0]
    cid = lax.broadcasted_iota(I32, (CLS_PAD, tm), 0)
    onehot = (cid == cls)

    @pl.when(jnp.logical_and(ph == 0, t == 0))
    def _():
        cnt_scr[...] = jnp.zeros(cnt_scr.shape, F32)

    @pl.when(ph == 0)
    def _():
        cnt_scr[...] += jnp.sum(onehot.astype(F32), axis=-1, keepdims=True)

    @pl.when(jnp.logical_and(ph == 1, t == 0))
    def _():
        cnt = cnt_scr[...]
        padded = jnp.ceil(cnt / moe_tile) * moe_tile
        off = _prefix_excl(padded)
        off_scr[...] = off
        run_scr[...] = jnp.zeros(run_scr.shape, F32)
        cnt_ref[...] = cnt.astype(I32)
        off_ref[...] = off.astype(I32)
        end = (off + padded)[:, 0:1]
        total = jnp.max(end, axis=0, keepdims=True)
        n_used = total / moe_tile
        p = lax.broadcasted_iota(I32, (1, n_tiles_pad), 1).astype(F32)
        start = jnp.minimum(p, n_used - 1.0) * moe_tile
        cid2 = lax.broadcasted_iota(I32, (CLS_PAD, n_tiles_pad), 0)
        before = jnp.logical_and(end <= start, cid2 < N_CLASSES)
        tcls = jnp.sum(before.astype(F32), axis=0, keepdims=True).astype(I32)
        grp = tcls // N_PAIRS
        pr = tcls - grp * N_PAIRS
        lo = (pr >= EXP_PER_GROUP - 1).astype(I32) + (pr >= 2 * EXP_PER_GROUP - 3).astype(I32)
        base = jnp.where(lo == 0, 0, jnp.where(lo == 1, EXP_PER_GROUP - 1, 2 * EXP_PER_GROUP - 3))
        hi = pr - base + lo + 1
        rid = lax.broadcasted_iota(I32, (V7X_SUBLANES, n_tiles_pad), 0)
        ea = grp * EXP_PER_GROUP + lo
        eb = grp * EXP_PER_GROUP + hi
        nu = jnp.broadcast_to(n_used.astype(I32), (1, n_tiles_pad))
        tiles_ref[...] = jnp.where(rid == 0, ea, jnp.where(rid == 1, eb, jnp.where(rid == 2, nu, 0)))

    @pl.when(ph == 1)
    def _():
        r = lax.broadcasted_iota(I32, (tm, tm), 0)
        c = lax.broadcasted_iota(I32, (tm, tm), 1)
        upper = jnp.where(r <= c, 1.0, 0.0).astype(BF16)
        oh = jnp.where(onehot, 1.0, 0.0)
        incl = _dot(oh.astype(BF16), upper)
        slot = off_scr[...][:, 0:1] + run_scr[...][:, 0:1] + incl - 1.0
        pos_ref[0] = jnp.sum(oh * slot, axis=0, keepdims=True).astype(I32)
        run_scr[...] += jnp.sum(oh, axis=-1, keepdims=True)


def _sort(cls, moe_tile, n_tiles_pad):
    nt, _, tm = cls.shape
    blk = pl.BlockSpec((1, 1, tm), lambda ph, t: (t, 0, 0))
    oblk = pl.BlockSpec((1, 1, tm), lambda ph, t: (t * ph, 0, 0))
    whole = lambda shape: pl.BlockSpec(shape, lambda ph, t: (0, 0))
    return pl.pallas_call(
        functools.partial(_sort_kernel, moe_tile=moe_tile, n_tiles_pad=n_tiles_pad),
        out_shape=[jax.ShapeDtypeStruct((nt, 1, tm), I32),
                   jax.ShapeDtypeStruct((CLS_PAD, V7X_LANES), I32),
                   jax.ShapeDtypeStruct((CLS_PAD, V7X_LANES), I32),
                   jax.ShapeDtypeStruct((V7X_SUBLANES, n_tiles_pad), I32)],
        grid=(2, nt),
        in_specs=[blk],
        out_specs=[oblk, whole((CLS_PAD, V7X_LANES)), whole((CLS_PAD, V7X_LANES)),
                   whole((V7X_SUBLANES, n_tiles_pad))],
        scratch_shapes=[pltpu.VMEM((CLS_PAD, V7X_LANES), F32)] * 3,
        compiler_params=_cparams(("arbitrary",) * 2),
        name="class_sort",
    )(cls)


def _permute_kernel(pos_ref, cnt_ref, off_ref, src_ref, dst_ref, zblk, sem, zsem, *, moe_tile):
    t = pl.program_id(0)
    tm = pos_ref.shape[2]

    def row_copy(r):
        return pltpu.make_async_copy(src_ref.at[pl.ds(r, 1)],
                                     dst_ref.at[pl.ds(pos_ref[0, 0, r], 1)], sem)

    def start(r, carry):
        row_copy(r).start()
        return carry

    def wait(r, carry):
        row_copy(r).wait()
        return carry

    lax.fori_loop(0, tm, start, 0)

    @pl.when(t == 0)
    def _():
        zblk[...] = jnp.zeros(zblk.shape, zblk.dtype)
        used = 0
        for c in range(N_CLASSES):
            cnt = cnt_ref[c, 0]
            off = off_ref[c, 0]
            padded = ((cnt + moe_tile - 1) // moe_tile) * moe_tile
            used = off + padded

            def pad_copy(r, off=off):
                return pltpu.make_async_copy(zblk.at[pl.ds(0, 1)], dst_ref.at[pl.ds(off + r, 1)], zsem)

            def pad_start(r, carry, copy=pad_copy):
                copy(r).start()
                return carry

            def pad_wait(r, carry, copy=pad_copy):
                copy(r).wait()
                return carry

            lax.fori_loop(cnt, padded, pad_start, 0)
            lax.fori_loop(cnt, padded, pad_wait, 0)

        def tile_copy(p):
            return pltpu.make_async_copy(zblk, dst_ref.at[pl.ds(p * moe_tile, moe_tile)], zsem)

        def tile_start(p, carry):
            tile_copy(p).start()
            return carry

        def tile_wait(p, carry):
            tile_copy(p).wait()
            return carry

        first, last = used // moe_tile, dst_ref.shape[0] // moe_tile
        lax.fori_loop(first, last, tile_start, 0)
        lax.fori_loop(first, last, tile_wait, 0)

    lax.fori_loop(0, tm, wait, 0)


def _permute(pos, cnt, off, src, n_rows_out, moe_tile):
    nt, _, tm = pos.shape
    width = src.shape[1]
    smem = lambda shape, imap: pl.BlockSpec(shape, imap, memory_space=pltpu.SMEM)
    return pl.pallas_call(
        functools.partial(_permute_kernel, moe_tile=moe_tile),
        out_shape=jax.ShapeDtypeStruct((n_rows_out, width), src.dtype),
        grid=(nt,),
        in_specs=[smem((1, 1, tm), lambda t: (t, 0, 0)),
                  smem(cnt.shape, lambda t: (0, 0)),
                  smem(off.shape, lambda t: (0, 0)),
                  pl.BlockSpec((tm, width), lambda t: (t, 0))],
        out_specs=pl.BlockSpec(memory_space=pl.ANY),
        scratch_shapes=[pltpu.VMEM((moe_tile, width), src.dtype),
                        pltpu.SemaphoreType.DMA, pltpu.SemaphoreType.DMA],
        compiler_params=_cparams(("arbitrary",)),
        name="permute_rows",
    )(pos, cnt, off, src)


def _unpermute_kernel(pos_ref, src_ref, dst_ref, sem):
    tm = pos_ref.shape[2]

    def row_copy(r):
        return pltpu.make_async_copy(src_ref.at[pl.ds(pos_ref[0, 0, r], 1)],
                                     dst_ref.at[pl.ds(r, 1)], sem)

    def start(r, carry):
        row_copy(r).start()
        return carry

    def wait(r, carry):
        row_copy(r).wait()
        return carry

    lax.fori_loop(0, tm, start, 0)
    lax.fori_loop(0, tm, wait, 0)


def _unpermute(pos, src, n_rows_out):
    nt, _, tm = pos.shape
    width = src.shape[1]
    return pl.pallas_call(
        _unpermute_kernel,
        out_shape=jax.ShapeDtypeStruct((n_rows_out, width), src.dtype),
        grid=(nt,),
        in_specs=[pl.BlockSpec((1, 1, tm), lambda t: (t, 0, 0), memory_space=pltpu.SMEM),
                  pl.BlockSpec(memory_space=pl.ANY)],
        out_specs=pl.BlockSpec((tm, width), lambda t: (t, 0)),
        scratch_shapes=[pltpu.SemaphoreType.DMA],
        compiler_params=_cparams(("arbitrary",)),
        name="unpermute_rows",
    )(pos, src)


def _moe_kernel(ea_ref, eb_ref, nu_ref, xs_ref, gf_ref, wgu_a, wdn_a, wgu_b, wdn_b, y_ref):
    p = pl.program_id(0)
    d = y_ref.shape[1]

    @pl.when(p < nu_ref[0])
    def _():
        x = xs_ref[:, :d]
        wa = xs_ref[:, d:d + 1]
        wb = xs_ref[:, d + 1:d + 2]
        hn = (_rms(x) * gf_ref[...]).astype(BF16)

        def expert(wgu, wdn):
            gu = _dot(hn, wgu[0])
            gate = gu[:, :D_FF]
            he = gate * _sigmoid(gate) * gu[:, D_FF:]
            return _dot(he.astype(BF16), wdn[0])

        y_ref[...] = x + wa * expert(wgu_a, wdn_a) + wb * expert(wgu_b, wdn_b)

    @pl.when(p >= nu_ref[0])
    def _():
        y_ref[...] = jnp.zeros(y_ref.shape, y_ref.dtype)


def _moe(ea, eb, nu, xs, g_ffn, w_gu, w_dn, moe_tile):
    n_rows, width = xs.shape
    d = width - PAY
    n_tiles = n_rows // moe_tile
    used = lambda p, ea, eb, nu: (jnp.minimum(p, nu[0] - 1), 0)
    return pl.pallas_call(
        _moe_kernel,
        out_shape=jax.ShapeDtypeStruct((n_rows, d), F32),
        grid_spec=pltpu.PrefetchScalarGridSpec(
            num_scalar_prefetch=3,
            grid=(n_tiles,),
            in_specs=[pl.BlockSpec((moe_tile, width), used),
                      pl.BlockSpec((1, d), lambda p, ea, eb, nu: (0, 0)),
                      pl.BlockSpec((1, d, 2 * D_FF), lambda p, ea, eb, nu: (ea[p], 0, 0)),
                      pl.BlockSpec((1, D_FF, d), lambda p, ea, eb, nu: (ea[p], 0, 0)),
                      pl.BlockSpec((1, d, 2 * D_FF), lambda p, ea, eb, nu: (eb[p], 0, 0)),
                      pl.BlockSpec((1, D_FF, d), lambda p, ea, eb, nu: (eb[p], 0, 0))],
            out_specs=pl.BlockSpec((moe_tile, d), lambda p, ea, eb, nu: (p, 0)),
        ),
        compiler_params=_cparams(("arbitrary",), 48),
        name="moe_sorted",
    )(ea, eb, nu, xs, g_ffn.reshape(1, d), w_gu, w_dn, w_gu, w_dn)


def _hier_moe_residual(x1e, cls, g_ffn, w_gu, w_dn):
    n = x1e.shape[0]
    moe_tile = min(MOE_TILE, n)
    n_tiles = n // moe_tile + N_CLASSES
    n_tiles_pad = -(-n_tiles // V7X_LANES) * V7X_LANES
    pos, cnt, off, tiles = _sort(cls, moe_tile, n_tiles_pad)
    xs = _permute(pos, cnt, off, x1e, n_tiles * moe_tile, moe_tile)
    ys = _moe(tiles[0, :n_tiles], tiles[1, :n_tiles], tiles[2, :1], xs, g_ffn, w_gu, w_dn, moe_tile)
    return _unpermute(pos, ys, n)


def _rotary_tables(pos):
    half = R_DK // 2
    inv = 1.0 / (ROPE_BASE ** jnp.linspace(0.0, 1.0, half, dtype=F32))
    ang = pos.astype(F32)[:, None] * inv[None, :]
    cos = jnp.cos(ang)
    sin = jnp.sin(ang)
    return jnp.concatenate([cos, cos], axis=-1), jnp.concatenate([-sin, sin], axis=-1)


def _token_group(x, pos, lw, lam_init, log_gamma, state0, cache):
    bsz, t_len, d = x.shape
    x2d = x.reshape(bsz * t_len, d)
    cos_tab, sin_tab = _rotary_tables(pos)
    tm = _row_tile(bsz * t_len)
    if t_len < tm:
        reps = tm // t_len
        cos_tab = jnp.tile(cos_tab, (reps, 1))
        sin_tab = jnp.tile(sin_tab, (reps, 1))
    rqk, rv, rgs, dq, dk, dv, sgr, sgd = _in_proj(
        x2d, lw["g_mix"], lw["w_in"], cos_tab, sin_tab, lw["g_q"], lw["g_k"])
    ogr, s_fin = _retention(rqk, rv, rgs, state0, lw["g_ret"], log_gamma, bsz, t_len)
    if cache is None:
        od = _attn_prompt(dq, dk, dv, lw["lam_p"], lw["g_sub"], lam_init, bsz, t_len)
    else:
        od = _attn_sample(dq, dk, dv, cache[0], cache[1], lw["lam_p"], lw["g_sub"], lam_init,
                          bsz, t_len)
    x1e, cls = _merge(ogr, od, sgr, sgd, x2d, lw["w_ret_o"], lw["w_diff_o"], lw["w_out"],
                      lw["g_ffn"], lw["rw_hi"], lw["rw_lo"], lw["rb"])
    y = _hier_moe_residual(x1e, cls, lw["g_ffn"], lw["w_gu"], lw["w_dn"])
    k_new = dk.reshape(bsz, t_len, D_HEADS, 2 * D_DH)
    v_new = dv.reshape(bsz, t_len, D_HEADS, D_DV)
    return y.reshape(bsz, t_len, d), k_new, v_new, s_fin


def _layer_weights(l, g_mix, w_in, g_q, g_k, lambda_q1, lambda_k1, lambda_q2, lambda_k2, g_ret,
                   w_ret_o, g_sub, w_diff_o, w_out, g_ffn, w_group, b_group, w_expert, b_expert,
                   w_gate, w_up, w_down):
    d = w_in.shape[1]
    n_r = N_GROUPS + N_EXPERTS
    rw = jnp.concatenate([w_group[l], w_expert[l]], axis=1).astype(F32).T
    rw = jnp.zeros((CLS_PAD, d), F32).at[:n_r].set(rw)
    rw_hi = rw.astype(BF16)
    rw_lo = (rw - rw_hi.astype(F32)).astype(BF16)
    rb = jnp.concatenate([b_group[l], b_expert[l]]).astype(F32)
    rb = jnp.zeros((CLS_PAD,), F32).at[:n_r].set(rb)
    rb = jnp.broadcast_to(rb[:, None], (CLS_PAD, V7X_LANES))
    return dict(
        g_mix=g_mix[l], w_in=w_in[l].astype(BF16), g_q=g_q[l], g_k=g_k[l],
        lam_p=jnp.stack([lambda_q1[l], lambda_k1[l], lambda_q2[l], lambda_k2[l]]).astype(F32),
        g_ret=g_ret[l], w_ret_o=w_ret_o[l].astype(BF16), g_sub=g_sub[l],
        w_diff_o=w_diff_o[l].astype(BF16), w_out=w_out[l].astype(BF16), g_ffn=g_ffn[l],
        rw_hi=rw_hi, rw_lo=rw_lo, rb=rb,
        w_gu=jnp.concatenate([w_gate[l], w_up[l]], axis=-1).astype(BF16),
        w_dn=w_down[l].astype(BF16))


def kernel(x_prompt, x_sample, cache_k, cache_v, state_ret, g_mix, w_in, g_q, g_k, lambda_q1, lambda_k1, lambda_q2, lambda_k2, g_ret, w_ret_o, g_sub, w_diff_o, w_out, g_ffn, w_group, b_group, w_expert, b_expert, w_gate, w_up, w_down):
    depth = w_in.shape[0]
    bp, tp, _ = x_prompt.shape
    bs, ts, _ = x_sample.shape
    past = cache_k.shape[2]
    log_gamma = jnp.log1p(-jnp.exp2(-5.0 - jnp.arange(R_HEADS, dtype=F32)))
    pos_p = jnp.arange(tp, dtype=jnp.int32)
    pos_s = past + jnp.arange(ts, dtype=jnp.int32)
    zero_state = jnp.zeros((bp, R_HEADS, R_DK, R_DV), F32)
    xp, xs = x_prompt, x_sample
    outs = [[] for _ in range(6)]
    for l in range(depth):
        lam_init = 0.8 - 0.6 * math.exp(-0.3 * l)
        lw = _layer_weights(l, g_mix, w_in, g_q, g_k, lambda_q1, lambda_k1, lambda_q2, lambda_k2,
                            g_ret, w_ret_o, g_sub, w_diff_o, w_out, g_ffn, w_group, b_group,
                            w_expert, b_expert, w_gate, w_up, w_down)
        xp, kp, vp, sp = _token_group(xp, pos_p, lw, lam_init, log_gamma, zero_state, None)
        cache = (cache_k[l].reshape(bs, past, D_HEADS * 2 * D_DH), cache_v[l].reshape(bs, past, D_HEADS * D_DV))
        xs, ks, vs, ss = _token_group(xs, pos_s, lw, lam_init, log_gamma,
                                      state_ret[l].astype(F32), cache)
        for lst, val in zip(outs, (kp, vp, sp, ks, vs, ss)):
            lst.append(val)
    return (xp, xs) + tuple(jnp.stack(o) for o in outs)
```

```python
import functools
import math

import jax
import jax.numpy as jnp
from jax import lax
from jax.experimental import pallas as pl
from jax.experimental.pallas import tpu as pltpu

F32 = jnp.float32
BF16 = jnp.bfloat16
I32 = jnp.int32

CHUNK = 64
EPS = 1e-6
R_HEADS = 4
R_DK = 128
R_DV = 256
ROPE_BASE = 10000.0
D_HEADS = 4
D_DH = 128
D_DV = 256
N_GROUPS = 4
EXP_PER_GROUP = 4
N_EXPERTS = N_GROUPS * EXP_PER_GROUP
N_PAIRS = EXP_PER_GROUP * (EXP_PER_GROUP - 1) // 2
N_CLASSES = N_GROUPS * N_PAIRS
D_FF = 512
SEG = 1024
N_SEG = 8

V7X_LANES = 128
V7X_SUBLANES = 8
V7X_VMEM_LIMIT_BYTES = 56 * 1024 * 1024

ROW_TILE = 512
RET_SUPER = 512
ATT_TILE = 512
MOE_TILE = 256
ROW_DMA_UNROLL = 8
CLS_PAD = 32
NEG_BIG = -1e30
LOG2E = math.log2(math.e)
PAY = V7X_LANES


def _row_tile(n):
    return min(ROW_TILE, n)


def _cparams(sem, vmem_mb=None):
    kw = dict(dimension_semantics=sem)
    if vmem_mb is not None:
        kw["vmem_limit_bytes"] = min(vmem_mb * 1024 * 1024, V7X_VMEM_LIMIT_BYTES)
    return pltpu.CompilerParams(**kw)


def _resident(shape):
    nd = len(shape)
    return pl.BlockSpec(shape, lambda *_: (0,) * nd, pipeline_mode=pl.Buffered(1))


def _rms(x, eps=EPS):
    return x * lax.rsqrt(jnp.mean(x * x, axis=-1, keepdims=True) + eps)


def _sigmoid(x):
    return 1.0 / (1.0 + jnp.exp(-x))


def _dot(a, b):
    return jnp.dot(a, b, preferred_element_type=F32)


def _dot_nt(a, b):
    return lax.dot_general(a, b, (((1,), (1,)), ((), ())), preferred_element_type=F32)


def _dot_tn(a, b):
    return lax.dot_general(a, b, (((0,), (0,)), ((), ())), preferred_element_type=F32)


def _inproj_kernel(x_ref, g_ref, w_ref, cos_ref, sin_ref, gq_ref, gk_ref,
                   rqk_ref, rv_ref, rgs_ref, dq_ref, dk_ref, dv_ref, sgr_ref, sgd_ref):
    x = x_ref[...]
    hb = (_rms(x) * g_ref[...]).astype(BF16)

    def seg(s):
        return _dot(hb, w_ref[:, s * SEG:(s + 1) * SEG])

    cos = cos_ref[...]
    sin = sin_ref[...]
    z = seg(0)
    for j in range(2 * R_HEADS):
        v = z[:, j * R_DK:(j + 1) * R_DK]
        r = v * cos + pltpu.roll(v, R_DK // 2, 1) * sin
        if j >= R_HEADS:
            r = r * (R_DK ** -0.5)
        rqk_ref[:, j * R_DK:(j + 1) * R_DK] = r.astype(BF16)

    rv_ref[...] = seg(1).astype(BF16)
    z = seg(2)
    rgs_ref[...] = (z * _sigmoid(z)).astype(BF16)

    z = seg(3)
    gq = gq_ref[...] * (D_DH ** -0.5 * LOG2E)
    for j in range(2 * D_HEADS):
        v = z[:, j * D_DH:(j + 1) * D_DH]
        dq_ref[:, j * D_DH:(j + 1) * D_DH] = (_rms(v) * gq).astype(BF16)
    z = seg(4)
    gk = gk_ref[...]
    for j in range(2 * D_HEADS):
        v = z[:, j * D_DH:(j + 1) * D_DH]
        dk_ref[:, j * D_DH:(j + 1) * D_DH] = _rms(v) * gk
    dv_ref[...] = seg(5)
    sgr_ref[...] = _sigmoid(seg(6)).astype(BF16)
    sgd_ref[...] = _sigmoid(seg(7)).astype(BF16)


def _in_proj(x2d, g_mix, w_in_bf, cos_tab, sin_tab, g_q, g_k):
    n, d = x2d.shape
    tm = _row_tile(n)
    nt = n // tm
    ntab = cos_tab.shape[0] // tm
    row = lambda w: pl.BlockSpec((tm, w), lambda i: (i, 0))
    tab = pl.BlockSpec((tm, R_DK), lambda i: (i % ntab, 0))
    outs = [jax.ShapeDtypeStruct((n, SEG), dt)
            for dt in (BF16, BF16, BF16, BF16, F32, F32, BF16, BF16)]
    return pl.pallas_call(
        _inproj_kernel,
        out_shape=outs,
        grid=(nt,),
        in_specs=[row(d), _resident((1, d)), _resident(w_in_bf.shape), tab, tab,
                  _resident((1, D_DH)), _resident((1, D_DH))],
        out_specs=[row(SEG)] * 8,
        compiler_params=_cparams(("arbitrary",), 56),
        name="in_proj",
    )(x2d, g_mix.reshape(1, d), w_in_bf, cos_tab, sin_tab,
      g_q.reshape(1, D_DH), g_k.reshape(1, D_DH))


def _ret_kernel(lg_ref, q_ref, k_ref, v_ref, rgs_ref, s0_ref, g_ref,
                o_ref, sfin_ref, s_scr, d_scr, *, c_len):
    h = pl.program_id(1)
    c = pl.program_id(2)
    lg = lg_ref[h]

    @pl.when(c == 0)
    def _():
        s_scr[...] = s0_ref[0, 0]
        t = lax.broadcasted_iota(I32, (c_len, c_len), 0)
        s = lax.broadcasted_iota(I32, (c_len, c_len), 1)
        dist = jnp.abs(t - s).astype(F32)
        vis = (s // CHUNK) <= (t // CHUNK)
        d_scr[...] = jnp.where(vis, jnp.exp(lg * dist), 0.0)

    q = q_ref[0]
    k = k_ref[0]
    v = v_ref[0]
    pos = lax.broadcasted_iota(I32, (c_len, 1), 0).astype(F32)
    state = s_scr[...]
    s = _dot_nt(q, k) * d_scr[...]
    o = _dot(s.astype(BF16), v)
    o = o + jnp.exp(lg * (pos + 1.0)) * _dot(q, state.astype(BF16))
    kw = (k.astype(F32) * jnp.exp(lg * (c_len - 1.0 - pos))).astype(BF16)
    decay = jnp.exp(lg * jnp.full((1, R_DV), float(c_len), F32))
    new_state = state * decay + _dot_tn(kw, v)
    s_scr[...] = new_state

    o_ref[0] = (_rms(o) * g_ref[...] * rgs_ref[0].astype(F32)).astype(BF16)

    @pl.when(c == pl.num_programs(2) - 1)
    def _():
        sfin_ref[0, 0] = new_state


def _retention(rqk, rv, rgs, state0, g_ret, log_gamma, bsz, t_len):
    c_len = min(RET_SUPER, t_len)
    nc = t_len // c_len
    rqk3 = rqk.reshape(bsz, t_len, SEG)
    rv3 = rv.reshape(bsz, t_len, SEG)
    rgs3 = rgs.reshape(bsz, t_len, SEG)
    qspec = pl.BlockSpec((1, c_len, R_DK), lambda b, h, c, lg: (b, c, h))
    kspec = pl.BlockSpec((1, c_len, R_DK), lambda b, h, c, lg: (b, c, R_HEADS + h))
    vspec = pl.BlockSpec((1, c_len, R_DV), lambda b, h, c, lg: (b, c, h))
    sspec = pl.BlockSpec((1, 1, R_DK, R_DV), lambda b, h, c, lg: (b, h, 0, 0))
    gspec = pl.BlockSpec((1, R_DV), lambda b, h, c, lg: (0, 0))
    o, sfin = pl.pallas_call(
        functools.partial(_ret_kernel, c_len=c_len),
        out_shape=[jax.ShapeDtypeStruct((bsz, t_len, SEG), BF16),
                   jax.ShapeDtypeStruct((bsz, R_HEADS, R_DK, R_DV), F32)],
        grid_spec=pltpu.PrefetchScalarGridSpec(
            num_scalar_prefetch=1,
            grid=(bsz, R_HEADS, nc),
            in_specs=[qspec, kspec, vspec, vspec, sspec, gspec],
            out_specs=[vspec, sspec],
            scratch_shapes=[pltpu.VMEM((R_DK, R_DV), F32), pltpu.VMEM((c_len, c_len), F32)],
        ),
        compiler_params=_cparams(("arbitrary",) * 3),
        name="retention",
    )(log_gamma, rqk3, rqk3, rv3, rgs3, state0, g_ret.reshape(1, R_DV))
    return o.reshape(bsz * t_len, SEG), sfin


def _lambda_full(lp_ref, lam_init):
    lp = lp_ref[...]
    a = jnp.sum(lp[0:1] * lp[1:2], axis=-1, keepdims=True)
    b = jnp.sum(lp[2:3] * lp[3:4], axis=-1, keepdims=True)
    return jnp.exp(a) - jnp.exp(b) + lam_init


def _attn_finish(o1, l1, o2, l2, lam, g, lam_init):
    o = o1 / l1 - lam * (o2 / l2)
    return (_rms(o) * g * (1.0 - lam_init)).astype(BF16)


def _attn_prompt_kernel(q_ref, k_ref, v_ref, lp_ref, g_ref, o_ref,
                        kb, vt, m1, l1, a1, m2, l2, a2, *, tile, lam_init):
    i = pl.program_id(2)
    nblk = kb.shape[0]

    @pl.when(i == 0)
    def _():
        for c in range(nblk):
            kb[c] = k_ref[0, c * tile:(c + 1) * tile, :].astype(BF16)
            vt[c] = v_ref[0, c * tile:(c + 1) * tile, :].T.astype(BF16)

    q = q_ref[0]
    q1 = q[:, :D_DH]
    q2 = q[:, D_DH:]
    for m, l, a in ((m1, l1, a1), (m2, l2, a2)):
        m[...] = jnp.full(m.shape, NEG_BIG, F32)
        l[...] = jnp.zeros(l.shape, F32)
        a[...] = jnp.zeros(a.shape, F32)

    def update(s, vts, m, l, a):
        m_old = m[...]
        m_new = jnp.maximum(m_old, jnp.max(s, axis=0, keepdims=True))
        alpha = jnp.exp2(m_old - m_new)
        p = jnp.exp2(s - m_new)
        l[...] = alpha * l[...] + jnp.sum(p, axis=0, keepdims=True)
        a[...] = alpha * a[...] + _dot(vts, p.astype(BF16))
        m[...] = m_new

    def block(j, mask):
        ks = kb[j]
        vts = vt[j]
        s1 = _dot_nt(ks[:, :D_DH], q1)
        s2 = _dot_nt(ks[:, D_DH:], q2)
        if mask is not None:
            s1 = jnp.where(mask, s1, NEG_BIG)
            s2 = jnp.where(mask, s2, NEG_BIG)
        update(s1, vts, m1, l1, a1)
        update(s2, vts, m2, l2, a2)

    def body(jj, carry):
        block(2 * jj, None)
        block(2 * jj + 1, None)
        return carry

    lax.fori_loop(0, i // 2, body, 0)

    @pl.when(i % 2 == 1)
    def _():
        block(i - 1, None)

    kk = lax.broadcasted_iota(I32, (tile, tile), 0)
    qq = lax.broadcasted_iota(I32, (tile, tile), 1)
    block(i, (kk // CHUNK) <= (qq // CHUNK))

    lam = _lambda_full(lp_ref, lam_init)
    ot = a1[...] * (1.0 / l1[...]) - lam * (a2[...] * (1.0 / l2[...]))
    o_ref[0] = (_rms(ot.T) * g_ref[...] * (1.0 - lam_init)).astype(BF16)


def _attn_prompt(dq, dk, dv, lam_p, g_sub, lam_init, bsz, t_len):
    tile = min(ATT_TILE, t_len)
    nq = t_len // tile
    w = 2 * D_DH
    q3 = dq.reshape(bsz, t_len, SEG)
    k3 = dk.reshape(bsz, t_len, SEG)
    v3 = dv.reshape(bsz, t_len, SEG)
    qspec = pl.BlockSpec((1, tile, w), lambda b, h, i: (b, i, h))
    kvspec = pl.BlockSpec((1, t_len, w), lambda b, h, i: (b, 0, h))
    col = lambda: pltpu.VMEM((1, tile), F32)
    acc = lambda: pltpu.VMEM((D_DV, tile), F32)
    out = pl.pallas_call(
        functools.partial(_attn_prompt_kernel, tile=tile, lam_init=lam_init),
        out_shape=jax.ShapeDtypeStruct((bsz, t_len, SEG), BF16),
        grid=(bsz, D_HEADS, nq),
        in_specs=[qspec, kvspec, kvspec,
                  pl.BlockSpec((4, D_DH), lambda b, h, i: (0, 0)),
                  pl.BlockSpec((1, D_DV), lambda b, h, i: (0, 0))],
        out_specs=qspec,
        scratch_shapes=[pltpu.VMEM((nq, tile, w), BF16), pltpu.VMEM((nq, D_DV, tile), BF16),
                        col(), col(), acc(), col(), col(), acc()],
        compiler_params=_cparams(("arbitrary",) * 3, 48),
        name="attn_prompt",
    )(q3, k3, v3, lam_p, g_sub.reshape(1, D_DV))
    return out.reshape(bsz * t_len, SEG)


def _attn_sample_kernel(q_ref, kc_ref, vc_ref, kn_ref, vn_ref, lp_ref, g_ref, o_ref, *, lam_init):
    q = q_ref[0]
    kc = kc_ref[0].astype(BF16)
    vc = vc_ref[0].astype(BF16)
    kn = kn_ref[0].astype(BF16)
    vn = vn_ref[0].astype(BF16)

    def one(qm, lo):
        sc = _dot_nt(qm, kc[:, lo:lo + D_DH])
        sn = _dot_nt(qm, kn[:, lo:lo + D_DH])
        m = jnp.maximum(jnp.max(sc, axis=-1, keepdims=True), jnp.max(sn, axis=-1, keepdims=True))
        pc = jnp.exp2(sc - m)
        pn = jnp.exp2(sn - m)
        l = jnp.sum(pc, axis=-1, keepdims=True) + jnp.sum(pn, axis=-1, keepdims=True)
        return _dot(pc.astype(BF16), vc) + _dot(pn.astype(BF16), vn), l

    o1, l1 = one(q[:, :D_DH], 0)
    o2, l2 = one(q[:, D_DH:], D_DH)
    lam = _lambda_full(lp_ref, lam_init)
    o_ref[0] = _attn_finish(o1, l1, o2, l2, lam, g_ref[...], lam_init)


def _attn_sample(dq, dk, dv, cache_k, cache_v, lam_p, g_sub, lam_init, bsz, t_len):
    past = cache_k.shape[1]
    w = 2 * D_DH
    q3 = dq.reshape(bsz, t_len, SEG)
    kn3 = dk.reshape(bsz, t_len, SEG)
    vn3 = dv.reshape(bsz, t_len, SEG)
    kc3 = cache_k.reshape(bsz, past, SEG)
    vc3 = cache_v.reshape(bsz, past, SEG)
    nspec = pl.BlockSpec((1, t_len, w), lambda b, h: (b, 0, h))
    cspec = pl.BlockSpec((1, past, w), lambda b, h: (b, 0, h))
    out = pl.pallas_call(
        functools.partial(_attn_sample_kernel, lam_init=lam_init),
        out_shape=jax.ShapeDtypeStruct((bsz, t_len, SEG), BF16),
        grid=(bsz, D_HEADS),
        in_specs=[nspec, cspec, cspec, nspec, nspec,
                  pl.BlockSpec((4, D_DH), lambda b, h: (0, 0)),
                  pl.BlockSpec((1, D_DV), lambda b, h: (0, 0))],
        out_specs=nspec,
        compiler_params=_cparams(("arbitrary",) * 2, 48),
        name="attn_sample",
    )(q3, kc3, vc3, kn3, vn3, lam_p, g_sub.reshape(1, D_DV))
    return out.reshape(bsz * t_len, SEG)


def _split_hi_lo(x):
    hi = x.astype(BF16)
    lo = (x - hi.astype(F32)).astype(BF16)
    return hi, lo


def _merge_kernel(ogr_ref, od_ref, sgr_ref, sgd_ref, x_ref, wr_ref, wd_ref, wo_ref,
                  gf_ref, rw_hi_ref, rw_lo_ref, rb_ref, x1e_ref, cls_ref):
    tm = x_ref.shape[0]
    y_r = _dot(ogr_ref[...], wr_ref[...])
    y_d = _dot(od_ref[...], wd_ref[...])
    m = sgr_ref[...].astype(F32) * y_r + sgd_ref[...].astype(F32) * y_d
    x1 = x_ref[...] + _dot(m.astype(BF16), wo_ref[...])

    hn = _rms(x1) * gf_ref[...]
    h_hi, h_lo = _split_hi_lo(hn)
    w_hi = rw_hi_ref[...]
    lt = _dot_nt(w_hi, h_hi) + _dot_nt(rw_lo_ref[...], h_hi) + _dot_nt(w_hi, h_lo)
    lt = lt + rb_ref[...][:, 0:1]
    g = [lt[i:i + 1, :] for i in range(N_GROUPS)]
    e = [lt[N_GROUPS + i:N_GROUPS + i + 1, :] for i in range(N_EXPERTS)]

    gmax = functools.reduce(jnp.maximum, g)
    gid = jnp.full(g[0].shape, N_GROUPS - 1, I32)
    for i in range(N_GROUPS - 2, -1, -1):
        gid = jnp.where(g[i] == gmax, i, gid)
    g_w = 1.0 / functools.reduce(lambda a, b: a + b, [jnp.exp(v - gmax) for v in g])

    es = []
    for j in range(EXP_PER_GROUP):
        v = e[(N_GROUPS - 1) * EXP_PER_GROUP + j]
        for i in range(N_GROUPS - 2, -1, -1):
            v = jnp.where(gid == i, e[i * EXP_PER_GROUP + j], v)
        es.append(v)

    def first_argmax(vals):
        mx = functools.reduce(jnp.maximum, vals)
        idx = jnp.full(mx.shape, len(vals) - 1, I32)
        for i in range(len(vals) - 2, -1, -1):
            idx = jnp.where(vals[i] == mx, i, idx)
        return mx, idx

    l1, i1 = first_argmax(es)
    rest = [jnp.where(i1 == j, -jnp.inf, es[j]) for j in range(EXP_PER_GROUP)]
    l2, i2 = first_argmax(rest)
    t = jnp.exp(l2 - l1)
    c1 = g_w / (1.0 + t)
    c2 = g_w * t / (1.0 + t)
    lo = jnp.minimum(i1, i2)
    hi = jnp.maximum(i1, i2)
    base = jnp.where(lo == 0, 0, jnp.where(lo == 1, EXP_PER_GROUP - 1, 2 * EXP_PER_GROUP - 3))
    cls = gid * N_PAIRS + base + hi - lo - 1
    wa = jnp.where(i1 < i2, c1, c2)
    wb = jnp.where(i1 < i2, c2, c1)

    cls_ref[0] = cls
    rid = lax.broadcasted_iota(I32, (PAY, tm), 0)
    pay = jnp.where(rid == 0, wa, jnp.where(rid == 1, wb, 0.0))
    d = x_ref.shape[1]
    x1e_ref[:, :d] = x1
    x1e_ref[:, d:] = pay.T


def _merge(ogr, od, sgr, sgd, x2d, w_ret_o, w_diff_o, w_out, g_ffn, rw_hi, rw_lo, rb):
    n, d = x2d.shape
    tm = _row_tile(n)
    nt = n // tm
    row = lambda w: pl.BlockSpec((tm, w), lambda i: (i, 0))
    return pl.pallas_call(
        _merge_kernel,
        out_shape=[jax.ShapeDtypeStruct((n, d + PAY), F32),
                   jax.ShapeDtypeStruct((nt, 1, tm), I32)],
        grid=(nt,),
        in_specs=[row(SEG), row(SEG), row(SEG), row(SEG), row(d),
                  _resident(w_ret_o.shape), _resident(w_diff_o.shape), _resident(w_out.shape),
                  _resident((1, d)), _resident(rw_hi.shape), _resident(rw_lo.shape),
                  _resident(rb.shape)],
        out_specs=[row(d + PAY), pl.BlockSpec((1, 1, tm), lambda i: (i, 0, 0))],
        compiler_params=_cparams(("arbitrary",), 48),
        name="merge_route",
    )(ogr, od, sgr, sgd, x2d, w_ret_o, w_diff_o, w_out, g_ffn.reshape(1, d), rw_hi, rw_lo, rb)


def _prefix_excl(x):
    rid = lax.broadcasted_iota(I32, x.shape, 0)
    inc = x
    s = 1
    while s < x.shape[0]:
        inc = inc + jnp.where(rid >= s, pltpu.roll(inc, s, 0), 0.0)
        s *= 2
    return inc - x


def _sort_kernel(cls_ref, pos_ref, cnt_ref, off_ref, tiles_ref, cnt_scr, run_scr, off_scr,
                 *, moe_tile, n_tiles_pad):
    ph = pl.program_id(0)
    t = pl.program_id(1)
    tm = cls_ref.shape[2]
    cls = cls_ref[0]
    cid = lax.broadcasted_iota(I32, (CLS_PAD, tm), 0)
    onehot = (cid == cls)

    @pl.when(jnp.logical_and(ph == 0, t == 0))
    def _():
        cnt_scr[...] = jnp.zeros(cnt_scr.shape, F32)

    @pl.when(ph == 0)
    def _():
        cnt_scr[...] += jnp.sum(onehot.astype(F32), axis=-1, keepdims=True)

    @pl.when(jnp.logical_and(ph == 1, t == 0))
    def _():
        cnt = cnt_scr[...]
        padded = jnp.ceil(cnt / moe_tile) * moe_tile
        off = _prefix_excl(padded)
        off_scr[...] = off
        run_scr[...] = jnp.zeros(run_scr.shape, F32)
        cnt_ref[...] = cnt.astype(I32)
        off_ref[...] = off.astype(I32)
        end = (off + padded)[:, 0:1]
        total = jnp.max(end, axis=0, keepdims=True)
        n_used = total / moe_tile
        p = lax.broadcasted_iota(I32, (1, n_tiles_pad), 1).astype(F32)
        start = jnp.minimum(p, n_used - 1.0) * moe_tile
        cid2 = lax.broadcasted_iota(I32, (CLS_PAD, n_tiles_pad), 0)
        before = jnp.logical_and(end <= start, cid2 < N_CLASSES)
        tcls = jnp.sum(before.astype(F32), axis=0, keepdims=True).astype(I32)
        grp = tcls // N_PAIRS
        pr = tcls - grp * N_PAIRS
        lo = (pr >= EXP_PER_GROUP - 1).astype(I32) + (pr >= 2 * EXP_PER_GROUP - 3).astype(I32)
        base = jnp.where(lo == 0, 0, jnp.where(lo == 1, EXP_PER_GROUP - 1, 2 * EXP_PER_GROUP - 3))
        hi = pr - base + lo + 1
        rid = lax.broadcasted_iota(I32, (V7X_SUBLANES, n_tiles_pad), 0)
        ea = grp * EXP_PER_GROUP + lo
        eb = grp * EXP_PER_GROUP + hi
        nu = jnp.broadcast_to(n_used.astype(I32), (1, n_tiles_pad))
        tiles_ref[...] = jnp.where(rid == 0, ea, jnp.where(rid == 1, eb, jnp.where(rid == 2, nu, 0)))

    @pl.when(ph == 1)
    def _():
        r = lax.broadcasted_iota(I32, (tm, tm), 0)
        c = lax.broadcasted_iota(I32, (tm, tm), 1)
        upper = jnp.where(r <= c, 1.0, 0.0).astype(BF16)
        oh = jnp.where(onehot, 1.0, 0.0)
        incl = _dot(oh.astype(BF16), upper)
        slot = off_scr[...][:, 0:1] + run_scr[...][:, 0:1] + incl - 1.0
        pos_ref[0] = jnp.sum(oh * slot, axis=0, keepdims=True).astype(I32)
        run_scr[...] += jnp.sum(oh, axis=-1, keepdims=True)


def _sort(cls, moe_tile, n_tiles_pad):
    nt, _, tm = cls.shape
    blk = pl.BlockSpec((1, 1, tm), lambda ph, t: (t, 0, 0))
    oblk = pl.BlockSpec((1, 1, tm), lambda ph, t: (t * ph, 0, 0))
    whole = lambda shape: pl.BlockSpec(shape, lambda ph, t: (0, 0))
    return pl.pallas_call(
        functools.partial(_sort_kernel, moe_tile=moe_tile, n_tiles_pad=n_tiles_pad),
        out_shape=[jax.ShapeDtypeStruct((nt, 1, tm), I32),
                   jax.ShapeDtypeStruct((CLS_PAD, V7X_LANES), I32),
                   jax.ShapeDtypeStruct((CLS_PAD, V7X_LANES), I32),
                   jax.ShapeDtypeStruct((V7X_SUBLANES, n_tiles_pad), I32)],
        grid=(2, nt),
        in_specs=[blk],
        out_specs=[oblk, whole((CLS_PAD, V7X_LANES)), whole((CLS_PAD, V7X_LANES)),
                   whole((V7X_SUBLANES, n_tiles_pad))],
        scratch_shapes=[pltpu.VMEM((CLS_PAD, V7X_LANES), F32)] * 3,
        compiler_params=_cparams(("arbitrary",) * 2),
        name="class_sort",
    )(cls)


def _permute_kernel(pos_ref, cnt_ref, off_ref, src_ref, dst_ref, zblk, sem, zsem, *, moe_tile):
    t = pl.program_id(0)
    tm = pos_ref.shape[2]

    def row_copy(r):
        return pltpu.make_async_copy(src_ref.at[pl.ds(r, 1)],
                                     dst_ref.at[pl.ds(pos_ref[0, 0, r], 1)], sem)

    def start(r, carry):
        row_copy(r).start()
        return carry

    def wait(r, carry):
        row_copy(r).wait()
        return carry

    lax.fori_loop(0, tm, start, 0, unroll=ROW_DMA_UNROLL)

    @pl.when(t == 0)
    def _():
        zblk[...] = jnp.zeros(zblk.shape, zblk.dtype)
        used = 0
        for c in range(N_CLASSES):
            cnt = cnt_ref[c, 0]
            off = off_ref[c, 0]
            padded = ((cnt + moe_tile - 1) // moe_tile) * moe_tile
            used = off + padded

            def pad_copy(r, off=off):
                return pltpu.make_async_copy(zblk.at[pl.ds(0, 1)], dst_ref.at[pl.ds(off + r, 1)], zsem)

            def pad_start(r, carry, copy=pad_copy):
                copy(r).start()
                return carry

            def pad_wait(r, carry, copy=pad_copy):
                copy(r).wait()
                return carry

            lax.fori_loop(cnt, padded, pad_start, 0)
            lax.fori_loop(cnt, padded, pad_wait, 0)

        def tile_copy(p):
            return pltpu.make_async_copy(zblk, dst_ref.at[pl.ds(p * moe_tile, moe_tile)], zsem)

        def tile_start(p, carry):
            tile_copy(p).start()
            return carry

        def tile_wait(p, carry):
            tile_copy(p).wait()
            return carry

        first, last = used // moe_tile, dst_ref.shape[0] // moe_tile
        lax.fori_loop(first, last, tile_start, 0)
        lax.fori_loop(first, last, tile_wait, 0)

    lax.fori_loop(0, tm, wait, 0, unroll=ROW_DMA_UNROLL)


def _permute(pos, cnt, off, src, n_rows_out, moe_tile):
    nt, _, tm = pos.shape
    width = src.shape[1]
    smem = lambda shape, imap: pl.BlockSpec(shape, imap, memory_space=pltpu.SMEM)
    return pl.pallas_call(
        functools.partial(_permute_kernel, moe_tile=moe_tile),
        out_shape=jax.ShapeDtypeStruct((n_rows_out, width), src.dtype),
        grid=(nt,),
        in_specs=[smem((1, 1, tm), lambda t: (t, 0, 0)),
                  smem(cnt.shape, lambda t: (0, 0)),
                  smem(off.shape, lambda t: (0, 0)),
                  pl.BlockSpec((tm, width), lambda t: (t, 0))],
        out_specs=pl.BlockSpec(memory_space=pl.ANY),
        scratch_shapes=[pltpu.VMEM((moe_tile, width), src.dtype),
                        pltpu.SemaphoreType.DMA, pltpu.SemaphoreType.DMA],
        compiler_params=_cparams(("arbitrary",)),
        name="permute_rows",
    )(pos, cnt, off, src)


def _unpermute_kernel(pos_ref, src_ref, dst_ref, sem):
    tm = pos_ref.shape[2]

    def row_copy(r):
        return pltpu.make_async_copy(src_ref.at[pl.ds(pos_ref[0, 0, r], 1)],
                                     dst_ref.at[pl.ds(r, 1)], sem)

    def start(r, carry):
        row_copy(r).start()
        return carry

    def wait(r, carry):
        row_copy(r).wait()
        return carry

    lax.fori_loop(0, tm, start, 0, unroll=ROW_DMA_UNROLL)
    lax.fori_loop(0, tm, wait, 0, unroll=ROW_DMA_UNROLL)


def _unpermute(pos, src, n_rows_out):
    nt, _, tm = pos.shape
    width = src.shape[1]
    return pl.pallas_call(
        _unpermute_kernel,
        out_shape=jax.ShapeDtypeStruct((n_rows_out, width), src.dtype),
        grid=(nt,),
        in_specs=[pl.BlockSpec((1, 1, tm), lambda t: (t, 0, 0), memory_space=pltpu.SMEM),
                  pl.BlockSpec(memory_space=pl.ANY)],
        out_specs=pl.BlockSpec((tm, width), lambda t: (t, 0)),
        scratch_shapes=[pltpu.SemaphoreType.DMA],
        compiler_params=_cparams(("arbitrary",)),
        name="unpermute_rows",
    )(pos, src)


def _moe_kernel(ea_ref, eb_ref, nu_ref, xs_ref, gf_ref, wgu_a, wdn_a, wgu_b, wdn_b, y_ref):
    p = pl.program_id(0)
    d = y_ref.shape[1]

    @pl.when(p < nu_ref[0])
    def _():
        x = xs_ref[:, :d]
        wa = xs_ref[:, d:d + 1]
        wb = xs_ref[:, d + 1:d + 2]
        hn = (_rms(x) * gf_ref[...]).astype(BF16)

        def expert(wgu, wdn):
            gu = _dot(hn, wgu[0])
            gate = gu[:, :D_FF]
            he = gate * _sigmoid(gate) * gu[:, D_FF:]
            return _dot(he.astype(BF16), wdn[0])

        y_ref[...] = x + wa * expert(wgu_a, wdn_a) + wb * expert(wgu_b, wdn_b)

    @pl.when(p >= nu_ref[0])
    def _():
        y_ref[...] = jnp.zeros(y_ref.shape, y_ref.dtype)


def _moe(ea, eb, nu, xs, g_ffn, w_gu, w_dn, moe_tile):
    n_rows, width = xs.shape
    d = width - PAY
    n_tiles = n_rows // moe_tile
    used = lambda p, ea, eb, nu: (jnp.minimum(p, nu[0] - 1), 0)
    return pl.pallas_call(
        _moe_kernel,
        out_shape=jax.ShapeDtypeStruct((n_rows, d), F32),
        grid_spec=pltpu.PrefetchScalarGridSpec(
            num_scalar_prefetch=3,
            grid=(n_tiles,),
            in_specs=[pl.BlockSpec((moe_tile, width), used),
                      pl.BlockSpec((1, d), lambda p, ea, eb, nu: (0, 0)),
                      pl.BlockSpec((1, d, 2 * D_FF), lambda p, ea, eb, nu: (ea[p], 0, 0)),
                      pl.BlockSpec((1, D_FF, d), lambda p, ea, eb, nu: (ea[p], 0, 0)),
                      pl.BlockSpec((1, d, 2 * D_FF), lambda p, ea, eb, nu: (eb[p], 0, 0)),
                      pl.BlockSpec((1, D_FF, d), lambda p, ea, eb, nu: (eb[p], 0, 0))],
            out_specs=pl.BlockSpec((moe_tile, d), lambda p, ea, eb, nu: (p, 0)),
        ),
        compiler_params=_cparams(("arbitrary",), 48),
        name="moe_sorted",
    )(ea, eb, nu, xs, g_ffn.reshape(1, d), w_gu, w_dn, w_gu, w_dn)


def _hier_moe_residual(x1e, cls, g_ffn, w_gu, w_dn):
    n = x1e.shape[0]
    moe_tile = min(MOE_TILE, n)
    n_tiles = n // moe_tile + N_CLASSES
    n_tiles_pad = -(-n_tiles // V7X_LANES) * V7X_LANES
    pos, cnt, off, tiles = _sort(cls, moe_tile, n_tiles_pad)
    xs = _permute(pos, cnt, off, x1e, n_tiles * moe_tile, moe_tile)
    ys = _moe(tiles[0, :n_tiles], tiles[1, :n_tiles], tiles[2, :1], xs, g_ffn, w_gu, w_dn, moe_tile)
    return _unpermute(pos, ys, n)


def _rotary_tables(pos):
    half = R_DK // 2
    inv = 1.0 / (ROPE_BASE ** jnp.linspace(0.0, 1.0, half, dtype=F32))
    ang = pos.astype(F32)[:, None] * inv[None, :]
    cos = jnp.cos(ang)
    sin = jnp.sin(ang)
    return jnp.concatenate([cos, cos], axis=-1), jnp.concatenate([-sin, sin], axis=-1)


def _token_group(x, pos, lw, lam_init, log_gamma, state0, cache):
    bsz, t_len, d = x.shape
    x2d = x.reshape(bsz * t_len, d)
    cos_tab, sin_tab = _rotary_tables(pos)
    tm = _row_tile(bsz * t_len)
    if t_len < tm:
        reps = tm // t_len
        cos_tab = jnp.tile(cos_tab, (reps, 1))
        sin_tab = jnp.tile(sin_tab, (reps, 1))
    rqk, rv, rgs, dq, dk, dv, sgr, sgd = _in_proj(
        x2d, lw["g_mix"], lw["w_in"], cos_tab, sin_tab, lw["g_q"], lw["g_k"])
    ogr, s_fin = _retention(rqk, rv, rgs, state0, lw["g_ret"], log_gamma, bsz, t_len)
    if cache is None:
        od = _attn_prompt(dq, dk, dv, lw["lam_p"], lw["g_sub"], lam_init, bsz, t_len)
    else:
        od = _attn_sample(dq, dk, dv, cache[0], cache[1], lw["lam_p"], lw["g_sub"], lam_init,
                          bsz, t_len)
    x1e, cls = _merge(ogr, od, sgr, sgd, x2d, lw["w_ret_o"], lw["w_diff_o"], lw["w_out"],
                      lw["g_ffn"], lw["rw_hi"], lw["rw_lo"], lw["rb"])
    y = _hier_moe_residual(x1e, cls, lw["g_ffn"], lw["w_gu"], lw["w_dn"])
    k_new = dk.reshape(bsz, t_len, D_HEADS, 2 * D_DH)
    v_new = dv.reshape(bsz, t_len, D_HEADS, D_DV)
    return y.reshape(bsz, t_len, d), k_new, v_new, s_fin


def _layer_weights(l, g_mix, w_in, g_q, g_k, lambda_q1, lambda_k1, lambda_q2, lambda_k2, g_ret,
                   w_ret_o, g_sub, w_diff_o, w_out, g_ffn, w_group, b_group, w_expert, b_expert,
                   w_gate, w_up, w_down):
    d = w_in.shape[1]
    n_r = N_GROUPS + N_EXPERTS
    rw = jnp.concatenate([w_group[l], w_expert[l]], axis=1).astype(F32).T
    rw = jnp.zeros((CLS_PAD, d), F32).at[:n_r].set(rw)
    rw_hi = rw.astype(BF16)
    rw_lo = (rw - rw_hi.astype(F32)).astype(BF16)
    rb = jnp.concatenate([b_group[l], b_expert[l]]).astype(F32)
    rb = jnp.zeros((CLS_PAD,), F32).at[:n_r].set(rb)
    rb = jnp.broadcast_to(rb[:, None], (CLS_PAD, V7X_LANES))
    return dict(
        g_mix=g_mix[l], w_in=w_in[l].astype(BF16), g_q=g_q[l], g_k=g_k[l],
        lam_p=jnp.stack([lambda_q1[l], lambda_k1[l], lambda_q2[l], lambda_k2[l]]).astype(F32),
        g_ret=g_ret[l], w_ret_o=w_ret_o[l].astype(BF16), g_sub=g_sub[l],
        w_diff_o=w_diff_o[l].astype(BF16), w_out=w_out[l].astype(BF16), g_ffn=g_ffn[l],
        rw_hi=rw_hi, rw_lo=rw_lo, rb=rb,
        w_gu=jnp.concatenate([w_gate[l], w_up[l]], axis=-1).astype(BF16),
        w_dn=w_down[l].astype(BF16))


def kernel(x_prompt, x_sample, cache_k, cache_v, state_ret, g_mix, w_in, g_q, g_k, lambda_q1, lambda_k1, lambda_q2, lambda_k2, g_ret, w_ret_o, g_sub, w_diff_o, w_out, g_ffn, w_group, b_group, w_expert, b_expert, w_gate, w_up, w_down):
    depth = w_in.shape[0]
    bp, tp, _ = x_prompt.shape
    bs, ts, _ = x_sample.shape
    past = cache_k.shape[2]
    log_gamma = jnp.log1p(-jnp.exp2(-5.0 - jnp.arange(R_HEADS, dtype=F32)))
    pos_p = jnp.arange(tp, dtype=jnp.int32)
    pos_s = past + jnp.arange(ts, dtype=jnp.int32)
    zero_state = jnp.zeros((bp, R_HEADS, R_DK, R_DV), F32)
    xp, xs = x_prompt, x_sample
    outs = [[] for _ in range(6)]
    for l in range(depth):
        lam_init = 0.8 - 0.6 * math.exp(-0.3 * l)
        lw = _layer_weights(l, g_mix, w_in, g_q, g_k, lambda_q1, lambda_k1, lambda_q2, lambda_k2,
                            g_ret, w_ret_o, g_sub, w_diff_o, w_out, g_ffn, w_group, b_group,
                            w_expert, b_expert, w_gate, w_up, w_down)
        xp, kp, vp, sp = _token_group(xp, pos_p, lw, lam_init, log_gamma, zero_state, None)
        cache = (cache_k[l].reshape(bs, past, D_HEADS * 2 * D_DH), cache_v[l].reshape(bs, past, D_HEADS * D_DV))
        xs, ks, vs, ss = _token_group(xs, pos_s, lw, lam_init, log_gamma,
                                      state_ret[l].astype(F32), cache)
        for lst, val in zip(outs, (kp, vp, sp, ks, vs, ss)):
            lst.append(val)
    return (xp, xs) + tuple(jnp.stack(o) for o in outs)
```

```python
import functools
import math

import jax
import jax.numpy as jnp
from jax import lax
from jax.experimental import pallas as pl
from jax.experimental.pallas import tpu as pltpu

F32 = jnp.float32
BF16 = jnp.bfloat16
I32 = jnp.int32

CHUNK = 64
EPS = 1e-6
R_HEADS = 4
R_DK = 128
R_DV = 256
ROPE_BASE = 10000.0
D_HEADS = 4
D_DH = 128
D_DV = 256
N_GROUPS = 4
EXP_PER_GROUP = 4
N_EXPERTS = N_GROUPS * EXP_PER_GROUP
N_PAIRS = EXP_PER_GROUP * (EXP_PER_GROUP - 1) // 2
N_CLASSES = N_GROUPS * N_PAIRS
D_FF = 512
SEG = 1024
N_SEG = 8

V7X_LANES = 128
V7X_SUBLANES = 8
V7X_VMEM_LIMIT_BYTES = 56 * 1024 * 1024

ROW_TILE = 512
RET_SUPER = 512
ATT_TILE = 512
SAMPLE_CHUNK = 1024
MOE_TILE = 256
ROW_DMA_UNROLL = 8
CLS_PAD = 32
NEG_BIG = -1e30
LOG2E = math.log2(math.e)
PAY = V7X_LANES


def _row_tile(n):
    return min(ROW_TILE, n)


def _cparams(sem, vmem_mb=None):
    kw = dict(dimension_semantics=sem)
    if vmem_mb is not None:
        kw["vmem_limit_bytes"] = min(vmem_mb * 1024 * 1024, V7X_VMEM_LIMIT_BYTES)
    return pltpu.CompilerParams(**kw)


def _resident(shape):
    nd = len(shape)
    return pl.BlockSpec(shape, lambda *_: (0,) * nd, pipeline_mode=pl.Buffered(1))


def _rms(x, eps=EPS):
    return x * lax.rsqrt(jnp.mean(x * x, axis=-1, keepdims=True) + eps)


def _sigmoid(x):
    return 1.0 / (1.0 + jnp.exp(-x))


def _dot(a, b):
    return jnp.dot(a, b, preferred_element_type=F32)


def _dot_nt(a, b):
    return lax.dot_general(a, b, (((1,), (1,)), ((), ())), preferred_element_type=F32)


def _dot_tn(a, b):
    return lax.dot_general(a, b, (((0,), (0,)), ((), ())), preferred_element_type=F32)


def _store_cache_rows(ref, z):
    rows = z.shape[0]
    for half in range(2):
        for h in range(D_HEADS):
            lo = h * D_DV + half * V7X_LANES
            ref[pl.ds(half * D_HEADS + h, rows, stride=2 * D_HEADS), :] = z[:, lo:lo + V7X_LANES]


def _inproj_kernel(x_ref, g_ref, w_ref, cos_ref, sin_ref, gq_ref, gk_ref,
                   rqk_ref, rv_ref, rgs_ref, dq_ref, kb_ref, vt_ref, dk_ref, dv_ref, sgr_ref, sgd_ref):
    x = x_ref[...]
    hb = (_rms(x) * g_ref[...]).astype(BF16)

    def seg(s):
        return _dot(hb, w_ref[:, s * SEG:(s + 1) * SEG])

    cos = cos_ref[...]
    sin = sin_ref[...]
    z = seg(0)
    for j in range(2 * R_HEADS):
        v = z[:, j * R_DK:(j + 1) * R_DK]
        r = v * cos + pltpu.roll(v, R_DK // 2, 1) * sin
        if j >= R_HEADS:
            r = r * (R_DK ** -0.5)
        rqk_ref[:, j * R_DK:(j + 1) * R_DK] = r.astype(BF16)

    rv_ref[...] = seg(1).astype(BF16)
    z = seg(2)
    rgs_ref[...] = (z * _sigmoid(z)).astype(BF16)

    z = seg(3)
    gq = gq_ref[...] * (D_DH ** -0.5 * LOG2E)
    for j in range(2 * D_HEADS):
        v = z[:, j * D_DH:(j + 1) * D_DH]
        dq_ref[:, j * D_DH:(j + 1) * D_DH] = (_rms(v) * gq).astype(BF16)
    z = seg(4)
    gk = gk_ref[...]
    kn = jnp.concatenate([_rms(z[:, j * D_DH:(j + 1) * D_DH]) * gk for j in range(2 * D_HEADS)], axis=1)
    kb_ref[...] = kn.astype(BF16)
    _store_cache_rows(dk_ref, kn)
    z = seg(5)
    _store_cache_rows(dv_ref, z)
    vt_ref[0] = z.T.astype(BF16)
    sgr_ref[...] = _sigmoid(seg(6)).astype(BF16)
    sgd_ref[...] = _sigmoid(seg(7)).astype(BF16)


def _in_proj(x2d, g_mix, w_in_bf, cos_tab, sin_tab, g_q, g_k):
    n, d = x2d.shape
    tm = _row_tile(n)
    nt = n // tm
    ntab = cos_tab.shape[0] // tm
    row = lambda w: pl.BlockSpec((tm, w), lambda i: (i, 0))
    tab = pl.BlockSpec((tm, R_DK), lambda i: (i % ntab, 0))
    flat = jax.ShapeDtypeStruct((n, SEG), BF16)
    cache_rows = jax.ShapeDtypeStruct((n * 2 * D_HEADS, V7X_LANES), F32)
    cache_spec = pl.BlockSpec((tm * 2 * D_HEADS, V7X_LANES), lambda i: (i, 0))
    outs = [flat, flat, flat, flat, flat, jax.ShapeDtypeStruct((nt, SEG, tm), BF16),
            cache_rows, cache_rows, flat, flat]
    return pl.pallas_call(
        _inproj_kernel,
        out_shape=outs,
        grid=(nt,),
        in_specs=[row(d), _resident((1, d)), _resident(w_in_bf.shape), tab, tab,
                  _resident((1, D_DH)), _resident((1, D_DH))],
        out_specs=[row(SEG)] * 5 + [pl.BlockSpec((1, SEG, tm), lambda i: (i, 0, 0)),
                                    cache_spec, cache_spec, row(SEG), row(SEG)],
        compiler_params=_cparams(("arbitrary",), 56),
        name="in_proj",
    )(x2d, g_mix.reshape(1, d), w_in_bf, cos_tab, sin_tab,
      g_q.reshape(1, D_DH), g_k.reshape(1, D_DH))


def _ret_kernel(lg_ref, q_ref, k_ref, v_ref, rgs_ref, s0_ref, g_ref,
                o_ref, sfin_ref, s_scr, d_scr, *, c_len):
    h = pl.program_id(1)
    c = pl.program_id(2)
    lg = lg_ref[h]

    @pl.when(c == 0)
    def _():
        s_scr[...] = s0_ref[0, 0]
        t = lax.broadcasted_iota(I32, (c_len, c_len), 0)
        s = lax.broadcasted_iota(I32, (c_len, c_len), 1)
        dist = jnp.abs(t - s).astype(F32)
        vis = (s // CHUNK) <= (t // CHUNK)
        d_scr[...] = jnp.where(vis, jnp.exp(lg * dist), 0.0)

    q = q_ref[0]
    k = k_ref[0]
    v = v_ref[0]
    pos = lax.broadcasted_iota(I32, (c_len, 1), 0).astype(F32)
    state = s_scr[...]
    s = _dot_nt(q, k) * d_scr[...]
    o = _dot(s.astype(BF16), v)
    o = o + jnp.exp(lg * (pos + 1.0)) * _dot(q, state.astype(BF16))
    kw = (k.astype(F32) * jnp.exp(lg * (c_len - 1.0 - pos))).astype(BF16)
    decay = jnp.exp(lg * jnp.full((1, R_DV), float(c_len), F32))
    new_state = state * decay + _dot_tn(kw, v)
    s_scr[...] = new_state

    o_ref[0] = (_rms(o) * g_ref[...] * rgs_ref[0].astype(F32)).astype(BF16)

    @pl.when(c == pl.num_programs(2) - 1)
    def _():
        sfin_ref[0, 0] = new_state


def _retention(rqk, rv, rgs, state0, g_ret, log_gamma, bsz, t_len):
    c_len = min(RET_SUPER, t_len)
    nc = t_len // c_len
    rqk3 = rqk.reshape(bsz, t_len, SEG)
    rv3 = rv.reshape(bsz, t_len, SEG)
    rgs3 = rgs.reshape(bsz, t_len, SEG)
    qspec = pl.BlockSpec((1, c_len, R_DK), lambda b, h, c, lg: (b, c, h))
    kspec = pl.BlockSpec((1, c_len, R_DK), lambda b, h, c, lg: (b, c, R_HEADS + h))
    vspec = pl.BlockSpec((1, c_len, R_DV), lambda b, h, c, lg: (b, c, h))
    sspec = pl.BlockSpec((1, 1, R_DK, R_DV), lambda b, h, c, lg: (b, h, 0, 0))
    gspec = pl.BlockSpec((1, R_DV), lambda b, h, c, lg: (0, 0))
    o, sfin = pl.pallas_call(
        functools.partial(_ret_kernel, c_len=c_len),
        out_shape=[jax.ShapeDtypeStruct((bsz, t_len, SEG), BF16),
                   jax.ShapeDtypeStruct((bsz, R_HEADS, R_DK, R_DV), F32)],
        grid_spec=pltpu.PrefetchScalarGridSpec(
            num_scalar_prefetch=1,
            grid=(bsz, R_HEADS, nc),
            in_specs=[qspec, kspec, vspec, vspec, sspec, gspec],
            out_specs=[vspec, sspec],
            scratch_shapes=[pltpu.VMEM((R_DK, R_DV), F32), pltpu.VMEM((c_len, c_len), F32)],
        ),
        compiler_params=_cparams(("arbitrary",) * 3),
        name="retention",
    )(log_gamma, rqk3, rqk3, rv3, rgs3, state0, g_ret.reshape(1, R_DV))
    return o.reshape(bsz * t_len, SEG), sfin


def _lambda_full(lp_ref, lam_init):
    lp = lp_ref[...]
    a = jnp.sum(lp[0:1] * lp[1:2], axis=-1, keepdims=True)
    b = jnp.sum(lp[2:3] * lp[3:4], axis=-1, keepdims=True)
    return jnp.exp(a) - jnp.exp(b) + lam_init


def _attn_finish(o1, l1, o2, l2, lam, g, lam_init):
    o = o1 / l1 - lam * (o2 / l2)
    return (_rms(o) * g * (1.0 - lam_init)).astype(BF16)


def _attn_prompt_kernel(q_ref, k_ref, vt_ref, lp_ref, g_ref, o_ref,
                        m1, l1, a1, m2, l2, a2, *, tile, lam_init):
    i = pl.program_id(2)

    q = q_ref[0]
    q1 = q[:, :D_DH]
    q2 = q[:, D_DH:]
    for m, l, a in ((m1, l1, a1), (m2, l2, a2)):
        m[...] = jnp.full(m.shape, NEG_BIG, F32)
        l[...] = jnp.zeros(l.shape, F32)
        a[...] = jnp.zeros(a.shape, F32)

    def update(s, vts, m, l, a):
        m_old = m[...]
        m_new = jnp.maximum(m_old, jnp.max(s, axis=0, keepdims=True))
        alpha = jnp.exp2(m_old - m_new)
        p = jnp.exp2(s - m_new)
        l[...] = alpha * l[...] + jnp.sum(p, axis=0, keepdims=True)
        a[...] = alpha * a[...] + _dot(vts, p.astype(BF16))
        m[...] = m_new

    def block(j, mask):
        ks = k_ref[0, pl.ds(pl.multiple_of(j * tile, tile), tile), :]
        vts = vt_ref[0, j]
        s1 = _dot_nt(ks[:, :D_DH], q1)
        s2 = _dot_nt(ks[:, D_DH:], q2)
        if mask is not None:
            s1 = jnp.where(mask, s1, NEG_BIG)
            s2 = jnp.where(mask, s2, NEG_BIG)
        update(s1, vts, m1, l1, a1)
        update(s2, vts, m2, l2, a2)

    def body(jj, carry):
        block(2 * jj, None)
        block(2 * jj + 1, None)
        return carry

    lax.fori_loop(0, i // 2, body, 0)

    @pl.when(i % 2 == 1)
    def _():
        block(i - 1, None)

    kk = lax.broadcasted_iota(I32, (tile, tile), 0)
    qq = lax.broadcasted_iota(I32, (tile, tile), 1)
    block(i, (kk // CHUNK) <= (qq // CHUNK))

    lam = _lambda_full(lp_ref, lam_init)
    ot = a1[...] * (1.0 / l1[...]) - lam * (a2[...] * (1.0 / l2[...]))
    o_ref[0] = (_rms(ot.T) * g_ref[...] * (1.0 - lam_init)).astype(BF16)


def _attn_prompt(dq, kb, vt, lam_p, g_sub, lam_init, bsz, t_len):
    tile = vt.shape[2]
    assert t_len % tile == 0 and tile % CHUNK == 0
    nq = t_len // tile
    w = 2 * D_DH
    q3 = dq.reshape(bsz, t_len, SEG)
    k3 = kb.reshape(bsz, t_len, SEG)
    vt4 = vt.reshape(bsz, nq, SEG, tile)
    qspec = pl.BlockSpec((1, tile, w), lambda b, h, i: (b, i, h))
    kspec = pl.BlockSpec((1, t_len, w), lambda b, h, i: (b, 0, h))
    vspec = pl.BlockSpec((1, nq, D_DV, tile), lambda b, h, i: (b, 0, h, 0))
    col = lambda: pltpu.VMEM((1, tile), F32)
    acc = lambda: pltpu.VMEM((D_DV, tile), F32)
    out = pl.pallas_call(
        functools.partial(_attn_prompt_kernel, tile=tile, lam_init=lam_init),
        out_shape=jax.ShapeDtypeStruct((bsz, t_len, SEG), BF16),
        grid=(bsz, D_HEADS, nq),
        in_specs=[qspec, kspec, vspec,
                  pl.BlockSpec((4, D_DH), lambda b, h, i: (0, 0)),
                  pl.BlockSpec((1, D_DV), lambda b, h, i: (0, 0))],
        out_specs=qspec,
        scratch_shapes=[col(), col(), acc(), col(), col(), acc()],
        compiler_params=_cparams(("arbitrary",) * 3, 32),
        name="attn_prompt",
    )(q3, k3, vt4, lam_p, g_sub.reshape(1, D_DV))
    return out.reshape(bsz * t_len, SEG)


def _attn_sample_kernel(q_ref, kc_ref, vc_ref, kn_ref, vn_ref, lp_ref, g_ref, o_ref,
                        m_scr, l_scr, a_scr, *, lam_init, chunk, t_new):
    c = pl.program_id(1)

    @pl.when(c == 0)
    def _():
        m_scr[...] = jnp.full(m_scr.shape, NEG_BIG, F32)
        l_scr[...] = jnp.zeros(l_scr.shape, F32)
        a_scr[...] = jnp.zeros(a_scr.shape, F32)

    q = q_ref[0]

    def head_rows(ref, h, half, n):
        return ref[pl.ds(half * D_HEADS + h, n, stride=2 * D_HEADS), :].astype(BF16)

    def absorb(k_src, v_src, n):
        for h in range(D_HEADS):
            v = jnp.concatenate([head_rows(v_src, h, 0, n), head_rows(v_src, h, 1, n)], axis=1)
            for mp in range(2):
                idx = 2 * h + mp
                lo = h * 2 * D_DH + mp * D_DH
                s = _dot_nt(q[:, lo:lo + D_DH], head_rows(k_src, h, mp, n))
                m_old = m_scr[idx]
                m_new = jnp.maximum(m_old, jnp.max(s, axis=-1, keepdims=True))
                alpha = jnp.exp2(m_old - m_new)
                p = jnp.exp2(s - m_new)
                l_scr[idx] = alpha * l_scr[idx] + jnp.sum(p, axis=-1, keepdims=True)
                a_scr[idx] = alpha * a_scr[idx] + _dot(p.astype(BF16), v)
                m_scr[idx] = m_new

    absorb(kc_ref, vc_ref, chunk)

    @pl.when(c == pl.num_programs(1) - 1)
    def _():
        absorb(kn_ref, vn_ref, t_new)
        lam = _lambda_full(lp_ref, lam_init)
        for h in range(D_HEADS):
            o_ref[0, :, h * D_DV:(h + 1) * D_DV] = _attn_finish(
                a_scr[2 * h], l_scr[2 * h], a_scr[2 * h + 1], l_scr[2 * h + 1], lam, g_ref[...], lam_init)


def _attn_sample(dq, dk_rows, dv_rows, cache_k_rows, cache_v_rows, lam_p, g_sub, lam_init, bsz, t_len):
    rpf = 2 * D_HEADS
    past = cache_k_rows.shape[0] // (bsz * rpf)
    chunk = min(SAMPLE_CHUNK, past)
    nc = past // chunk
    q3 = dq.reshape(bsz, t_len, SEG)
    qspec = pl.BlockSpec((1, t_len, SEG), lambda b, c: (b, 0, 0))
    cspec = pl.BlockSpec((chunk * rpf, V7X_LANES), lambda b, c: (b * nc + c, 0))
    nspec = pl.BlockSpec((t_len * rpf, V7X_LANES), lambda b, c: (b, 0))
    n_stat = 2 * D_HEADS
    out = pl.pallas_call(
        functools.partial(_attn_sample_kernel, lam_init=lam_init, chunk=chunk, t_new=t_len),
        out_shape=jax.ShapeDtypeStruct((bsz, t_len, SEG), BF16),
        grid=(bsz, nc),
        in_specs=[qspec, cspec, cspec, nspec, nspec,
                  pl.BlockSpec((4, D_DH), lambda b, c: (0, 0)),
                  pl.BlockSpec((1, D_DV), lambda b, c: (0, 0))],
        out_specs=qspec,
        scratch_shapes=[pltpu.VMEM((n_stat, t_len, 1), F32), pltpu.VMEM((n_stat, t_len, 1), F32),
                        pltpu.VMEM((n_stat, t_len, D_DV), F32)],
        compiler_params=_cparams(("arbitrary",) * 2, 32),
        name="attn_sample",
    )(q3, cache_k_rows, cache_v_rows, dk_rows, dv_rows, lam_p, g_sub.reshape(1, D_DV))
    return out.reshape(bsz * t_len, SEG)


def _split_hi_lo(x):
    hi = x.astype(BF16)
    lo = (x - hi.astype(F32)).astype(BF16)
    return hi, lo


def _merge_kernel(ogr_ref, od_ref, sgr_ref, sgd_ref, x_ref, wr_ref, wd_ref, wo_ref,
                  gf_ref, rw_hi_ref, rw_lo_ref, rb_ref, x1e_ref, cls_ref):
    tm = x_ref.shape[0]
    y_r = _dot(ogr_ref[...], wr_ref[...])
    y_d = _dot(od_ref[...], wd_ref[...])
    m = sgr_ref[...].astype(F32) * y_r + sgd_ref[...].astype(F32) * y_d
    x1 = x_ref[...] + _dot(m.astype(BF16), wo_ref[...])

    hn = _rms(x1) * gf_ref[...]
    h_hi, h_lo = _split_hi_lo(hn)
    w_hi = rw_hi_ref[...]
    lt = _dot_nt(w_hi, h_hi) + _dot_nt(rw_lo_ref[...], h_hi) + _dot_nt(w_hi, h_lo)
    lt = lt + rb_ref[...][:, 0:1]
    g = [lt[i:i + 1, :] for i in range(N_GROUPS)]
    e = [lt[N_GROUPS + i:N_GROUPS + i + 1, :] for i in range(N_EXPERTS)]

    gmax = functools.reduce(jnp.maximum, g)
    gid = jnp.full(g[0].shape, N_GROUPS - 1, I32)
    for i in range(N_GROUPS - 2, -1, -1):
        gid = jnp.where(g[i] == gmax, i, gid)
    g_w = 1.0 / functools.reduce(lambda a, b: a + b, [jnp.exp(v - gmax) for v in g])

    es = []
    for j in range(EXP_PER_GROUP):
        v = e[(N_GROUPS - 1) * EXP_PER_GROUP + j]
        for i in range(N_GROUPS - 2, -1, -1):
            v = jnp.where(gid == i, e[i * EXP_PER_GROUP + j], v)
        es.append(v)

    def first_argmax(vals):
        mx = functools.reduce(jnp.maximum, vals)
        idx = jnp.full(mx.shape, len(vals) - 1, I32)
        for i in range(len(vals) - 2, -1, -1):
            idx = jnp.where(vals[i] == mx, i, idx)
        return mx, idx

    l1, i1 = first_argmax(es)
    rest = [jnp.where(i1 == j, -jnp.inf, es[j]) for j in range(EXP_PER_GROUP)]
    l2, i2 = first_argmax(rest)
    t = jnp.exp(l2 - l1)
    c1 = g_w / (1.0 + t)
    c2 = g_w * t / (1.0 + t)
    lo = jnp.minimum(i1, i2)
    hi = jnp.maximum(i1, i2)
    base = jnp.where(lo == 0, 0, jnp.where(lo == 1, EXP_PER_GROUP - 1, 2 * EXP_PER_GROUP - 3))
    cls = gid * N_PAIRS + base + hi - lo - 1
    wa = jnp.where(i1 < i2, c1, c2)
    wb = jnp.where(i1 < i2, c2, c1)

    cls_ref[0] = cls
    rid = lax.broadcasted_iota(I32, (PAY, tm), 0)
    pay = jnp.where(rid == 0, wa, jnp.where(rid == 1, wb, 0.0))
    d = x_ref.shape[1]
    x1e_ref[:, :d] = x1
    x1e_ref[:, d:] = pay.T


def _merge(ogr, od, sgr, sgd, x2d, w_ret_o, w_diff_o, w_out, g_ffn, rw_hi, rw_lo, rb):
    n, d = x2d.shape
    tm = _row_tile(n)
    nt = n // tm
    row = lambda w: pl.BlockSpec((tm, w), lambda i: (i, 0))
    return pl.pallas_call(
        _merge_kernel,
        out_shape=[jax.ShapeDtypeStruct((n, d + PAY), F32),
                   jax.ShapeDtypeStruct((nt, 1, tm), I32)],
        grid=(nt,),
        in_specs=[row(SEG), row(SEG), row(SEG), row(SEG), row(d),
                  _resident(w_ret_o.shape), _resident(w_diff_o.shape), _resident(w_out.shape),
                  _resident((1, d)), _resident(rw_hi.shape), _resident(rw_lo.shape),
                  _resident(rb.shape)],
        out_specs=[row(d + PAY), pl.BlockSpec((1, 1, tm), lambda i: (i, 0, 0))],
        compiler_params=_cparams(("arbitrary",), 48),
        name="merge_route",
    )(ogr, od, sgr, sgd, x2d, w_ret_o, w_diff_o, w_out, g_ffn.reshape(1, d), rw_hi, rw_lo, rb)


def _prefix_excl(x):
    rid = lax.broadcasted_iota(I32, x.shape, 0)
    inc = x
    s = 1
    while s < x.shape[0]:
        inc = inc + jnp.where(rid >= s, pltpu.roll(inc, s, 0), 0.0)
        s *= 2
    return inc - x


def _sort_kernel(cls_ref, pos_ref, cnt_ref, off_ref, tiles_ref, cnt_scr, run_scr, off_scr,
                 *, moe_tile, n_tiles_pad):
    ph = pl.program_id(0)
    t = pl.program_id(1)
    tm = cls_ref.shape[2]
    cls = cls_ref[0]
    cid = lax.broadcasted_iota(I32, (CLS_PAD, tm), 0)
    onehot = (cid == cls)

    @pl.when(jnp.logical_and(ph == 0, t == 0))
    def _():
        cnt_scr[...] = jnp.zeros(cnt_scr.shape, F32)

    @pl.when(ph == 0)
    def _():
        cnt_scr[...] += jnp.sum(onehot.astype(F32), axis=-1, keepdims=True)

    @pl.when(jnp.logical_and(ph == 1, t == 0))
    def _():
        cnt = cnt_scr[...]
        padded = jnp.ceil(cnt / moe_tile) * moe_tile
        off = _prefix_excl(padded)
        off_scr[...] = off
        run_scr[...] = jnp.zeros(run_scr.shape, F32)
        cnt_ref[...] = cnt.astype(I32)
        off_ref[...] = off.astype(I32)
        end = (off + padded)[:, 0:1]
        total = jnp.max(end, axis=0, keepdims=True)
        n_used = total / moe_tile
        p = lax.broadcasted_iota(I32, (1, n_tiles_pad), 1).astype(F32)
        start = jnp.minimum(p, n_used - 1.0) * moe_tile
        cid2 = lax.broadcasted_iota(I32, (CLS_PAD, n_tiles_pad), 0)
        before = jnp.logical_and(end <= start, cid2 < N_CLASSES)
        tcls = jnp.sum(before.astype(F32), axis=0, keepdims=True).astype(I32)
        grp = tcls // N_PAIRS
        pr = tcls - grp * N_PAIRS
        lo = (pr >= EXP_PER_GROUP - 1).astype(I32) + (pr >= 2 * EXP_PER_GROUP - 3).astype(I32)
        base = jnp.where(lo == 0, 0, jnp.where(lo == 1, EXP_PER_GROUP - 1, 2 * EXP_PER_GROUP - 3))
        hi = pr - base + lo + 1
        rid = lax.broadcasted_iota(I32, (V7X_SUBLANES, n_tiles_pad), 0)
        ea = grp * EXP_PER_GROUP + lo
        eb = grp * EXP_PER_GROUP + hi
        nu = jnp.broadcast_to(n_used.astype(I32), (1, n_tiles_pad))
        tiles_ref[...] = jnp.where(rid == 0, ea, jnp.where(rid == 1, eb, jnp.where(rid == 2, nu, 0)))

    @pl.when(ph == 1)
    def _():
        r = lax.broadcasted_iota(I32, (tm, tm), 0)
        c = lax.broadcasted_iota(I32, (tm, tm), 1)
        upper = jnp.where(r <= c, 1.0, 0.0).astype(BF16)
        oh = jnp.where(onehot, 1.0, 0.0)
        incl = _dot(oh.astype(BF16), upper)
        slot = off_scr[...][:, 0:1] + run_scr[...][:, 0:1] + incl - 1.0
        pos_ref[0] = jnp.sum(oh * slot, axis=0, keepdims=True).astype(I32)
        run_scr[...] += jnp.sum(oh, axis=-1, keepdims=True)


def _sort(cls, moe_tile, n_tiles_pad):
    nt, _, tm = cls.shape
    blk = pl.BlockSpec((1, 1, tm), lambda ph, t: (t, 0, 0))
    oblk = pl.BlockSpec((1, 1, tm), lambda ph, t: (t * ph, 0, 0))
    whole = lambda shape: pl.BlockSpec(shape, lambda ph, t: (0, 0))
    return pl.pallas_call(
        functools.partial(_sort_kernel, moe_tile=moe_tile, n_tiles_pad=n_tiles_pad),
        out_shape=[jax.ShapeDtypeStruct((nt, 1, tm), I32),
                   jax.ShapeDtypeStruct((CLS_PAD, V7X_LANES), I32),
                   jax.ShapeDtypeStruct((CLS_PAD, V7X_LANES), I32),
                   jax.ShapeDtypeStruct((V7X_SUBLANES, n_tiles_pad), I32)],
        grid=(2, nt),
        in_specs=[blk],
        out_specs=[oblk, whole((CLS_PAD, V7X_LANES)), whole((CLS_PAD, V7X_LANES)),
                   whole((V7X_SUBLANES, n_tiles_pad))],
        scratch_shapes=[pltpu.VMEM((CLS_PAD, V7X_LANES), F32)] * 3,
        compiler_params=_cparams(("arbitrary",) * 2),
        name="class_sort",
    )(cls)


def _permute_kernel(pos_ref, cnt_ref, off_ref, src_ref, dst_ref, zblk, sem, zsem, *, moe_tile):
    t = pl.program_id(0)
    tm = pos_ref.shape[2]

    def row_copy(r):
        return pltpu.make_async_copy(src_ref.at[pl.ds(r, 1)],
                                     dst_ref.at[pl.ds(pos_ref[0, 0, r], 1)], sem)

    def start(r, carry):
        row_copy(r).start()
        return carry

    def wait(r, carry):
        row_copy(r).wait()
        return carry

    lax.fori_loop(0, tm, start, 0, unroll=ROW_DMA_UNROLL)

    @pl.when(t == 0)
    def _():
        zblk[...] = jnp.zeros(zblk.shape, zblk.dtype)
        used = 0
        for c in range(N_CLASSES):
            cnt = cnt_ref[c, 0]
            off = off_ref[c, 0]
            padded = ((cnt + moe_tile - 1) // moe_tile) * moe_tile
            used = off + padded

            def pad_copy(r, off=off):
                return pltpu.make_async_copy(zblk.at[pl.ds(0, 1)], dst_ref.at[pl.ds(off + r, 1)], zsem)

            def pad_start(r, carry, copy=pad_copy):
                copy(r).start()
                return carry

            def pad_wait(r, carry, copy=pad_copy):
                copy(r).wait()
                return carry

            lax.fori_loop(cnt, padded, pad_start, 0)
            lax.fori_loop(cnt, padded, pad_wait, 0)

        def tile_copy(p):
            return pltpu.make_async_copy(zblk, dst_ref.at[pl.ds(p * moe_tile, moe_tile)], zsem)

        def tile_start(p, carry):
            tile_copy(p).start()
            return carry

        def tile_wait(p, carry):
            tile_copy(p).wait()
            return carry

        first, last = used // moe_tile, dst_ref.shape[0] // moe_tile
        lax.fori_loop(first, last, tile_start, 0)
        lax.fori_loop(first, last, tile_wait, 0)

    lax.fori_loop(0, tm, wait, 0, unroll=ROW_DMA_UNROLL)


def _permute(pos, cnt, off, src, n_rows_out, moe_tile):
    nt, _, tm = pos.shape
    width = src.shape[1]
    smem = lambda shape, imap: pl.BlockSpec(shape, imap, memory_space=pltpu.SMEM)
    return pl.pallas_call(
        functools.partial(_permute_kernel, moe_tile=moe_tile),
        out_shape=jax.ShapeDtypeStruct((n_rows_out, width), src.dtype),
        grid=(nt,),
        in_specs=[smem((1, 1, tm), lambda t: (t, 0, 0)),
                  smem(cnt.shape, lambda t: (0, 0)),
                  smem(off.shape, lambda t: (0, 0)),
                  pl.BlockSpec((tm, width), lambda t: (t, 0))],
        out_specs=pl.BlockSpec(memory_space=pl.ANY),
        scratch_shapes=[pltpu.VMEM((moe_tile, width), src.dtype),
                        pltpu.SemaphoreType.DMA, pltpu.SemaphoreType.DMA],
        compiler_params=_cparams(("arbitrary",)),
        name="permute_rows",
    )(pos, cnt, off, src)


def _unpermute_kernel(pos_ref, src_ref, dst_ref, sem):
    tm = pos_ref.shape[2]

    def row_copy(r):
        return pltpu.make_async_copy(src_ref.at[pl.ds(pos_ref[0, 0, r], 1)],
                                     dst_ref.at[pl.ds(r, 1)], sem)

    def start(r, carry):
        row_copy(r).start()
        return carry

    def wait(r, carry):
        row_copy(r).wait()
        return carry

    lax.fori_loop(0, tm, start, 0, unroll=ROW_DMA_UNROLL)
    lax.fori_loop(0, tm, wait, 0, unroll=ROW_DMA_UNROLL)


def _unpermute(pos, src, n_rows_out):
    nt, _, tm = pos.shape
    width = src.shape[1]
    return pl.pallas_call(
        _unpermute_kernel,
        out_shape=jax.ShapeDtypeStruct((n_rows_out, width), src.dtype),
        grid=(nt,),
        in_specs=[pl.BlockSpec((1, 1, tm), lambda t: (t, 0, 0), memory_space=pltpu.SMEM),
                  pl.BlockSpec(memory_space=pl.ANY)],
        out_specs=pl.BlockSpec((tm, width), lambda t: (t, 0)),
        scratch_shapes=[pltpu.SemaphoreType.DMA],
        compiler_params=_cparams(("arbitrary",)),
        name="unpermute_rows",
    )(pos, src)


def _moe_kernel(ea_ref, eb_ref, nu_ref, xs_ref, gf_ref, wgu_a, wdn_a, wgu_b, wdn_b, y_ref):
    p = pl.program_id(0)
    d = y_ref.shape[1]

    @pl.when(p < nu_ref[0])
    def _():
        x = xs_ref[:, :d]
        wa = xs_ref[:, d:d + 1]
        wb = xs_ref[:, d + 1:d + 2]
        hn = (_rms(x) * gf_ref[...]).astype(BF16)

        def expert(wgu, wdn):
            gu = _dot(hn, wgu[0])
            gate = gu[:, :D_FF]
            he = gate * _sigmoid(gate) * gu[:, D_FF:]
            return _dot(he.astype(BF16), wdn[0])

        y_ref[...] = x + wa * expert(wgu_a, wdn_a) + wb * expert(wgu_b, wdn_b)

    @pl.when(p >= nu_ref[0])
    def _():
        y_ref[...] = jnp.zeros(y_ref.shape, y_ref.dtype)


def _moe(ea, eb, nu, xs, g_ffn, w_gu, w_dn, moe_tile):
    n_rows, width = xs.shape
    d = width - PAY
    n_tiles = n_rows // moe_tile
    used = lambda p, ea, eb, nu: (jnp.minimum(p, nu[0] - 1), 0)
    return pl.pallas_call(
        _moe_kernel,
        out_shape=jax.ShapeDtypeStruct((n_rows, d), F32),
        grid_spec=pltpu.PrefetchScalarGridSpec(
            num_scalar_prefetch=3,
            grid=(n_tiles,),
            in_specs=[pl.BlockSpec((moe_tile, width), used),
                      pl.BlockSpec((1, d), lambda p, ea, eb, nu: (0, 0)),
                      pl.BlockSpec((1, d, 2 * D_FF), lambda p, ea, eb, nu: (ea[p], 0, 0)),
                      pl.BlockSpec((1, D_FF, d), lambda p, ea, eb, nu: (ea[p], 0, 0)),
                      pl.BlockSpec((1, d, 2 * D_FF), lambda p, ea, eb, nu: (eb[p], 0, 0)),
                      pl.BlockSpec((1, D_FF, d), lambda p, ea, eb, nu: (eb[p], 0, 0))],
            out_specs=pl.BlockSpec((moe_tile, d), lambda p, ea, eb, nu: (p, 0)),
        ),
        compiler_params=_cparams(("arbitrary",), 48),
        name="moe_sorted",
    )(ea, eb, nu, xs, g_ffn.reshape(1, d), w_gu, w_dn, w_gu, w_dn)


def _hier_moe_residual(x1e, cls, g_ffn, w_gu, w_dn):
    n = x1e.shape[0]
    moe_tile = min(MOE_TILE, n)
    n_tiles = n // moe_tile + N_CLASSES
    n_tiles_pad = -(-n_tiles // V7X_LANES) * V7X_LANES
    pos, cnt, off, tiles = _sort(cls, moe_tile, n_tiles_pad)
    xs = _permute(pos, cnt, off, x1e, n_tiles * moe_tile, moe_tile)
    ys = _moe(tiles[0, :n_tiles], tiles[1, :n_tiles], tiles[2, :1], xs, g_ffn, w_gu, w_dn, moe_tile)
    return _unpermute(pos, ys, n)


def _rotary_tables(pos):
    half = R_DK // 2
    inv = 1.0 / (ROPE_BASE ** jnp.linspace(0.0, 1.0, half, dtype=F32))
    ang = pos.astype(F32)[:, None] * inv[None, :]
    cos = jnp.cos(ang)
    sin = jnp.sin(ang)
    return jnp.concatenate([cos, cos], axis=-1), jnp.concatenate([-sin, sin], axis=-1)


def _token_group(x, pos, lw, lam_init, log_gamma, state0, cache):
    bsz, t_len, d = x.shape
    x2d = x.reshape(bsz * t_len, d)
    cos_tab, sin_tab = _rotary_tables(pos)
    tm = _row_tile(bsz * t_len)
    if t_len < tm:
        reps = tm // t_len
        cos_tab = jnp.tile(cos_tab, (reps, 1))
        sin_tab = jnp.tile(sin_tab, (reps, 1))
    rqk, rv, rgs, dq, kb, vt, dk_rows, dv_rows, sgr, sgd = _in_proj(
        x2d, lw["g_mix"], lw["w_in"], cos_tab, sin_tab, lw["g_q"], lw["g_k"])
    ogr, s_fin = _retention(rqk, rv, rgs, state0, lw["g_ret"], log_gamma, bsz, t_len)
    if cache is None:
        od = _attn_prompt(dq, kb, vt, lw["lam_p"], lw["g_sub"], lam_init, bsz, t_len)
    else:
        od = _attn_sample(dq, dk_rows, dv_rows, _to_cache_rows(cache[0]), _to_cache_rows(cache[1]),
                          lw["lam_p"], lw["g_sub"], lam_init, bsz, t_len)
    x1e, cls = _merge(ogr, od, sgr, sgd, x2d, lw["w_ret_o"], lw["w_diff_o"], lw["w_out"],
                      lw["g_ffn"], lw["rw_hi"], lw["rw_lo"], lw["rb"])
    y = _hier_moe_residual(x1e, cls, lw["g_ffn"], lw["w_gu"], lw["w_dn"])
    return (y.reshape(bsz, t_len, d), _from_cache_rows(dk_rows, bsz, t_len),
            _from_cache_rows(dv_rows, bsz, t_len), s_fin)


def _to_cache_rows(c):
    b, p, h, w = c.shape
    halves = w // V7X_LANES
    return c.reshape(b * p, h, halves, V7X_LANES).transpose(0, 2, 1, 3).reshape(b * p * h * halves, V7X_LANES)


def _from_cache_rows(rows, bsz, t_len):
    halves = D_DV // V7X_LANES
    r = rows.reshape(bsz * t_len, halves, D_HEADS, V7X_LANES).transpose(0, 2, 1, 3)
    return r.reshape(bsz, t_len, D_HEADS, D_DV)


def _layer_weights(l, g_mix, w_in, g_q, g_k, lambda_q1, lambda_k1, lambda_q2, lambda_k2, g_ret,
                   w_ret_o, g_sub, w_diff_o, w_out, g_ffn, w_group, b_group, w_expert, b_expert,
                   w_gate, w_up, w_down):
    d = w_in.shape[1]
    n_r = N_GROUPS + N_EXPERTS
    rw = jnp.concatenate([w_group[l], w_expert[l]], axis=1).astype(F32).T
    rw = jnp.zeros((CLS_PAD, d), F32).at[:n_r].set(rw)
    rw_hi = rw.astype(BF16)
    rw_lo = (rw - rw_hi.astype(F32)).astype(BF16)
    rb = jnp.concatenate([b_group[l], b_expert[l]]).astype(F32)
    rb = jnp.zeros((CLS_PAD,), F32).at[:n_r].set(rb)
    rb = jnp.broadcast_to(rb[:, None], (CLS_PAD, V7X_LANES))
    return dict(
        g_mix=g_mix[l], w_in=w_in[l].astype(BF16), g_q=g_q[l], g_k=g_k[l],
        lam_p=jnp.stack([lambda_q1[l], lambda_k1[l], lambda_q2[l], lambda_k2[l]]).astype(F32),
        g_ret=g_ret[l], w_ret_o=w_ret_o[l].astype(BF16), g_sub=g_sub[l],
        w_diff_o=w_diff_o[l].astype(BF16), w_out=w_out[l].astype(BF16), g_ffn=g_ffn[l],
        rw_hi=rw_hi, rw_lo=rw_lo, rb=rb,
        w_gu=jnp.concatenate([w_gate[l], w_up[l]], axis=-1).astype(BF16),
        w_dn=w_down[l].astype(BF16))


def kernel(x_prompt, x_sample, cache_k, cache_v, state_ret, g_mix, w_in, g_q, g_k, lambda_q1, lambda_k1, lambda_q2, lambda_k2, g_ret, w_ret_o, g_sub, w_diff_o, w_out, g_ffn, w_group, b_group, w_expert, b_expert, w_gate, w_up, w_down):
    depth = w_in.shape[0]
    bp, tp, _ = x_prompt.shape
    bs, ts, _ = x_sample.shape
    past = cache_k.shape[2]
    log_gamma = jnp.log1p(-jnp.exp2(-5.0 - jnp.arange(R_HEADS, dtype=F32)))
    pos_p = jnp.arange(tp, dtype=jnp.int32)
    pos_s = past + jnp.arange(ts, dtype=jnp.int32)
    zero_state = jnp.zeros((bp, R_HEADS, R_DK, R_DV), F32)
    xp, xs = x_prompt, x_sample
    outs = [[] for _ in range(6)]
    for l in range(depth):
        lam_init = 0.8 - 0.6 * math.exp(-0.3 * l)
        lw = _layer_weights(l, g_mix, w_in, g_q, g_k, lambda_q1, lambda_k1, lambda_q2, lambda_k2,
                            g_ret, w_ret_o, g_sub, w_diff_o, w_out, g_ffn, w_group, b_group,
                            w_expert, b_expert, w_gate, w_up, w_down)
        xp, kp, vp, sp = _token_group(xp, pos_p, lw, lam_init, log_gamma, zero_state, None)
        cache = (cache_k[l], cache_v[l])
        xs, ks, vs, ss = _token_group(xs, pos_s, lw, lam_init, log_gamma,
                                      state_ret[l].astype(F32), cache)
        for lst, val in zip(outs, (kp, vp, sp, ks, vs, ss)):
            lst.append(val)
    return (xp, xs) + tuple(jnp.stack(o) for o in outs)
```

```python
import functools
import math

import jax
import jax.numpy as jnp
from jax import lax
from jax.experimental import pallas as pl
from jax.experimental.pallas import tpu as pltpu

F32 = jnp.float32
BF16 = jnp.bfloat16
I32 = jnp.int32

CHUNK = 64
EPS = 1e-6
R_HEADS = 4
R_DK = 128
R_DV = 256
ROPE_BASE = 10000.0
D_HEADS = 4
D_DH = 128
D_DV = 256
N_GROUPS = 4
EXP_PER_GROUP = 4
N_EXPERTS = N_GROUPS * EXP_PER_GROUP
N_PAIRS = EXP_PER_GROUP * (EXP_PER_GROUP - 1) // 2
N_CLASSES = N_GROUPS * N_PAIRS
D_FF = 512
SEG = 1024
N_SEG = 8

V7X_LANES = 128
V7X_SUBLANES = 8
V7X_VMEM_LIMIT_BYTES = 56 * 1024 * 1024

ROW_TILE = 512
RET_SUPER = 512
ATT_TILE = 512
SAMPLE_CHUNK = 1024
MOE_TILE = 256
ROW_DMA_UNROLL = 8
CLS_PAD = 32
NEG_BIG = -1e30
LOG2E = math.log2(math.e)
SCORE_BOUND_MARGIN = 1.01
MAX_SHIFT_LOG2 = 100.0
PAY = V7X_LANES


def _row_tile(n):
    return min(ROW_TILE, n)


def _cparams(sem, vmem_mb=None):
    kw = dict(dimension_semantics=sem)
    if vmem_mb is not None:
        kw["vmem_limit_bytes"] = min(vmem_mb * 1024 * 1024, V7X_VMEM_LIMIT_BYTES)
    return pltpu.CompilerParams(**kw)


def _resident(shape):
    nd = len(shape)
    return pl.BlockSpec(shape, lambda *_: (0,) * nd, pipeline_mode=pl.Buffered(1))


def _rms(x, eps=EPS):
    return x * lax.rsqrt(jnp.mean(x * x, axis=-1, keepdims=True) + eps)


def _sigmoid(x):
    return 1.0 / (1.0 + jnp.exp(-x))


def _dot(a, b):
    return jnp.dot(a, b, preferred_element_type=F32)


def _dot_nt(a, b):
    return lax.dot_general(a, b, (((1,), (1,)), ((), ())), preferred_element_type=F32)


def _dot_tn(a, b):
    return lax.dot_general(a, b, (((0,), (0,)), ((), ())), preferred_element_type=F32)


def _store_cache_rows(ref, z):
    rows = z.shape[0]
    for half in range(2):
        for h in range(D_HEADS):
            lo = h * D_DV + half * V7X_LANES
            ref[pl.ds(half * D_HEADS + h, rows, stride=2 * D_HEADS), :] = z[:, lo:lo + V7X_LANES]


def _inproj_kernel(x_ref, g_ref, w_ref, cos_ref, sin_ref, gq_ref, gk_ref,
                   rqk_ref, rv_ref, rgs_ref, dq_ref, kb_ref, vt_ref, dk_ref, dv_ref, sgr_ref, sgd_ref):
    x = x_ref[...]
    hb = (_rms(x) * g_ref[...]).astype(BF16)

    def seg(s):
        return _dot(hb, w_ref[:, s * SEG:(s + 1) * SEG])

    cos = cos_ref[...]
    sin = sin_ref[...]
    z = seg(0)
    for j in range(2 * R_HEADS):
        v = z[:, j * R_DK:(j + 1) * R_DK]
        r = v * cos + pltpu.roll(v, R_DK // 2, 1) * sin
        if j >= R_HEADS:
            r = r * (R_DK ** -0.5)
        rqk_ref[:, j * R_DK:(j + 1) * R_DK] = r.astype(BF16)

    rv_ref[...] = seg(1).astype(BF16)
    z = seg(2)
    rgs_ref[...] = (z * _sigmoid(z)).astype(BF16)

    z = seg(3)
    gq = gq_ref[...] * (D_DH ** -0.5 * LOG2E)
    for j in range(2 * D_HEADS):
        v = z[:, j * D_DH:(j + 1) * D_DH]
        dq_ref[:, j * D_DH:(j + 1) * D_DH] = (_rms(v) * gq).astype(BF16)
    z = seg(4)
    gk = gk_ref[...]
    kn = jnp.concatenate([_rms(z[:, j * D_DH:(j + 1) * D_DH]) * gk for j in range(2 * D_HEADS)], axis=1)
    kb_ref[...] = kn.astype(BF16)
    _store_cache_rows(dk_ref, kn)
    z = seg(5)
    _store_cache_rows(dv_ref, z)
    vt_ref[0] = z.T.astype(BF16)
    sgr_ref[...] = _sigmoid(seg(6)).astype(BF16)
    sgd_ref[...] = _sigmoid(seg(7)).astype(BF16)


def _in_proj(x2d, g_mix, w_in_bf, cos_tab, sin_tab, g_q, g_k):
    n, d = x2d.shape
    tm = _row_tile(n)
    nt = n // tm
    ntab = cos_tab.shape[0] // tm
    row = lambda w: pl.BlockSpec((tm, w), lambda i: (i, 0))
    tab = pl.BlockSpec((tm, R_DK), lambda i: (i % ntab, 0))
    flat = jax.ShapeDtypeStruct((n, SEG), BF16)
    cache_rows = jax.ShapeDtypeStruct((n * 2 * D_HEADS, V7X_LANES), F32)
    cache_spec = pl.BlockSpec((tm * 2 * D_HEADS, V7X_LANES), lambda i: (i, 0))
    outs = [flat, flat, flat, flat, flat, jax.ShapeDtypeStruct((nt, SEG, tm), BF16),
            cache_rows, cache_rows, flat, flat]
    return pl.pallas_call(
        _inproj_kernel,
        out_shape=outs,
        grid=(nt,),
        in_specs=[row(d), _resident((1, d)), _resident(w_in_bf.shape), tab, tab,
                  _resident((1, D_DH)), _resident((1, D_DH))],
        out_specs=[row(SEG)] * 5 + [pl.BlockSpec((1, SEG, tm), lambda i: (i, 0, 0)),
                                    cache_spec, cache_spec, row(SEG), row(SEG)],
        compiler_params=_cparams(("arbitrary",), 56),
        name="in_proj",
    )(x2d, g_mix.reshape(1, d), w_in_bf, cos_tab, sin_tab,
      g_q.reshape(1, D_DH), g_k.reshape(1, D_DH))


def _ret_kernel(lg_ref, q_ref, k_ref, v_ref, rgs_ref, s0_ref, g_ref,
                o_ref, sfin_ref, s_scr, d_scr, *, c_len):
    h = pl.program_id(1)
    c = pl.program_id(2)
    lg = lg_ref[h]

    @pl.when(c == 0)
    def _():
        s_scr[...] = s0_ref[0, 0]
        t = lax.broadcasted_iota(I32, (c_len, c_len), 0)
        s = lax.broadcasted_iota(I32, (c_len, c_len), 1)
        dist = jnp.abs(t - s).astype(F32)
        vis = (s // CHUNK) <= (t // CHUNK)
        d_scr[...] = jnp.where(vis, jnp.exp(lg * dist), 0.0)

    q = q_ref[0]
    k = k_ref[0]
    v = v_ref[0]
    pos = lax.broadcasted_iota(I32, (c_len, 1), 0).astype(F32)
    state = s_scr[...]
    s = _dot_nt(q, k) * d_scr[...]
    o = _dot(s.astype(BF16), v)
    o = o + jnp.exp(lg * (pos + 1.0)) * _dot(q, state.astype(BF16))
    kw = (k.astype(F32) * jnp.exp(lg * (c_len - 1.0 - pos))).astype(BF16)
    decay = jnp.exp(lg * jnp.full((1, R_DV), float(c_len), F32))
    new_state = state * decay + _dot_tn(kw, v)
    s_scr[...] = new_state

    o_ref[0] = (_rms(o) * g_ref[...] * rgs_ref[0].astype(F32)).astype(BF16)

    @pl.when(c == pl.num_programs(2) - 1)
    def _():
        sfin_ref[0, 0] = new_state


def _retention(rqk, rv, rgs, state0, g_ret, log_gamma, bsz, t_len):
    c_len = min(RET_SUPER, t_len)
    nc = t_len // c_len
    rqk3 = rqk.reshape(bsz, t_len, SEG)
    rv3 = rv.reshape(bsz, t_len, SEG)
    rgs3 = rgs.reshape(bsz, t_len, SEG)
    qspec = pl.BlockSpec((1, c_len, R_DK), lambda b, h, c, lg: (b, c, h))
    kspec = pl.BlockSpec((1, c_len, R_DK), lambda b, h, c, lg: (b, c, R_HEADS + h))
    vspec = pl.BlockSpec((1, c_len, R_DV), lambda b, h, c, lg: (b, c, h))
    sspec = pl.BlockSpec((1, 1, R_DK, R_DV), lambda b, h, c, lg: (b, h, 0, 0))
    gspec = pl.BlockSpec((1, R_DV), lambda b, h, c, lg: (0, 0))
    o, sfin = pl.pallas_call(
        functools.partial(_ret_kernel, c_len=c_len),
        out_shape=[jax.ShapeDtypeStruct((bsz, t_len, SEG), BF16),
                   jax.ShapeDtypeStruct((bsz, R_HEADS, R_DK, R_DV), F32)],
        grid_spec=pltpu.PrefetchScalarGridSpec(
            num_scalar_prefetch=1,
            grid=(bsz, R_HEADS, nc),
            in_specs=[qspec, kspec, vspec, vspec, sspec, gspec],
            out_specs=[vspec, sspec],
            scratch_shapes=[pltpu.VMEM((R_DK, R_DV), F32), pltpu.VMEM((c_len, c_len), F32)],
        ),
        compiler_params=_cparams(("arbitrary",) * 3),
        name="retention",
    )(log_gamma, rqk3, rqk3, rv3, rgs3, state0, g_ret.reshape(1, R_DV))
    return o.reshape(bsz * t_len, SEG), sfin


def _lambda_full(lp_ref, lam_init):
    lp = lp_ref[...]
    a = jnp.sum(lp[0:1] * lp[1:2], axis=-1, keepdims=True)
    b = jnp.sum(lp[2:3] * lp[3:4], axis=-1, keepdims=True)
    return jnp.exp(a) - jnp.exp(b) + lam_init


def _attn_finish(o1, l1, o2, l2, lam, g, lam_init):
    o = o1 / l1 - lam * (o2 / l2)
    return (_rms(o) * g * (1.0 - lam_init)).astype(BF16)


def _attn_prompt_kernel(bound_ref, q_ref, k_ref, vt_ref, lp_ref, g_ref, o_ref,
                        m1, l1, a1, m2, l2, a2, bias, *, tile, lam_init, fixed_shift):
    i = pl.program_id(2)

    q = q_ref[0]
    q1 = q[:, :D_DH]
    q2 = q[:, D_DH:]
    for m, l, a in ((m1, l1, a1), (m2, l2, a2)):
        m[...] = jnp.full(m.shape, NEG_BIG, F32)
        l[...] = jnp.zeros(l.shape, F32)
        a[...] = jnp.zeros(a.shape, F32)

    def update(s, vts, m, l, a):
        if fixed_shift:
            p = jnp.exp2(s - bound_ref[0])
            l[...] += jnp.sum(p, axis=0, keepdims=True)
            a[...] += _dot(vts, p.astype(BF16))
            return
        m_old = m[...]
        m_new = jnp.maximum(m_old, jnp.max(s, axis=0, keepdims=True))
        alpha = jnp.exp2(m_old - m_new)
        p = jnp.exp2(s - m_new)
        l[...] = alpha * l[...] + jnp.sum(p, axis=0, keepdims=True)
        a[...] = alpha * a[...] + _dot(vts, p.astype(BF16))
        m[...] = m_new

    def block(j, mask):
        ks = k_ref[0, pl.ds(pl.multiple_of(j * tile, tile), tile), :]
        vts = vt_ref[0, j]
        s1 = _dot_nt(ks[:, :D_DH], q1)
        s2 = _dot_nt(ks[:, D_DH:], q2)
        if mask is not None:
            s1 = s1 + mask
            s2 = s2 + mask
        update(s1, vts, m1, l1, a1)
        update(s2, vts, m2, l2, a2)

    def body(jj, carry):
        block(2 * jj, None)
        block(2 * jj + 1, None)
        return carry

    lax.fori_loop(0, i // 2, body, 0)

    @pl.when(i % 2 == 1)
    def _():
        block(i - 1, None)

    @pl.when((pl.program_id(0) == 0) & (pl.program_id(1) == 0) & (i == 0))
    def _():
        kk = lax.broadcasted_iota(I32, (tile, tile), 0)
        qq = lax.broadcasted_iota(I32, (tile, tile), 1)
        bias[...] = jnp.where((kk // CHUNK) <= (qq // CHUNK), 0.0, NEG_BIG)

    block(i, bias[...])

    lam = _lambda_full(lp_ref, lam_init)
    ot = a1[...] * (1.0 / l1[...]) - lam * (a2[...] * (1.0 / l2[...]))
    o_ref[0] = (_rms(ot.T) * g_ref[...] * (1.0 - lam_init)).astype(BF16)


def _attn_prompt(dq, kb, vt, g_q, g_k, lam_p, g_sub, lam_init, bsz, t_len):
    tile = vt.shape[2]
    assert t_len % tile == 0 and tile % CHUNK == 0
    nq = t_len // tile
    w = 2 * D_DH
    q3 = dq.reshape(bsz, t_len, SEG)
    k3 = kb.reshape(bsz, t_len, SEG)
    vt4 = vt.reshape(bsz, nq, SEG, tile)
    qspec = pl.BlockSpec((1, tile, w), lambda b, h, i: (b, i, h))
    kspec = pl.BlockSpec((1, t_len, w), lambda b, h, i: (b, 0, h))
    vspec = pl.BlockSpec((1, nq, D_DV, tile), lambda b, h, i: (b, 0, h, 0))
    col = lambda: pltpu.VMEM((1, tile), F32)
    acc = lambda: pltpu.VMEM((D_DV, tile), F32)

    def call(fixed_shift):
        return pl.pallas_call(
            functools.partial(_attn_prompt_kernel, tile=tile, lam_init=lam_init, fixed_shift=fixed_shift),
            out_shape=jax.ShapeDtypeStruct((bsz, t_len, SEG), BF16),
            grid=(bsz, D_HEADS, nq),
            in_specs=[pl.BlockSpec(memory_space=pltpu.SMEM), qspec, kspec, vspec,
                      pl.BlockSpec((4, D_DH), lambda b, h, i: (0, 0)),
                      pl.BlockSpec((1, D_DV), lambda b, h, i: (0, 0))],
            out_specs=qspec,
            scratch_shapes=[col(), col(), acc(), col(), col(), acc(), pltpu.VMEM((tile, tile), F32)],
            compiler_params=_cparams(("arbitrary",) * 3, 32),
            name="attn_prompt_fixed" if fixed_shift else "attn_prompt_online",
        )

    bound = SCORE_BOUND_MARGIN * D_DH * (D_DH ** -0.5 * LOG2E) * jnp.max(jnp.abs(g_q * g_k))
    args = (bound.reshape(1).astype(F32), q3, k3, vt4, lam_p, g_sub.reshape(1, D_DV))
    out = lax.cond(2.0 * bound < MAX_SHIFT_LOG2,
                   lambda *a: call(True)(*a), lambda *a: call(False)(*a), *args)
    return out.reshape(bsz * t_len, SEG)


def _attn_sample_kernel(q_ref, kc_ref, vc_ref, kn_ref, vn_ref, lp_ref, g_ref, o_ref,
                        m_scr, l_scr, a_scr, *, lam_init, chunk, t_new):
    c = pl.program_id(1)

    @pl.when(c == 0)
    def _():
        m_scr[...] = jnp.full(m_scr.shape, NEG_BIG, F32)
        l_scr[...] = jnp.zeros(l_scr.shape, F32)
        a_scr[...] = jnp.zeros(a_scr.shape, F32)

    q = q_ref[0]

    def head_rows(ref, h, half, n):
        return ref[pl.ds(half * D_HEADS + h, n, stride=2 * D_HEADS), :].astype(BF16)

    def absorb(k_src, v_src, n):
        for h in range(D_HEADS):
            v = jnp.concatenate([head_rows(v_src, h, 0, n), head_rows(v_src, h, 1, n)], axis=1)
            for mp in range(2):
                idx = 2 * h + mp
                lo = h * 2 * D_DH + mp * D_DH
                s = _dot_nt(q[:, lo:lo + D_DH], head_rows(k_src, h, mp, n))
                m_old = m_scr[idx]
                m_new = jnp.maximum(m_old, jnp.max(s, axis=-1, keepdims=True))
                alpha = jnp.exp2(m_old - m_new)
                p = jnp.exp2(s - m_new)
                l_scr[idx] = alpha * l_scr[idx] + jnp.sum(p, axis=-1, keepdims=True)
                a_scr[idx] = alpha * a_scr[idx] + _dot(p.astype(BF16), v)
                m_scr[idx] = m_new

    absorb(kc_ref, vc_ref, chunk)

    @pl.when(c == pl.num_programs(1) - 1)
    def _():
        absorb(kn_ref, vn_ref, t_new)
        lam = _lambda_full(lp_ref, lam_init)
        for h in range(D_HEADS):
            o_ref[0, :, h * D_DV:(h + 1) * D_DV] = _attn_finish(
                a_scr[2 * h], l_scr[2 * h], a_scr[2 * h + 1], l_scr[2 * h + 1], lam, g_ref[...], lam_init)


def _attn_sample(dq, dk_rows, dv_rows, cache_k_rows, cache_v_rows, lam_p, g_sub, lam_init, bsz, t_len):
    rpf = 2 * D_HEADS
    past = cache_k_rows.shape[0] // (bsz * rpf)
    chunk = min(SAMPLE_CHUNK, past)
    nc = past // chunk
    q3 = dq.reshape(bsz, t_len, SEG)
    qspec = pl.BlockSpec((1, t_len, SEG), lambda b, c: (b, 0, 0))
    cspec = pl.BlockSpec((chunk * rpf, V7X_LANES), lambda b, c: (b * nc + c, 0))
    nspec = pl.BlockSpec((t_len * rpf, V7X_LANES), lambda b, c: (b, 0))
    n_stat = 2 * D_HEADS
    out = pl.pallas_call(
        functools.partial(_attn_sample_kernel, lam_init=lam_init, chunk=chunk, t_new=t_len),
        out_shape=jax.ShapeDtypeStruct((bsz, t_len, SEG), BF16),
        grid=(bsz, nc),
        in_specs=[qspec, cspec, cspec, nspec, nspec,
                  pl.BlockSpec((4, D_DH), lambda b, c: (0, 0)),
                  pl.BlockSpec((1, D_DV), lambda b, c: (0, 0))],
        out_specs=qspec,
        scratch_shapes=[pltpu.VMEM((n_stat, t_len, 1), F32), pltpu.VMEM((n_stat, t_len, 1), F32),
                        pltpu.VMEM((n_stat, t_len, D_DV), F32)],
        compiler_params=_cparams(("arbitrary",) * 2, 32),
        name="attn_sample",
    )(q3, cache_k_rows, cache_v_rows, dk_rows, dv_rows, lam_p, g_sub.reshape(1, D_DV))
    return out.reshape(bsz * t_len, SEG)


def _split_hi_lo(x):
    hi = x.astype(BF16)
    lo = (x - hi.astype(F32)).astype(BF16)
    return hi, lo


def _merge_kernel(ogr_ref, od_ref, sgr_ref, sgd_ref, x_ref, wr_ref, wd_ref, wo_ref,
                  gf_ref, rw_hi_ref, rw_lo_ref, rb_ref, x1e_ref, cls_ref):
    tm = x_ref.shape[0]
    y_r = _dot(ogr_ref[...], wr_ref[...])
    y_d = _dot(od_ref[...], wd_ref[...])
    m = sgr_ref[...].astype(F32) * y_r + sgd_ref[...].astype(F32) * y_d
    x1 = x_ref[...] + _dot(m.astype(BF16), wo_ref[...])

    hn = _rms(x1) * gf_ref[...]
    h_hi, h_lo = _split_hi_lo(hn)
    w_hi = rw_hi_ref[...]
    lt = _dot_nt(w_hi, h_hi) + _dot_nt(rw_lo_ref[...], h_hi) + _dot_nt(w_hi, h_lo)
    lt = lt + rb_ref[...][:, 0:1]
    g = [lt[i:i + 1, :] for i in range(N_GROUPS)]
    e = [lt[N_GROUPS + i:N_GROUPS + i + 1, :] for i in range(N_EXPERTS)]

    gmax = functools.reduce(jnp.maximum, g)
    gid = jnp.full(g[0].shape, N_GROUPS - 1, I32)
    for i in range(N_GROUPS - 2, -1, -1):
        gid = jnp.where(g[i] == gmax, i, gid)
    g_w = 1.0 / functools.reduce(lambda a, b: a + b, [jnp.exp(v - gmax) for v in g])

    es = []
    for j in range(EXP_PER_GROUP):
        v = e[(N_GROUPS - 1) * EXP_PER_GROUP + j]
        for i in range(N_GROUPS - 2, -1, -1):
            v = jnp.where(gid == i, e[i * EXP_PER_GROUP + j], v)
        es.append(v)

    def first_argmax(vals):
        mx = functools.reduce(jnp.maximum, vals)
        idx = jnp.full(mx.shape, len(vals) - 1, I32)
        for i in range(len(vals) - 2, -1, -1):
            idx = jnp.where(vals[i] == mx, i, idx)
        return mx, idx

    l1, i1 = first_argmax(es)
    rest = [jnp.where(i1 == j, -jnp.inf, es[j]) for j in range(EXP_PER_GROUP)]
    l2, i2 = first_argmax(rest)
    t = jnp.exp(l2 - l1)
    c1 = g_w / (1.0 + t)
    c2 = g_w * t / (1.0 + t)
    lo = jnp.minimum(i1, i2)
    hi = jnp.maximum(i1, i2)
    base = jnp.where(lo == 0, 0, jnp.where(lo == 1, EXP_PER_GROUP - 1, 2 * EXP_PER_GROUP - 3))
    cls = gid * N_PAIRS + base + hi - lo - 1
    wa = jnp.where(i1 < i2, c1, c2)
    wb = jnp.where(i1 < i2, c2, c1)

    cls_ref[0] = cls
    rid = lax.broadcasted_iota(I32, (PAY, tm), 0)
    pay = jnp.where(rid == 0, wa, jnp.where(rid == 1, wb, 0.0))
    d = x_ref.shape[1]
    x1e_ref[:, :d] = x1
    x1e_ref[:, d:] = pay.T


def _merge(ogr, od, sgr, sgd, x2d, w_ret_o, w_diff_o, w_out, g_ffn, rw_hi, rw_lo, rb):
    n, d = x2d.shape
    tm = _row_tile(n)
    nt = n // tm
    row = lambda w: pl.BlockSpec((tm, w), lambda i: (i, 0))
    return pl.pallas_call(
        _merge_kernel,
        out_shape=[jax.ShapeDtypeStruct((n, d + PAY), F32),
                   jax.ShapeDtypeStruct((nt, 1, tm), I32)],
        grid=(nt,),
        in_specs=[row(SEG), row(SEG), row(SEG), row(SEG), row(d),
                  _resident(w_ret_o.shape), _resident(w_diff_o.shape), _resident(w_out.shape),
                  _resident((1, d)), _resident(rw_hi.shape), _resident(rw_lo.shape),
                  _resident(rb.shape)],
        out_specs=[row(d + PAY), pl.BlockSpec((1, 1, tm), lambda i: (i, 0, 0))],
        compiler_params=_cparams(("arbitrary",), 48),
        name="merge_route",
    )(ogr, od, sgr, sgd, x2d, w_ret_o, w_diff_o, w_out, g_ffn.reshape(1, d), rw_hi, rw_lo, rb)


def _prefix_excl(x):
    rid = lax.broadcasted_iota(I32, x.shape, 0)
    inc = x
    s = 1
    while s < x.shape[0]:
        inc = inc + jnp.where(rid >= s, pltpu.roll(inc, s, 0), 0.0)
        s *= 2
    return inc - x


def _sort_kernel(cls_ref, pos_ref, cnt_ref, off_ref, tiles_ref, cnt_scr, run_scr, off_scr,
                 *, moe_tile, n_tiles_pad):
    ph = pl.program_id(0)
    t = pl.program_id(1)
    tm = cls_ref.shape[2]
    cls = cls_ref[0]
    cid = lax.broadcasted_iota(I32, (CLS_PAD, tm), 0)
    onehot = (cid == cls)

    @pl.when(jnp.logical_and(ph == 0, t == 0))
    def _():
        cnt_scr[...] = jnp.zeros(cnt_scr.shape, F32)

    @pl.when(ph == 0)
    def _():
        cnt_scr[...] += jnp.sum(onehot.astype(F32), axis=-1, keepdims=True)

    @pl.when(jnp.logical_and(ph == 1, t == 0))
    def _():
        cnt = cnt_scr[...]
        padded = jnp.ceil(cnt / moe_tile) * moe_tile
        off = _prefix_excl(padded)
        off_scr[...] = off
        run_scr[...] = jnp.zeros(run_scr.shape, F32)
        cnt_ref[...] = cnt.astype(I32)
        off_ref[...] = off.astype(I32)
        end = (off + padded)[:, 0:1]
        total = jnp.max(end, axis=0, keepdims=True)
        n_used = total / moe_tile
        p = lax.broadcasted_iota(I32, (1, n_tiles_pad), 1).astype(F32)
        start = jnp.minimum(p, n_used - 1.0) * moe_tile
        cid2 = lax.broadcasted_iota(I32, (CLS_PAD, n_tiles_pad), 0)
        before = jnp.logical_and(end <= start, cid2 < N_CLASSES)
        tcls = jnp.sum(before.astype(F32), axis=0, keepdims=True).astype(I32)
        grp = tcls // N_PAIRS
        pr = tcls - grp * N_PAIRS
        lo = (pr >= EXP_PER_GROUP - 1).astype(I32) + (pr >= 2 * EXP_PER_GROUP - 3).astype(I32)
        base = jnp.where(lo == 0, 0, jnp.where(lo == 1, EXP_PER_GROUP - 1, 2 * EXP_PER_GROUP - 3))
        hi = pr - base + lo + 1
        rid = lax.broadcasted_iota(I32, (V7X_SUBLANES, n_tiles_pad), 0)
        ea = grp * EXP_PER_GROUP + lo
        eb = grp * EXP_PER_GROUP + hi
        nu = jnp.broadcast_to(n_used.astype(I32), (1, n_tiles_pad))
        tiles_ref[...] = jnp.where(rid == 0, ea, jnp.where(rid == 1, eb, jnp.where(rid == 2, nu, 0)))

    @pl.when(ph == 1)
    def _():
        r = lax.broadcasted_iota(I32, (tm, tm), 0)
        c = lax.broadcasted_iota(I32, (tm, tm), 1)
        upper = jnp.where(r <= c, 1.0, 0.0).astype(BF16)
        oh = jnp.where(onehot, 1.0, 0.0)
        incl = _dot(oh.astype(BF16), upper)
        slot = off_scr[...][:, 0:1] + run_scr[...][:, 0:1] + incl - 1.0
        pos_ref[0] = jnp.sum(oh * slot, axis=0, keepdims=True).astype(I32)
        run_scr[...] += jnp.sum(oh, axis=-1, keepdims=True)


def _sort(cls, moe_tile, n_tiles_pad):
    nt, _, tm = cls.shape
    blk = pl.BlockSpec((1, 1, tm), lambda ph, t: (t, 0, 0))
    oblk = pl.BlockSpec((1, 1, tm), lambda ph, t: (t * ph, 0, 0))
    whole = lambda shape: pl.BlockSpec(shape, lambda ph, t: (0, 0))
    return pl.pallas_call(
        functools.partial(_sort_kernel, moe_tile=moe_tile, n_tiles_pad=n_tiles_pad),
        out_shape=[jax.ShapeDtypeStruct((nt, 1, tm), I32),
                   jax.ShapeDtypeStruct((CLS_PAD, V7X_LANES), I32),
                   jax.ShapeDtypeStruct((CLS_PAD, V7X_LANES), I32),
                   jax.ShapeDtypeStruct((V7X_SUBLANES, n_tiles_pad), I32)],
        grid=(2, nt),
        in_specs=[blk],
        out_specs=[oblk, whole((CLS_PAD, V7X_LANES)), whole((CLS_PAD, V7X_LANES)),
                   whole((V7X_SUBLANES, n_tiles_pad))],
        scratch_shapes=[pltpu.VMEM((CLS_PAD, V7X_LANES), F32)] * 3,
        compiler_params=_cparams(("arbitrary",) * 2),
        name="class_sort",
    )(cls)


def _permute_kernel(pos_ref, cnt_ref, off_ref, src_ref, dst_ref, zblk, sem, zsem, *, moe_tile):
    t = pl.program_id(0)
    tm = pos_ref.shape[2]

    def row_copy(r):
        return pltpu.make_async_copy(src_ref.at[pl.ds(r, 1)],
                                     dst_ref.at[pl.ds(pos_ref[0, 0, r], 1)], sem)

    def start(r, carry):
        row_copy(r).start()
        return carry

    def wait(r, carry):
        row_copy(r).wait()
        return carry

    lax.fori_loop(0, tm, start, 0, unroll=ROW_DMA_UNROLL)

    @pl.when(t == 0)
    def _():
        zblk[...] = jnp.zeros(zblk.shape, zblk.dtype)
        used = 0
        for c in range(N_CLASSES):
            cnt = cnt_ref[c, 0]
            off = off_ref[c, 0]
            padded = ((cnt + moe_tile - 1) // moe_tile) * moe_tile
            used = off + padded

            def pad_copy(r, off=off):
                return pltpu.make_async_copy(zblk.at[pl.ds(0, 1)], dst_ref.at[pl.ds(off + r, 1)], zsem)

            def pad_start(r, carry, copy=pad_copy):
                copy(r).start()
                return carry

            def pad_wait(r, carry, copy=pad_copy):
                copy(r).wait()
                return carry

            lax.fori_loop(cnt, padded, pad_start, 0)
            lax.fori_loop(cnt, padded, pad_wait, 0)

        def tile_copy(p):
            return pltpu.make_async_copy(zblk, dst_ref.at[pl.ds(p * moe_tile, moe_tile)], zsem)

        def tile_start(p, carry):
            tile_copy(p).start()
            return carry

        def tile_wait(p, carry):
            tile_copy(p).wait()
            return carry

        first, last = used // moe_tile, dst_ref.shape[0] // moe_tile
        lax.fori_loop(first, last, tile_start, 0)
        lax.fori_loop(first, last, tile_wait, 0)

    lax.fori_loop(0, tm, wait, 0, unroll=ROW_DMA_UNROLL)


def _permute(pos, cnt, off, src, n_rows_out, moe_tile):
    nt, _, tm = pos.shape
    width = src.shape[1]
    smem = lambda shape, imap: pl.BlockSpec(shape, imap, memory_space=pltpu.SMEM)
    return pl.pallas_call(
        functools.partial(_permute_kernel, moe_tile=moe_tile),
        out_shape=jax.ShapeDtypeStruct((n_rows_out, width), src.dtype),
        grid=(nt,),
        in_specs=[smem((1, 1, tm), lambda t: (t, 0, 0)),
                  smem(cnt.shape, lambda t: (0, 0)),
                  smem(off.shape, lambda t: (0, 0)),
                  pl.BlockSpec((tm, width), lambda t: (t, 0))],
        out_specs=pl.BlockSpec(memory_space=pl.ANY),
        scratch_shapes=[pltpu.VMEM((moe_tile, width), src.dtype),
                        pltpu.SemaphoreType.DMA, pltpu.SemaphoreType.DMA],
        compiler_params=_cparams(("arbitrary",)),
        name="permute_rows",
    )(pos, cnt, off, src)


def _unpermute_kernel(pos_ref, src_ref, dst_ref, sem):
    tm = pos_ref.shape[2]

    def row_copy(r):
        return pltpu.make_async_copy(src_ref.at[pl.ds(pos_ref[0, 0, r], 1)],
                                     dst_ref.at[pl.ds(r, 1)], sem)

    def start(r, carry):
        row_copy(r).start()
        return carry

    def wait(r, carry):
        row_copy(r).wait()
        return carry

    lax.fori_loop(0, tm, start, 0, unroll=ROW_DMA_UNROLL)
    lax.fori_loop(0, tm, wait, 0, unroll=ROW_DMA_UNROLL)


def _unpermute(pos, src, n_rows_out):
    nt, _, tm = pos.shape
    width = src.shape[1]
    return pl.pallas_call(
        _unpermute_kernel,
        out_shape=jax.ShapeDtypeStruct((n_rows_out, width), src.dtype),
        grid=(nt,),
        in_specs=[pl.BlockSpec((1, 1, tm), lambda t: (t, 0, 0), memory_space=pltpu.SMEM),
                  pl.BlockSpec(memory_space=pl.ANY)],
        out_specs=pl.BlockSpec((tm, width), lambda t: (t, 0)),
        scratch_shapes=[pltpu.SemaphoreType.DMA],
        compiler_params=_cparams(("arbitrary",)),
        name="unpermute_rows",
    )(pos, src)


def _moe_kernel(ea_ref, eb_ref, nu_ref, xs_ref, gf_ref, wgu_a, wdn_a, wgu_b, wdn_b, y_ref):
    p = pl.program_id(0)
    d = y_ref.shape[1]

    @pl.when(p < nu_ref[0])
    def _():
        x = xs_ref[:, :d]
        wa = xs_ref[:, d:d + 1]
        wb = xs_ref[:, d + 1:d + 2]
        hn = (_rms(x) * gf_ref[...]).astype(BF16)

        def expert(wgu, wdn):
            gu = _dot(hn, wgu[0])
            gate = gu[:, :D_FF]
            he = gate * _sigmoid(gate) * gu[:, D_FF:]
            return _dot(he.astype(BF16), wdn[0])

        y_ref[...] = x + wa * expert(wgu_a, wdn_a) + wb * expert(wgu_b, wdn_b)

    @pl.when(p >= nu_ref[0])
    def _():
        y_ref[...] = jnp.zeros(y_ref.shape, y_ref.dtype)


def _moe(ea, eb, nu, xs, g_ffn, w_gu, w_dn, moe_tile):
    n_rows, width = xs.shape
    d = width - PAY
    n_tiles = n_rows // moe_tile
    used = lambda p, ea, eb, nu: (jnp.minimum(p, nu[0] - 1), 0)
    return pl.pallas_call(
        _moe_kernel,
        out_shape=jax.ShapeDtypeStruct((n_rows, d), F32),
        grid_spec=pltpu.PrefetchScalarGridSpec(
            num_scalar_prefetch=3,
            grid=(n_tiles,),
            in_specs=[pl.BlockSpec((moe_tile, width), used),
                      pl.BlockSpec((1, d), lambda p, ea, eb, nu: (0, 0)),
                      pl.BlockSpec((1, d, 2 * D_FF), lambda p, ea, eb, nu: (ea[p], 0, 0)),
                      pl.BlockSpec((1, D_FF, d), lambda p, ea, eb, nu: (ea[p], 0, 0)),
                      pl.BlockSpec((1, d, 2 * D_FF), lambda p, ea, eb, nu: (eb[p], 0, 0)),
                      pl.BlockSpec((1, D_FF, d), lambda p, ea, eb, nu: (eb[p], 0, 0))],
            out_specs=pl.BlockSpec((moe_tile, d), lambda p, ea, eb, nu: (p, 0)),
        ),
        compiler_params=_cparams(("arbitrary",), 48),
        name="moe_sorted",
    )(ea, eb, nu, xs, g_ffn.reshape(1, d), w_gu, w_dn, w_gu, w_dn)


def _hier_moe_residual(x1e, cls, g_ffn, w_gu, w_dn):
    n = x1e.shape[0]
    moe_tile = min(MOE_TILE, n)
    n_tiles = n // moe_tile + N_CLASSES
    n_tiles_pad = -(-n_tiles // V7X_LANES) * V7X_LANES
    pos, cnt, off, tiles = _sort(cls, moe_tile, n_tiles_pad)
    xs = _permute(pos, cnt, off, x1e, n_tiles * moe_tile, moe_tile)
    ys = _moe(tiles[0, :n_tiles], tiles[1, :n_tiles], tiles[2, :1], xs, g_ffn, w_gu, w_dn, moe_tile)
    return _unpermute(pos, ys, n)


def _rotary_tables(pos):
    half = R_DK // 2
    inv = 1.0 / (ROPE_BASE ** jnp.linspace(0.0, 1.0, half, dtype=F32))
    ang = pos.astype(F32)[:, None] * inv[None, :]
    cos = jnp.cos(ang)
    sin = jnp.sin(ang)
    return jnp.concatenate([cos, cos], axis=-1), jnp.concatenate([-sin, sin], axis=-1)


def _token_group(x, pos, lw, lam_init, log_gamma, state0, cache):
    bsz, t_len, d = x.shape
    x2d = x.reshape(bsz * t_len, d)
    cos_tab, sin_tab = _rotary_tables(pos)
    tm = _row_tile(bsz * t_len)
    if t_len < tm:
        reps = tm // t_len
        cos_tab = jnp.tile(cos_tab, (reps, 1))
        sin_tab = jnp.tile(sin_tab, (reps, 1))
    rqk, rv, rgs, dq, kb, vt, dk_rows, dv_rows, sgr, sgd = _in_proj(
        x2d, lw["g_mix"], lw["w_in"], cos_tab, sin_tab, lw["g_q"], lw["g_k"])
    ogr, s_fin = _retention(rqk, rv, rgs, state0, lw["g_ret"], log_gamma, bsz, t_len)
    if cache is None:
        od = _attn_prompt(dq, kb, vt, lw["g_q"], lw["g_k"], lw["lam_p"], lw["g_sub"], lam_init,
                          bsz, t_len)
    else:
        od = _attn_sample(dq, dk_rows, dv_rows, _to_cache_rows(cache[0]), _to_cache_rows(cache[1]),
                          lw["lam_p"], lw["g_sub"], lam_init, bsz, t_len)
    x1e, cls = _merge(ogr, od, sgr, sgd, x2d, lw["w_ret_o"], lw["w_diff_o"], lw["w_out"],
                      lw["g_ffn"], lw["rw_hi"], lw["rw_lo"], lw["rb"])
    y = _hier_moe_residual(x1e, cls, lw["g_ffn"], lw["w_gu"], lw["w_dn"])
    return (y.reshape(bsz, t_len, d), _from_cache_rows(dk_rows, bsz, t_len),
            _from_cache_rows(dv_rows, bsz, t_len), s_fin)


def _to_cache_rows(c):
    b, p, h, w = c.shape
    halves = w // V7X_LANES
    return c.reshape(b * p, h, halves, V7X_LANES).transpose(0, 2, 1, 3).reshape(b * p * h * halves, V7X_LANES)


def _from_cache_rows(rows, bsz, t_len):
    halves = D_DV // V7X_LANES
    r = rows.reshape(bsz * t_len, halves, D_HEADS, V7X_LANES).transpose(0, 2, 1, 3)
    return r.reshape(bsz, t_len, D_HEADS, D_DV)


def _layer_weights(l, g_mix, w_in, g_q, g_k, lambda_q1, lambda_k1, lambda_q2, lambda_k2, g_ret,
                   w_ret_o, g_sub, w_diff_o, w_out, g_ffn, w_group, b_group, w_expert, b_expert,
                   w_gate, w_up, w_down):
    d = w_in.shape[1]
    n_r = N_GROUPS + N_EXPERTS
    rw = jnp.concatenate([w_group[l], w_expert[l]], axis=1).astype(F32).T
    rw = jnp.zeros((CLS_PAD, d), F32).at[:n_r].set(rw)
    rw_hi = rw.astype(BF16)
    rw_lo = (rw - rw_hi.astype(F32)).astype(BF16)
    rb = jnp.concatenate([b_group[l], b_expert[l]]).astype(F32)
    rb = jnp.zeros((CLS_PAD,), F32).at[:n_r].set(rb)
    rb = jnp.broadcast_to(rb[:, None], (CLS_PAD, V7X_LANES))
    return dict(
        g_mix=g_mix[l], w_in=w_in[l].astype(BF16), g_q=g_q[l], g_k=g_k[l],
        lam_p=jnp.stack([lambda_q1[l], lambda_k1[l], lambda_q2[l], lambda_k2[l]]).astype(F32),
        g_ret=g_ret[l], w_ret_o=w_ret_o[l].astype(BF16), g_sub=g_sub[l],
        w_diff_o=w_diff_o[l].astype(BF16), w_out=w_out[l].astype(BF16), g_ffn=g_ffn[l],
        rw_hi=rw_hi, rw_lo=rw_lo, rb=rb,
        w_gu=jnp.concatenate([w_gate[l], w_up[l]], axis=-1).astype(BF16),
        w_dn=w_down[l].astype(BF16))


def kernel(x_prompt, x_sample, cache_k, cache_v, state_ret, g_mix, w_in, g_q, g_k, lambda_q1, lambda_k1, lambda_q2, lambda_k2, g_ret, w_ret_o, g_sub, w_diff_o, w_out, g_ffn, w_group, b_group, w_expert, b_expert, w_gate, w_up, w_down):
    depth = w_in.shape[0]
    bp, tp, _ = x_prompt.shape
    bs, ts, _ = x_sample.shape
    past = cache_k.shape[2]
    log_gamma = jnp.log1p(-jnp.exp2(-5.0 - jnp.arange(R_HEADS, dtype=F32)))
    pos_p = jnp.arange(tp, dtype=jnp.int32)
    pos_s = past + jnp.arange(ts, dtype=jnp.int32)
    zero_state = jnp.zeros((bp, R_HEADS, R_DK, R_DV), F32)
    xp, xs = x_prompt, x_sample
    outs = [[] for _ in range(6)]
    for l in range(depth):
        lam_init = 0.8 - 0.6 * math.exp(-0.3 * l)
        lw = _layer_weights(l, g_mix, w_in, g_q, g_k, lambda_q1, lambda_k1, lambda_q2, lambda_k2,
                            g_ret, w_ret_o, g_sub, w_diff_o, w_out, g_ffn, w_group, b_group,
                            w_expert, b_expert, w_gate, w_up, w_down)
        xp, kp, vp, sp = _token_group(xp, pos_p, lw, lam_init, log_gamma, zero_state, None)
        cache = (cache_k[l], cache_v[l])
        xs, ks, vs, ss = _token_group(xs, pos_s, lw, lam_init, log_gamma,
                                      state_ret[l].astype(F32), cache)
        for lst, val in zip(outs, (kp, vp, sp, ks, vs, ss)):
            lst.append(val)
    return (xp, xs) + tuple(jnp.stack(o) for o in outs)
```

```python
import functools
import math

import jax
import jax.numpy as jnp
from jax import lax
from jax.experimental import pallas as pl
from jax.experimental.pallas import tpu as pltpu

F32 = jnp.float32
BF16 = jnp.bfloat16
I32 = jnp.int32

CHUNK = 64
EPS = 1e-6
R_HEADS = 4
R_DK = 128
R_DV = 256
ROPE_BASE = 10000.0
D_HEADS = 4
D_DH = 128
D_DV = 256
N_GROUPS = 4
EXP_PER_GROUP = 4
N_EXPERTS = N_GROUPS * EXP_PER_GROUP
N_PAIRS = EXP_PER_GROUP * (EXP_PER_GROUP - 1) // 2
N_CLASSES = N_GROUPS * N_PAIRS
D_FF = 512
SEG = 1024
N_SEG = 8

V7X_LANES = 128
V7X_SUBLANES = 8
V7X_VMEM_LIMIT_BYTES = 56 * 1024 * 1024

ROW_TILE = 512
RET_SUPER = 512
ATT_TILE = 512
MERGE_TILES_PER_STEP = 2
SAMPLE_CHUNK = 1024
MOE_TILE = 256
ROW_DMA_UNROLL = 8
CLS_PAD = 32
NEG_BIG = -1e30
LOG2E = math.log2(math.e)
SCORE_BOUND_MARGIN = 1.01
MAX_SHIFT_LOG2 = 100.0
PAY = V7X_LANES


def _row_tile(n):
    return min(ROW_TILE, n)


def _cparams(sem, vmem_mb=None):
    kw = dict(dimension_semantics=sem)
    if vmem_mb is not None:
        kw["vmem_limit_bytes"] = min(vmem_mb * 1024 * 1024, V7X_VMEM_LIMIT_BYTES)
    return pltpu.CompilerParams(**kw)


def _resident(shape):
    nd = len(shape)
    return pl.BlockSpec(shape, lambda *_: (0,) * nd, pipeline_mode=pl.Buffered(1))


def _rms(x, eps=EPS):
    return x * lax.rsqrt(jnp.mean(x * x, axis=-1, keepdims=True) + eps)


def _sigmoid(x):
    return 1.0 / (1.0 + jnp.exp(-x))


def _dot(a, b):
    return jnp.dot(a, b, preferred_element_type=F32)


def _dot_nt(a, b):
    return lax.dot_general(a, b, (((1,), (1,)), ((), ())), preferred_element_type=F32)


def _dot_tn(a, b):
    return lax.dot_general(a, b, (((0,), (0,)), ((), ())), preferred_element_type=F32)


def _store_cache_rows(ref, z):
    rows = z.shape[0]
    for half in range(2):
        for h in range(D_HEADS):
            lo = h * D_DV + half * V7X_LANES
            ref[pl.ds(half * D_HEADS + h, rows, stride=2 * D_HEADS), :] = z[:, lo:lo + V7X_LANES]


def _inproj_kernel(x_ref, g_ref, w_ref, cos_ref, sin_ref, gq_ref, gk_ref,
                   rqk_ref, rv_ref, rgs_ref, dq_ref, kb_ref, vt_ref, dk_ref, dv_ref, sgr_ref, sgd_ref):
    x = x_ref[...]
    hb = (_rms(x) * g_ref[...]).astype(BF16)

    def seg(s):
        return _dot(hb, w_ref[:, s * SEG:(s + 1) * SEG])

    cos = cos_ref[...]
    sin = sin_ref[...]
    z = seg(0)
    for j in range(2 * R_HEADS):
        v = z[:, j * R_DK:(j + 1) * R_DK]
        r = v * cos + pltpu.roll(v, R_DK // 2, 1) * sin
        if j >= R_HEADS:
            r = r * (R_DK ** -0.5)
        rqk_ref[:, j * R_DK:(j + 1) * R_DK] = r.astype(BF16)

    rv_ref[...] = seg(1).astype(BF16)
    z = seg(2)
    rgs_ref[...] = (z * _sigmoid(z)).astype(BF16)

    z = seg(3)
    gq = gq_ref[...] * (D_DH ** -0.5 * LOG2E)
    for j in range(2 * D_HEADS):
        v = z[:, j * D_DH:(j + 1) * D_DH]
        dq_ref[:, j * D_DH:(j + 1) * D_DH] = (_rms(v) * gq).astype(BF16)
    z = seg(4)
    gk = gk_ref[...]
    kn = jnp.concatenate([_rms(z[:, j * D_DH:(j + 1) * D_DH]) * gk for j in range(2 * D_HEADS)], axis=1)
    kb_ref[...] = kn.astype(BF16)
    _store_cache_rows(dk_ref, kn)
    z = seg(5)
    _store_cache_rows(dv_ref, z)
    vt_ref[0] = z.T.astype(BF16)
    sgr_ref[...] = _sigmoid(seg(6)).astype(BF16)
    sgd_ref[...] = _sigmoid(seg(7)).astype(BF16)


def _in_proj(x2d, g_mix, w_in_bf, cos_tab, sin_tab, g_q, g_k):
    n, d = x2d.shape
    tm = _row_tile(n)
    nt = n // tm
    ntab = cos_tab.shape[0] // tm
    row = lambda w: pl.BlockSpec((tm, w), lambda i: (i, 0))
    tab = pl.BlockSpec((tm, R_DK), lambda i: (i % ntab, 0))
    flat = jax.ShapeDtypeStruct((n, SEG), BF16)
    cache_rows = jax.ShapeDtypeStruct((n * 2 * D_HEADS, V7X_LANES), F32)
    cache_spec = pl.BlockSpec((tm * 2 * D_HEADS, V7X_LANES), lambda i: (i, 0))
    outs = [flat, flat, flat, flat, flat, jax.ShapeDtypeStruct((nt, SEG, tm), BF16),
            cache_rows, cache_rows, flat, flat]
    return pl.pallas_call(
        _inproj_kernel,
        out_shape=outs,
        grid=(nt,),
        in_specs=[row(d), _resident((1, d)), _resident(w_in_bf.shape), tab, tab,
                  _resident((1, D_DH)), _resident((1, D_DH))],
        out_specs=[row(SEG)] * 5 + [pl.BlockSpec((1, SEG, tm), lambda i: (i, 0, 0)),
                                    cache_spec, cache_spec, row(SEG), row(SEG)],
        compiler_params=_cparams(("arbitrary",), 56),
        name="in_proj",
    )(x2d, g_mix.reshape(1, d), w_in_bf, cos_tab, sin_tab,
      g_q.reshape(1, D_DH), g_k.reshape(1, D_DH))


def _ret_kernel(lg_ref, qk_ref, v_ref, rgs_ref, s0_ref, g_ref,
                o_ref, sfin_ref, s_scr, d_scr, lam_scr, wk_scr, dec_scr, *, c_len):
    c = pl.program_id(1)

    @pl.when((pl.program_id(0) == 0) & (c == 0))
    def _():
        t = lax.broadcasted_iota(I32, (c_len, c_len), 0)
        s = lax.broadcasted_iota(I32, (c_len, c_len), 1)
        dist = jnp.abs(t - s).astype(F32)
        vis = (s // CHUNK) <= (t // CHUNK)
        pos_v = lax.broadcasted_iota(I32, (c_len, R_DV), 0).astype(F32)
        pos_k = lax.broadcasted_iota(I32, (c_len, R_DK), 0).astype(F32)
        for h in range(R_HEADS):
            lg = lg_ref[h]
            d_scr[h] = jnp.where(vis, jnp.exp(lg * dist), 0.0)
            lam_scr[h] = jnp.exp(lg * (pos_v + 1.0))
            wk_scr[h] = jnp.exp(lg * (c_len - 1.0 - pos_k))
            dec_scr[h] = jnp.exp(lg * jnp.full((V7X_SUBLANES, R_DV), float(c_len), F32))

    @pl.when(c == 0)
    def _():
        s_scr[...] = s0_ref[0]

    for h in range(R_HEADS):
        q = qk_ref[0, :, h * R_DK:(h + 1) * R_DK]
        k = qk_ref[0, :, (R_HEADS + h) * R_DK:(R_HEADS + h + 1) * R_DK]
        v = v_ref[0, :, h * R_DV:(h + 1) * R_DV]
        state = s_scr[h]
        s = _dot_nt(q, k) * d_scr[h]
        o = _dot(s.astype(BF16), v) + lam_scr[h] * _dot(q, state.astype(BF16))
        kw = (k.astype(F32) * wk_scr[h]).astype(BF16)
        s_scr[h] = state * dec_scr[h][0:1, :] + _dot_tn(kw, v)
        gate = rgs_ref[0, :, h * R_DV:(h + 1) * R_DV].astype(F32)
        o_ref[0, :, h * R_DV:(h + 1) * R_DV] = (_rms(o) * g_ref[...] * gate).astype(BF16)

    @pl.when(c == pl.num_programs(1) - 1)
    def _():
        sfin_ref[0] = s_scr[...]


def _retention(rqk, rv, rgs, state0, g_ret, log_gamma, bsz, t_len):
    c_len = min(RET_SUPER, t_len)
    nc = t_len // c_len
    rqk3 = rqk.reshape(bsz, t_len, SEG)
    rv3 = rv.reshape(bsz, t_len, SEG)
    rgs3 = rgs.reshape(bsz, t_len, SEG)
    rows = pl.BlockSpec((1, c_len, SEG), lambda b, c, lg: (b, c, 0))
    sspec = pl.BlockSpec((1, R_HEADS, R_DK, R_DV), lambda b, c, lg: (b, 0, 0, 0))
    gspec = pl.BlockSpec((1, R_DV), lambda b, c, lg: (0, 0))
    o, sfin = pl.pallas_call(
        functools.partial(_ret_kernel, c_len=c_len),
        out_shape=[jax.ShapeDtypeStruct((bsz, t_len, SEG), BF16),
                   jax.ShapeDtypeStruct((bsz, R_HEADS, R_DK, R_DV), F32)],
        grid_spec=pltpu.PrefetchScalarGridSpec(
            num_scalar_prefetch=1,
            grid=(bsz, nc),
            in_specs=[rows, rows, rows, sspec, gspec],
            out_specs=[rows, sspec],
            scratch_shapes=[pltpu.VMEM((R_HEADS, R_DK, R_DV), F32),
                            pltpu.VMEM((R_HEADS, c_len, c_len), F32),
                            pltpu.VMEM((R_HEADS, c_len, R_DV), F32),
                            pltpu.VMEM((R_HEADS, c_len, R_DK), F32),
                            pltpu.VMEM((R_HEADS, V7X_SUBLANES, R_DV), F32)],
        ),
        compiler_params=_cparams(("arbitrary",) * 2, 32),
        name="retention",
    )(log_gamma, rqk3, rv3, rgs3, state0, g_ret.reshape(1, R_DV))
    return o.reshape(bsz * t_len, SEG), sfin


def _lambda_full(lp_ref, lam_init):
    lp = lp_ref[...]
    a = jnp.sum(lp[0:1] * lp[1:2], axis=-1, keepdims=True)
    b = jnp.sum(lp[2:3] * lp[3:4], axis=-1, keepdims=True)
    return jnp.exp(a) - jnp.exp(b) + lam_init


def _attn_finish(o1, l1, o2, l2, lam, g, lam_init):
    o = o1 / l1 - lam * (o2 / l2)
    return (_rms(o) * g * (1.0 - lam_init)).astype(BF16)


def _attn_prompt_kernel(bound_ref, q_ref, k_ref, vt_ref, lp_ref, g_ref, o_ref,
                        m1, l1, a1, m2, l2, a2, bias, *, tile, lam_init, fixed_shift):
    i = pl.program_id(2)

    q = q_ref[0]
    q1 = q[:, :D_DH]
    q2 = q[:, D_DH:]
    for m, l, a in ((m1, l1, a1), (m2, l2, a2)):
        m[...] = jnp.full(m.shape, NEG_BIG, F32)
        l[...] = jnp.zeros(l.shape, F32)
        a[...] = jnp.zeros(a.shape, F32)

    def update(s, vts, m, l, a):
        if fixed_shift:
            p = jnp.exp2(s - bound_ref[0])
            l[...] += jnp.sum(p, axis=0, keepdims=True)
            a[...] += _dot(vts, p.astype(BF16))
            return
        m_old = m[...]
        m_new = jnp.maximum(m_old, jnp.max(s, axis=0, keepdims=True))
        alpha = jnp.exp2(m_old - m_new)
        p = jnp.exp2(s - m_new)
        l[...] = alpha * l[...] + jnp.sum(p, axis=0, keepdims=True)
        a[...] = alpha * a[...] + _dot(vts, p.astype(BF16))
        m[...] = m_new

    def block(j, mask):
        ks = k_ref[0, pl.ds(pl.multiple_of(j * tile, tile), tile), :]
        vts = vt_ref[0, j]
        s1 = _dot_nt(ks[:, :D_DH], q1)
        s2 = _dot_nt(ks[:, D_DH:], q2)
        if mask is not None:
            s1 = s1 + mask
            s2 = s2 + mask
        update(s1, vts, m1, l1, a1)
        update(s2, vts, m2, l2, a2)

    def body(jj, carry):
        block(2 * jj, None)
        block(2 * jj + 1, None)
        return carry

    lax.fori_loop(0, i // 2, body, 0)

    @pl.when(i % 2 == 1)
    def _():
        block(i - 1, None)

    @pl.when((pl.program_id(0) == 0) & (pl.program_id(1) == 0) & (i == 0))
    def _():
        kk = lax.broadcasted_iota(I32, (tile, tile), 0)
        qq = lax.broadcasted_iota(I32, (tile, tile), 1)
        bias[...] = jnp.where((kk // CHUNK) <= (qq // CHUNK), 0.0, NEG_BIG)

    block(i, bias[...])

    lam = _lambda_full(lp_ref, lam_init)
    ot = a1[...] * (1.0 / l1[...]) - lam * (a2[...] * (1.0 / l2[...]))
    o_ref[0] = (_rms(ot.T) * g_ref[...] * (1.0 - lam_init)).astype(BF16)


def _attn_prompt(dq, kb, vt, g_q, g_k, lam_p, g_sub, lam_init, bsz, t_len):
    tile = vt.shape[2]
    assert t_len % tile == 0 and tile % CHUNK == 0
    nq = t_len // tile
    w = 2 * D_DH
    q3 = dq.reshape(bsz, t_len, SEG)
    k3 = kb.reshape(bsz, t_len, SEG)
    vt4 = vt.reshape(bsz, nq, SEG, tile)
    qspec = pl.BlockSpec((1, tile, w), lambda b, h, i: (b, i, h))
    kspec = pl.BlockSpec((1, t_len, w), lambda b, h, i: (b, 0, h))
    vspec = pl.BlockSpec((1, nq, D_DV, tile), lambda b, h, i: (b, 0, h, 0))
    col = lambda: pltpu.VMEM((1, tile), F32)
    acc = lambda: pltpu.VMEM((D_DV, tile), F32)

    def call(fixed_shift):
        return pl.pallas_call(
            functools.partial(_attn_prompt_kernel, tile=tile, lam_init=lam_init, fixed_shift=fixed_shift),
            out_shape=jax.ShapeDtypeStruct((bsz, t_len, SEG), BF16),
            grid=(bsz, D_HEADS, nq),
            in_specs=[pl.BlockSpec(memory_space=pltpu.SMEM), qspec, kspec, vspec,
                      pl.BlockSpec((4, D_DH), lambda b, h, i: (0, 0)),
                      pl.BlockSpec((1, D_DV), lambda b, h, i: (0, 0))],
            out_specs=qspec,
            scratch_shapes=[col(), col(), acc(), col(), col(), acc(), pltpu.VMEM((tile, tile), F32)],
            compiler_params=_cparams(("arbitrary",) * 3, 32),
            name="attn_prompt_fixed" if fixed_shift else "attn_prompt_online",
        )

    bound = SCORE_BOUND_MARGIN * D_DH * (D_DH ** -0.5 * LOG2E) * jnp.max(jnp.abs(g_q * g_k))
    args = (bound.reshape(1).astype(F32), q3, k3, vt4, lam_p, g_sub.reshape(1, D_DV))
    out = lax.cond(2.0 * bound < MAX_SHIFT_LOG2,
                   lambda *a: call(True)(*a), lambda *a: call(False)(*a), *args)
    return out.reshape(bsz * t_len, SEG)


def _attn_sample_kernel(q_ref, kc_ref, vc_ref, kn_ref, vn_ref, lp_ref, g_ref, o_ref,
                        m_scr, l_scr, a_scr, *, lam_init, chunk, t_new):
    c = pl.program_id(1)

    @pl.when(c == 0)
    def _():
        m_scr[...] = jnp.full(m_scr.shape, NEG_BIG, F32)
        l_scr[...] = jnp.zeros(l_scr.shape, F32)
        a_scr[...] = jnp.zeros(a_scr.shape, F32)

    q = q_ref[0]

    def head_rows(ref, h, half, n):
        return ref[pl.ds(half * D_HEADS + h, n, stride=2 * D_HEADS), :].astype(BF16)

    def absorb(k_src, v_src, n):
        for h in range(D_HEADS):
            v = jnp.concatenate([head_rows(v_src, h, 0, n), head_rows(v_src, h, 1, n)], axis=1)
            for mp in range(2):
                idx = 2 * h + mp
                lo = h * 2 * D_DH + mp * D_DH
                s = _dot_nt(q[:, lo:lo + D_DH], head_rows(k_src, h, mp, n))
                m_old = m_scr[idx]
                m_new = jnp.maximum(m_old, jnp.max(s, axis=-1, keepdims=True))
                alpha = jnp.exp2(m_old - m_new)
                p = jnp.exp2(s - m_new)
                l_scr[idx] = alpha * l_scr[idx] + jnp.sum(p, axis=-1, keepdims=True)
                a_scr[idx] = alpha * a_scr[idx] + _dot(p.astype(BF16), v)
                m_scr[idx] = m_new

    absorb(kc_ref, vc_ref, chunk)

    @pl.when(c == pl.num_programs(1) - 1)
    def _():
        absorb(kn_ref, vn_ref, t_new)
        lam = _lambda_full(lp_ref, lam_init)
        for h in range(D_HEADS):
            o_ref[0, :, h * D_DV:(h + 1) * D_DV] = _attn_finish(
                a_scr[2 * h], l_scr[2 * h], a_scr[2 * h + 1], l_scr[2 * h + 1], lam, g_ref[...], lam_init)


def _attn_sample(dq, dk_rows, dv_rows, cache_k_rows, cache_v_rows, lam_p, g_sub, lam_init, bsz, t_len):
    rpf = 2 * D_HEADS
    past = cache_k_rows.shape[0] // (bsz * rpf)
    chunk = min(SAMPLE_CHUNK, past)
    nc = past // chunk
    q3 = dq.reshape(bsz, t_len, SEG)
    qspec = pl.BlockSpec((1, t_len, SEG), lambda b, c: (b, 0, 0))
    cspec = pl.BlockSpec((chunk * rpf, V7X_LANES), lambda b, c: (b * nc + c, 0))
    nspec = pl.BlockSpec((t_len * rpf, V7X_LANES), lambda b, c: (b, 0))
    n_stat = 2 * D_HEADS
    out = pl.pallas_call(
        functools.partial(_attn_sample_kernel, lam_init=lam_init, chunk=chunk, t_new=t_len),
        out_shape=jax.ShapeDtypeStruct((bsz, t_len, SEG), BF16),
        grid=(bsz, nc),
        in_specs=[qspec, cspec, cspec, nspec, nspec,
                  pl.BlockSpec((4, D_DH), lambda b, c: (0, 0)),
                  pl.BlockSpec((1, D_DV), lambda b, c: (0, 0))],
        out_specs=qspec,
        scratch_shapes=[pltpu.VMEM((n_stat, t_len, 1), F32), pltpu.VMEM((n_stat, t_len, 1), F32),
                        pltpu.VMEM((n_stat, t_len, D_DV), F32)],
        compiler_params=_cparams(("arbitrary",) * 2, 32),
        name="attn_sample",
    )(q3, cache_k_rows, cache_v_rows, dk_rows, dv_rows, lam_p, g_sub.reshape(1, D_DV))
    return out.reshape(bsz * t_len, SEG)


def _split_hi_lo(x):
    hi = x.astype(BF16)
    lo = (x - hi.astype(F32)).astype(BF16)
    return hi, lo


def _merge_kernel(*refs):
    cls_ref = refs[-1]
    for sub in range(cls_ref.shape[0]):
        _merge_tile(sub, cls_ref.shape[2], *refs)


def _merge_tile(sub, tm, ogr_ref, od_ref, sgr_ref, sgd_ref, x_ref, wr_ref, wd_ref, wo_ref,
                gf_ref, rw_hi_ref, rw_lo_ref, rb_ref, x1e_ref, cls_ref):
    rs = slice(sub * tm, (sub + 1) * tm)
    y_r = _dot(ogr_ref[rs, :], wr_ref[...])
    y_d = _dot(od_ref[rs, :], wd_ref[...])
    m = sgr_ref[rs, :].astype(F32) * y_r + sgd_ref[rs, :].astype(F32) * y_d
    x1 = x_ref[rs, :] + _dot(m.astype(BF16), wo_ref[...])

    hn = _rms(x1) * gf_ref[...]
    h_hi, h_lo = _split_hi_lo(hn)
    w_hi = rw_hi_ref[...]
    lt = _dot_nt(w_hi, h_hi) + _dot_nt(rw_lo_ref[...], h_hi) + _dot_nt(w_hi, h_lo)
    lt = lt + rb_ref[...][:, 0:1]
    g = [lt[i:i + 1, :] for i in range(N_GROUPS)]
    e = [lt[N_GROUPS + i:N_GROUPS + i + 1, :] for i in range(N_EXPERTS)]

    gmax = functools.reduce(jnp.maximum, g)
    gid = jnp.full(g[0].shape, N_GROUPS - 1, I32)
    for i in range(N_GROUPS - 2, -1, -1):
        gid = jnp.where(g[i] == gmax, i, gid)
    g_w = 1.0 / functools.reduce(lambda a, b: a + b, [jnp.exp(v - gmax) for v in g])

    es = []
    for j in range(EXP_PER_GROUP):
        v = e[(N_GROUPS - 1) * EXP_PER_GROUP + j]
        for i in range(N_GROUPS - 2, -1, -1):
            v = jnp.where(gid == i, e[i * EXP_PER_GROUP + j], v)
        es.append(v)

    def first_argmax(vals):
        mx = functools.reduce(jnp.maximum, vals)
        idx = jnp.full(mx.shape, len(vals) - 1, I32)
        for i in range(len(vals) - 2, -1, -1):
            idx = jnp.where(vals[i] == mx, i, idx)
        return mx, idx

    l1, i1 = first_argmax(es)
    rest = [jnp.where(i1 == j, -jnp.inf, es[j]) for j in range(EXP_PER_GROUP)]
    l2, i2 = first_argmax(rest)
    t = jnp.exp(l2 - l1)
    c1 = g_w / (1.0 + t)
    c2 = g_w * t / (1.0 + t)
    lo = jnp.minimum(i1, i2)
    hi = jnp.maximum(i1, i2)
    base = jnp.where(lo == 0, 0, jnp.where(lo == 1, EXP_PER_GROUP - 1, 2 * EXP_PER_GROUP - 3))
    cls = gid * N_PAIRS + base + hi - lo - 1
    wa = jnp.where(i1 < i2, c1, c2)
    wb = jnp.where(i1 < i2, c2, c1)

    cls_ref[sub] = cls
    rid = lax.broadcasted_iota(I32, (PAY, tm), 0)
    pay = jnp.where(rid == 0, wa, jnp.where(rid == 1, wb, 0.0))
    d = x_ref.shape[1]
    x1e_ref[rs, :d] = x1
    x1e_ref[rs, d:] = pay.T


def _merge(ogr, od, sgr, sgd, x2d, w_ret_o, w_diff_o, w_out, g_ffn, rw_hi, rw_lo, rb):
    n, d = x2d.shape
    tm = _row_tile(n)
    nt = n // tm
    n_sub = MERGE_TILES_PER_STEP if nt % MERGE_TILES_PER_STEP == 0 else 1
    row = lambda w: pl.BlockSpec((n_sub * tm, w), lambda i: (i, 0))
    return pl.pallas_call(
        _merge_kernel,
        out_shape=[jax.ShapeDtypeStruct((n, d + PAY), F32),
                   jax.ShapeDtypeStruct((nt, 1, tm), I32)],
        grid=(nt // n_sub,),
        in_specs=[row(SEG), row(SEG), row(SEG), row(SEG), row(d),
                  _resident(w_ret_o.shape), _resident(w_diff_o.shape), _resident(w_out.shape),
                  _resident((1, d)), _resident(rw_hi.shape), _resident(rw_lo.shape),
                  _resident(rb.shape)],
        out_specs=[row(d + PAY), pl.BlockSpec((n_sub, 1, tm), lambda i: (i, 0, 0))],
        compiler_params=_cparams(("arbitrary",), 48),
        name="merge_route",
    )(ogr, od, sgr, sgd, x2d, w_ret_o, w_diff_o, w_out, g_ffn.reshape(1, d), rw_hi, rw_lo, rb)


def _prefix_excl(x):
    rid = lax.broadcasted_iota(I32, x.shape, 0)
    inc = x
    s = 1
    while s < x.shape[0]:
        inc = inc + jnp.where(rid >= s, pltpu.roll(inc, s, 0), 0.0)
        s *= 2
    return inc - x


def _sort_kernel(cls_ref, pos_ref, cnt_ref, off_ref, tiles_ref, cnt_scr, run_scr, off_scr,
                 *, moe_tile, n_tiles_pad):
    ph = pl.program_id(0)
    t = pl.program_id(1)
    tm = cls_ref.shape[2]
    cls = cls_ref[0]
    cid = lax.broadcasted_iota(I32, (CLS_PAD, tm), 0)
    onehot = (cid == cls)

    @pl.when(jnp.logical_and(ph == 0, t == 0))
    def _():
        cnt_scr[...] = jnp.zeros(cnt_scr.shape, F32)

    @pl.when(ph == 0)
    def _():
        cnt_scr[...] += jnp.sum(onehot.astype(F32), axis=-1, keepdims=True)

    @pl.when(jnp.logical_and(ph == 1, t == 0))
    def _():
        cnt = cnt_scr[...]
        padded = jnp.ceil(cnt / moe_tile) * moe_tile
        off = _prefix_excl(padded)
        off_scr[...] = off
        run_scr[...] = jnp.zeros(run_scr.shape, F32)
        cnt_ref[...] = cnt.astype(I32)
        off_ref[...] = off.astype(I32)
        end = (off + padded)[:, 0:1]
        total = jnp.max(end, axis=0, keepdims=True)
        n_used = total / moe_tile
        p = lax.broadcasted_iota(I32, (1, n_tiles_pad), 1).astype(F32)
        start = jnp.minimum(p, n_used - 1.0) * moe_tile
        cid2 = lax.broadcasted_iota(I32, (CLS_PAD, n_tiles_pad), 0)
        before = jnp.logical_and(end <= start, cid2 < N_CLASSES)
        tcls = jnp.sum(before.astype(F32), axis=0, keepdims=True).astype(I32)
        grp = tcls // N_PAIRS
        pr = tcls - grp * N_PAIRS
        lo = (pr >= EXP_PER_GROUP - 1).astype(I32) + (pr >= 2 * EXP_PER_GROUP - 3).astype(I32)
        base = jnp.where(lo == 0, 0, jnp.where(lo == 1, EXP_PER_GROUP - 1, 2 * EXP_PER_GROUP - 3))
        hi = pr - base + lo + 1
        rid = lax.broadcasted_iota(I32, (V7X_SUBLANES, n_tiles_pad), 0)
        ea = grp * EXP_PER_GROUP + lo
        eb = grp * EXP_PER_GROUP + hi
        nu = jnp.broadcast_to(n_used.astype(I32), (1, n_tiles_pad))
        tiles_ref[...] = jnp.where(rid == 0, ea, jnp.where(rid == 1, eb, jnp.where(rid == 2, nu, 0)))

    @pl.when(ph == 1)
    def _():
        r = lax.broadcasted_iota(I32, (tm, tm), 0)
        c = lax.broadcasted_iota(I32, (tm, tm), 1)
        upper = jnp.where(r <= c, 1.0, 0.0).astype(BF16)
        oh = jnp.where(onehot, 1.0, 0.0)
        incl = _dot(oh.astype(BF16), upper)
        slot = off_scr[...][:, 0:1] + run_scr[...][:, 0:1] + incl - 1.0
        pos_ref[0] = jnp.sum(oh * slot, axis=0, keepdims=True).astype(I32)
        run_scr[...] += jnp.sum(oh, axis=-1, keepdims=True)


def _sort(cls, moe_tile, n_tiles_pad):
    nt, _, tm = cls.shape
    blk = pl.BlockSpec((1, 1, tm), lambda ph, t: (t, 0, 0))
    oblk = pl.BlockSpec((1, 1, tm), lambda ph, t: (t * ph, 0, 0))
    whole = lambda shape: pl.BlockSpec(shape, lambda ph, t: (0, 0))
    return pl.pallas_call(
        functools.partial(_sort_kernel, moe_tile=moe_tile, n_tiles_pad=n_tiles_pad),
        out_shape=[jax.ShapeDtypeStruct((nt, 1, tm), I32),
                   jax.ShapeDtypeStruct((CLS_PAD, V7X_LANES), I32),
                   jax.ShapeDtypeStruct((CLS_PAD, V7X_LANES), I32),
                   jax.ShapeDtypeStruct((V7X_SUBLANES, n_tiles_pad), I32)],
        grid=(2, nt),
        in_specs=[blk],
        out_specs=[oblk, whole((CLS_PAD, V7X_LANES)), whole((CLS_PAD, V7X_LANES)),
                   whole((V7X_SUBLANES, n_tiles_pad))],
        scratch_shapes=[pltpu.VMEM((CLS_PAD, V7X_LANES), F32)] * 3,
        compiler_params=_cparams(("arbitrary",) * 2),
        name="class_sort",
    )(cls)


def _permute_kernel(pos_ref, cnt_ref, off_ref, src_ref, dst_ref, zblk, sem, zsem, *, moe_tile):
    t = pl.program_id(0)
    tm = pos_ref.shape[2]

    def row_copy(r):
        return pltpu.make_async_copy(src_ref.at[pl.ds(r, 1)],
                                     dst_ref.at[pl.ds(pos_ref[0, 0, r], 1)], sem)

    def start(r, carry):
        row_copy(r).start()
        return carry

    def wait(r, carry):
        row_copy(r).wait()
        return carry

    lax.fori_loop(0, tm, start, 0, unroll=ROW_DMA_UNROLL)

    @pl.when(t == 0)
    def _():
        zblk[...] = jnp.zeros(zblk.shape, zblk.dtype)
        used = 0
        for c in range(N_CLASSES):
            cnt = cnt_ref[c, 0]
            off = off_ref[c, 0]
            padded = ((cnt + moe_tile - 1) // moe_tile) * moe_tile
            used = off + padded

            def pad_copy(r, off=off):
                return pltpu.make_async_copy(zblk.at[pl.ds(0, 1)], dst_ref.at[pl.ds(off + r, 1)], zsem)

            def pad_start(r, carry, copy=pad_copy):
                copy(r).start()
                return carry

            def pad_wait(r, carry, copy=pad_copy):
                copy(r).wait()
                return carry

            lax.fori_loop(cnt, padded, pad_start, 0)
            lax.fori_loop(cnt, padded, pad_wait, 0)

        def tile_copy(p):
            return pltpu.make_async_copy(zblk, dst_ref.at[pl.ds(p * moe_tile, moe_tile)], zsem)

        def tile_start(p, carry):
            tile_copy(p).start()
            return carry

        def tile_wait(p, carry):
            tile_copy(p).wait()
            return carry

        first, last = used // moe_tile, dst_ref.shape[0] // moe_tile
        lax.fori_loop(first, last, tile_start, 0)
        lax.fori_loop(first, last, tile_wait, 0)

    lax.fori_loop(0, tm, wait, 0, unroll=ROW_DMA_UNROLL)


def _permute(pos, cnt, off, src, n_rows_out, moe_tile):
    nt, _, tm = pos.shape
    width = src.shape[1]
    smem = lambda shape, imap: pl.BlockSpec(shape, imap, memory_space=pltpu.SMEM)
    return pl.pallas_call(
        functools.partial(_permute_kernel, moe_tile=moe_tile),
        out_shape=jax.ShapeDtypeStruct((n_rows_out, width), src.dtype),
        grid=(nt,),
        in_specs=[smem((1, 1, tm), lambda t: (t, 0, 0)),
                  smem(cnt.shape, lambda t: (0, 0)),
                  smem(off.shape, lambda t: (0, 0)),
                  pl.BlockSpec((tm, width), lambda t: (t, 0))],
        out_specs=pl.BlockSpec(memory_space=pl.ANY),
        scratch_shapes=[pltpu.VMEM((moe_tile, width), src.dtype),
                        pltpu.SemaphoreType.DMA, pltpu.SemaphoreType.DMA],
        compiler_params=_cparams(("arbitrary",)),
        name="permute_rows",
    )(pos, cnt, off, src)


def _unpermute_kernel(pos_ref, src_ref, dst_ref, sem):
    tm = pos_ref.shape[2]

    def row_copy(r):
        return pltpu.make_async_copy(src_ref.at[pl.ds(pos_ref[0, 0, r], 1)],
                                     dst_ref.at[pl.ds(r, 1)], sem)

    def start(r, carry):
        row_copy(r).start()
        return carry

    def wait(r, carry):
        row_copy(r).wait()
        return carry

    lax.fori_loop(0, tm, start, 0, unroll=ROW_DMA_UNROLL)
    lax.fori_loop(0, tm, wait, 0, unroll=ROW_DMA_UNROLL)


def _unpermute(pos, src, n_rows_out):
    nt, _, tm = pos.shape
    width = src.shape[1]
    return pl.pallas_call(
        _unpermute_kernel,
        out_shape=jax.ShapeDtypeStruct((n_rows_out, width), src.dtype),
        grid=(nt,),
        in_specs=[pl.BlockSpec((1, 1, tm), lambda t: (t, 0, 0), memory_space=pltpu.SMEM),
                  pl.BlockSpec(memory_space=pl.ANY)],
        out_specs=pl.BlockSpec((tm, width), lambda t: (t, 0)),
        scratch_shapes=[pltpu.SemaphoreType.DMA],
        compiler_params=_cparams(("arbitrary",)),
        name="unpermute_rows",
    )(pos, src)


def _moe_kernel(ea_ref, eb_ref, nu_ref, xs_ref, gf_ref, wgu_a, wdn_a, wgu_b, wdn_b, y_ref):
    p = pl.program_id(0)
    d = y_ref.shape[1]

    @pl.when(p < nu_ref[0])
    def _():
        x = xs_ref[:, :d]
        wa = xs_ref[:, d:d + 1]
        wb = xs_ref[:, d + 1:d + 2]
        hn = (_rms(x) * gf_ref[...]).astype(BF16)

        def expert(wgu, wdn):
            gu = _dot(hn, wgu[0])
            gate = gu[:, :D_FF]
            he = gate * _sigmoid(gate) * gu[:, D_FF:]
            return _dot(he.astype(BF16), wdn[0])

        y_ref[...] = x + wa * expert(wgu_a, wdn_a) + wb * expert(wgu_b, wdn_b)

    @pl.when(p >= nu_ref[0])
    def _():
        y_ref[...] = jnp.zeros(y_ref.shape, y_ref.dtype)


def _moe(ea, eb, nu, xs, g_ffn, w_gu, w_dn, moe_tile):
    n_rows, width = xs.shape
    d = width - PAY
    n_tiles = n_rows // moe_tile
    used = lambda p, ea, eb, nu: (jnp.minimum(p, nu[0] - 1), 0)
    return pl.pallas_call(
        _moe_kernel,
        out_shape=jax.ShapeDtypeStruct((n_rows, d), F32),
        grid_spec=pltpu.PrefetchScalarGridSpec(
            num_scalar_prefetch=3,
            grid=(n_tiles,),
            in_specs=[pl.BlockSpec((moe_tile, width), used),
                      pl.BlockSpec((1, d), lambda p, ea, eb, nu: (0, 0)),
                      pl.BlockSpec((1, d, 2 * D_FF), lambda p, ea, eb, nu: (ea[p], 0, 0)),
                      pl.BlockSpec((1, D_FF, d), lambda p, ea, eb, nu: (ea[p], 0, 0)),
                      pl.BlockSpec((1, d, 2 * D_FF), lambda p, ea, eb, nu: (eb[p], 0, 0)),
                      pl.BlockSpec((1, D_FF, d), lambda p, ea, eb, nu: (eb[p], 0, 0))],
            out_specs=pl.BlockSpec((moe_tile, d), lambda p, ea, eb, nu: (p, 0)),
        ),
        compiler_params=_cparams(("arbitrary",), 48),
        name="moe_sorted",
    )(ea, eb, nu, xs, g_ffn.reshape(1, d), w_gu, w_dn, w_gu, w_dn)


def _hier_moe_residual(x1e, cls, g_ffn, w_gu, w_dn):
    n = x1e.shape[0]
    moe_tile = min(MOE_TILE, n)
    n_tiles = n // moe_tile + N_CLASSES
    n_tiles_pad = -(-n_tiles // V7X_LANES) * V7X_LANES
    pos, cnt, off, tiles = _sort(cls, moe_tile, n_tiles_pad)
    xs = _permute(pos, cnt, off, x1e, n_tiles * moe_tile, moe_tile)
    ys = _moe(tiles[0, :n_tiles], tiles[1, :n_tiles], tiles[2, :1], xs, g_ffn, w_gu, w_dn, moe_tile)
    return _unpermute(pos, ys, n)


def _rotary_tables(pos):
    half = R_DK // 2
    inv = 1.0 / (ROPE_BASE ** jnp.linspace(0.0, 1.0, half, dtype=F32))
    ang = pos.astype(F32)[:, None] * inv[None, :]
    cos = jnp.cos(ang)
    sin = jnp.sin(ang)
    return jnp.concatenate([cos, cos], axis=-1), jnp.concatenate([-sin, sin], axis=-1)


def _token_group(x, pos, lw, lam_init, log_gamma, state0, cache):
    bsz, t_len, d = x.shape
    x2d = x.reshape(bsz * t_len, d)
    cos_tab, sin_tab = _rotary_tables(pos)
    tm = _row_tile(bsz * t_len)
    if t_len < tm:
        reps = tm // t_len
        cos_tab = jnp.tile(cos_tab, (reps, 1))
        sin_tab = jnp.tile(sin_tab, (reps, 1))
    rqk, rv, rgs, dq, kb, vt, dk_rows, dv_rows, sgr, sgd = _in_proj(
        x2d, lw["g_mix"], lw["w_in"], cos_tab, sin_tab, lw["g_q"], lw["g_k"])
    ogr, s_fin = _retention(rqk, rv, rgs, state0, lw["g_ret"], log_gamma, bsz, t_len)
    if cache is None:
        od = _attn_prompt(dq, kb, vt, lw["g_q"], lw["g_k"], lw["lam_p"], lw["g_sub"], lam_init,
                          bsz, t_len)
    else:
        od = _attn_sample(dq, dk_rows, dv_rows, _to_cache_rows(cache[0]), _to_cache_rows(cache[1]),
                          lw["lam_p"], lw["g_sub"], lam_init, bsz, t_len)
    x1e, cls = _merge(ogr, od, sgr, sgd, x2d, lw["w_ret_o"], lw["w_diff_o"], lw["w_out"],
                      lw["g_ffn"], lw["rw_hi"], lw["rw_lo"], lw["rb"])
    y = _hier_moe_residual(x1e, cls, lw["g_ffn"], lw["w_gu"], lw["w_dn"])
    return (y.reshape(bsz, t_len, d), _from_cache_rows(dk_rows, bsz, t_len),
            _from_cache_rows(dv_rows, bsz, t_len), s_fin)


def _to_cache_rows(c):
    b, p, h, w = c.shape
    halves = w // V7X_LANES
    return c.reshape(b * p, h, halves, V7X_LANES).transpose(0, 2, 1, 3).reshape(b * p * h * halves, V7X_LANES)


def _from_cache_rows(rows, bsz, t_len):
    halves = D_DV // V7X_LANES
    r = rows.reshape(bsz * t_len, halves, D_HEADS, V7X_LANES).transpose(0, 2, 1, 3)
    return r.reshape(bsz, t_len, D_HEADS, D_DV)


def _layer_weights(l, g_mix, w_in, g_q, g_k, lambda_q1, lambda_k1, lambda_q2, lambda_k2, g_ret,
                   w_ret_o, g_sub, w_diff_o, w_out, g_ffn, w_group, b_group, w_expert, b_expert,
                   w_gate, w_up, w_down):
    d = w_in.shape[1]
    n_r = N_GROUPS + N_EXPERTS
    rw = jnp.concatenate([w_group[l], w_expert[l]], axis=1).astype(F32).T
    rw = jnp.zeros((CLS_PAD, d), F32).at[:n_r].set(rw)
    rw_hi = rw.astype(BF16)
    rw_lo = (rw - rw_hi.astype(F32)).astype(BF16)
    rb = jnp.concatenate([b_group[l], b_expert[l]]).astype(F32)
    rb = jnp.zeros((CLS_PAD,), F32).at[:n_r].set(rb)
    rb = jnp.broadcast_to(rb[:, None], (CLS_PAD, V7X_LANES))
    return dict(
        g_mix=g_mix[l], w_in=w_in[l].astype(BF16), g_q=g_q[l], g_k=g_k[l],
        lam_p=jnp.stack([lambda_q1[l], lambda_k1[l], lambda_q2[l], lambda_k2[l]]).astype(F32),
        g_ret=g_ret[l], w_ret_o=w_ret_o[l].astype(BF16), g_sub=g_sub[l],
        w_diff_o=w_diff_o[l].astype(BF16), w_out=w_out[l].astype(BF16), g_ffn=g_ffn[l],
        rw_hi=rw_hi, rw_lo=rw_lo, rb=rb,
        w_gu=jnp.concatenate([w_gate[l], w_up[l]], axis=-1).astype(BF16),
        w_dn=w_down[l].astype(BF16))


def kernel(x_prompt, x_sample, cache_k, cache_v, state_ret, g_mix, w_in, g_q, g_k, lambda_q1, lambda_k1, lambda_q2, lambda_k2, g_ret, w_ret_o, g_sub, w_diff_o, w_out, g_ffn, w_group, b_group, w_expert, b_expert, w_gate, w_up, w_down):
    depth = w_in.shape[0]
    bp, tp, _ = x_prompt.shape
    bs, ts, _ = x_sample.shape
    past = cache_k.shape[2]
    log_gamma = jnp.log1p(-jnp.exp2(-5.0 - jnp.arange(R_HEADS, dtype=F32)))
    pos_p = jnp.arange(tp, dtype=jnp.int32)
    pos_s = past + jnp.arange(ts, dtype=jnp.int32)
    zero_state = jnp.zeros((bp, R_HEADS, R_DK, R_DV), F32)
    xp, xs = x_prompt, x_sample
    outs = [[] for _ in range(6)]
    for l in range(depth):
        lam_init = 0.8 - 0.6 * math.exp(-0.3 * l)
        lw = _layer_weights(l, g_mix, w_in, g_q, g_k, lambda_q1, lambda_k1, lambda_q2, lambda_k2,
                            g_ret, w_ret_o, g_sub, w_diff_o, w_out, g_ffn, w_group, b_group,
                            w_expert, b_expert, w_gate, w_up, w_down)
        xp, kp, vp, sp = _token_group(xp, pos_p, lw, lam_init, log_gamma, zero_state, None)
        cache = (cache_k[l], cache_v[l])
        xs, ks, vs, ss = _token_group(xs, pos_s, lw, lam_init, log_gamma,
                                      state_ret[l].astype(F32), cache)
        for lst, val in zip(outs, (kp, vp, sp, ks, vs, ss)):
            lst.append(val)
    return (xp, xs) + tuple(jnp.stack(o) for o in outs)
```

```python
import functools
import math

import jax
import jax.numpy as jnp
from jax import lax
from jax.experimental import pallas as pl
from jax.experimental.pallas import tpu as pltpu

F32 = jnp.float32
BF16 = jnp.bfloat16
I32 = jnp.int32

CHUNK = 64
EPS = 1e-6
R_HEADS = 4
R_DK = 128
R_DV = 256
ROPE_BASE = 10000.0
D_HEADS = 4
D_DH = 128
D_DV = 256
N_GROUPS = 4
EXP_PER_GROUP = 4
N_EXPERTS = N_GROUPS * EXP_PER_GROUP
N_PAIRS = EXP_PER_GROUP * (EXP_PER_GROUP - 1) // 2
N_CLASSES = N_GROUPS * N_PAIRS
D_FF = 512
SEG = 1024
N_SEG = 8

V7X_LANES = 128
V7X_SUBLANES = 8
V7X_VMEM_LIMIT_BYTES = 56 * 1024 * 1024

ROW_TILE = 512
RET_SUPER = 512
ATT_TILE = 512
MERGE_TILES_PER_STEP = 2
SORT_TILES_PER_STEP = 8
MOE_TILE_MIN = 32
SAMPLE_CHUNK = 1024
MOE_TILE = 256
ROW_DMA_UNROLL = True
CLS_PAD = 32
NEG_BIG = -1e30
LOG2E = math.log2(math.e)
SCORE_BOUND_MARGIN = 1.01
MAX_SHIFT_LOG2 = 100.0
PAY = V7X_LANES


def _row_tile(n):
    return min(ROW_TILE, n)


def _cparams(sem, vmem_mb=None):
    kw = dict(dimension_semantics=sem)
    if vmem_mb is not None:
        kw["vmem_limit_bytes"] = min(vmem_mb * 1024 * 1024, V7X_VMEM_LIMIT_BYTES)
    return pltpu.CompilerParams(**kw)


def _resident(shape):
    nd = len(shape)
    return pl.BlockSpec(shape, lambda *_: (0,) * nd, pipeline_mode=pl.Buffered(1))


def _rms(x, eps=EPS):
    return x * lax.rsqrt(jnp.mean(x * x, axis=-1, keepdims=True) + eps)


def _sigmoid(x):
    return 1.0 / (1.0 + jnp.exp(-x))


def _dot(a, b):
    return jnp.dot(a, b, preferred_element_type=F32)


def _dot_nt(a, b):
    return lax.dot_general(a, b, (((1,), (1,)), ((), ())), preferred_element_type=F32)


def _dot_tn(a, b):
    return lax.dot_general(a, b, (((0,), (0,)), ((), ())), preferred_element_type=F32)


def _store_cache_rows(ref, z):
    rows = z.shape[0]
    for half in range(2):
        for h in range(D_HEADS):
            lo = h * D_DV + half * V7X_LANES
            ref[pl.ds(half * D_HEADS + h, rows, stride=2 * D_HEADS), :] = z[:, lo:lo + V7X_LANES]


def _inproj_kernel(x_ref, g_ref, w_ref, cos_ref, sin_ref, gq_ref, gk_ref,
                   rqk_ref, rv_ref, rgs_ref, dq_ref, kb_ref, vt_ref, dk_ref, dv_ref, sgr_ref, sgd_ref):
    x = x_ref[...]
    hb = (_rms(x) * g_ref[...]).astype(BF16)

    def seg(s):
        return _dot(hb, w_ref[:, s * SEG:(s + 1) * SEG])

    cos = cos_ref[...]
    sin = sin_ref[...]
    z = seg(0)
    for j in range(2 * R_HEADS):
        v = z[:, j * R_DK:(j + 1) * R_DK]
        r = v * cos + pltpu.roll(v, R_DK // 2, 1) * sin
        if j >= R_HEADS:
            r = r * (R_DK ** -0.5)
        rqk_ref[:, j * R_DK:(j + 1) * R_DK] = r.astype(BF16)

    rv_ref[...] = seg(1).astype(BF16)
    z = seg(2)
    rgs_ref[...] = (z * _sigmoid(z)).astype(BF16)

    z = seg(3)
    gq = gq_ref[...] * (D_DH ** -0.5 * LOG2E)
    for j in range(2 * D_HEADS):
        v = z[:, j * D_DH:(j + 1) * D_DH]
        dq_ref[:, j * D_DH:(j + 1) * D_DH] = (_rms(v) * gq).astype(BF16)
    z = seg(4)
    gk = gk_ref[...]
    kn = jnp.concatenate([_rms(z[:, j * D_DH:(j + 1) * D_DH]) * gk for j in range(2 * D_HEADS)], axis=1)
    kb_ref[...] = kn.astype(BF16)
    _store_cache_rows(dk_ref, kn)
    z = seg(5)
    _store_cache_rows(dv_ref, z)
    vt_ref[0] = z.T.astype(BF16)
    sgr_ref[...] = _sigmoid(seg(6)).astype(BF16)
    sgd_ref[...] = _sigmoid(seg(7)).astype(BF16)


def _in_proj(x2d, g_mix, w_in_bf, cos_tab, sin_tab, g_q, g_k):
    n, d = x2d.shape
    tm = _row_tile(n)
    nt = n // tm
    ntab = cos_tab.shape[0] // tm
    row = lambda w: pl.BlockSpec((tm, w), lambda i: (i, 0))
    tab = pl.BlockSpec((tm, R_DK), lambda i: (i % ntab, 0))
    flat = jax.ShapeDtypeStruct((n, SEG), BF16)
    cache_rows = jax.ShapeDtypeStruct((n * 2 * D_HEADS, V7X_LANES), F32)
    cache_spec = pl.BlockSpec((tm * 2 * D_HEADS, V7X_LANES), lambda i: (i, 0))
    outs = [flat, flat, flat, flat, flat, jax.ShapeDtypeStruct((nt, SEG, tm), BF16),
            cache_rows, cache_rows, flat, flat]
    return pl.pallas_call(
        _inproj_kernel,
        out_shape=outs,
        grid=(nt,),
        in_specs=[row(d), _resident((1, d)), _resident(w_in_bf.shape), tab, tab,
                  _resident((1, D_DH)), _resident((1, D_DH))],
        out_specs=[row(SEG)] * 5 + [pl.BlockSpec((1, SEG, tm), lambda i: (i, 0, 0)),
                                    cache_spec, cache_spec, row(SEG), row(SEG)],
        compiler_params=_cparams(("arbitrary",), 56),
        name="in_proj",
    )(x2d, g_mix.reshape(1, d), w_in_bf, cos_tab, sin_tab,
      g_q.reshape(1, D_DH), g_k.reshape(1, D_DH))


def _ret_kernel(lg_ref, qk_ref, v_ref, rgs_ref, s0_ref, g_ref,
                o_ref, sfin_ref, s_scr, d_scr, lam_scr, wk_scr, dec_scr, *, c_len):
    c = pl.program_id(1)

    @pl.when((pl.program_id(0) == 0) & (c == 0))
    def _():
        t = lax.broadcasted_iota(I32, (c_len, c_len), 0)
        s = lax.broadcasted_iota(I32, (c_len, c_len), 1)
        dist = jnp.abs(t - s).astype(F32)
        vis = (s // CHUNK) <= (t // CHUNK)
        pos_v = lax.broadcasted_iota(I32, (c_len, R_DV), 0).astype(F32)
        pos_k = lax.broadcasted_iota(I32, (c_len, R_DK), 0).astype(F32)
        for h in range(R_HEADS):
            lg = lg_ref[h]
            d_scr[h] = jnp.where(vis, jnp.exp(lg * dist), 0.0)
            lam_scr[h] = jnp.exp(lg * (pos_v + 1.0))
            wk_scr[h] = jnp.exp(lg * (c_len - 1.0 - pos_k))
            dec_scr[h] = jnp.exp(lg * jnp.full((V7X_SUBLANES, R_DV), float(c_len), F32))

    @pl.when(c == 0)
    def _():
        s_scr[...] = s0_ref[0]

    for h in range(R_HEADS):
        q = qk_ref[0, :, h * R_DK:(h + 1) * R_DK]
        k = qk_ref[0, :, (R_HEADS + h) * R_DK:(R_HEADS + h + 1) * R_DK]
        v = v_ref[0, :, h * R_DV:(h + 1) * R_DV]
        state = s_scr[h]
        s = _dot_nt(q, k) * d_scr[h]
        o = _dot(s.astype(BF16), v) + lam_scr[h] * _dot(q, state.astype(BF16))
        kw = (k.astype(F32) * wk_scr[h]).astype(BF16)
        s_scr[h] = state * dec_scr[h][0:1, :] + _dot_tn(kw, v)
        gate = rgs_ref[0, :, h * R_DV:(h + 1) * R_DV].astype(F32)
        o_ref[0, :, h * R_DV:(h + 1) * R_DV] = (_rms(o) * g_ref[...] * gate).astype(BF16)

    @pl.when(c == pl.num_programs(1) - 1)
    def _():
        sfin_ref[0] = s_scr[...]


def _retention(rqk, rv, rgs, state0, g_ret, log_gamma, bsz, t_len):
    c_len = min(RET_SUPER, t_len)
    nc = t_len // c_len
    rqk3 = rqk.reshape(bsz, t_len, SEG)
    rv3 = rv.reshape(bsz, t_len, SEG)
    rgs3 = rgs.reshape(bsz, t_len, SEG)
    rows = pl.BlockSpec((1, c_len, SEG), lambda b, c, lg: (b, c, 0))
    sspec = pl.BlockSpec((1, R_HEADS, R_DK, R_DV), lambda b, c, lg: (b, 0, 0, 0))
    gspec = pl.BlockSpec((1, R_DV), lambda b, c, lg: (0, 0))
    o, sfin = pl.pallas_call(
        functools.partial(_ret_kernel, c_len=c_len),
        out_shape=[jax.ShapeDtypeStruct((bsz, t_len, SEG), BF16),
                   jax.ShapeDtypeStruct((bsz, R_HEADS, R_DK, R_DV), F32)],
        grid_spec=pltpu.PrefetchScalarGridSpec(
            num_scalar_prefetch=1,
            grid=(bsz, nc),
            in_specs=[rows, rows, rows, sspec, gspec],
            out_specs=[rows, sspec],
            scratch_shapes=[pltpu.VMEM((R_HEADS, R_DK, R_DV), F32),
                            pltpu.VMEM((R_HEADS, c_len, c_len), F32),
                            pltpu.VMEM((R_HEADS, c_len, R_DV), F32),
                            pltpu.VMEM((R_HEADS, c_len, R_DK), F32),
                            pltpu.VMEM((R_HEADS, V7X_SUBLANES, R_DV), F32)],
        ),
        compiler_params=_cparams(("arbitrary",) * 2, 32),
        name="retention",
    )(log_gamma, rqk3, rv3, rgs3, state0, g_ret.reshape(1, R_DV))
    return o.reshape(bsz * t_len, SEG), sfin


def _lambda_full(lp_ref, lam_init):
    lp = lp_ref[...]
    a = jnp.sum(lp[0:1] * lp[1:2], axis=-1, keepdims=True)
    b = jnp.sum(lp[2:3] * lp[3:4], axis=-1, keepdims=True)
    return jnp.exp(a) - jnp.exp(b) + lam_init


def _attn_finish(o1, l1, o2, l2, lam, g, lam_init):
    o = o1 / l1 - lam * (o2 / l2)
    return (_rms(o) * g * (1.0 - lam_init)).astype(BF16)


def _attn_prompt_kernel(bound_ref, q_ref, k_ref, vt_ref, lp_ref, g_ref, o_ref,
                        m1, l1, a1, m2, l2, a2, bias, *, tile, lam_init, fixed_shift):
    i = pl.program_id(2)

    q = q_ref[0]
    q1 = q[:, :D_DH]
    q2 = q[:, D_DH:]
    for m, l, a in ((m1, l1, a1), (m2, l2, a2)):
        m[...] = jnp.full(m.shape, NEG_BIG, F32)
        l[...] = jnp.zeros(l.shape, F32)
        a[...] = jnp.zeros(a.shape, F32)

    def update(s, vts, m, l, a):
        if fixed_shift:
            p = jnp.exp2(s - bound_ref[0])
            l[...] += jnp.sum(p, axis=0, keepdims=True)
            a[...] += _dot(vts, p.astype(BF16))
            return
        m_old = m[...]
        m_new = jnp.maximum(m_old, jnp.max(s, axis=0, keepdims=True))
        alpha = jnp.exp2(m_old - m_new)
        p = jnp.exp2(s - m_new)
        l[...] = alpha * l[...] + jnp.sum(p, axis=0, keepdims=True)
        a[...] = alpha * a[...] + _dot(vts, p.astype(BF16))
        m[...] = m_new

    def block(j, mask):
        ks = k_ref[0, pl.ds(pl.multiple_of(j * tile, tile), tile), :]
        vts = vt_ref[0, j]
        s1 = _dot_nt(ks[:, :D_DH], q1)
        s2 = _dot_nt(ks[:, D_DH:], q2)
        if mask is not None:
            s1 = s1 + mask
            s2 = s2 + mask
        update(s1, vts, m1, l1, a1)
        update(s2, vts, m2, l2, a2)

    def body(jj, carry):
        block(2 * jj, None)
        block(2 * jj + 1, None)
        return carry

    lax.fori_loop(0, i // 2, body, 0)

    @pl.when(i % 2 == 1)
    def _():
        block(i - 1, None)

    @pl.when((pl.program_id(0) == 0) & (pl.program_id(1) == 0) & (i == 0))
    def _():
        kk = lax.broadcasted_iota(I32, (tile, tile), 0)
        qq = lax.broadcasted_iota(I32, (tile, tile), 1)
        bias[...] = jnp.where((kk // CHUNK) <= (qq // CHUNK), 0.0, NEG_BIG)

    block(i, bias[...])

    lam = _lambda_full(lp_ref, lam_init)
    ot = a1[...] * (1.0 / l1[...]) - lam * (a2[...] * (1.0 / l2[...]))
    o_ref[0] = (_rms(ot.T) * g_ref[...] * (1.0 - lam_init)).astype(BF16)


def _attn_prompt(dq, kb, vt, g_q, g_k, lam_p, g_sub, lam_init, bsz, t_len):
    tile = vt.shape[2]
    assert t_len % tile == 0 and tile % CHUNK == 0
    nq = t_len // tile
    w = 2 * D_DH
    q3 = dq.reshape(bsz, t_len, SEG)
    k3 = kb.reshape(bsz, t_len, SEG)
    vt4 = vt.reshape(bsz, nq, SEG, tile)
    qspec = pl.BlockSpec((1, tile, w), lambda b, h, i: (b, i, h))
    kspec = pl.BlockSpec((1, t_len, w), lambda b, h, i: (b, 0, h))
    vspec = pl.BlockSpec((1, nq, D_DV, tile), lambda b, h, i: (b, 0, h, 0))
    col = lambda: pltpu.VMEM((1, tile), F32)
    acc = lambda: pltpu.VMEM((D_DV, tile), F32)

    def call(fixed_shift):
        return pl.pallas_call(
            functools.partial(_attn_prompt_kernel, tile=tile, lam_init=lam_init, fixed_shift=fixed_shift),
            out_shape=jax.ShapeDtypeStruct((bsz, t_len, SEG), BF16),
            grid=(bsz, D_HEADS, nq),
            in_specs=[pl.BlockSpec(memory_space=pltpu.SMEM), qspec, kspec, vspec,
                      pl.BlockSpec((4, D_DH), lambda b, h, i: (0, 0)),
                      pl.BlockSpec((1, D_DV), lambda b, h, i: (0, 0))],
            out_specs=qspec,
            scratch_shapes=[col(), col(), acc(), col(), col(), acc(), pltpu.VMEM((tile, tile), F32)],
            compiler_params=_cparams(("arbitrary",) * 3, 32),
            name="attn_prompt_fixed" if fixed_shift else "attn_prompt_online",
        )

    bound = SCORE_BOUND_MARGIN * D_DH * (D_DH ** -0.5 * LOG2E) * jnp.max(jnp.abs(g_q * g_k))
    args = (bound.reshape(1).astype(F32), q3, k3, vt4, lam_p, g_sub.reshape(1, D_DV))
    out = lax.cond(2.0 * bound < MAX_SHIFT_LOG2,
                   lambda *a: call(True)(*a), lambda *a: call(False)(*a), *args)
    return out.reshape(bsz * t_len, SEG)


def _attn_sample_kernel(q_ref, kc_ref, vc_ref, kn_ref, vn_ref, lp_ref, g_ref, o_ref,
                        m_scr, l_scr, a_scr, *, lam_init, chunk, t_new):
    c = pl.program_id(1)

    @pl.when(c == 0)
    def _():
        m_scr[...] = jnp.full(m_scr.shape, NEG_BIG, F32)
        l_scr[...] = jnp.zeros(l_scr.shape, F32)
        a_scr[...] = jnp.zeros(a_scr.shape, F32)

    q = q_ref[0]

    def head_rows(ref, h, half, n):
        return ref[pl.ds(half * D_HEADS + h, n, stride=2 * D_HEADS), :].astype(BF16)

    def absorb(k_src, v_src, n):
        for h in range(D_HEADS):
            v = jnp.concatenate([head_rows(v_src, h, 0, n), head_rows(v_src, h, 1, n)], axis=1)
            for mp in range(2):
                idx = 2 * h + mp
                lo = h * 2 * D_DH + mp * D_DH
                s = _dot_nt(q[:, lo:lo + D_DH], head_rows(k_src, h, mp, n))
                m_old = m_scr[idx]
                m_new = jnp.maximum(m_old, jnp.max(s, axis=-1, keepdims=True))
                alpha = jnp.exp2(m_old - m_new)
                p = jnp.exp2(s - m_new)
                l_scr[idx] = alpha * l_scr[idx] + jnp.sum(p, axis=-1, keepdims=True)
                a_scr[idx] = alpha * a_scr[idx] + _dot(p.astype(BF16), v)
                m_scr[idx] = m_new

    absorb(kc_ref, vc_ref, chunk)

    @pl.when(c == pl.num_programs(1) - 1)
    def _():
        absorb(kn_ref, vn_ref, t_new)
        lam = _lambda_full(lp_ref, lam_init)
        for h in range(D_HEADS):
            o_ref[0, :, h * D_DV:(h + 1) * D_DV] = _attn_finish(
                a_scr[2 * h], l_scr[2 * h], a_scr[2 * h + 1], l_scr[2 * h + 1], lam, g_ref[...], lam_init)


def _attn_sample(dq, dk_rows, dv_rows, cache_k_rows, cache_v_rows, lam_p, g_sub, lam_init, bsz, t_len):
    rpf = 2 * D_HEADS
    past = cache_k_rows.shape[0] // (bsz * rpf)
    chunk = min(SAMPLE_CHUNK, past)
    nc = past // chunk
    q3 = dq.reshape(bsz, t_len, SEG)
    qspec = pl.BlockSpec((1, t_len, SEG), lambda b, c: (b, 0, 0))
    cspec = pl.BlockSpec((chunk * rpf, V7X_LANES), lambda b, c: (b * nc + c, 0))
    nspec = pl.BlockSpec((t_len * rpf, V7X_LANES), lambda b, c: (b, 0))
    n_stat = 2 * D_HEADS
    out = pl.pallas_call(
        functools.partial(_attn_sample_kernel, lam_init=lam_init, chunk=chunk, t_new=t_len),
        out_shape=jax.ShapeDtypeStruct((bsz, t_len, SEG), BF16),
        grid=(bsz, nc),
        in_specs=[qspec, cspec, cspec, nspec, nspec,
                  pl.BlockSpec((4, D_DH), lambda b, c: (0, 0)),
                  pl.BlockSpec((1, D_DV), lambda b, c: (0, 0))],
        out_specs=qspec,
        scratch_shapes=[pltpu.VMEM((n_stat, t_len, 1), F32), pltpu.VMEM((n_stat, t_len, 1), F32),
                        pltpu.VMEM((n_stat, t_len, D_DV), F32)],
        compiler_params=_cparams(("arbitrary",) * 2, 32),
        name="attn_sample",
    )(q3, cache_k_rows, cache_v_rows, dk_rows, dv_rows, lam_p, g_sub.reshape(1, D_DV))
    return out.reshape(bsz * t_len, SEG)


def _split_hi_lo(x):
    hi = x.astype(BF16)
    lo = (x - hi.astype(F32)).astype(BF16)
    return hi, lo


def _merge_kernel(*refs):
    cls_ref = refs[-1]
    for sub in range(cls_ref.shape[0]):
        _merge_tile(sub, cls_ref.shape[2], *refs)


def _merge_tile(sub, tm, ogr_ref, od_ref, sgr_ref, sgd_ref, x_ref, wr_ref, wd_ref, wo_ref,
                gf_ref, rw_hi_ref, rw_lo_ref, rb_ref, x1e_ref, cls_ref):
    rs = slice(sub * tm, (sub + 1) * tm)
    y_r = _dot(ogr_ref[rs, :], wr_ref[...])
    y_d = _dot(od_ref[rs, :], wd_ref[...])
    m = sgr_ref[rs, :].astype(F32) * y_r + sgd_ref[rs, :].astype(F32) * y_d
    x1 = x_ref[rs, :] + _dot(m.astype(BF16), wo_ref[...])

    hn = _rms(x1) * gf_ref[...]
    h_hi, h_lo = _split_hi_lo(hn)
    w_hi = rw_hi_ref[...]
    lt = _dot_nt(w_hi, h_hi) + _dot_nt(rw_lo_ref[...], h_hi) + _dot_nt(w_hi, h_lo)
    lt = lt + rb_ref[...][:, 0:1]
    g = [lt[i:i + 1, :] for i in range(N_GROUPS)]
    e = [lt[N_GROUPS + i:N_GROUPS + i + 1, :] for i in range(N_EXPERTS)]

    gmax = functools.reduce(jnp.maximum, g)
    gid = jnp.full(g[0].shape, N_GROUPS - 1, I32)
    for i in range(N_GROUPS - 2, -1, -1):
        gid = jnp.where(g[i] == gmax, i, gid)
    g_w = 1.0 / functools.reduce(lambda a, b: a + b, [jnp.exp(v - gmax) for v in g])

    es = []
    for j in range(EXP_PER_GROUP):
        v = e[(N_GROUPS - 1) * EXP_PER_GROUP + j]
        for i in range(N_GROUPS - 2, -1, -1):
            v = jnp.where(gid == i, e[i * EXP_PER_GROUP + j], v)
        es.append(v)

    def first_argmax(vals):
        mx = functools.reduce(jnp.maximum, vals)
        idx = jnp.full(mx.shape, len(vals) - 1, I32)
        for i in range(len(vals) - 2, -1, -1):
            idx = jnp.where(vals[i] == mx, i, idx)
        return mx, idx

    l1, i1 = first_argmax(es)
    rest = [jnp.where(i1 == j, -jnp.inf, es[j]) for j in range(EXP_PER_GROUP)]
    l2, i2 = first_argmax(rest)
    t = jnp.exp(l2 - l1)
    c1 = g_w / (1.0 + t)
    c2 = g_w * t / (1.0 + t)
    lo = jnp.minimum(i1, i2)
    hi = jnp.maximum(i1, i2)
    base = jnp.where(lo == 0, 0, jnp.where(lo == 1, EXP_PER_GROUP - 1, 2 * EXP_PER_GROUP - 3))
    cls = gid * N_PAIRS + base + hi - lo - 1
    wa = jnp.where(i1 < i2, c1, c2)
    wb = jnp.where(i1 < i2, c2, c1)

    cls_ref[sub] = cls
    rid = lax.broadcasted_iota(I32, (PAY, tm), 0)
    pay = jnp.where(rid == 0, wa, jnp.where(rid == 1, wb, 0.0))
    d = x_ref.shape[1]
    x1e_ref[rs, :d] = x1
    x1e_ref[rs, d:] = pay.T


def _merge(ogr, od, sgr, sgd, x2d, w_ret_o, w_diff_o, w_out, g_ffn, rw_hi, rw_lo, rb):
    n, d = x2d.shape
    tm = _row_tile(n)
    nt = n // tm
    n_sub = MERGE_TILES_PER_STEP if nt % MERGE_TILES_PER_STEP == 0 else 1
    row = lambda w: pl.BlockSpec((n_sub * tm, w), lambda i: (i, 0))
    return pl.pallas_call(
        _merge_kernel,
        out_shape=[jax.ShapeDtypeStruct((n, d + PAY), F32),
                   jax.ShapeDtypeStruct((nt, 1, tm), I32)],
        grid=(nt // n_sub,),
        in_specs=[row(SEG), row(SEG), row(SEG), row(SEG), row(d),
                  _resident(w_ret_o.shape), _resident(w_diff_o.shape), _resident(w_out.shape),
                  _resident((1, d)), _resident(rw_hi.shape), _resident(rw_lo.shape),
                  _resident(rb.shape)],
        out_specs=[row(d + PAY), pl.BlockSpec((n_sub, 1, tm), lambda i: (i, 0, 0))],
        compiler_params=_cparams(("arbitrary",), 48),
        name="merge_route",
    )(ogr, od, sgr, sgd, x2d, w_ret_o, w_diff_o, w_out, g_ffn.reshape(1, d), rw_hi, rw_lo, rb)


def _prefix_excl(x):
    rid = lax.broadcasted_iota(I32, x.shape, 0)
    inc = x
    s = 1
    while s < x.shape[0]:
        inc = inc + jnp.where(rid >= s, pltpu.roll(inc, s, 0), 0.0)
        s *= 2
    return inc - x


def _sort_kernel(cls_ref, pos_ref, cnt_ref, off_ref, tiles_ref, cnt_scr, run_scr, off_scr,
                 *, moe_tile, n_tiles_pad):
    ph = pl.program_id(0)
    t = pl.program_id(1)
    k_tiles, _, tm = cls_ref.shape
    cid = lax.broadcasted_iota(I32, (CLS_PAD, tm), 0)

    def onehot(j):
        return jnp.where(cid == cls_ref[j], 1.0, 0.0)

    @pl.when(jnp.logical_and(ph == 0, t == 0))
    def _():
        cnt_scr[...] = jnp.zeros(cnt_scr.shape, F32)

    @pl.when(ph == 0)
    def _():
        for j in range(k_tiles):
            cnt_scr[...] += jnp.sum(onehot(j), axis=-1, keepdims=True)

    @pl.when(jnp.logical_and(ph == 1, t == 0))
    def _():
        cnt = cnt_scr[...]
        padded = jnp.ceil(cnt / moe_tile) * moe_tile
        off = _prefix_excl(padded)
        off_scr[...] = off
        run_scr[...] = jnp.zeros(run_scr.shape, F32)
        cnt_ref[...] = cnt.astype(I32)
        off_ref[...] = off.astype(I32)
        end = (off + padded)[:, 0:1]
        total = jnp.max(end, axis=0, keepdims=True)
        n_used = total / moe_tile
        p = lax.broadcasted_iota(I32, (1, n_tiles_pad), 1).astype(F32)
        start = jnp.minimum(p, n_used - 1.0) * moe_tile
        cid2 = lax.broadcasted_iota(I32, (CLS_PAD, n_tiles_pad), 0)
        before = jnp.logical_and(end <= start, cid2 < N_CLASSES)
        tcls = jnp.sum(before.astype(F32), axis=0, keepdims=True).astype(I32)
        grp = tcls // N_PAIRS
        pr = tcls - grp * N_PAIRS
        lo = (pr >= EXP_PER_GROUP - 1).astype(I32) + (pr >= 2 * EXP_PER_GROUP - 3).astype(I32)
        base = jnp.where(lo == 0, 0, jnp.where(lo == 1, EXP_PER_GROUP - 1, 2 * EXP_PER_GROUP - 3))
        hi = pr - base + lo + 1
        rid = lax.broadcasted_iota(I32, (V7X_SUBLANES, n_tiles_pad), 0)
        ea = grp * EXP_PER_GROUP + lo
        eb = grp * EXP_PER_GROUP + hi
        nu = jnp.broadcast_to(n_used.astype(I32), (1, n_tiles_pad))
        tiles_ref[...] = jnp.where(rid == 0, ea, jnp.where(rid == 1, eb, jnp.where(rid == 2, nu, 0)))

    @pl.when(ph == 1)
    def _():
        r = lax.broadcasted_iota(I32, (tm, tm), 0)
        c = lax.broadcasted_iota(I32, (tm, tm), 1)
        upper = jnp.where(r <= c, 1.0, 0.0).astype(BF16)
        for j in range(k_tiles):
            oh = onehot(j)
            incl = _dot(oh.astype(BF16), upper)
            slot = off_scr[...][:, 0:1] + run_scr[...][:, 0:1] + incl - 1.0
            pos_ref[j] = jnp.sum(oh * slot, axis=0, keepdims=True).astype(I32)
            run_scr[...] += jnp.sum(oh, axis=-1, keepdims=True)


def _sort(cls, moe_tile, n_tiles_pad):
    nt, _, tm = cls.shape
    k_tiles = SORT_TILES_PER_STEP if nt % SORT_TILES_PER_STEP == 0 else 1
    blk = pl.BlockSpec((k_tiles, 1, tm), lambda ph, t: (t, 0, 0))
    oblk = pl.BlockSpec((k_tiles, 1, tm), lambda ph, t: (t * ph, 0, 0))
    whole = lambda shape: pl.BlockSpec(shape, lambda ph, t: (0, 0))
    return pl.pallas_call(
        functools.partial(_sort_kernel, moe_tile=moe_tile, n_tiles_pad=n_tiles_pad),
        out_shape=[jax.ShapeDtypeStruct((nt, 1, tm), I32),
                   jax.ShapeDtypeStruct((CLS_PAD, V7X_LANES), I32),
                   jax.ShapeDtypeStruct((CLS_PAD, V7X_LANES), I32),
                   jax.ShapeDtypeStruct((V7X_SUBLANES, n_tiles_pad), I32)],
        grid=(2, nt // k_tiles),
        in_specs=[blk],
        out_specs=[oblk, whole((CLS_PAD, V7X_LANES)), whole((CLS_PAD, V7X_LANES)),
                   whole((V7X_SUBLANES, n_tiles_pad))],
        scratch_shapes=[pltpu.VMEM((CLS_PAD, V7X_LANES), F32)] * 3,
        compiler_params=_cparams(("arbitrary",) * 2),
        name="class_sort",
    )(cls)


def _permute_kernel(pos_ref, cnt_ref, off_ref, src_ref, dst_ref, zblk, sem, zsem, *, moe_tile):
    t = pl.program_id(0)
    tm = pos_ref.shape[2]

    def row_copy(r):
        return pltpu.make_async_copy(src_ref.at[pl.ds(r, 1)],
                                     dst_ref.at[pl.ds(pos_ref[0, 0, r], 1)], sem)

    def start(r, carry):
        row_copy(r).start()
        return carry

    def wait(r, carry):
        row_copy(r).wait()
        return carry

    lax.fori_loop(0, tm, start, 0, unroll=ROW_DMA_UNROLL)

    @pl.when(t == 0)
    def _():
        zblk[...] = jnp.zeros(zblk.shape, zblk.dtype)
        used = 0
        for c in range(N_CLASSES):
            cnt = cnt_ref[c, 0]
            off = off_ref[c, 0]
            padded = ((cnt + moe_tile - 1) // moe_tile) * moe_tile
            used = off + padded

            def pad_copy(r, off=off):
                return pltpu.make_async_copy(zblk.at[pl.ds(0, 1)], dst_ref.at[pl.ds(off + r, 1)], zsem)

            def pad_start(r, carry, copy=pad_copy):
                copy(r).start()
                return carry

            def pad_wait(r, carry, copy=pad_copy):
                copy(r).wait()
                return carry

            lax.fori_loop(cnt, padded, pad_start, 0)
            lax.fori_loop(cnt, padded, pad_wait, 0)

        def tile_copy(p):
            return pltpu.make_async_copy(zblk, dst_ref.at[pl.ds(p * moe_tile, moe_tile)], zsem)

        def tile_start(p, carry):
            tile_copy(p).start()
            return carry

        def tile_wait(p, carry):
            tile_copy(p).wait()
            return carry

        first, last = used // moe_tile, dst_ref.shape[0] // moe_tile
        lax.fori_loop(first, last, tile_start, 0)
        lax.fori_loop(first, last, tile_wait, 0)

    lax.fori_loop(0, tm, wait, 0, unroll=ROW_DMA_UNROLL)


def _permute(pos, cnt, off, src, n_rows_out, moe_tile):
    nt, _, tm = pos.shape
    width = src.shape[1]
    smem = lambda shape, imap: pl.BlockSpec(shape, imap, memory_space=pltpu.SMEM)
    return pl.pallas_call(
        functools.partial(_permute_kernel, moe_tile=moe_tile),
        out_shape=jax.ShapeDtypeStruct((n_rows_out, width), src.dtype),
        grid=(nt,),
        in_specs=[smem((1, 1, tm), lambda t: (t, 0, 0)),
                  smem(cnt.shape, lambda t: (0, 0)),
                  smem(off.shape, lambda t: (0, 0)),
                  pl.BlockSpec((tm, width), lambda t: (t, 0))],
        out_specs=pl.BlockSpec(memory_space=pl.ANY),
        scratch_shapes=[pltpu.VMEM((moe_tile, width), src.dtype),
                        pltpu.SemaphoreType.DMA, pltpu.SemaphoreType.DMA],
        compiler_params=_cparams(("arbitrary",)),
        name="permute_rows",
    )(pos, cnt, off, src)


def _unpermute_kernel(pos_ref, src_ref, dst_ref, sem):
    tm = pos_ref.shape[2]

    def row_copy(r):
        return pltpu.make_async_copy(src_ref.at[pl.ds(pos_ref[0, 0, r], 1)],
                                     dst_ref.at[pl.ds(r, 1)], sem)

    def start(r, carry):
        row_copy(r).start()
        return carry

    def wait(r, carry):
        row_copy(r).wait()
        return carry

    lax.fori_loop(0, tm, start, 0, unroll=ROW_DMA_UNROLL)
    lax.fori_loop(0, tm, wait, 0, unroll=ROW_DMA_UNROLL)


def _unpermute(pos, src, n_rows_out):
    nt, _, tm = pos.shape
    width = src.shape[1]
    return pl.pallas_call(
        _unpermute_kernel,
        out_shape=jax.ShapeDtypeStruct((n_rows_out, width), src.dtype),
        grid=(nt,),
        in_specs=[pl.BlockSpec((1, 1, tm), lambda t: (t, 0, 0), memory_space=pltpu.SMEM),
                  pl.BlockSpec(memory_space=pl.ANY)],
        out_specs=pl.BlockSpec((tm, width), lambda t: (t, 0)),
        scratch_shapes=[pltpu.SemaphoreType.DMA],
        compiler_params=_cparams(("arbitrary",)),
        name="unpermute_rows",
    )(pos, src)


def _moe_kernel(ea_ref, eb_ref, nu_ref, xs_ref, gf_ref, wgu_a, wdn_a, wgu_b, wdn_b, y_ref):
    p = pl.program_id(0)
    d = y_ref.shape[1]

    @pl.when(p < nu_ref[0])
    def _():
        x = xs_ref[:, :d]
        wa = xs_ref[:, d:d + 1]
        wb = xs_ref[:, d + 1:d + 2]
        hn = (_rms(x) * gf_ref[...]).astype(BF16)

        def expert(wgu, wdn):
            gu = _dot(hn, wgu[0])
            gate = gu[:, :D_FF]
            he = gate * _sigmoid(gate) * gu[:, D_FF:]
            return _dot(he.astype(BF16), wdn[0])

        y_ref[...] = x + wa * expert(wgu_a, wdn_a) + wb * expert(wgu_b, wdn_b)

    @pl.when(p >= nu_ref[0])
    def _():
        y_ref[...] = jnp.zeros(y_ref.shape, y_ref.dtype)


def _moe(ea, eb, nu, xs, g_ffn, w_gu, w_dn, moe_tile):
    n_rows, width = xs.shape
    d = width - PAY
    n_tiles = n_rows // moe_tile
    used = lambda p, ea, eb, nu: (jnp.minimum(p, nu[0] - 1), 0)
    return pl.pallas_call(
        _moe_kernel,
        out_shape=jax.ShapeDtypeStruct((n_rows, d), F32),
        grid_spec=pltpu.PrefetchScalarGridSpec(
            num_scalar_prefetch=3,
            grid=(n_tiles,),
            in_specs=[pl.BlockSpec((moe_tile, width), used),
                      pl.BlockSpec((1, d), lambda p, ea, eb, nu: (0, 0)),
                      pl.BlockSpec((1, d, 2 * D_FF), lambda p, ea, eb, nu: (ea[p], 0, 0)),
                      pl.BlockSpec((1, D_FF, d), lambda p, ea, eb, nu: (ea[p], 0, 0)),
                      pl.BlockSpec((1, d, 2 * D_FF), lambda p, ea, eb, nu: (eb[p], 0, 0)),
                      pl.BlockSpec((1, D_FF, d), lambda p, ea, eb, nu: (eb[p], 0, 0))],
            out_specs=pl.BlockSpec((moe_tile, d), lambda p, ea, eb, nu: (p, 0)),
        ),
        compiler_params=_cparams(("arbitrary",), 48),
        name="moe_sorted",
    )(ea, eb, nu, xs, g_ffn.reshape(1, d), w_gu, w_dn, w_gu, w_dn)


def _hier_moe_residual(x1e, cls, g_ffn, w_gu, w_dn):
    n = x1e.shape[0]
    moe_tile = min(MOE_TILE, max(MOE_TILE_MIN, pl.next_power_of_2(n // N_CLASSES) // 2))
    n_tiles = n // moe_tile + N_CLASSES
    n_tiles_pad = -(-n_tiles // V7X_LANES) * V7X_LANES
    pos, cnt, off, tiles = _sort(cls, moe_tile, n_tiles_pad)
    xs = _permute(pos, cnt, off, x1e, n_tiles * moe_tile, moe_tile)
    ys = _moe(tiles[0, :n_tiles], tiles[1, :n_tiles], tiles[2, :1], xs, g_ffn, w_gu, w_dn, moe_tile)
    return _unpermute(pos, ys, n)


def _rotary_tables(pos):
    half = R_DK // 2
    inv = 1.0 / (ROPE_BASE ** jnp.linspace(0.0, 1.0, half, dtype=F32))
    ang = pos.astype(F32)[:, None] * inv[None, :]
    cos = jnp.cos(ang)
    sin = jnp.sin(ang)
    return jnp.concatenate([cos, cos], axis=-1), jnp.concatenate([-sin, sin], axis=-1)


def _token_group(x, pos, lw, lam_init, log_gamma, state0, cache):
    bsz, t_len, d = x.shape
    x2d = x.reshape(bsz * t_len, d)
    cos_tab, sin_tab = _rotary_tables(pos)
    tm = _row_tile(bsz * t_len)
    if t_len < tm:
        reps = tm // t_len
        cos_tab = jnp.tile(cos_tab, (reps, 1))
        sin_tab = jnp.tile(sin_tab, (reps, 1))
    rqk, rv, rgs, dq, kb, vt, dk_rows, dv_rows, sgr, sgd = _in_proj(
        x2d, lw["g_mix"], lw["w_in"], cos_tab, sin_tab, lw["g_q"], lw["g_k"])
    ogr, s_fin = _retention(rqk, rv, rgs, state0, lw["g_ret"], log_gamma, bsz, t_len)
    if cache is None:
        od = _attn_prompt(dq, kb, vt, lw["g_q"], lw["g_k"], lw["lam_p"], lw["g_sub"], lam_init,
                          bsz, t_len)
    else:
        od = _attn_sample(dq, dk_rows, dv_rows, _to_cache_rows(cache[0]), _to_cache_rows(cache[1]),
                          lw["lam_p"], lw["g_sub"], lam_init, bsz, t_len)
    x1e, cls = _merge(ogr, od, sgr, sgd, x2d, lw["w_ret_o"], lw["w_diff_o"], lw["w_out"],
                      lw["g_ffn"], lw["rw_hi"], lw["rw_lo"], lw["rb"])
    y = _hier_moe_residual(x1e, cls, lw["g_ffn"], lw["w_gu"], lw["w_dn"])
    return (y.reshape(bsz, t_len, d), _from_cache_rows(dk_rows, bsz, t_len),
            _from_cache_rows(dv_rows, bsz, t_len), s_fin)


def _to_cache_rows(c):
    b, p, h, w = c.shape
    halves = w // V7X_LANES
    return c.reshape(b * p, h, halves, V7X_LANES).transpose(0, 2, 1, 3).reshape(b * p * h * halves, V7X_LANES)


def _from_cache_rows(rows, bsz, t_len):
    halves = D_DV // V7X_LANES
    r = rows.reshape(bsz * t_len, halves, D_HEADS, V7X_LANES).transpose(0, 2, 1, 3)
    return r.reshape(bsz, t_len, D_HEADS, D_DV)


def _layer_weights(l, g_mix, w_in, g_q, g_k, lambda_q1, lambda_k1, lambda_q2, lambda_k2, g_ret,
                   w_ret_o, g_sub, w_diff_o, w_out, g_ffn, w_group, b_group, w_expert, b_expert,
                   w_gate, w_up, w_down):
    d = w_in.shape[1]
    n_r = N_GROUPS + N_EXPERTS
    rw = jnp.concatenate([w_group[l], w_expert[l]], axis=1).astype(F32).T
    rw = jnp.zeros((CLS_PAD, d), F32).at[:n_r].set(rw)
    rw_hi = rw.astype(BF16)
    rw_lo = (rw - rw_hi.astype(F32)).astype(BF16)
    rb = jnp.concatenate([b_group[l], b_expert[l]]).astype(F32)
    rb = jnp.zeros((CLS_PAD,), F32).at[:n_r].set(rb)
    rb = jnp.broadcast_to(rb[:, None], (CLS_PAD, V7X_LANES))
    return dict(
        g_mix=g_mix[l], w_in=w_in[l].astype(BF16), g_q=g_q[l], g_k=g_k[l],
        lam_p=jnp.stack([lambda_q1[l], lambda_k1[l], lambda_q2[l], lambda_k2[l]]).astype(F32),
        g_ret=g_ret[l], w_ret_o=w_ret_o[l].astype(BF16), g_sub=g_sub[l],
        w_diff_o=w_diff_o[l].astype(BF16), w_out=w_out[l].astype(BF16), g_ffn=g_ffn[l],
        rw_hi=rw_hi, rw_lo=rw_lo, rb=rb,
        w_gu=jnp.concatenate([w_gate[l], w_up[l]], axis=-1).astype(BF16),
        w_dn=w_down[l].astype(BF16))


def kernel(x_prompt, x_sample, cache_k, cache_v, state_ret, g_mix, w_in, g_q, g_k, lambda_q1, lambda_k1, lambda_q2, lambda_k2, g_ret, w_ret_o, g_sub, w_diff_o, w_out, g_ffn, w_group, b_group, w_expert, b_expert, w_gate, w_up, w_down):
    depth = w_in.shape[0]
    bp, tp, _ = x_prompt.shape
    bs, ts, _ = x_sample.shape
    past = cache_k.shape[2]
    log_gamma = jnp.log1p(-jnp.exp2(-5.0 - jnp.arange(R_HEADS, dtype=F32)))
    pos_p = jnp.arange(tp, dtype=jnp.int32)
    pos_s = past + jnp.arange(ts, dtype=jnp.int32)
    zero_state = jnp.zeros((bp, R_HEADS, R_DK, R_DV), F32)
    xp, xs = x_prompt, x_sample
    outs = [[] for _ in range(6)]
    for l in range(depth):
        lam_init = 0.8 - 0.6 * math.exp(-0.3 * l)
        lw = _layer_weights(l, g_mix, w_in, g_q, g_k, lambda_q1, lambda_k1, lambda_q2, lambda_k2,
                            g_ret, w_ret_o, g_sub, w_diff_o, w_out, g_ffn, w_group, b_group,
                            w_expert, b_expert, w_gate, w_up, w_down)
        xp, kp, vp, sp = _token_group(xp, pos_p, lw, lam_init, log_gamma, zero_state, None)
        cache = (cache_k[l], cache_v[l])
        xs, ks, vs, ss = _token_group(xs, pos_s, lw, lam_init, log_gamma,
                                      state_ret[l].astype(F32), cache)
        for lst, val in zip(outs, (kp, vp, sp, ks, vs, ss)):
            lst.append(val)
    return (xp, xs) + tuple(jnp.stack(o) for o in outs)
```

```python
import functools
import math

import jax
import jax.numpy as jnp
from jax import lax
from jax.experimental import pallas as pl
from jax.experimental.pallas import tpu as pltpu

F32 = jnp.float32
BF16 = jnp.bfloat16
I32 = jnp.int32

CHUNK = 64
EPS = 1e-6
R_HEADS = 4
R_DK = 128
R_DV = 256
ROPE_BASE = 10000.0
D_HEADS = 4
D_DH = 128
D_DV = 256
N_GROUPS = 4
EXP_PER_GROUP = 4
N_EXPERTS = N_GROUPS * EXP_PER_GROUP
N_PAIRS = EXP_PER_GROUP * (EXP_PER_GROUP - 1) // 2
N_CLASSES = N_GROUPS * N_PAIRS
D_FF = 512
SEG = 1024
N_SEG = 8

V7X_LANES = 128
V7X_SUBLANES = 8
V7X_VMEM_LIMIT_BYTES = 56 * 1024 * 1024

ROW_TILE = 512
RET_SUPER = 512
ATT_HEADS_PER_STEP = 2
MERGE_TILES_PER_STEP = 2
SORT_TILES_PER_STEP = 8
MOE_TILE_MIN = 32
SAMPLE_CHUNK = 1024
MOE_TILE = 256
ROW_DMA_UNROLL = True
CLS_PAD = 32
NEG_BIG = -1e30
LOG2E = math.log2(math.e)
SCORE_BOUND_MARGIN = 1.01
MAX_SHIFT_LOG2 = 100.0
PAY = V7X_LANES


def _row_tile(n):
    return min(ROW_TILE, n)


def _cparams(sem, vmem_mb=None):
    kw = dict(dimension_semantics=sem)
    if vmem_mb is not None:
        kw["vmem_limit_bytes"] = min(vmem_mb * 1024 * 1024, V7X_VMEM_LIMIT_BYTES)
    return pltpu.CompilerParams(**kw)


def _resident(shape):
    nd = len(shape)
    return pl.BlockSpec(shape, lambda *_: (0,) * nd, pipeline_mode=pl.Buffered(1))


def _rms(x, eps=EPS):
    return x * lax.rsqrt(jnp.mean(x * x, axis=-1, keepdims=True) + eps)


def _sigmoid(x):
    return 1.0 / (1.0 + jnp.exp(-x))


def _dot(a, b):
    return jnp.dot(a, b, preferred_element_type=F32)


def _dot_nt(a, b):
    return lax.dot_general(a, b, (((1,), (1,)), ((), ())), preferred_element_type=F32)


def _dot_tn(a, b):
    return lax.dot_general(a, b, (((0,), (0,)), ((), ())), preferred_element_type=F32)


def _store_cache_rows(ref, z):
    rows = z.shape[0]
    for half in range(2):
        for h in range(D_HEADS):
            lo = h * D_DV + half * V7X_LANES
            ref[pl.ds(half * D_HEADS + h, rows, stride=2 * D_HEADS), :] = z[:, lo:lo + V7X_LANES]


def _inproj_kernel(x_ref, g_ref, w_ref, cos_ref, sin_ref, gq_ref, gk_ref,
                   rqk_ref, rv_ref, rgs_ref, dq_ref, kb_ref, vt_ref, dk_ref, dv_ref, sgr_ref, sgd_ref):
    x = x_ref[...]
    hb = (_rms(x) * g_ref[...]).astype(BF16)

    def seg(s):
        return _dot(hb, w_ref[:, s * SEG:(s + 1) * SEG])

    cos = cos_ref[...]
    sin = sin_ref[...]
    z = seg(0)
    for j in range(2 * R_HEADS):
        v = z[:, j * R_DK:(j + 1) * R_DK]
        r = v * cos + pltpu.roll(v, R_DK // 2, 1) * sin
        if j >= R_HEADS:
            r = r * (R_DK ** -0.5)
        rqk_ref[:, j * R_DK:(j + 1) * R_DK] = r.astype(BF16)

    rv_ref[...] = seg(1).astype(BF16)
    z = seg(2)
    rgs_ref[...] = (z * _sigmoid(z)).astype(BF16)

    z = seg(3)
    gq = gq_ref[...] * (D_DH ** -0.5 * LOG2E)
    for j in range(2 * D_HEADS):
        v = z[:, j * D_DH:(j + 1) * D_DH]
        dq_ref[:, j * D_DH:(j + 1) * D_DH] = (_rms(v) * gq).astype(BF16)
    z = seg(4)
    gk = gk_ref[...]
    kn = jnp.concatenate([_rms(z[:, j * D_DH:(j + 1) * D_DH]) * gk for j in range(2 * D_HEADS)], axis=1)
    kb_ref[...] = kn.astype(BF16)
    _store_cache_rows(dk_ref, kn)
    z = seg(5)
    _store_cache_rows(dv_ref, z)
    vt_ref[0] = z.T.astype(BF16)
    sgr_ref[...] = _sigmoid(seg(6)).astype(BF16)
    sgd_ref[...] = _sigmoid(seg(7)).astype(BF16)


def _in_proj(x2d, g_mix, w_in_bf, cos_tab, sin_tab, g_q, g_k):
    n, d = x2d.shape
    tm = _row_tile(n)
    nt = n // tm
    ntab = cos_tab.shape[0] // tm
    row = lambda w: pl.BlockSpec((tm, w), lambda i: (i, 0))
    tab = pl.BlockSpec((tm, R_DK), lambda i: (i % ntab, 0))
    flat = jax.ShapeDtypeStruct((n, SEG), BF16)
    cache_rows = jax.ShapeDtypeStruct((n * 2 * D_HEADS, V7X_LANES), F32)
    cache_spec = pl.BlockSpec((tm * 2 * D_HEADS, V7X_LANES), lambda i: (i, 0))
    outs = [flat, flat, flat, flat, flat, jax.ShapeDtypeStruct((nt, SEG, tm), BF16),
            cache_rows, cache_rows, flat, flat]
    return pl.pallas_call(
        _inproj_kernel,
        out_shape=outs,
        grid=(nt,),
        in_specs=[row(d), _resident((1, d)), _resident(w_in_bf.shape), tab, tab,
                  _resident((1, D_DH)), _resident((1, D_DH))],
        out_specs=[row(SEG)] * 5 + [pl.BlockSpec((1, SEG, tm), lambda i: (i, 0, 0)),
                                    cache_spec, cache_spec, row(SEG), row(SEG)],
        compiler_params=_cparams(("arbitrary",), 56),
        name="in_proj",
    )(x2d, g_mix.reshape(1, d), w_in_bf, cos_tab, sin_tab,
      g_q.reshape(1, D_DH), g_k.reshape(1, D_DH))


def _ret_kernel(lg_ref, qk_ref, v_ref, rgs_ref, s0_ref, g_ref,
                o_ref, sfin_ref, s_scr, d_scr, lam_scr, wk_scr, dec_scr, *, c_len):
    c = pl.program_id(1)

    @pl.when((pl.program_id(0) == 0) & (c == 0))
    def _():
        t = lax.broadcasted_iota(I32, (c_len, c_len), 0)
        s = lax.broadcasted_iota(I32, (c_len, c_len), 1)
        dist = jnp.abs(t - s).astype(F32)
        vis = (s // CHUNK) <= (t // CHUNK)
        pos_v = lax.broadcasted_iota(I32, (c_len, R_DV), 0).astype(F32)
        pos_k = lax.broadcasted_iota(I32, (c_len, R_DK), 0).astype(F32)
        for h in range(R_HEADS):
            lg = lg_ref[h]
            d_scr[h] = jnp.where(vis, jnp.exp(lg * dist), 0.0)
            lam_scr[h] = jnp.exp(lg * (pos_v + 1.0))
            wk_scr[h] = jnp.exp(lg * (c_len - 1.0 - pos_k))
            dec_scr[h] = jnp.exp(lg * jnp.full((V7X_SUBLANES, R_DV), float(c_len), F32))

    @pl.when(c == 0)
    def _():
        s_scr[...] = s0_ref[0]

    for h in range(R_HEADS):
        q = qk_ref[0, :, h * R_DK:(h + 1) * R_DK]
        k = qk_ref[0, :, (R_HEADS + h) * R_DK:(R_HEADS + h + 1) * R_DK]
        v = v_ref[0, :, h * R_DV:(h + 1) * R_DV]
        state = s_scr[h]
        s = _dot_nt(q, k) * d_scr[h]
        o = _dot(s.astype(BF16), v) + lam_scr[h] * _dot(q, state.astype(BF16))
        kw = (k.astype(F32) * wk_scr[h]).astype(BF16)
        s_scr[h] = state * dec_scr[h][0:1, :] + _dot_tn(kw, v)
        gate = rgs_ref[0, :, h * R_DV:(h + 1) * R_DV].astype(F32)
        o_ref[0, :, h * R_DV:(h + 1) * R_DV] = (_rms(o) * g_ref[...] * gate).astype(BF16)

    @pl.when(c == pl.num_programs(1) - 1)
    def _():
        sfin_ref[0] = s_scr[...]


def _retention(rqk, rv, rgs, state0, g_ret, log_gamma, bsz, t_len):
    c_len = min(RET_SUPER, t_len)
    nc = t_len // c_len
    rqk3 = rqk.reshape(bsz, t_len, SEG)
    rv3 = rv.reshape(bsz, t_len, SEG)
    rgs3 = rgs.reshape(bsz, t_len, SEG)
    rows = pl.BlockSpec((1, c_len, SEG), lambda b, c, lg: (b, c, 0))
    sspec = pl.BlockSpec((1, R_HEADS, R_DK, R_DV), lambda b, c, lg: (b, 0, 0, 0))
    gspec = pl.BlockSpec((1, R_DV), lambda b, c, lg: (0, 0))
    o, sfin = pl.pallas_call(
        functools.partial(_ret_kernel, c_len=c_len),
        out_shape=[jax.ShapeDtypeStruct((bsz, t_len, SEG), BF16),
                   jax.ShapeDtypeStruct((bsz, R_HEADS, R_DK, R_DV), F32)],
        grid_spec=pltpu.PrefetchScalarGridSpec(
            num_scalar_prefetch=1,
            grid=(bsz, nc),
            in_specs=[rows, rows, rows, sspec, gspec],
            out_specs=[rows, sspec],
            scratch_shapes=[pltpu.VMEM((R_HEADS, R_DK, R_DV), F32),
                            pltpu.VMEM((R_HEADS, c_len, c_len), F32),
                            pltpu.VMEM((R_HEADS, c_len, R_DV), F32),
                            pltpu.VMEM((R_HEADS, c_len, R_DK), F32),
                            pltpu.VMEM((R_HEADS, V7X_SUBLANES, R_DV), F32)],
        ),
        compiler_params=_cparams(("arbitrary",) * 2, 32),
        name="retention",
    )(log_gamma, rqk3, rv3, rgs3, state0, g_ret.reshape(1, R_DV))
    return o.reshape(bsz * t_len, SEG), sfin


def _lambda_full(lp_ref, lam_init):
    lp = lp_ref[...]
    a = jnp.sum(lp[0:1] * lp[1:2], axis=-1, keepdims=True)
    b = jnp.sum(lp[2:3] * lp[3:4], axis=-1, keepdims=True)
    return jnp.exp(a) - jnp.exp(b) + lam_init


def _attn_finish(o1, l1, o2, l2, lam, g, lam_init):
    o = o1 / l1 - lam * (o2 / l2)
    return (_rms(o) * g * (1.0 - lam_init)).astype(BF16)


def _attn_prompt_kernel(bound_ref, q_ref, k_ref, vt_ref, lp_ref, g_ref, o_ref,
                        m_scr, l_scr, a_scr, bias, *, tile, heads, lam_init, fixed_shift):
    i = pl.program_id(2)
    w = 2 * D_DH

    @pl.when((pl.program_id(0) == 0) & (pl.program_id(1) == 0) & (i == 0))
    def _():
        kk = lax.broadcasted_iota(I32, (tile, tile), 0)
        qq = lax.broadcasted_iota(I32, (tile, tile), 1)
        bias[...] = jnp.where((kk // CHUNK) <= (qq // CHUNK), 0.0, NEG_BIG)

    def update(s, vts, idx, first):
        if fixed_shift:
            p = jnp.exp2(s - bound_ref[0])
            psum = jnp.sum(p, axis=0, keepdims=True)
            pv = _dot(vts, p.astype(BF16))
            l_scr[idx] = psum if first else l_scr[idx] + psum
            a_scr[idx] = pv if first else a_scr[idx] + pv
            return
        smax = jnp.max(s, axis=0, keepdims=True)
        m_new = smax if first else jnp.maximum(m_scr[idx], smax)
        p = jnp.exp2(s - m_new)
        psum = jnp.sum(p, axis=0, keepdims=True)
        pv = _dot(vts, p.astype(BF16))
        if first:
            l_scr[idx] = psum
            a_scr[idx] = pv
        else:
            alpha = jnp.exp2(m_scr[idx] - m_new)
            l_scr[idx] = alpha * l_scr[idx] + psum
            a_scr[idx] = alpha * a_scr[idx] + pv
        m_scr[idx] = m_new

    def block(hh, j, mask=None, first=False):
        ks = k_ref[0, pl.ds(pl.multiple_of(j * tile, tile), tile), hh * w:(hh + 1) * w]
        vts = vt_ref[0, j, hh * D_DV:(hh + 1) * D_DV, :]
        q = q_ref[0, :, hh * w:(hh + 1) * w]
        s1 = _dot_nt(ks[:, :D_DH], q[:, :D_DH])
        s2 = _dot_nt(ks[:, D_DH:], q[:, D_DH:])
        if mask is not None:
            s1 = s1 + mask
            s2 = s2 + mask
        update(s1, vts, 2 * hh, first)
        update(s2, vts, 2 * hh + 1, first)

    for hh in range(heads):
        block(hh, i, bias[...], first=True)

    def body(jj, carry):
        for hh in range(heads):
            block(hh, 2 * jj)
            block(hh, 2 * jj + 1)
        return carry

    lax.fori_loop(0, i // 2, body, 0)

    @pl.when(i % 2 == 1)
    def _():
        for hh in range(heads):
            block(hh, i - 1)

    lam = _lambda_full(lp_ref, lam_init)
    for hh in range(heads):
        a1, l1, a2, l2 = a_scr[2 * hh], l_scr[2 * hh], a_scr[2 * hh + 1], l_scr[2 * hh + 1]
        ot = a1 * (1.0 / l1) - lam * (a2 * (1.0 / l2))
        o_ref[0, :, hh * D_DV:(hh + 1) * D_DV] = (
            _rms(ot.T) * g_ref[...] * (1.0 - lam_init)).astype(BF16)


def _attn_prompt(dq, kb, vt, g_q, g_k, lam_p, g_sub, lam_init, bsz, t_len):
    tile = vt.shape[2]
    assert t_len % tile == 0 and tile % CHUNK == 0
    nq = t_len // tile
    w = 2 * D_DH
    q3 = dq.reshape(bsz, t_len, SEG)
    k3 = kb.reshape(bsz, t_len, SEG)
    vt4 = vt.reshape(bsz, nq, SEG, tile)
    hps = ATT_HEADS_PER_STEP
    qspec = pl.BlockSpec((1, tile, hps * w), lambda b, h, i: (b, i, h))
    kspec = pl.BlockSpec((1, t_len, hps * w), lambda b, h, i: (b, 0, h))
    vspec = pl.BlockSpec((1, nq, hps * D_DV, tile), lambda b, h, i: (b, 0, h, 0))
    n_stat = 2 * hps
    col = lambda: pltpu.VMEM((n_stat, 1, tile), F32)

    def call(fixed_shift):
        return pl.pallas_call(
            functools.partial(_attn_prompt_kernel, tile=tile, heads=hps, lam_init=lam_init,
                              fixed_shift=fixed_shift),
            out_shape=jax.ShapeDtypeStruct((bsz, t_len, SEG), BF16),
            grid=(bsz, D_HEADS // hps, nq),
            in_specs=[pl.BlockSpec(memory_space=pltpu.SMEM), qspec, kspec, vspec,
                      pl.BlockSpec((4, D_DH), lambda b, h, i: (0, 0)),
                      pl.BlockSpec((1, D_DV), lambda b, h, i: (0, 0))],
            out_specs=qspec,
            scratch_shapes=[col(), col(), pltpu.VMEM((n_stat, D_DV, tile), F32),
                            pltpu.VMEM((tile, tile), F32)],
            compiler_params=_cparams(("arbitrary",) * 3, 40),
            name="attn_prompt_fixed" if fixed_shift else "attn_prompt_online",
        )

    bound = SCORE_BOUND_MARGIN * D_DH * (D_DH ** -0.5 * LOG2E) * jnp.max(jnp.abs(g_q * g_k))
    args = (bound.reshape(1).astype(F32), q3, k3, vt4, lam_p, g_sub.reshape(1, D_DV))
    out = lax.cond(2.0 * bound < MAX_SHIFT_LOG2,
                   lambda *a: call(True)(*a), lambda *a: call(False)(*a), *args)
    return out.reshape(bsz * t_len, SEG)


def _attn_sample_kernel(q_ref, kc_ref, vc_ref, kn_ref, vn_ref, lp_ref, g_ref, o_ref,
                        m_scr, l_scr, a_scr, *, lam_init, chunk, t_new):
    c = pl.program_id(1)

    @pl.when(c == 0)
    def _():
        m_scr[...] = jnp.full(m_scr.shape, NEG_BIG, F32)
        l_scr[...] = jnp.zeros(l_scr.shape, F32)
        a_scr[...] = jnp.zeros(a_scr.shape, F32)

    q = q_ref[0]

    def head_rows(ref, h, half, n):
        return ref[pl.ds(half * D_HEADS + h, n, stride=2 * D_HEADS), :].astype(BF16)

    def absorb(k_src, v_src, n):
        for h in range(D_HEADS):
            v = jnp.concatenate([head_rows(v_src, h, 0, n), head_rows(v_src, h, 1, n)], axis=1)
            for mp in range(2):
                idx = 2 * h + mp
                lo = h * 2 * D_DH + mp * D_DH
                s = _dot_nt(q[:, lo:lo + D_DH], head_rows(k_src, h, mp, n))
                m_old = m_scr[idx]
                m_new = jnp.maximum(m_old, jnp.max(s, axis=-1, keepdims=True))
                alpha = jnp.exp2(m_old - m_new)
                p = jnp.exp2(s - m_new)
                l_scr[idx] = alpha * l_scr[idx] + jnp.sum(p, axis=-1, keepdims=True)
                a_scr[idx] = alpha * a_scr[idx] + _dot(p.astype(BF16), v)
                m_scr[idx] = m_new

    absorb(kc_ref, vc_ref, chunk)

    @pl.when(c == pl.num_programs(1) - 1)
    def _():
        absorb(kn_ref, vn_ref, t_new)
        lam = _lambda_full(lp_ref, lam_init)
        for h in range(D_HEADS):
            o_ref[0, :, h * D_DV:(h + 1) * D_DV] = _attn_finish(
                a_scr[2 * h], l_scr[2 * h], a_scr[2 * h + 1], l_scr[2 * h + 1], lam, g_ref[...], lam_init)


def _attn_sample(dq, dk_rows, dv_rows, cache_k_rows, cache_v_rows, lam_p, g_sub, lam_init, bsz, t_len):
    rpf = 2 * D_HEADS
    past = cache_k_rows.shape[0] // (bsz * rpf)
    chunk = min(SAMPLE_CHUNK, past)
    nc = past // chunk
    q3 = dq.reshape(bsz, t_len, SEG)
    qspec = pl.BlockSpec((1, t_len, SEG), lambda b, c: (b, 0, 0))
    cspec = pl.BlockSpec((chunk * rpf, V7X_LANES), lambda b, c: (b * nc + c, 0))
    nspec = pl.BlockSpec((t_len * rpf, V7X_LANES), lambda b, c: (b, 0))
    n_stat = 2 * D_HEADS
    out = pl.pallas_call(
        functools.partial(_attn_sample_kernel, lam_init=lam_init, chunk=chunk, t_new=t_len),
        out_shape=jax.ShapeDtypeStruct((bsz, t_len, SEG), BF16),
        grid=(bsz, nc),
        in_specs=[qspec, cspec, cspec, nspec, nspec,
                  pl.BlockSpec((4, D_DH), lambda b, c: (0, 0)),
                  pl.BlockSpec((1, D_DV), lambda b, c: (0, 0))],
        out_specs=qspec,
        scratch_shapes=[pltpu.VMEM((n_stat, t_len, 1), F32), pltpu.VMEM((n_stat, t_len, 1), F32),
                        pltpu.VMEM((n_stat, t_len, D_DV), F32)],
        compiler_params=_cparams(("arbitrary",) * 2, 32),
        name="attn_sample",
    )(q3, cache_k_rows, cache_v_rows, dk_rows, dv_rows, lam_p, g_sub.reshape(1, D_DV))
    return out.reshape(bsz * t_len, SEG)


def _split_hi_lo(x):
    hi = x.astype(BF16)
    lo = (x - hi.astype(F32)).astype(BF16)
    return hi, lo


def _merge_kernel(*refs):
    cls_ref = refs[-1]
    for sub in range(cls_ref.shape[0]):
        _merge_tile(sub, cls_ref.shape[2], *refs)


def _merge_tile(sub, tm, ogr_ref, od_ref, sgr_ref, sgd_ref, x_ref, wr_ref, wd_ref, wo_ref,
                gf_ref, rw_hi_ref, rw_lo_ref, rb_ref, x1e_ref, cls_ref):
    rs = slice(sub * tm, (sub + 1) * tm)
    y_r = _dot(ogr_ref[rs, :], wr_ref[...])
    y_d = _dot(od_ref[rs, :], wd_ref[...])
    m = sgr_ref[rs, :].astype(F32) * y_r + sgd_ref[rs, :].astype(F32) * y_d
    x1 = x_ref[rs, :] + _dot(m.astype(BF16), wo_ref[...])

    hn = _rms(x1) * gf_ref[...]
    h_hi, h_lo = _split_hi_lo(hn)
    w_hi = rw_hi_ref[...]
    both = _dot_nt(jnp.concatenate([w_hi, rw_lo_ref[...]], axis=0), h_hi)
    lt = both[:CLS_PAD] + both[CLS_PAD:] + _dot_nt(w_hi, h_lo)
    lt = lt + rb_ref[...][:, 0:1]
    g = [lt[i:i + 1, :] for i in range(N_GROUPS)]
    e = [lt[N_GROUPS + i:N_GROUPS + i + 1, :] for i in range(N_EXPERTS)]

    gmax = functools.reduce(jnp.maximum, g)
    gid = jnp.full(g[0].shape, N_GROUPS - 1, I32)
    for i in range(N_GROUPS - 2, -1, -1):
        gid = jnp.where(g[i] == gmax, i, gid)
    g_w = 1.0 / functools.reduce(lambda a, b: a + b, [jnp.exp(v - gmax) for v in g])

    es = []
    for j in range(EXP_PER_GROUP):
        v = e[(N_GROUPS - 1) * EXP_PER_GROUP + j]
        for i in range(N_GROUPS - 2, -1, -1):
            v = jnp.where(gid == i, e[i * EXP_PER_GROUP + j], v)
        es.append(v)

    def first_argmax(vals):
        mx = functools.reduce(jnp.maximum, vals)
        idx = jnp.full(mx.shape, len(vals) - 1, I32)
        for i in range(len(vals) - 2, -1, -1):
            idx = jnp.where(vals[i] == mx, i, idx)
        return mx, idx

    l1, i1 = first_argmax(es)
    rest = [jnp.where(i1 == j, -jnp.inf, es[j]) for j in range(EXP_PER_GROUP)]
    l2, i2 = first_argmax(rest)
    t = jnp.exp(l2 - l1)
    c1 = g_w / (1.0 + t)
    c2 = g_w * t / (1.0 + t)
    lo = jnp.minimum(i1, i2)
    hi = jnp.maximum(i1, i2)
    base = jnp.where(lo == 0, 0, jnp.where(lo == 1, EXP_PER_GROUP - 1, 2 * EXP_PER_GROUP - 3))
    cls = gid * N_PAIRS + base + hi - lo - 1
    wa = jnp.where(i1 < i2, c1, c2)
    wb = jnp.where(i1 < i2, c2, c1)

    cls_ref[sub] = cls
    rid = lax.broadcasted_iota(I32, (PAY, tm), 0)
    pay = jnp.where(rid == 0, wa, jnp.where(rid == 1, wb, 0.0))
    d = x_ref.shape[1]
    x1e_ref[rs, :d] = x1
    x1e_ref[rs, d:] = pay.T


def _merge(ogr, od, sgr, sgd, x2d, w_ret_o, w_diff_o, w_out, g_ffn, rw_hi, rw_lo, rb):
    n, d = x2d.shape
    tm = _row_tile(n)
    nt = n // tm
    n_sub = MERGE_TILES_PER_STEP if nt % MERGE_TILES_PER_STEP == 0 else 1
    row = lambda w: pl.BlockSpec((n_sub * tm, w), lambda i: (i, 0))
    return pl.pallas_call(
        _merge_kernel,
        out_shape=[jax.ShapeDtypeStruct((n, d + PAY), F32),
                   jax.ShapeDtypeStruct((nt, 1, tm), I32)],
        grid=(nt // n_sub,),
        in_specs=[row(SEG), row(SEG), row(SEG), row(SEG), row(d),
                  _resident(w_ret_o.shape), _resident(w_diff_o.shape), _resident(w_out.shape),
                  _resident((1, d)), _resident(rw_hi.shape), _resident(rw_lo.shape),
                  _resident(rb.shape)],
        out_specs=[row(d + PAY), pl.BlockSpec((n_sub, 1, tm), lambda i: (i, 0, 0))],
        compiler_params=_cparams(("arbitrary",), 48),
        name="merge_route",
    )(ogr, od, sgr, sgd, x2d, w_ret_o, w_diff_o, w_out, g_ffn.reshape(1, d), rw_hi, rw_lo, rb)


def _prefix_excl(x):
    rid = lax.broadcasted_iota(I32, x.shape, 0)
    inc = x
    s = 1
    while s < x.shape[0]:
        inc = inc + jnp.where(rid >= s, pltpu.roll(inc, s, 0), 0.0)
        s *= 2
    return inc - x


def _sort_kernel(cls_ref, pos_ref, cnt_ref, off_ref, tiles_ref, cnt_scr, run_scr, off_scr,
                 *, moe_tile, n_tiles_pad):
    ph = pl.program_id(0)
    t = pl.program_id(1)
    k_tiles, _, tm = cls_ref.shape
    cid = lax.broadcasted_iota(I32, (CLS_PAD, tm), 0)

    def onehot(j):
        return jnp.where(cid == cls_ref[j], 1.0, 0.0)

    @pl.when(jnp.logical_and(ph == 0, t == 0))
    def _():
        cnt_scr[...] = jnp.zeros(cnt_scr.shape, F32)

    @pl.when(ph == 0)
    def _():
        for j in range(k_tiles):
            cnt_scr[...] += jnp.sum(onehot(j), axis=-1, keepdims=True)

    @pl.when(jnp.logical_and(ph == 1, t == 0))
    def _():
        cnt = cnt_scr[...]
        padded = jnp.ceil(cnt / moe_tile) * moe_tile
        off = _prefix_excl(padded)
        off_scr[...] = off
        run_scr[...] = jnp.zeros(run_scr.shape, F32)
        cnt_ref[...] = cnt.astype(I32)
        off_ref[...] = off.astype(I32)
        end = (off + padded)[:, 0:1]
        total = jnp.max(end, axis=0, keepdims=True)
        n_used = total / moe_tile
        p = lax.broadcasted_iota(I32, (1, n_tiles_pad), 1).astype(F32)
        start = jnp.minimum(p, n_used - 1.0) * moe_tile
        cid2 = lax.broadcasted_iota(I32, (CLS_PAD, n_tiles_pad), 0)
        before = jnp.logical_and(end <= start, cid2 < N_CLASSES)
        tcls = jnp.sum(before.astype(F32), axis=0, keepdims=True).astype(I32)
        grp = tcls // N_PAIRS
        pr = tcls - grp * N_PAIRS
        lo = (pr >= EXP_PER_GROUP - 1).astype(I32) + (pr >= 2 * EXP_PER_GROUP - 3).astype(I32)
        base = jnp.where(lo == 0, 0, jnp.where(lo == 1, EXP_PER_GROUP - 1, 2 * EXP_PER_GROUP - 3))
        hi = pr - base + lo + 1
        rid = lax.broadcasted_iota(I32, (V7X_SUBLANES, n_tiles_pad), 0)
        ea = grp * EXP_PER_GROUP + lo
        eb = grp * EXP_PER_GROUP + hi
        nu = jnp.broadcast_to(n_used.astype(I32), (1, n_tiles_pad))
        tiles_ref[...] = jnp.where(rid == 0, ea, jnp.where(rid == 1, eb, jnp.where(rid == 2, nu, 0)))

    @pl.when(ph == 1)
    def _():
        r = lax.broadcasted_iota(I32, (tm, tm), 0)
        c = lax.broadcasted_iota(I32, (tm, tm), 1)
        upper = jnp.where(r <= c, 1.0, 0.0).astype(BF16)
        for j in range(k_tiles):
            oh = onehot(j)
            incl = _dot(oh.astype(BF16), upper)
            slot = off_scr[...][:, 0:1] + run_scr[...][:, 0:1] + incl - 1.0
            pos_ref[j] = jnp.sum(oh * slot, axis=0, keepdims=True).astype(I32)
            run_scr[...] += jnp.sum(oh, axis=-1, keepdims=True)


def _sort(cls, moe_tile, n_tiles_pad):
    nt, _, tm = cls.shape
    k_tiles = SORT_TILES_PER_STEP if nt % SORT_TILES_PER_STEP == 0 else 1
    blk = pl.BlockSpec((k_tiles, 1, tm), lambda ph, t: (t, 0, 0))
    oblk = pl.BlockSpec((k_tiles, 1, tm), lambda ph, t: (t * ph, 0, 0))
    whole = lambda shape: pl.BlockSpec(shape, lambda ph, t: (0, 0))
    return pl.pallas_call(
        functools.partial(_sort_kernel, moe_tile=moe_tile, n_tiles_pad=n_tiles_pad),
        out_shape=[jax.ShapeDtypeStruct((nt, 1, tm), I32),
                   jax.ShapeDtypeStruct((CLS_PAD, V7X_LANES), I32),
                   jax.ShapeDtypeStruct((CLS_PAD, V7X_LANES), I32),
                   jax.ShapeDtypeStruct((V7X_SUBLANES, n_tiles_pad), I32)],
        grid=(2, nt // k_tiles),
        in_specs=[blk],
        out_specs=[oblk, whole((CLS_PAD, V7X_LANES)), whole((CLS_PAD, V7X_LANES)),
                   whole((V7X_SUBLANES, n_tiles_pad))],
        scratch_shapes=[pltpu.VMEM((CLS_PAD, V7X_LANES), F32)] * 3,
        compiler_params=_cparams(("arbitrary",) * 2),
        name="class_sort",
    )(cls)


def _permute_kernel(pos_ref, cnt_ref, off_ref, src_ref, dst_ref, zblk, sem, zsem, *, moe_tile):
    t = pl.program_id(0)
    tm = pos_ref.shape[2]

    def row_copy(r):
        return pltpu.make_async_copy(src_ref.at[pl.ds(r, 1)],
                                     dst_ref.at[pl.ds(pos_ref[0, 0, r], 1)], sem)

    def start(r, carry):
        row_copy(r).start()
        return carry

    def wait(r, carry):
        row_copy(r).wait()
        return carry

    lax.fori_loop(0, tm, start, 0, unroll=ROW_DMA_UNROLL)

    @pl.when(t == 0)
    def _():
        zblk[...] = jnp.zeros(zblk.shape, zblk.dtype)
        used = 0
        for c in range(N_CLASSES):
            cnt = cnt_ref[c, 0]
            off = off_ref[c, 0]
            padded = ((cnt + moe_tile - 1) // moe_tile) * moe_tile
            used = off + padded

            def pad_copy(r, off=off):
                return pltpu.make_async_copy(zblk.at[pl.ds(0, 1)], dst_ref.at[pl.ds(off + r, 1)], zsem)

            def pad_start(r, carry, copy=pad_copy):
                copy(r).start()
                return carry

            def pad_wait(r, carry, copy=pad_copy):
                copy(r).wait()
                return carry

            lax.fori_loop(cnt, padded, pad_start, 0)
            lax.fori_loop(cnt, padded, pad_wait, 0)

        def tile_copy(p):
            return pltpu.make_async_copy(zblk, dst_ref.at[pl.ds(p * moe_tile, moe_tile)], zsem)

        def tile_start(p, carry):
            tile_copy(p).start()
            return carry

        def tile_wait(p, carry):
            tile_copy(p).wait()
            return carry

        first, last = used // moe_tile, dst_ref.shape[0] // moe_tile
        lax.fori_loop(first, last, tile_start, 0)
        lax.fori_loop(first, last, tile_wait, 0)

    lax.fori_loop(0, tm, wait, 0, unroll=ROW_DMA_UNROLL)


def _permute(pos, cnt, off, src, n_rows_out, moe_tile):
    nt, _, tm = pos.shape
    width = src.shape[1]
    smem = lambda shape, imap: pl.BlockSpec(shape, imap, memory_space=pltpu.SMEM)
    return pl.pallas_call(
        functools.partial(_permute_kernel, moe_tile=moe_tile),
        out_shape=jax.ShapeDtypeStruct((n_rows_out, width), src.dtype),
        grid=(nt,),
        in_specs=[smem((1, 1, tm), lambda t: (t, 0, 0)),
                  smem(cnt.shape, lambda t: (0, 0)),
                  smem(off.shape, lambda t: (0, 0)),
                  pl.BlockSpec((tm, width), lambda t: (t, 0))],
        out_specs=pl.BlockSpec(memory_space=pl.ANY),
        scratch_shapes=[pltpu.VMEM((moe_tile, width), src.dtype),
                        pltpu.SemaphoreType.DMA, pltpu.SemaphoreType.DMA],
        compiler_params=_cparams(("arbitrary",)),
        name="permute_rows",
    )(pos, cnt, off, src)


def _unpermute_kernel(pos_ref, src_ref, dst_ref, sem):
    tm = pos_ref.shape[2]

    def row_copy(r):
        return pltpu.make_async_copy(src_ref.at[pl.ds(pos_ref[0, 0, r], 1)],
                                     dst_ref.at[pl.ds(r, 1)], sem)

    def start(r, carry):
        row_copy(r).start()
        return carry

    def wait(r, carry):
        row_copy(r).wait()
        return carry

    lax.fori_loop(0, tm, start, 0, unroll=ROW_DMA_UNROLL)
    lax.fori_loop(0, tm, wait, 0, unroll=ROW_DMA_UNROLL)


def _unpermute(pos, src, n_rows_out):
    nt, _, tm = pos.shape
    width = src.shape[1]
    return pl.pallas_call(
        _unpermute_kernel,
        out_shape=jax.ShapeDtypeStruct((n_rows_out, width), src.dtype),
        grid=(nt,),
        in_specs=[pl.BlockSpec((1, 1, tm), lambda t: (t, 0, 0), memory_space=pltpu.SMEM),
                  pl.BlockSpec(memory_space=pl.ANY)],
        out_specs=pl.BlockSpec((tm, width), lambda t: (t, 0)),
        scratch_shapes=[pltpu.SemaphoreType.DMA],
        compiler_params=_cparams(("arbitrary",)),
        name="unpermute_rows",
    )(pos, src)


def _moe_kernel(ea_ref, eb_ref, nu_ref, xs_ref, gf_ref, wgu_a, wdn_a, wgu_b, wdn_b, y_ref):
    p = pl.program_id(0)
    d = y_ref.shape[1]

    @pl.when(p < nu_ref[0])
    def _():
        x = xs_ref[:, :d]
        wa = xs_ref[:, d:d + 1]
        wb = xs_ref[:, d + 1:d + 2]
        hn = (_rms(x) * gf_ref[...]).astype(BF16)

        def expert(wgu, wdn):
            gu = _dot(hn, wgu[0])
            gate = gu[:, :D_FF]
            he = gate * _sigmoid(gate) * gu[:, D_FF:]
            return _dot(he.astype(BF16), wdn[0])

        y_ref[...] = x + wa * expert(wgu_a, wdn_a) + wb * expert(wgu_b, wdn_b)

    @pl.when(p >= nu_ref[0])
    def _():
        y_ref[...] = jnp.zeros(y_ref.shape, y_ref.dtype)


def _moe(ea, eb, nu, xs, g_ffn, w_gu, w_dn, moe_tile):
    n_rows, width = xs.shape
    d = width - PAY
    n_tiles = n_rows // moe_tile
    used = lambda p, ea, eb, nu: (jnp.minimum(p, nu[0] - 1), 0)
    return pl.pallas_call(
        _moe_kernel,
        out_shape=jax.ShapeDtypeStruct((n_rows, d), F32),
        grid_spec=pltpu.PrefetchScalarGridSpec(
            num_scalar_prefetch=3,
            grid=(n_tiles,),
            in_specs=[pl.BlockSpec((moe_tile, width), used),
                      pl.BlockSpec((1, d), lambda p, ea, eb, nu: (0, 0)),
                      pl.BlockSpec((1, d, 2 * D_FF), lambda p, ea, eb, nu: (ea[p], 0, 0)),
                      pl.BlockSpec((1, D_FF, d), lambda p, ea, eb, nu: (ea[p], 0, 0)),
                      pl.BlockSpec((1, d, 2 * D_FF), lambda p, ea, eb, nu: (eb[p], 0, 0)),
                      pl.BlockSpec((1, D_FF, d), lambda p, ea, eb, nu: (eb[p], 0, 0))],
            out_specs=pl.BlockSpec((moe_tile, d), lambda p, ea, eb, nu: (p, 0)),
        ),
        compiler_params=_cparams(("arbitrary",), 48),
        name="moe_sorted",
    )(ea, eb, nu, xs, g_ffn.reshape(1, d), w_gu, w_dn, w_gu, w_dn)


def _hier_moe_residual(x1e, cls, g_ffn, w_gu, w_dn):
    n = x1e.shape[0]
    moe_tile = min(MOE_TILE, max(MOE_TILE_MIN, pl.next_power_of_2(n // N_CLASSES) // 2))
    n_tiles = n // moe_tile + N_CLASSES
    n_tiles_pad = -(-n_tiles // V7X_LANES) * V7X_LANES
    pos, cnt, off, tiles = _sort(cls, moe_tile, n_tiles_pad)
    xs = _permute(pos, cnt, off, x1e, n_tiles * moe_tile, moe_tile)
    ys = _moe(tiles[0, :n_tiles], tiles[1, :n_tiles], tiles[2, :1], xs, g_ffn, w_gu, w_dn, moe_tile)
    return _unpermute(pos, ys, n)


def _rotary_tables(pos):
    half = R_DK // 2
    inv = 1.0 / (ROPE_BASE ** jnp.linspace(0.0, 1.0, half, dtype=F32))
    ang = pos.astype(F32)[:, None] * inv[None, :]
    cos = jnp.cos(ang)
    sin = jnp.sin(ang)
    return jnp.concatenate([cos, cos], axis=-1), jnp.concatenate([-sin, sin], axis=-1)


def _token_group(x, pos, lw, lam_init, log_gamma, state0, cache):
    bsz, t_len, d = x.shape
    x2d = x.reshape(bsz * t_len, d)
    cos_tab, sin_tab = _rotary_tables(pos)
    tm = _row_tile(bsz * t_len)
    if t_len < tm:
        reps = tm // t_len
        cos_tab = jnp.tile(cos_tab, (reps, 1))
        sin_tab = jnp.tile(sin_tab, (reps, 1))
    rqk, rv, rgs, dq, kb, vt, dk_rows, dv_rows, sgr, sgd = _in_proj(
        x2d, lw["g_mix"], lw["w_in"], cos_tab, sin_tab, lw["g_q"], lw["g_k"])
    ogr, s_fin = _retention(rqk, rv, rgs, state0, lw["g_ret"], log_gamma, bsz, t_len)
    if cache is None:
        od = _attn_prompt(dq, kb, vt, lw["g_q"], lw["g_k"], lw["lam_p"], lw["g_sub"], lam_init,
                          bsz, t_len)
    else:
        od = _attn_sample(dq, dk_rows, dv_rows, _to_cache_rows(cache[0]), _to_cache_rows(cache[1]),
                          lw["lam_p"], lw["g_sub"], lam_init, bsz, t_len)
    x1e, cls = _merge(ogr, od, sgr, sgd, x2d, lw["w_ret_o"], lw["w_diff_o"], lw["w_out"],
                      lw["g_ffn"], lw["rw_hi"], lw["rw_lo"], lw["rb"])
    y = _hier_moe_residual(x1e, cls, lw["g_ffn"], lw["w_gu"], lw["w_dn"])
    return (y.reshape(bsz, t_len, d), _from_cache_rows(dk_rows, bsz, t_len),
            _from_cache_rows(dv_rows, bsz, t_len), s_fin)


def _to_cache_rows(c):
    b, p, h, w = c.shape
    halves = w // V7X_LANES
    return c.reshape(b * p, h, halves, V7X_LANES).transpose(0, 2, 1, 3).reshape(b * p * h * halves, V7X_LANES)


def _from_cache_rows(rows, bsz, t_len):
    halves = D_DV // V7X_LANES
    r = rows.reshape(bsz * t_len, halves, D_HEADS, V7X_LANES).transpose(0, 2, 1, 3)
    return r.reshape(bsz, t_len, D_HEADS, D_DV)


def _layer_weights(l, g_mix, w_in, g_q, g_k, lambda_q1, lambda_k1, lambda_q2, lambda_k2, g_ret,
                   w_ret_o, g_sub, w_diff_o, w_out, g_ffn, w_group, b_group, w_expert, b_expert,
                   w_gate, w_up, w_down):
    d = w_in.shape[1]
    n_r = N_GROUPS + N_EXPERTS
    rw = jnp.concatenate([w_group[l], w_expert[l]], axis=1).astype(F32).T
    rw = jnp.zeros((CLS_PAD, d), F32).at[:n_r].set(rw)
    rw_hi = rw.astype(BF16)
    rw_lo = (rw - rw_hi.astype(F32)).astype(BF16)
    rb = jnp.concatenate([b_group[l], b_expert[l]]).astype(F32)
    rb = jnp.zeros((CLS_PAD,), F32).at[:n_r].set(rb)
    rb = jnp.broadcast_to(rb[:, None], (CLS_PAD, V7X_LANES))
    return dict(
        g_mix=g_mix[l], w_in=w_in[l].astype(BF16), g_q=g_q[l], g_k=g_k[l],
        lam_p=jnp.stack([lambda_q1[l], lambda_k1[l], lambda_q2[l], lambda_k2[l]]).astype(F32),
        g_ret=g_ret[l], w_ret_o=w_ret_o[l].astype(BF16), g_sub=g_sub[l],
        w_diff_o=w_diff_o[l].astype(BF16), w_out=w_out[l].astype(BF16), g_ffn=g_ffn[l],
        rw_hi=rw_hi, rw_lo=rw_lo, rb=rb,
        w_gu=jnp.concatenate([w_gate[l], w_up[l]], axis=-1).astype(BF16),
        w_dn=w_down[l].astype(BF16))


def kernel(x_prompt, x_sample, cache_k, cache_v, state_ret, g_mix, w_in, g_q, g_k, lambda_q1, lambda_k1, lambda_q2, lambda_k2, g_ret, w_ret_o, g_sub, w_diff_o, w_out, g_ffn, w_group, b_group, w_expert, b_expert, w_gate, w_up, w_down):
    depth = w_in.shape[0]
    bp, tp, _ = x_prompt.shape
    bs, ts, _ = x_sample.shape
    past = cache_k.shape[2]
    log_gamma = jnp.log1p(-jnp.exp2(-5.0 - jnp.arange(R_HEADS, dtype=F32)))
    pos_p = jnp.arange(tp, dtype=jnp.int32)
    pos_s = past + jnp.arange(ts, dtype=jnp.int32)
    zero_state = jnp.zeros((bp, R_HEADS, R_DK, R_DV), F32)
    xp, xs = x_prompt, x_sample
    outs = [[] for _ in range(6)]
    for l in range(depth):
        lam_init = 0.8 - 0.6 * math.exp(-0.3 * l)
        lw = _layer_weights(l, g_mix, w_in, g_q, g_k, lambda_q1, lambda_k1, lambda_q2, lambda_k2,
                            g_ret, w_ret_o, g_sub, w_diff_o, w_out, g_ffn, w_group, b_group,
                            w_expert, b_expert, w_gate, w_up, w_down)
        xp, kp, vp, sp = _token_group(xp, pos_p, lw, lam_init, log_gamma, zero_state, None)
        cache = (cache_k[l], cache_v[l])
        xs, ks, vs, ss = _token_group(xs, pos_s, lw, lam_init, log_gamma,
                                      state_ret[l].astype(F32), cache)
        for lst, val in zip(outs, (kp, vp, sp, ks, vs, ss)):
            lst.append(val)
    return (xp, xs) + tuple(jnp.stack(o) for o in outs)
```

```python
import functools
import math

import jax
import jax.numpy as jnp
from jax import lax
from jax.experimental import pallas as pl
from jax.experimental.pallas import tpu as pltpu

F32 = jnp.float32
BF16 = jnp.bfloat16
I32 = jnp.int32

CHUNK = 64
EPS = 1e-6
R_HEADS = 4
R_DK = 128
R_DV = 256
ROPE_BASE = 10000.0
D_HEADS = 4
D_DH = 128
D_DV = 256
N_GROUPS = 4
EXP_PER_GROUP = 4
N_EXPERTS = N_GROUPS * EXP_PER_GROUP
N_PAIRS = EXP_PER_GROUP * (EXP_PER_GROUP - 1) // 2
N_CLASSES = N_GROUPS * N_PAIRS
D_FF = 512
SEG = 1024
N_SEG = 8

V7X_LANES = 128
V7X_SUBLANES = 8
V7X_VMEM_LIMIT_BYTES = 56 * 1024 * 1024

ROW_TILE = 512
RET_SUPER = 512
ATT_HEADS_PER_STEP = 2
MERGE_TILES_PER_STEP = 2
SORT_TILES_PER_STEP = 8
MOE_TILE_MIN = 32
SAMPLE_CHUNK = 1024
MOE_TILE = 256
ROW_DMA_UNROLL = True
CLS_PAD = 32
NEG_BIG = -1e30
LOG2E = math.log2(math.e)
SCORE_BOUND_MARGIN = 1.01
MAX_SHIFT_LOG2 = 100.0
PAY = V7X_LANES


def _row_tile(n):
    return min(ROW_TILE, n)


def _cparams(sem, vmem_mb=None):
    kw = dict(dimension_semantics=sem)
    if vmem_mb is not None:
        kw["vmem_limit_bytes"] = min(vmem_mb * 1024 * 1024, V7X_VMEM_LIMIT_BYTES)
    return pltpu.CompilerParams(**kw)


def _resident(shape):
    nd = len(shape)
    return pl.BlockSpec(shape, lambda *_: (0,) * nd, pipeline_mode=pl.Buffered(1))


def _rms(x, eps=EPS):
    return x * lax.rsqrt(jnp.mean(x * x, axis=-1, keepdims=True) + eps)


def _sigmoid(x):
    return 1.0 / (1.0 + jnp.exp(-x))


def _dot(a, b):
    return jnp.dot(a, b, preferred_element_type=F32)


def _dot_nt(a, b):
    return lax.dot_general(a, b, (((1,), (1,)), ((), ())), preferred_element_type=F32)


def _dot_tn(a, b):
    return lax.dot_general(a, b, (((0,), (0,)), ((), ())), preferred_element_type=F32)


def _store_cache_rows(ref, z):
    rows = z.shape[0]
    for half in range(2):
        for h in range(D_HEADS):
            lo = h * D_DV + half * V7X_LANES
            ref[pl.ds(half * D_HEADS + h, rows, stride=2 * D_HEADS), :] = z[:, lo:lo + V7X_LANES]


def _inproj_kernel(x_ref, g_ref, w_ref, cos_ref, sin_ref, gq_ref, gk_ref,
                   rqk_ref, rv_ref, rgs_ref, dq_ref, kb_ref, vt_ref, dk_ref, dv_ref, sgr_ref, sgd_ref):
    x = x_ref[...]
    hb = (_rms(x) * g_ref[...]).astype(BF16)

    def seg(s):
        return _dot(hb, w_ref[:, s * SEG:(s + 1) * SEG])

    cos = cos_ref[...]
    sin = sin_ref[...]
    z = seg(0)
    for j in range(2 * R_HEADS):
        v = z[:, j * R_DK:(j + 1) * R_DK]
        r = v * cos + pltpu.roll(v, R_DK // 2, 1) * sin
        if j >= R_HEADS:
            r = r * (R_DK ** -0.5)
        rqk_ref[:, j * R_DK:(j + 1) * R_DK] = r.astype(BF16)

    z = seg(2)
    rgs_ref[...] = (z * _sigmoid(z)).astype(BF16)

    z = seg(3)
    gq = gq_ref[...] * (D_DH ** -0.5 * LOG2E)
    for j in range(2 * D_HEADS):
        v = z[:, j * D_DH:(j + 1) * D_DH]
        dq_ref[:, j * D_DH:(j + 1) * D_DH] = (_rms(v) * gq).astype(BF16)
    z = seg(4)
    gk = gk_ref[...]
    kn = jnp.concatenate([_rms(z[:, j * D_DH:(j + 1) * D_DH]) * gk for j in range(2 * D_HEADS)], axis=1)
    kb_ref[...] = kn.astype(BF16)
    _store_cache_rows(dk_ref, kn)
    z = seg(5)
    _store_cache_rows(dv_ref, z)
    vt_ref[0] = z.T.astype(BF16)
    sgr_ref[...] = _sigmoid(seg(6)).astype(BF16)
    sgd_ref[...] = _sigmoid(seg(7)).astype(BF16)
    rv_ref[...] = seg(1).astype(BF16)


def _in_proj(x2d, g_mix, w_in_bf, cos_tab, sin_tab, g_q, g_k):
    n, d = x2d.shape
    tm = _row_tile(n)
    nt = n // tm
    ntab = cos_tab.shape[0] // tm
    row = lambda w: pl.BlockSpec((tm, w), lambda i: (i, 0))
    tab = pl.BlockSpec((tm, R_DK), lambda i: (i % ntab, 0))
    flat = jax.ShapeDtypeStruct((n, SEG), BF16)
    cache_rows = jax.ShapeDtypeStruct((n * 2 * D_HEADS, V7X_LANES), F32)
    cache_spec = pl.BlockSpec((tm * 2 * D_HEADS, V7X_LANES), lambda i: (i, 0))
    outs = [flat, flat, flat, flat, flat, jax.ShapeDtypeStruct((nt, SEG, tm), BF16),
            cache_rows, cache_rows, flat, flat]
    return pl.pallas_call(
        _inproj_kernel,
        out_shape=outs,
        grid=(nt,),
        in_specs=[row(d), _resident((1, d)), _resident(w_in_bf.shape), tab, tab,
                  _resident((1, D_DH)), _resident((1, D_DH))],
        out_specs=[row(SEG)] * 5 + [pl.BlockSpec((1, SEG, tm), lambda i: (i, 0, 0)),
                                    cache_spec, cache_spec, row(SEG), row(SEG)],
        compiler_params=_cparams(("arbitrary",), 56),
        name="in_proj",
    )(x2d, g_mix.reshape(1, d), w_in_bf, cos_tab, sin_tab,
      g_q.reshape(1, D_DH), g_k.reshape(1, D_DH))


def _ret_kernel(lg_ref, qk_ref, v_ref, rgs_ref, s0_ref, g_ref,
                o_ref, sfin_ref, s_scr, d_scr, lam_scr, wk_scr, dec_scr, *, c_len):
    c = pl.program_id(1)

    @pl.when((pl.program_id(0) == 0) & (c == 0))
    def _():
        t = lax.broadcasted_iota(I32, (c_len, c_len), 0)
        s = lax.broadcasted_iota(I32, (c_len, c_len), 1)
        dist = jnp.abs(t - s).astype(F32)
        vis = (s // CHUNK) <= (t // CHUNK)
        pos_v = lax.broadcasted_iota(I32, (c_len, R_DV), 0).astype(F32)
        pos_k = lax.broadcasted_iota(I32, (c_len, R_DK), 0).astype(F32)
        for h in range(R_HEADS):
            lg = lg_ref[h]
            d_scr[h] = jnp.where(vis, jnp.exp(lg * dist), 0.0)
            lam_scr[h] = jnp.exp(lg * (pos_v + 1.0))
            wk_scr[h] = jnp.exp(lg * (c_len - 1.0 - pos_k))
            dec_scr[h] = jnp.exp(lg * jnp.full((V7X_SUBLANES, R_DV), float(c_len), F32))

    @pl.when(c == 0)
    def _():
        s_scr[...] = s0_ref[0]

    for h in range(R_HEADS):
        q = qk_ref[0, :, h * R_DK:(h + 1) * R_DK]
        k = qk_ref[0, :, (R_HEADS + h) * R_DK:(R_HEADS + h + 1) * R_DK]
        v = v_ref[0, :, h * R_DV:(h + 1) * R_DV]
        state = s_scr[h]
        s = _dot_nt(q, k) * d_scr[h]
        o = _dot(s.astype(BF16), v) + lam_scr[h] * _dot(q, state.astype(BF16))
        kw = (k.astype(F32) * wk_scr[h]).astype(BF16)
        s_scr[h] = state * dec_scr[h][0:1, :] + _dot_tn(kw, v)
        gate = rgs_ref[0, :, h * R_DV:(h + 1) * R_DV].astype(F32)
        o_ref[0, :, h * R_DV:(h + 1) * R_DV] = (_rms(o) * g_ref[...] * gate).astype(BF16)

    @pl.when(c == pl.num_programs(1) - 1)
    def _():
        sfin_ref[0] = s_scr[...]


def _retention(rqk, rv, rgs, state0, g_ret, log_gamma, bsz, t_len):
    c_len = min(RET_SUPER, t_len)
    nc = t_len // c_len
    rqk3 = rqk.reshape(bsz, t_len, SEG)
    rv3 = rv.reshape(bsz, t_len, SEG)
    rgs3 = rgs.reshape(bsz, t_len, SEG)
    rows = pl.BlockSpec((1, c_len, SEG), lambda b, c, lg: (b, c, 0))
    sspec = pl.BlockSpec((1, R_HEADS, R_DK, R_DV), lambda b, c, lg: (b, 0, 0, 0))
    gspec = pl.BlockSpec((1, R_DV), lambda b, c, lg: (0, 0))
    o, sfin = pl.pallas_call(
        functools.partial(_ret_kernel, c_len=c_len),
        out_shape=[jax.ShapeDtypeStruct((bsz, t_len, SEG), BF16),
                   jax.ShapeDtypeStruct((bsz, R_HEADS, R_DK, R_DV), F32)],
        grid_spec=pltpu.PrefetchScalarGridSpec(
            num_scalar_prefetch=1,
            grid=(bsz, nc),
            in_specs=[rows, rows, rows, sspec, gspec],
            out_specs=[rows, sspec],
            scratch_shapes=[pltpu.VMEM((R_HEADS, R_DK, R_DV), F32),
                            pltpu.VMEM((R_HEADS, c_len, c_len), F32),
                            pltpu.VMEM((R_HEADS, c_len, R_DV), F32),
                            pltpu.VMEM((R_HEADS, c_len, R_DK), F32),
                            pltpu.VMEM((R_HEADS, V7X_SUBLANES, R_DV), F32)],
        ),
        compiler_params=_cparams(("arbitrary",) * 2, 32),
        name="retention",
    )(log_gamma, rqk3, rv3, rgs3, state0, g_ret.reshape(1, R_DV))
    return o.reshape(bsz * t_len, SEG), sfin


def _lambda_full(lp_ref, lam_init):
    lp = lp_ref[...]
    a = jnp.sum(lp[0:1] * lp[1:2], axis=-1, keepdims=True)
    b = jnp.sum(lp[2:3] * lp[3:4], axis=-1, keepdims=True)
    return jnp.exp(a) - jnp.exp(b) + lam_init


def _attn_finish(o1, l1, o2, l2, lam, g, lam_init):
    o = o1 / l1 - lam * (o2 / l2)
    return (_rms(o) * g * (1.0 - lam_init)).astype(BF16)


def _attn_prompt_kernel(bound_ref, q_ref, k_ref, vt_ref, lp_ref, g_ref, o_ref,
                        m_scr, l_scr, a_scr, bias, *, tile, heads, lam_init, fixed_shift):
    i = pl.program_id(2)
    w = 2 * D_DH

    @pl.when((pl.program_id(0) == 0) & (pl.program_id(1) == 0) & (i == 0))
    def _():
        kk = lax.broadcasted_iota(I32, (tile, tile), 0)
        qq = lax.broadcasted_iota(I32, (tile, tile), 1)
        bias[...] = jnp.where((kk // CHUNK) <= (qq // CHUNK), 0.0, NEG_BIG)

    def update(s, vts, idx, first):
        if fixed_shift:
            p = jnp.exp2(s - bound_ref[0])
            psum = jnp.sum(p, axis=0, keepdims=True)
            pv = _dot(vts, p.astype(BF16))
            l_scr[idx] = psum if first else l_scr[idx] + psum
            a_scr[idx] = pv if first else a_scr[idx] + pv
            return
        smax = jnp.max(s, axis=0, keepdims=True)
        m_new = smax if first else jnp.maximum(m_scr[idx], smax)
        p = jnp.exp2(s - m_new)
        psum = jnp.sum(p, axis=0, keepdims=True)
        pv = _dot(vts, p.astype(BF16))
        if first:
            l_scr[idx] = psum
            a_scr[idx] = pv
        else:
            alpha = jnp.exp2(m_scr[idx] - m_new)
            l_scr[idx] = alpha * l_scr[idx] + psum
            a_scr[idx] = alpha * a_scr[idx] + pv
        m_scr[idx] = m_new

    def block(hh, j, mask=None, first=False):
        ks = k_ref[0, pl.ds(pl.multiple_of(j * tile, tile), tile), hh * w:(hh + 1) * w]
        vts = vt_ref[0, j, hh * D_DV:(hh + 1) * D_DV, :]
        q = q_ref[0, :, hh * w:(hh + 1) * w]
        s1 = _dot_nt(ks[:, :D_DH], q[:, :D_DH])
        s2 = _dot_nt(ks[:, D_DH:], q[:, D_DH:])
        if mask is not None:
            s1 = s1 + mask
            s2 = s2 + mask
        update(s1, vts, 2 * hh, first)
        update(s2, vts, 2 * hh + 1, first)

    for hh in range(heads):
        block(hh, i, bias[...], first=True)

    def body(jj, carry):
        for hh in range(heads):
            block(hh, 2 * jj)
            block(hh, 2 * jj + 1)
        return carry

    lax.fori_loop(0, i // 2, body, 0)

    @pl.when(i % 2 == 1)
    def _():
        for hh in range(heads):
            block(hh, i - 1)

    lam = _lambda_full(lp_ref, lam_init)
    for hh in range(heads):
        a1, l1, a2, l2 = a_scr[2 * hh], l_scr[2 * hh], a_scr[2 * hh + 1], l_scr[2 * hh + 1]
        ot = a1 * (1.0 / l1) - lam * (a2 * (1.0 / l2))
        scale = lax.rsqrt(jnp.mean(ot * ot, axis=0, keepdims=True) + EPS) * (1.0 - lam_init)
        o_ref[0, 0, hh * D_DV:(hh + 1) * D_DV, :] = (ot * scale * g_ref[...]).astype(BF16)


def _attn_prompt(dq, kb, vt, g_q, g_k, lam_p, g_sub, lam_init, bsz, t_len):
    tile = vt.shape[2]
    assert t_len % tile == 0 and tile % CHUNK == 0
    nq = t_len // tile
    w = 2 * D_DH
    q3 = dq.reshape(bsz, t_len, SEG)
    k3 = kb.reshape(bsz, t_len, SEG)
    vt4 = vt.reshape(bsz, nq, SEG, tile)
    hps = ATT_HEADS_PER_STEP
    qspec = pl.BlockSpec((1, tile, hps * w), lambda b, h, i: (b, i, h))
    kspec = pl.BlockSpec((1, t_len, hps * w), lambda b, h, i: (b, 0, h))
    vspec = pl.BlockSpec((1, nq, hps * D_DV, tile), lambda b, h, i: (b, 0, h, 0))
    n_stat = 2 * hps
    col = lambda: pltpu.VMEM((n_stat, 1, tile), F32)

    def call(fixed_shift):
        return pl.pallas_call(
            functools.partial(_attn_prompt_kernel, tile=tile, heads=hps, lam_init=lam_init,
                              fixed_shift=fixed_shift),
            out_shape=jax.ShapeDtypeStruct((bsz, nq, SEG, tile), BF16),
            grid=(bsz, D_HEADS // hps, nq),
            in_specs=[pl.BlockSpec(memory_space=pltpu.SMEM), qspec, kspec, vspec,
                      pl.BlockSpec((4, D_DH), lambda b, h, i: (0, 0)),
                      pl.BlockSpec((D_DV, tile), lambda b, h, i: (0, 0))],
            out_specs=pl.BlockSpec((1, 1, hps * D_DV, tile), lambda b, h, i: (b, i, h, 0)),
            scratch_shapes=[col(), col(), pltpu.VMEM((n_stat, D_DV, tile), F32),
                            pltpu.VMEM((tile, tile), F32)],
            compiler_params=_cparams(("arbitrary",) * 3, 40),
            name="attn_prompt_fixed" if fixed_shift else "attn_prompt_online",
        )

    bound = SCORE_BOUND_MARGIN * D_DH * (D_DH ** -0.5 * LOG2E) * jnp.max(jnp.abs(g_q * g_k))
    g_col = jnp.broadcast_to(g_sub.astype(F32)[:, None], (D_DV, tile))
    args = (bound.reshape(1).astype(F32), q3, k3, vt4, lam_p, g_col)
    out = lax.cond(2.0 * bound < MAX_SHIFT_LOG2,
                   lambda *a: call(True)(*a), lambda *a: call(False)(*a), *args)
    return out.reshape(bsz * nq, SEG, tile)


def _attn_sample_kernel(q_ref, kc_ref, vc_ref, kn_ref, vn_ref, lp_ref, g_ref, o_ref,
                        m_scr, l_scr, a_scr, *, lam_init, chunk, t_new):
    c = pl.program_id(1)

    @pl.when(c == 0)
    def _():
        m_scr[...] = jnp.full(m_scr.shape, NEG_BIG, F32)
        l_scr[...] = jnp.zeros(l_scr.shape, F32)
        a_scr[...] = jnp.zeros(a_scr.shape, F32)

    q = q_ref[0]

    def head_rows(ref, h, half, n):
        return ref[pl.ds(half * D_HEADS + h, n, stride=2 * D_HEADS), :].astype(BF16)

    def absorb(k_src, v_src, n):
        for h in range(D_HEADS):
            v = jnp.concatenate([head_rows(v_src, h, 0, n), head_rows(v_src, h, 1, n)], axis=1)
            for mp in range(2):
                idx = 2 * h + mp
                lo = h * 2 * D_DH + mp * D_DH
                s = _dot_nt(q[:, lo:lo + D_DH], head_rows(k_src, h, mp, n))
                m_old = m_scr[idx]
                m_new = jnp.maximum(m_old, jnp.max(s, axis=-1, keepdims=True))
                alpha = jnp.exp2(m_old - m_new)
                p = jnp.exp2(s - m_new)
                l_scr[idx] = alpha * l_scr[idx] + jnp.sum(p, axis=-1, keepdims=True)
                a_scr[idx] = alpha * a_scr[idx] + _dot(p.astype(BF16), v)
                m_scr[idx] = m_new

    absorb(kc_ref, vc_ref, chunk)

    @pl.when(c == pl.num_programs(1) - 1)
    def _():
        absorb(kn_ref, vn_ref, t_new)
        lam = _lambda_full(lp_ref, lam_init)
        for h in range(D_HEADS):
            o_ref[0, :, h * D_DV:(h + 1) * D_DV] = _attn_finish(
                a_scr[2 * h], l_scr[2 * h], a_scr[2 * h + 1], l_scr[2 * h + 1], lam, g_ref[...], lam_init)


def _attn_sample(dq, dk_rows, dv_rows, cache_k_rows, cache_v_rows, lam_p, g_sub, lam_init, bsz, t_len):
    rpf = 2 * D_HEADS
    past = cache_k_rows.shape[0] // (bsz * rpf)
    chunk = min(SAMPLE_CHUNK, past)
    nc = past // chunk
    q3 = dq.reshape(bsz, t_len, SEG)
    qspec = pl.BlockSpec((1, t_len, SEG), lambda b, c: (b, 0, 0))
    cspec = pl.BlockSpec((chunk * rpf, V7X_LANES), lambda b, c: (b * nc + c, 0))
    nspec = pl.BlockSpec((t_len * rpf, V7X_LANES), lambda b, c: (b, 0))
    n_stat = 2 * D_HEADS
    out = pl.pallas_call(
        functools.partial(_attn_sample_kernel, lam_init=lam_init, chunk=chunk, t_new=t_len),
        out_shape=jax.ShapeDtypeStruct((bsz, t_len, SEG), BF16),
        grid=(bsz, nc),
        in_specs=[qspec, cspec, cspec, nspec, nspec,
                  pl.BlockSpec((4, D_DH), lambda b, c: (0, 0)),
                  pl.BlockSpec((1, D_DV), lambda b, c: (0, 0))],
        out_specs=qspec,
        scratch_shapes=[pltpu.VMEM((n_stat, t_len, 1), F32), pltpu.VMEM((n_stat, t_len, 1), F32),
                        pltpu.VMEM((n_stat, t_len, D_DV), F32)],
        compiler_params=_cparams(("arbitrary",) * 2, 32),
        name="attn_sample",
    )(q3, cache_k_rows, cache_v_rows, dk_rows, dv_rows, lam_p, g_sub.reshape(1, D_DV))
    return out.reshape(bsz * t_len, SEG)


def _split_hi_lo(x):
    hi = x.astype(BF16)
    lo = (x - hi.astype(F32)).astype(BF16)
    return hi, lo


def _merge_kernel(*refs):
    cls_ref = refs[-1]
    for sub in range(cls_ref.shape[0]):
        _merge_tile(sub, cls_ref.shape[2], *refs)


def _merge_tile(sub, tm, ogr_ref, od_ref, sgr_ref, sgd_ref, x_ref, wr_ref, wd_ref, wo_ref,
                gf_ref, rw_hi_ref, rw_lo_ref, rb_ref, x1e_ref, cls_ref):
    rs = slice(sub * tm, (sub + 1) * tm)
    y_r = _dot(ogr_ref[rs, :], wr_ref[...])
    y_d = _dot_tn(od_ref[sub], wd_ref[...])
    m = sgr_ref[rs, :].astype(F32) * y_r + sgd_ref[rs, :].astype(F32) * y_d
    x1 = x_ref[rs, :] + _dot(m.astype(BF16), wo_ref[...])

    hn = _rms(x1) * gf_ref[...]
    h_hi, h_lo = _split_hi_lo(hn)
    w_hi = rw_hi_ref[...]
    both = _dot_nt(jnp.concatenate([w_hi, rw_lo_ref[...]], axis=0), h_hi)
    lt = both[:CLS_PAD] + both[CLS_PAD:] + _dot_nt(w_hi, h_lo)
    lt = lt + rb_ref[...][:, 0:1]
    g = [lt[i:i + 1, :] for i in range(N_GROUPS)]
    e = [lt[N_GROUPS + i:N_GROUPS + i + 1, :] for i in range(N_EXPERTS)]

    gmax = functools.reduce(jnp.maximum, g)
    gid = jnp.full(g[0].shape, N_GROUPS - 1, I32)
    for i in range(N_GROUPS - 2, -1, -1):
        gid = jnp.where(g[i] == gmax, i, gid)
    g_w = 1.0 / functools.reduce(lambda a, b: a + b, [jnp.exp(v - gmax) for v in g])

    es = []
    for j in range(EXP_PER_GROUP):
        v = e[(N_GROUPS - 1) * EXP_PER_GROUP + j]
        for i in range(N_GROUPS - 2, -1, -1):
            v = jnp.where(gid == i, e[i * EXP_PER_GROUP + j], v)
        es.append(v)

    def first_argmax(vals):
        mx = functools.reduce(jnp.maximum, vals)
        idx = jnp.full(mx.shape, len(vals) - 1, I32)
        for i in range(len(vals) - 2, -1, -1):
            idx = jnp.where(vals[i] == mx, i, idx)
        return mx, idx

    l1, i1 = first_argmax(es)
    rest = [jnp.where(i1 == j, -jnp.inf, es[j]) for j in range(EXP_PER_GROUP)]
    l2, i2 = first_argmax(rest)
    t = jnp.exp(l2 - l1)
    c1 = g_w / (1.0 + t)
    c2 = g_w * t / (1.0 + t)
    lo = jnp.minimum(i1, i2)
    hi = jnp.maximum(i1, i2)
    base = jnp.where(lo == 0, 0, jnp.where(lo == 1, EXP_PER_GROUP - 1, 2 * EXP_PER_GROUP - 3))
    cls = gid * N_PAIRS + base + hi - lo - 1
    wa = jnp.where(i1 < i2, c1, c2)
    wb = jnp.where(i1 < i2, c2, c1)

    cls_ref[sub] = cls
    rid = lax.broadcasted_iota(I32, (PAY, tm), 0)
    pay = jnp.where(rid == 0, wa, jnp.where(rid == 1, wb, 0.0))
    d = x_ref.shape[1]
    x1e_ref[rs, :d] = x1
    x1e_ref[rs, d:] = pay.T


def _merge(ogr, od, sgr, sgd, x2d, w_ret_o, w_diff_o, w_out, g_ffn, rw_hi, rw_lo, rb):
    n, d = x2d.shape
    tm = _row_tile(n)
    nt = n // tm
    n_sub = MERGE_TILES_PER_STEP if nt % MERGE_TILES_PER_STEP == 0 else 1
    row = lambda w: pl.BlockSpec((n_sub * tm, w), lambda i: (i, 0))
    return pl.pallas_call(
        _merge_kernel,
        out_shape=[jax.ShapeDtypeStruct((n, d + PAY), F32),
                   jax.ShapeDtypeStruct((nt, 1, tm), I32)],
        grid=(nt // n_sub,),
        in_specs=[row(SEG), pl.BlockSpec((n_sub, SEG, tm), lambda i: (i, 0, 0)), row(SEG), row(SEG), row(d),
                  _resident(w_ret_o.shape), _resident(w_diff_o.shape), _resident(w_out.shape),
                  _resident((1, d)), _resident(rw_hi.shape), _resident(rw_lo.shape),
                  _resident(rb.shape)],
        out_specs=[row(d + PAY), pl.BlockSpec((n_sub, 1, tm), lambda i: (i, 0, 0))],
        compiler_params=_cparams(("arbitrary",), 48),
        name="merge_route",
    )(ogr, od, sgr, sgd, x2d, w_ret_o, w_diff_o, w_out, g_ffn.reshape(1, d), rw_hi, rw_lo, rb)


def _prefix_excl(x):
    rid = lax.broadcasted_iota(I32, x.shape, 0)
    inc = x
    s = 1
    while s < x.shape[0]:
        inc = inc + jnp.where(rid >= s, pltpu.roll(inc, s, 0), 0.0)
        s *= 2
    return inc - x


def _sort_kernel(cls_ref, pos_ref, cnt_ref, off_ref, tiles_ref, cnt_scr, run_scr, off_scr,
                 *, moe_tile, n_tiles_pad):
    ph = pl.program_id(0)
    t = pl.program_id(1)
    k_tiles, _, tm = cls_ref.shape
    cid = lax.broadcasted_iota(I32, (CLS_PAD, tm), 0)

    def onehot(j):
        return jnp.where(cid == cls_ref[j], 1.0, 0.0)

    @pl.when(jnp.logical_and(ph == 0, t == 0))
    def _():
        cnt_scr[...] = jnp.zeros(cnt_scr.shape, F32)

    @pl.when(ph == 0)
    def _():
        for j in range(k_tiles):
            cnt_scr[...] += jnp.sum(onehot(j), axis=-1, keepdims=True)

    @pl.when(jnp.logical_and(ph == 1, t == 0))
    def _():
        cnt = cnt_scr[...]
        padded = jnp.ceil(cnt / moe_tile) * moe_tile
        off = _prefix_excl(padded)
        off_scr[...] = off
        run_scr[...] = jnp.zeros(run_scr.shape, F32)
        cnt_ref[...] = cnt.astype(I32)
        off_ref[...] = off.astype(I32)
        end = (off + padded)[:, 0:1]
        total = jnp.max(end, axis=0, keepdims=True)
        n_used = total / moe_tile
        p = lax.broadcasted_iota(I32, (1, n_tiles_pad), 1).astype(F32)
        start = jnp.minimum(p, n_used - 1.0) * moe_tile
        cid2 = lax.broadcasted_iota(I32, (CLS_PAD, n_tiles_pad), 0)
        before = jnp.logical_and(end <= start, cid2 < N_CLASSES)
        tcls = jnp.sum(before.astype(F32), axis=0, keepdims=True).astype(I32)
        grp = tcls // N_PAIRS
        pr = tcls - grp * N_PAIRS
        lo = (pr >= EXP_PER_GROUP - 1).astype(I32) + (pr >= 2 * EXP_PER_GROUP - 3).astype(I32)
        base = jnp.where(lo == 0, 0, jnp.where(lo == 1, EXP_PER_GROUP - 1, 2 * EXP_PER_GROUP - 3))
        hi = pr - base + lo + 1
        rid = lax.broadcasted_iota(I32, (V7X_SUBLANES, n_tiles_pad), 0)
        ea = grp * EXP_PER_GROUP + lo
        eb = grp * EXP_PER_GROUP + hi
        nu = jnp.broadcast_to(n_used.astype(I32), (1, n_tiles_pad))
        tiles_ref[...] = jnp.where(rid == 0, ea, jnp.where(rid == 1, eb, jnp.where(rid == 2, nu, 0)))

    @pl.when(ph == 1)
    def _():
        r = lax.broadcasted_iota(I32, (tm, tm), 0)
        c = lax.broadcasted_iota(I32, (tm, tm), 1)
        upper = jnp.where(r <= c, 1.0, 0.0).astype(BF16)
        for j in range(k_tiles):
            oh = onehot(j)
            incl = _dot(oh.astype(BF16), upper)
            slot = off_scr[...][:, 0:1] + run_scr[...][:, 0:1] + incl - 1.0
            pos_ref[j] = jnp.sum(oh * slot, axis=0, keepdims=True).astype(I32)
            run_scr[...] += jnp.sum(oh, axis=-1, keepdims=True)


def _sort(cls, moe_tile, n_tiles_pad):
    nt, _, tm = cls.shape
    k_tiles = SORT_TILES_PER_STEP if nt % SORT_TILES_PER_STEP == 0 else 1
    blk = pl.BlockSpec((k_tiles, 1, tm), lambda ph, t: (t, 0, 0))
    oblk = pl.BlockSpec((k_tiles, 1, tm), lambda ph, t: (t * ph, 0, 0))
    whole = lambda shape: pl.BlockSpec(shape, lambda ph, t: (0, 0))
    return pl.pallas_call(
        functools.partial(_sort_kernel, moe_tile=moe_tile, n_tiles_pad=n_tiles_pad),
        out_shape=[jax.ShapeDtypeStruct((nt, 1, tm), I32),
                   jax.ShapeDtypeStruct((CLS_PAD, V7X_LANES), I32),
                   jax.ShapeDtypeStruct((CLS_PAD, V7X_LANES), I32),
                   jax.ShapeDtypeStruct((V7X_SUBLANES, n_tiles_pad), I32)],
        grid=(2, nt // k_tiles),
        in_specs=[blk],
        out_specs=[oblk, whole((CLS_PAD, V7X_LANES)), whole((CLS_PAD, V7X_LANES)),
                   whole((V7X_SUBLANES, n_tiles_pad))],
        scratch_shapes=[pltpu.VMEM((CLS_PAD, V7X_LANES), F32)] * 3,
        compiler_params=_cparams(("arbitrary",) * 2),
        name="class_sort",
    )(cls)


def _permute_kernel(pos_ref, cnt_ref, off_ref, src_ref, dst_ref, zblk, sem, zsem, *, moe_tile):
    t = pl.program_id(0)
    tm = pos_ref.shape[2]

    def row_copy(r):
        return pltpu.make_async_copy(src_ref.at[pl.ds(r, 1)],
                                     dst_ref.at[pl.ds(pos_ref[0, 0, r], 1)], sem)

    def start(r, carry):
        row_copy(r).start()
        return carry

    def wait(r, carry):
        row_copy(r).wait()
        return carry

    lax.fori_loop(0, tm, start, 0, unroll=ROW_DMA_UNROLL)

    @pl.when(t == 0)
    def _():
        zblk[...] = jnp.zeros(zblk.shape, zblk.dtype)
        used = 0
        for c in range(N_CLASSES):
            cnt = cnt_ref[c, 0]
            off = off_ref[c, 0]
            padded = ((cnt + moe_tile - 1) // moe_tile) * moe_tile
            used = off + padded

            def pad_copy(r, off=off):
                return pltpu.make_async_copy(zblk.at[pl.ds(0, 1)], dst_ref.at[pl.ds(off + r, 1)], zsem)

            def pad_start(r, carry, copy=pad_copy):
                copy(r).start()
                return carry

            def pad_wait(r, carry, copy=pad_copy):
                copy(r).wait()
                return carry

            lax.fori_loop(cnt, padded, pad_start, 0)
            lax.fori_loop(cnt, padded, pad_wait, 0)

        def tile_copy(p):
            return pltpu.make_async_copy(zblk, dst_ref.at[pl.ds(p * moe_tile, moe_tile)], zsem)

        def tile_start(p, carry):
            tile_copy(p).start()
            return carry

        def tile_wait(p, carry):
            tile_copy(p).wait()
            return carry

        first, last = used // moe_tile, dst_ref.shape[0] // moe_tile
        lax.fori_loop(first, last, tile_start, 0)
        lax.fori_loop(first, last, tile_wait, 0)

    lax.fori_loop(0, tm, wait, 0, unroll=ROW_DMA_UNROLL)


def _permute(pos, cnt, off, src, n_rows_out, moe_tile):
    nt, _, tm = pos.shape
    width = src.shape[1]
    smem = lambda shape, imap: pl.BlockSpec(shape, imap, memory_space=pltpu.SMEM)
    return pl.pallas_call(
        functools.partial(_permute_kernel, moe_tile=moe_tile),
        out_shape=jax.ShapeDtypeStruct((n_rows_out, width), src.dtype),
        grid=(nt,),
        in_specs=[smem((1, 1, tm), lambda t: (t, 0, 0)),
                  smem(cnt.shape, lambda t: (0, 0)),
                  smem(off.shape, lambda t: (0, 0)),
                  pl.BlockSpec((tm, width), lambda t: (t, 0))],
        out_specs=pl.BlockSpec(memory_space=pl.ANY),
        scratch_shapes=[pltpu.VMEM((moe_tile, width), src.dtype),
                        pltpu.SemaphoreType.DMA, pltpu.SemaphoreType.DMA],
        compiler_params=_cparams(("arbitrary",)),
        name="permute_rows",
    )(pos, cnt, off, src)


def _store_row_tiles(ref, y):
    rows, width = y.shape
    k = width // V7X_LANES
    for j in range(k):
        ref[pl.ds(j, rows, stride=k), :] = y[:, j * V7X_LANES:(j + 1) * V7X_LANES]


def _load_row_tiles(ref, rows, k):
    return jnp.concatenate([ref[pl.ds(j, rows, stride=k), :] for j in range(k)], axis=1)


def _unpermute_kernel(pos_ref, src_ref, dst_ref, gbuf, sem):
    tm = pos_ref.shape[2]
    k = dst_ref.shape[1] // V7X_LANES

    def row_copy(r):
        slot = pl.multiple_of(pos_ref[0, 0, r] * k, k)
        return pltpu.make_async_copy(src_ref.at[pl.ds(slot, k)], gbuf.at[pl.ds(r * k, k)], sem)

    def start(r, carry):
        row_copy(r).start()
        return carry

    def wait(r, carry):
        row_copy(r).wait()
        return carry

    lax.fori_loop(0, tm, start, 0, unroll=ROW_DMA_UNROLL)
    lax.fori_loop(0, tm, wait, 0, unroll=ROW_DMA_UNROLL)
    dst_ref[...] = _load_row_tiles(gbuf, tm, k)


def _unpermute(pos, src_tiles, n_rows_out, width):
    nt, _, tm = pos.shape
    k = width // V7X_LANES
    return pl.pallas_call(
        _unpermute_kernel,
        out_shape=jax.ShapeDtypeStruct((n_rows_out, width), src_tiles.dtype),
        grid=(nt,),
        in_specs=[pl.BlockSpec((1, 1, tm), lambda t: (t, 0, 0), memory_space=pltpu.SMEM),
                  pl.BlockSpec(memory_space=pl.ANY)],
        out_specs=pl.BlockSpec((tm, width), lambda t: (t, 0)),
        scratch_shapes=[pltpu.VMEM((tm * k, V7X_LANES), src_tiles.dtype), pltpu.SemaphoreType.DMA],
        compiler_params=_cparams(("arbitrary",)),
        name="unpermute_rows",
    )(pos, src_tiles)


def _moe_kernel(ea_ref, eb_ref, nu_ref, xs_ref, gf_ref, wgu_a, wdn_a, wgu_b, wdn_b, y_ref):
    p = pl.program_id(0)
    d = xs_ref.shape[1] - PAY

    @pl.when(p < nu_ref[0])
    def _():
        x = xs_ref[:, :d]
        wa = xs_ref[:, d:d + 1]
        wb = xs_ref[:, d + 1:d + 2]
        hn = (_rms(x) * gf_ref[...]).astype(BF16)

        def expert(wgu, wdn):
            gu = _dot(hn, wgu[0])
            gate = gu[:, :D_FF]
            he = gate * _sigmoid(gate) * gu[:, D_FF:]
            return _dot(he.astype(BF16), wdn[0])

        _store_row_tiles(y_ref, x + wa * expert(wgu_a, wdn_a) + wb * expert(wgu_b, wdn_b))

    @pl.when(p >= nu_ref[0])
    def _():
        y_ref[...] = jnp.zeros(y_ref.shape, y_ref.dtype)


def _moe(ea, eb, nu, xs, g_ffn, w_gu, w_dn, moe_tile):
    n_rows, width = xs.shape
    d = width - PAY
    k = d // V7X_LANES
    n_tiles = n_rows // moe_tile
    used = lambda p, ea, eb, nu: (jnp.minimum(p, nu[0] - 1), 0)
    return pl.pallas_call(
        _moe_kernel,
        out_shape=jax.ShapeDtypeStruct((n_rows * k, V7X_LANES), F32),
        grid_spec=pltpu.PrefetchScalarGridSpec(
            num_scalar_prefetch=3,
            grid=(n_tiles,),
            in_specs=[pl.BlockSpec((moe_tile, width), used),
                      pl.BlockSpec((1, d), lambda p, ea, eb, nu: (0, 0)),
                      pl.BlockSpec((1, d, 2 * D_FF), lambda p, ea, eb, nu: (ea[p], 0, 0)),
                      pl.BlockSpec((1, D_FF, d), lambda p, ea, eb, nu: (ea[p], 0, 0)),
                      pl.BlockSpec((1, d, 2 * D_FF), lambda p, ea, eb, nu: (eb[p], 0, 0)),
                      pl.BlockSpec((1, D_FF, d), lambda p, ea, eb, nu: (eb[p], 0, 0))],
            out_specs=pl.BlockSpec((moe_tile * k, V7X_LANES), lambda p, ea, eb, nu: (p, 0)),
        ),
        compiler_params=_cparams(("arbitrary",), 48),
        name="moe_sorted",
    )(ea, eb, nu, xs, g_ffn.reshape(1, d), w_gu, w_dn, w_gu, w_dn)


def _hier_moe_residual(x1e, cls, g_ffn, w_gu, w_dn):
    n = x1e.shape[0]
    moe_tile = min(MOE_TILE, max(MOE_TILE_MIN, pl.next_power_of_2(n // N_CLASSES) // 2))
    n_tiles = n // moe_tile + N_CLASSES
    n_tiles_pad = -(-n_tiles // V7X_LANES) * V7X_LANES
    pos, cnt, off, tiles = _sort(cls, moe_tile, n_tiles_pad)
    xs = _permute(pos, cnt, off, x1e, n_tiles * moe_tile, moe_tile)
    ys = _moe(tiles[0, :n_tiles], tiles[1, :n_tiles], tiles[2, :1], xs, g_ffn, w_gu, w_dn, moe_tile)
    return _unpermute(pos, ys, n, x1e.shape[1] - PAY)


def _rotary_tables(pos):
    half = R_DK // 2
    inv = 1.0 / (ROPE_BASE ** jnp.linspace(0.0, 1.0, half, dtype=F32))
    ang = pos.astype(F32)[:, None] * inv[None, :]
    cos = jnp.cos(ang)
    sin = jnp.sin(ang)
    return jnp.concatenate([cos, cos], axis=-1), jnp.concatenate([-sin, sin], axis=-1)


def _token_group(x, pos, lw, lam_init, log_gamma, state0, cache):
    bsz, t_len, d = x.shape
    x2d = x.reshape(bsz * t_len, d)
    cos_tab, sin_tab = _rotary_tables(pos)
    tm = _row_tile(bsz * t_len)
    if t_len < tm:
        reps = tm // t_len
        cos_tab = jnp.tile(cos_tab, (reps, 1))
        sin_tab = jnp.tile(sin_tab, (reps, 1))
    rqk, rv, rgs, dq, kb, vt, dk_rows, dv_rows, sgr, sgd = _in_proj(
        x2d, lw["g_mix"], lw["w_in"], cos_tab, sin_tab, lw["g_q"], lw["g_k"])
    ogr, s_fin = _retention(rqk, rv, rgs, state0, lw["g_ret"], log_gamma, bsz, t_len)
    if cache is None:
        od = _attn_prompt(dq, kb, vt, lw["g_q"], lw["g_k"], lw["lam_p"], lw["g_sub"], lam_init,
                          bsz, t_len)
    else:
        od = _attn_sample(dq, dk_rows, dv_rows, _to_cache_rows(cache[0]), _to_cache_rows(cache[1]),
                          lw["lam_p"], lw["g_sub"], lam_init, bsz, t_len)
        od = od.reshape(-1, tm, SEG).transpose(0, 2, 1)
    x1e, cls = _merge(ogr, od, sgr, sgd, x2d, lw["w_ret_o"], lw["w_diff_o"], lw["w_out"],
                      lw["g_ffn"], lw["rw_hi"], lw["rw_lo"], lw["rb"])
    y = _hier_moe_residual(x1e, cls, lw["g_ffn"], lw["w_gu"], lw["w_dn"])
    return (y.reshape(bsz, t_len, d), _from_cache_rows(dk_rows, bsz, t_len),
            _from_cache_rows(dv_rows, bsz, t_len), s_fin)


def _to_cache_rows(c):
    b, p, h, w = c.shape
    halves = w // V7X_LANES
    return c.reshape(b * p, h, halves, V7X_LANES).transpose(0, 2, 1, 3).reshape(b * p * h * halves, V7X_LANES)


def _from_cache_rows(rows, bsz, t_len):
    halves = D_DV // V7X_LANES
    r = rows.reshape(bsz * t_len, halves, D_HEADS, V7X_LANES).transpose(0, 2, 1, 3)
    return r.reshape(bsz, t_len, D_HEADS, D_DV)


def _layer_weights(l, g_mix, w_in, g_q, g_k, lambda_q1, lambda_k1, lambda_q2, lambda_k2, g_ret,
                   w_ret_o, g_sub, w_diff_o, w_out, g_ffn, w_group, b_group, w_expert, b_expert,
                   w_gate, w_up, w_down):
    d = w_in.shape[1]
    n_r = N_GROUPS + N_EXPERTS
    rw = jnp.concatenate([w_group[l], w_expert[l]], axis=1).astype(F32).T
    rw = jnp.zeros((CLS_PAD, d), F32).at[:n_r].set(rw)
    rw_hi = rw.astype(BF16)
    rw_lo = (rw - rw_hi.astype(F32)).astype(BF16)
    rb = jnp.concatenate([b_group[l], b_expert[l]]).astype(F32)
    rb = jnp.zeros((CLS_PAD,), F32).at[:n_r].set(rb)
    rb = jnp.broadcast_to(rb[:, None], (CLS_PAD, V7X_LANES))
    return dict(
        g_mix=g_mix[l], w_in=w_in[l].astype(BF16), g_q=g_q[l], g_k=g_k[l],
        lam_p=jnp.stack([lambda_q1[l], lambda_k1[l], lambda_q2[l], lambda_k2[l]]).astype(F32),
        g_ret=g_ret[l], w_ret_o=w_ret_o[l].astype(BF16), g_sub=g_sub[l],
        w_diff_o=w_diff_o[l].astype(BF16), w_out=w_out[l].astype(BF16), g_ffn=g_ffn[l],
        rw_hi=rw_hi, rw_lo=rw_lo, rb=rb,
        w_gu=jnp.concatenate([w_gate[l], w_up[l]], axis=-1).astype(BF16),
        w_dn=w_down[l].astype(BF16))


def kernel(x_prompt, x_sample, cache_k, cache_v, state_ret, g_mix, w_in, g_q, g_k, lambda_q1, lambda_k1, lambda_q2, lambda_k2, g_ret, w_ret_o, g_sub, w_diff_o, w_out, g_ffn, w_group, b_group, w_expert, b_expert, w_gate, w_up, w_down):
    depth = w_in.shape[0]
    bp, tp, _ = x_prompt.shape
    bs, ts, _ = x_sample.shape
    past = cache_k.shape[2]
    log_gamma = jnp.log1p(-jnp.exp2(-5.0 - jnp.arange(R_HEADS, dtype=F32)))
    pos_p = jnp.arange(tp, dtype=jnp.int32)
    pos_s = past + jnp.arange(ts, dtype=jnp.int32)
    zero_state = jnp.zeros((bp, R_HEADS, R_DK, R_DV), F32)
    xp, xs = x_prompt, x_sample
    outs = [[] for _ in range(6)]
    for l in range(depth):
        lam_init = 0.8 - 0.6 * math.exp(-0.3 * l)
        lw = _layer_weights(l, g_mix, w_in, g_q, g_k, lambda_q1, lambda_k1, lambda_q2, lambda_k2,
                            g_ret, w_ret_o, g_sub, w_diff_o, w_out, g_ffn, w_group, b_group,
                            w_expert, b_expert, w_gate, w_up, w_down)
        xp, kp, vp, sp = _token_group(xp, pos_p, lw, lam_init, log_gamma, zero_state, None)
        cache = (cache_k[l], cache_v[l])
        xs, ks, vs, ss = _token_group(xs, pos_s, lw, lam_init, log_gamma,
                                      state_ret[l].astype(F32), cache)
        for lst, val in zip(outs, (kp, vp, sp, ks, vs, ss)):
            lst.append(val)
    return (xp, xs) + tuple(jnp.stack(o) for o in outs)
```

```python
import functools
import math

import jax
import jax.numpy as jnp
from jax import lax
from jax.experimental import pallas as pl
from jax.experimental.pallas import tpu as pltpu

F32 = jnp.float32
BF16 = jnp.bfloat16
I32 = jnp.int32

CHUNK = 64
EPS = 1e-6
R_HEADS = 4
R_DK = 128
R_DV = 256
ROPE_BASE = 10000.0
D_HEADS = 4
D_DH = 128
D_DV = 256
N_GROUPS = 4
EXP_PER_GROUP = 4
N_EXPERTS = N_GROUPS * EXP_PER_GROUP
N_PAIRS = EXP_PER_GROUP * (EXP_PER_GROUP - 1) // 2
N_CLASSES = N_GROUPS * N_PAIRS
D_FF = 512
SEG = 1024
N_SEG = 8

V7X_LANES = 128
V7X_SUBLANES = 8
V7X_VMEM_LIMIT_BYTES = 56 * 1024 * 1024

ROW_TILE = 512
RET_SUPER = 512
ATT_HEADS_PER_STEP = 2
MERGE_TILES_PER_STEP = 2
SORT_TILES_PER_STEP = 8
MOE_TILE_MIN = 32
SAMPLE_CHUNK = 1024
MOE_TILE = 256
N_DMA_PRIORITIES = 2
CLS_PAD = 32
NEG_BIG = -1e30
LOG2E = math.log2(math.e)
SCORE_BOUND_MARGIN = 1.01
MAX_SHIFT_LOG2 = 100.0
PAY = V7X_LANES


def _row_tile(n):
    return min(ROW_TILE, n)


def _cparams(sem, vmem_mb=None):
    kw = dict(dimension_semantics=sem)
    if vmem_mb is not None:
        kw["vmem_limit_bytes"] = min(vmem_mb * 1024 * 1024, V7X_VMEM_LIMIT_BYTES)
    return pltpu.CompilerParams(**kw)


def _resident(shape):
    nd = len(shape)
    return pl.BlockSpec(shape, lambda *_: (0,) * nd, pipeline_mode=pl.Buffered(1))


def _rms(x, eps=EPS):
    return x * lax.rsqrt(jnp.mean(x * x, axis=-1, keepdims=True) + eps)


def _sigmoid(x):
    return 1.0 / (1.0 + jnp.exp(-x))


def _dot(a, b):
    return jnp.dot(a, b, preferred_element_type=F32)


def _dot_nt(a, b):
    return lax.dot_general(a, b, (((1,), (1,)), ((), ())), preferred_element_type=F32)


def _dot_tn(a, b):
    return lax.dot_general(a, b, (((0,), (0,)), ((), ())), preferred_element_type=F32)


def _store_cache_rows(ref, z):
    rows = z.shape[0]
    for half in range(2):
        for h in range(D_HEADS):
            lo = h * D_DV + half * V7X_LANES
            ref[pl.ds(half * D_HEADS + h, rows, stride=2 * D_HEADS), :] = z[:, lo:lo + V7X_LANES]


def _inproj_kernel(x_ref, g_ref, w_ref, cos_ref, sin_ref, gq_ref, gk_ref,
                   rqk_ref, rv_ref, rgs_ref, dq_ref, kb_ref, vt_ref, dk_ref, dv_ref, sgr_ref, sgd_ref):
    x = x_ref[...]
    hb = (_rms(x) * g_ref[...]).astype(BF16)

    def seg(s):
        return _dot(hb, w_ref[:, s * SEG:(s + 1) * SEG])

    cos = cos_ref[...]
    sin = sin_ref[...]
    z = seg(0)
    for j in range(2 * R_HEADS):
        v = z[:, j * R_DK:(j + 1) * R_DK]
        r = v * cos + pltpu.roll(v, R_DK // 2, 1) * sin
        if j >= R_HEADS:
            r = r * (R_DK ** -0.5)
        rqk_ref[:, j * R_DK:(j + 1) * R_DK] = r.astype(BF16)

    z = seg(2)
    rgs_ref[...] = (z * _sigmoid(z)).astype(BF16)

    z = seg(3)
    gq = gq_ref[...] * (D_DH ** -0.5 * LOG2E)
    for j in range(2 * D_HEADS):
        v = z[:, j * D_DH:(j + 1) * D_DH]
        dq_ref[:, j * D_DH:(j + 1) * D_DH] = (_rms(v) * gq).astype(BF16)
    z = seg(4)
    gk = gk_ref[...]
    kn = jnp.concatenate([_rms(z[:, j * D_DH:(j + 1) * D_DH]) * gk for j in range(2 * D_HEADS)], axis=1)
    kb_ref[...] = kn.astype(BF16)
    _store_cache_rows(dk_ref, kn)
    z = seg(5)
    _store_cache_rows(dv_ref, z)
    vt_ref[0] = z.T.astype(BF16)
    sgr_ref[...] = _sigmoid(seg(6)).astype(BF16)
    sgd_ref[...] = _sigmoid(seg(7)).astype(BF16)
    rv_ref[...] = seg(1).astype(BF16)


def _in_proj(x2d, g_mix, w_in_bf, cos_tab, sin_tab, g_q, g_k):
    n, d = x2d.shape
    tm = _row_tile(n)
    nt = n // tm
    ntab = cos_tab.shape[0] // tm
    row = lambda w: pl.BlockSpec((tm, w), lambda i: (i, 0))
    tab = pl.BlockSpec((tm, R_DK), lambda i: (i % ntab, 0))
    flat = jax.ShapeDtypeStruct((n, SEG), BF16)
    cache_rows = jax.ShapeDtypeStruct((n * 2 * D_HEADS, V7X_LANES), F32)
    cache_spec = pl.BlockSpec((tm * 2 * D_HEADS, V7X_LANES), lambda i: (i, 0))
    outs = [flat, flat, flat, flat, flat, jax.ShapeDtypeStruct((nt, SEG, tm), BF16),
            cache_rows, cache_rows, flat, flat]
    return pl.pallas_call(
        _inproj_kernel,
        out_shape=outs,
        grid=(nt,),
        in_specs=[row(d), _resident((1, d)), _resident(w_in_bf.shape), tab, tab,
                  _resident((1, D_DH)), _resident((1, D_DH))],
        out_specs=[row(SEG)] * 5 + [pl.BlockSpec((1, SEG, tm), lambda i: (i, 0, 0)),
                                    cache_spec, cache_spec, row(SEG), row(SEG)],
        compiler_params=_cparams(("arbitrary",), 56),
        name="in_proj",
    )(x2d, g_mix.reshape(1, d), w_in_bf, cos_tab, sin_tab,
      g_q.reshape(1, D_DH), g_k.reshape(1, D_DH))


def _ret_kernel(lg_ref, qk_ref, v_ref, rgs_ref, s0_ref, g_ref,
                o_ref, sfin_ref, s_scr, d_scr, lam_scr, wk_scr, dec_scr, *, c_len):
    c = pl.program_id(1)

    @pl.when((pl.program_id(0) == 0) & (c == 0))
    def _():
        t = lax.broadcasted_iota(I32, (c_len, c_len), 0)
        s = lax.broadcasted_iota(I32, (c_len, c_len), 1)
        dist = jnp.abs(t - s).astype(F32)
        vis = (s // CHUNK) <= (t // CHUNK)
        pos_v = lax.broadcasted_iota(I32, (c_len, R_DV), 0).astype(F32)
        pos_k = lax.broadcasted_iota(I32, (c_len, R_DK), 0).astype(F32)
        for h in range(R_HEADS):
            lg = lg_ref[h]
            d_scr[h] = jnp.where(vis, jnp.exp(lg * dist), 0.0)
            lam_scr[h] = jnp.exp(lg * (pos_v + 1.0))
            wk_scr[h] = jnp.exp(lg * (c_len - 1.0 - pos_k))
            dec_scr[h] = jnp.exp(lg * jnp.full((V7X_SUBLANES, R_DV), float(c_len), F32))

    @pl.when(c == 0)
    def _():
        s_scr[...] = s0_ref[0]

    for h in range(R_HEADS):
        q = qk_ref[0, :, h * R_DK:(h + 1) * R_DK]
        k = qk_ref[0, :, (R_HEADS + h) * R_DK:(R_HEADS + h + 1) * R_DK]
        v = v_ref[0, :, h * R_DV:(h + 1) * R_DV]
        state = s_scr[h]
        s = _dot_nt(q, k) * d_scr[h]
        o = _dot(s.astype(BF16), v) + lam_scr[h] * _dot(q, state.astype(BF16))
        kw = (k.astype(F32) * wk_scr[h]).astype(BF16)
        s_scr[h] = state * dec_scr[h][0:1, :] + _dot_tn(kw, v)
        gate = rgs_ref[0, :, h * R_DV:(h + 1) * R_DV].astype(F32)
        o_ref[0, :, h * R_DV:(h + 1) * R_DV] = (_rms(o) * g_ref[...] * gate).astype(BF16)

    @pl.when(c == pl.num_programs(1) - 1)
    def _():
        sfin_ref[0] = s_scr[...]


def _retention(rqk, rv, rgs, state0, g_ret, log_gamma, bsz, t_len):
    c_len = min(RET_SUPER, t_len)
    nc = t_len // c_len
    rqk3 = rqk.reshape(bsz, t_len, SEG)
    rv3 = rv.reshape(bsz, t_len, SEG)
    rgs3 = rgs.reshape(bsz, t_len, SEG)
    rows = pl.BlockSpec((1, c_len, SEG), lambda b, c, lg: (b, c, 0))
    sspec = pl.BlockSpec((1, R_HEADS, R_DK, R_DV), lambda b, c, lg: (b, 0, 0, 0))
    gspec = pl.BlockSpec((1, R_DV), lambda b, c, lg: (0, 0))
    o, sfin = pl.pallas_call(
        functools.partial(_ret_kernel, c_len=c_len),
        out_shape=[jax.ShapeDtypeStruct((bsz, t_len, SEG), BF16),
                   jax.ShapeDtypeStruct((bsz, R_HEADS, R_DK, R_DV), F32)],
        grid_spec=pltpu.PrefetchScalarGridSpec(
            num_scalar_prefetch=1,
            grid=(bsz, nc),
            in_specs=[rows, rows, rows, sspec, gspec],
            out_specs=[rows, sspec],
            scratch_shapes=[pltpu.VMEM((R_HEADS, R_DK, R_DV), F32),
                            pltpu.VMEM((R_HEADS, c_len, c_len), F32),
                            pltpu.VMEM((R_HEADS, c_len, R_DV), F32),
                            pltpu.VMEM((R_HEADS, c_len, R_DK), F32),
                            pltpu.VMEM((R_HEADS, V7X_SUBLANES, R_DV), F32)],
        ),
        compiler_params=_cparams(("arbitrary",) * 2, 32),
        name="retention",
    )(log_gamma, rqk3, rv3, rgs3, state0, g_ret.reshape(1, R_DV))
    return o.reshape(bsz * t_len, SEG), sfin


def _lambda_full(lp_ref, lam_init):
    lp = lp_ref[...]
    a = jnp.sum(lp[0:1] * lp[1:2], axis=-1, keepdims=True)
    b = jnp.sum(lp[2:3] * lp[3:4], axis=-1, keepdims=True)
    return jnp.exp(a) - jnp.exp(b) + lam_init


def _attn_finish(o1, l1, o2, l2, lam, g, lam_init):
    o = o1 / l1 - lam * (o2 / l2)
    return (_rms(o) * g * (1.0 - lam_init)).astype(BF16)


def _attn_prompt_kernel(bound_ref, q_ref, k_ref, vt_ref, lp_ref, g_ref, o_ref,
                        m_scr, l_scr, a_scr, bias, *, tile, heads, lam_init, fixed_shift):
    i = pl.program_id(2)
    w = 2 * D_DH

    @pl.when((pl.program_id(0) == 0) & (pl.program_id(1) == 0) & (i == 0))
    def _():
        kk = lax.broadcasted_iota(I32, (tile, tile), 0)
        qq = lax.broadcasted_iota(I32, (tile, tile), 1)
        bias[...] = jnp.where((kk // CHUNK) <= (qq // CHUNK), 0.0, NEG_BIG)

    def update(s, vts, idx, first):
        if fixed_shift:
            p = jnp.exp2(s - bound_ref[0])
            psum = jnp.sum(p, axis=0, keepdims=True)
            pv = _dot(vts, p.astype(BF16))
            l_scr[idx] = psum if first else l_scr[idx] + psum
            a_scr[idx] = pv if first else a_scr[idx] + pv
            return
        smax = jnp.max(s, axis=0, keepdims=True)
        m_new = smax if first else jnp.maximum(m_scr[idx], smax)
        p = jnp.exp2(s - m_new)
        psum = jnp.sum(p, axis=0, keepdims=True)
        pv = _dot(vts, p.astype(BF16))
        if first:
            l_scr[idx] = psum
            a_scr[idx] = pv
        else:
            alpha = jnp.exp2(m_scr[idx] - m_new)
            l_scr[idx] = alpha * l_scr[idx] + psum
            a_scr[idx] = alpha * a_scr[idx] + pv
        m_scr[idx] = m_new

    def block(hh, j, mask=None, first=False):
        ks = k_ref[0, pl.ds(pl.multiple_of(j * tile, tile), tile), hh * w:(hh + 1) * w]
        vts = vt_ref[0, j, hh * D_DV:(hh + 1) * D_DV, :]
        q = q_ref[0, :, hh * w:(hh + 1) * w]
        s1 = _dot_nt(ks[:, :D_DH], q[:, :D_DH])
        s2 = _dot_nt(ks[:, D_DH:], q[:, D_DH:])
        if mask is not None:
            s1 = s1 + mask
            s2 = s2 + mask
        update(s1, vts, 2 * hh, first)
        update(s2, vts, 2 * hh + 1, first)

    for hh in range(heads):
        block(hh, i, bias[...], first=True)

    def body(jj, carry):
        for hh in range(heads):
            block(hh, 2 * jj)
            block(hh, 2 * jj + 1)
        return carry

    lax.fori_loop(0, i // 2, body, 0)

    @pl.when(i % 2 == 1)
    def _():
        for hh in range(heads):
            block(hh, i - 1)

    lam = _lambda_full(lp_ref, lam_init)
    for hh in range(heads):
        a1, l1, a2, l2 = a_scr[2 * hh], l_scr[2 * hh], a_scr[2 * hh + 1], l_scr[2 * hh + 1]
        ot = a1 * (1.0 / l1) - lam * (a2 * (1.0 / l2))
        scale = lax.rsqrt(jnp.mean(ot * ot, axis=0, keepdims=True) + EPS) * (1.0 - lam_init)
        o_ref[0, 0, hh * D_DV:(hh + 1) * D_DV, :] = (ot * scale * g_ref[...]).astype(BF16)


def _attn_prompt(dq, kb, vt, g_q, g_k, lam_p, g_sub, lam_init, bsz, t_len):
    tile = vt.shape[2]
    assert t_len % tile == 0 and tile % CHUNK == 0
    nq = t_len // tile
    w = 2 * D_DH
    q3 = dq.reshape(bsz, t_len, SEG)
    k3 = kb.reshape(bsz, t_len, SEG)
    vt4 = vt.reshape(bsz, nq, SEG, tile)
    hps = ATT_HEADS_PER_STEP
    qspec = pl.BlockSpec((1, tile, hps * w), lambda b, h, i: (b, i, h))
    kspec = pl.BlockSpec((1, t_len, hps * w), lambda b, h, i: (b, 0, h))
    vspec = pl.BlockSpec((1, nq, hps * D_DV, tile), lambda b, h, i: (b, 0, h, 0))
    n_stat = 2 * hps
    col = lambda: pltpu.VMEM((n_stat, 1, tile), F32)

    def call(fixed_shift):
        return pl.pallas_call(
            functools.partial(_attn_prompt_kernel, tile=tile, heads=hps, lam_init=lam_init,
                              fixed_shift=fixed_shift),
            out_shape=jax.ShapeDtypeStruct((bsz, nq, SEG, tile), BF16),
            grid=(bsz, D_HEADS // hps, nq),
            in_specs=[pl.BlockSpec(memory_space=pltpu.SMEM), qspec, kspec, vspec,
                      pl.BlockSpec((4, D_DH), lambda b, h, i: (0, 0)),
                      pl.BlockSpec((D_DV, tile), lambda b, h, i: (0, 0))],
            out_specs=pl.BlockSpec((1, 1, hps * D_DV, tile), lambda b, h, i: (b, i, h, 0)),
            scratch_shapes=[col(), col(), pltpu.VMEM((n_stat, D_DV, tile), F32),
                            pltpu.VMEM((tile, tile), F32)],
            compiler_params=_cparams(("arbitrary",) * 3, 40),
            name="attn_prompt_fixed" if fixed_shift else "attn_prompt_online",
        )

    bound = SCORE_BOUND_MARGIN * D_DH * (D_DH ** -0.5 * LOG2E) * jnp.max(jnp.abs(g_q * g_k))
    g_col = jnp.broadcast_to(g_sub.astype(F32)[:, None], (D_DV, tile))
    args = (bound.reshape(1).astype(F32), q3, k3, vt4, lam_p, g_col)
    out = lax.cond(2.0 * bound < MAX_SHIFT_LOG2,
                   lambda *a: call(True)(*a), lambda *a: call(False)(*a), *args)
    return out.reshape(bsz * nq, SEG, tile)


def _attn_sample_kernel(q_ref, kc_ref, vc_ref, kn_ref, vn_ref, lp_ref, g_ref, o_ref,
                        m_scr, l_scr, a_scr, *, lam_init, chunk, t_new):
    c = pl.program_id(1)

    @pl.when(c == 0)
    def _():
        m_scr[...] = jnp.full(m_scr.shape, NEG_BIG, F32)
        l_scr[...] = jnp.zeros(l_scr.shape, F32)
        a_scr[...] = jnp.zeros(a_scr.shape, F32)

    q = q_ref[0]

    def head_rows(ref, h, half, n):
        return ref[pl.ds(half * D_HEADS + h, n, stride=2 * D_HEADS), :].astype(BF16)

    def absorb(k_src, v_src, n):
        for h in range(D_HEADS):
            v = jnp.concatenate([head_rows(v_src, h, 0, n), head_rows(v_src, h, 1, n)], axis=1)
            for mp in range(2):
                idx = 2 * h + mp
                lo = h * 2 * D_DH + mp * D_DH
                s = _dot_nt(q[:, lo:lo + D_DH], head_rows(k_src, h, mp, n))
                m_old = m_scr[idx]
                m_new = jnp.maximum(m_old, jnp.max(s, axis=-1, keepdims=True))
                alpha = jnp.exp2(m_old - m_new)
                p = jnp.exp2(s - m_new)
                l_scr[idx] = alpha * l_scr[idx] + jnp.sum(p, axis=-1, keepdims=True)
                a_scr[idx] = alpha * a_scr[idx] + _dot(p.astype(BF16), v)
                m_scr[idx] = m_new

    absorb(kc_ref, vc_ref, chunk)

    @pl.when(c == pl.num_programs(1) - 1)
    def _():
        absorb(kn_ref, vn_ref, t_new)
        lam = _lambda_full(lp_ref, lam_init)
        for h in range(D_HEADS):
            o_ref[0, :, h * D_DV:(h + 1) * D_DV] = _attn_finish(
                a_scr[2 * h], l_scr[2 * h], a_scr[2 * h + 1], l_scr[2 * h + 1], lam, g_ref[...], lam_init)


def _attn_sample(dq, dk_rows, dv_rows, cache_k_rows, cache_v_rows, lam_p, g_sub, lam_init, bsz, t_len):
    rpf = 2 * D_HEADS
    past = cache_k_rows.shape[0] // (bsz * rpf)
    chunk = min(SAMPLE_CHUNK, past)
    nc = past // chunk
    q3 = dq.reshape(bsz, t_len, SEG)
    qspec = pl.BlockSpec((1, t_len, SEG), lambda b, c: (b, 0, 0))
    cspec = pl.BlockSpec((chunk * rpf, V7X_LANES), lambda b, c: (b * nc + c, 0))
    nspec = pl.BlockSpec((t_len * rpf, V7X_LANES), lambda b, c: (b, 0))
    n_stat = 2 * D_HEADS
    out = pl.pallas_call(
        functools.partial(_attn_sample_kernel, lam_init=lam_init, chunk=chunk, t_new=t_len),
        out_shape=jax.ShapeDtypeStruct((bsz, t_len, SEG), BF16),
        grid=(bsz, nc),
        in_specs=[qspec, cspec, cspec, nspec, nspec,
                  pl.BlockSpec((4, D_DH), lambda b, c: (0, 0)),
                  pl.BlockSpec((1, D_DV), lambda b, c: (0, 0))],
        out_specs=qspec,
        scratch_shapes=[pltpu.VMEM((n_stat, t_len, 1), F32), pltpu.VMEM((n_stat, t_len, 1), F32),
                        pltpu.VMEM((n_stat, t_len, D_DV), F32)],
        compiler_params=_cparams(("arbitrary",) * 2, 32),
        name="attn_sample",
    )(q3, cache_k_rows, cache_v_rows, dk_rows, dv_rows, lam_p, g_sub.reshape(1, D_DV))
    return out.reshape(bsz * t_len, SEG)


def _split_hi_lo(x):
    hi = x.astype(BF16)
    lo = (x - hi.astype(F32)).astype(BF16)
    return hi, lo


def _merge_kernel(*refs):
    cls_ref = refs[-1]
    for sub in range(cls_ref.shape[0]):
        _merge_tile(sub, cls_ref.shape[2], *refs)


def _merge_tile(sub, tm, ogr_ref, od_ref, sgr_ref, sgd_ref, x_ref, wr_ref, wd_ref, wo_ref,
                gf_ref, rw_hi_ref, rw_lo_ref, rb_ref, x1e_ref, cls_ref):
    rs = slice(sub * tm, (sub + 1) * tm)
    y_r = _dot(ogr_ref[rs, :], wr_ref[...])
    y_d = _dot_tn(od_ref[sub], wd_ref[...])
    m = sgr_ref[rs, :].astype(F32) * y_r + sgd_ref[rs, :].astype(F32) * y_d
    x1 = x_ref[rs, :] + _dot(m.astype(BF16), wo_ref[...])

    hn = _rms(x1) * gf_ref[...]
    h_hi, h_lo = _split_hi_lo(hn)
    w_hi = rw_hi_ref[...]
    both = _dot_nt(jnp.concatenate([w_hi, rw_lo_ref[...]], axis=0), h_hi)
    lt = both[:CLS_PAD] + both[CLS_PAD:] + _dot_nt(w_hi, h_lo)
    lt = lt + rb_ref[...][:, 0:1]
    g = [lt[i:i + 1, :] for i in range(N_GROUPS)]
    e = [lt[N_GROUPS + i:N_GROUPS + i + 1, :] for i in range(N_EXPERTS)]

    gmax = functools.reduce(jnp.maximum, g)
    gid = jnp.full(g[0].shape, N_GROUPS - 1, I32)
    for i in range(N_GROUPS - 2, -1, -1):
        gid = jnp.where(g[i] == gmax, i, gid)
    g_w = 1.0 / functools.reduce(lambda a, b: a + b, [jnp.exp(v - gmax) for v in g])

    es = []
    for j in range(EXP_PER_GROUP):
        v = e[(N_GROUPS - 1) * EXP_PER_GROUP + j]
        for i in range(N_GROUPS - 2, -1, -1):
            v = jnp.where(gid == i, e[i * EXP_PER_GROUP + j], v)
        es.append(v)

    def first_argmax(vals):
        mx = functools.reduce(jnp.maximum, vals)
        idx = jnp.full(mx.shape, len(vals) - 1, I32)
        for i in range(len(vals) - 2, -1, -1):
            idx = jnp.where(vals[i] == mx, i, idx)
        return mx, idx

    l1, i1 = first_argmax(es)
    rest = [jnp.where(i1 == j, -jnp.inf, es[j]) for j in range(EXP_PER_GROUP)]
    l2, i2 = first_argmax(rest)
    t = jnp.exp(l2 - l1)
    c1 = g_w / (1.0 + t)
    c2 = g_w * t / (1.0 + t)
    lo = jnp.minimum(i1, i2)
    hi = jnp.maximum(i1, i2)
    base = jnp.where(lo == 0, 0, jnp.where(lo == 1, EXP_PER_GROUP - 1, 2 * EXP_PER_GROUP - 3))
    cls = gid * N_PAIRS + base + hi - lo - 1
    wa = jnp.where(i1 < i2, c1, c2)
    wb = jnp.where(i1 < i2, c2, c1)

    cls_ref[sub] = cls
    rid = lax.broadcasted_iota(I32, (PAY, tm), 0)
    pay = jnp.where(rid == 0, wa, jnp.where(rid == 1, wb, 0.0))
    d = x_ref.shape[1]
    x1e_ref[rs, :d] = x1
    x1e_ref[rs, d:] = pay.T


def _merge(ogr, od, sgr, sgd, x2d, w_ret_o, w_diff_o, w_out, g_ffn, rw_hi, rw_lo, rb):
    n, d = x2d.shape
    tm = _row_tile(n)
    nt = n // tm
    n_sub = MERGE_TILES_PER_STEP if nt % MERGE_TILES_PER_STEP == 0 else 1
    row = lambda w: pl.BlockSpec((n_sub * tm, w), lambda i: (i, 0))
    return pl.pallas_call(
        _merge_kernel,
        out_shape=[jax.ShapeDtypeStruct((n, d + PAY), F32),
                   jax.ShapeDtypeStruct((nt, 1, tm), I32)],
        grid=(nt // n_sub,),
        in_specs=[row(SEG), pl.BlockSpec((n_sub, SEG, tm), lambda i: (i, 0, 0)), row(SEG), row(SEG), row(d),
                  _resident(w_ret_o.shape), _resident(w_diff_o.shape), _resident(w_out.shape),
                  _resident((1, d)), _resident(rw_hi.shape), _resident(rw_lo.shape),
                  _resident(rb.shape)],
        out_specs=[row(d + PAY), pl.BlockSpec((n_sub, 1, tm), lambda i: (i, 0, 0))],
        compiler_params=_cparams(("arbitrary",), 48),
        name="merge_route",
    )(ogr, od, sgr, sgd, x2d, w_ret_o, w_diff_o, w_out, g_ffn.reshape(1, d), rw_hi, rw_lo, rb)


def _prefix_excl(x):
    rid = lax.broadcasted_iota(I32, x.shape, 0)
    inc = x
    s = 1
    while s < x.shape[0]:
        inc = inc + jnp.where(rid >= s, pltpu.roll(inc, s, 0), 0.0)
        s *= 2
    return inc - x


def _sort_kernel(cls_ref, pos_ref, cnt_ref, off_ref, tiles_ref, cnt_scr, run_scr, off_scr,
                 *, moe_tile, n_tiles_pad):
    ph = pl.program_id(0)
    t = pl.program_id(1)
    k_tiles, _, tm = cls_ref.shape
    cid = lax.broadcasted_iota(I32, (CLS_PAD, tm), 0)

    def onehot(j):
        return jnp.where(cid == cls_ref[j], 1.0, 0.0)

    @pl.when(jnp.logical_and(ph == 0, t == 0))
    def _():
        cnt_scr[...] = jnp.zeros(cnt_scr.shape, F32)

    @pl.when(ph == 0)
    def _():
        for j in range(k_tiles):
            cnt_scr[...] += jnp.sum(onehot(j), axis=-1, keepdims=True)

    @pl.when(jnp.logical_and(ph == 1, t == 0))
    def _():
        cnt = cnt_scr[...]
        padded = jnp.ceil(cnt / moe_tile) * moe_tile
        off = _prefix_excl(padded)
        off_scr[...] = off
        run_scr[...] = jnp.zeros(run_scr.shape, F32)
        cnt_ref[...] = cnt.astype(I32)
        off_ref[...] = off.astype(I32)
        end = (off + padded)[:, 0:1]
        total = jnp.max(end, axis=0, keepdims=True)
        n_used = total / moe_tile
        p = lax.broadcasted_iota(I32, (1, n_tiles_pad), 1).astype(F32)
        start = jnp.minimum(p, n_used - 1.0) * moe_tile
        cid2 = lax.broadcasted_iota(I32, (CLS_PAD, n_tiles_pad), 0)
        before = jnp.logical_and(end <= start, cid2 < N_CLASSES)
        tcls = jnp.sum(before.astype(F32), axis=0, keepdims=True).astype(I32)
        grp = tcls // N_PAIRS
        pr = tcls - grp * N_PAIRS
        lo = (pr >= EXP_PER_GROUP - 1).astype(I32) + (pr >= 2 * EXP_PER_GROUP - 3).astype(I32)
        base = jnp.where(lo == 0, 0, jnp.where(lo == 1, EXP_PER_GROUP - 1, 2 * EXP_PER_GROUP - 3))
        hi = pr - base + lo + 1
        rid = lax.broadcasted_iota(I32, (V7X_SUBLANES, n_tiles_pad), 0)
        ea = grp * EXP_PER_GROUP + lo
        eb = grp * EXP_PER_GROUP + hi
        nu = jnp.broadcast_to(n_used.astype(I32), (1, n_tiles_pad))
        tiles_ref[...] = jnp.where(rid == 0, ea, jnp.where(rid == 1, eb, jnp.where(rid == 2, nu, 0)))

    @pl.when(ph == 1)
    def _():
        r = lax.broadcasted_iota(I32, (tm, tm), 0)
        c = lax.broadcasted_iota(I32, (tm, tm), 1)
        upper = jnp.where(r <= c, 1.0, 0.0).astype(BF16)
        for j in range(k_tiles):
            oh = onehot(j)
            incl = _dot(oh.astype(BF16), upper)
            slot = off_scr[...][:, 0:1] + run_scr[...][:, 0:1] + incl - 1.0
            pos_ref[j] = jnp.sum(oh * slot, axis=0, keepdims=True).astype(I32)
            run_scr[...] += jnp.sum(oh, axis=-1, keepdims=True)


def _sort(cls, moe_tile, n_tiles_pad):
    nt, _, tm = cls.shape
    k_tiles = SORT_TILES_PER_STEP if nt % SORT_TILES_PER_STEP == 0 else 1
    blk = pl.BlockSpec((k_tiles, 1, tm), lambda ph, t: (t, 0, 0))
    oblk = pl.BlockSpec((k_tiles, 1, tm), lambda ph, t: (t * ph, 0, 0))
    whole = lambda shape: pl.BlockSpec(shape, lambda ph, t: (0, 0))
    return pl.pallas_call(
        functools.partial(_sort_kernel, moe_tile=moe_tile, n_tiles_pad=n_tiles_pad),
        out_shape=[jax.ShapeDtypeStruct((nt, 1, tm), I32),
                   jax.ShapeDtypeStruct((CLS_PAD, V7X_LANES), I32),
                   jax.ShapeDtypeStruct((CLS_PAD, V7X_LANES), I32),
                   jax.ShapeDtypeStruct((V7X_SUBLANES, n_tiles_pad), I32)],
        grid=(2, nt // k_tiles),
        in_specs=[blk],
        out_specs=[oblk, whole((CLS_PAD, V7X_LANES)), whole((CLS_PAD, V7X_LANES)),
                   whole((V7X_SUBLANES, n_tiles_pad))],
        scratch_shapes=[pltpu.VMEM((CLS_PAD, V7X_LANES), F32)] * 3,
        compiler_params=_cparams(("arbitrary",) * 2),
        name="class_sort",
    )(cls)


def _permute_kernel(pos_ref, cnt_ref, off_ref, src_ref, dst_ref, zblk, sem, zsem, *, moe_tile):
    t = pl.program_id(0)
    tm = pos_ref.shape[2]

    def row_copy(r):
        return pltpu.make_async_copy(src_ref.at[pl.ds(r, 1)],
                                     dst_ref.at[pl.ds(pos_ref[0, 0, r], 1)], sem)

    for r in range(tm):
        row_copy(r).start(priority=r % N_DMA_PRIORITIES)

    @pl.when(t == 0)
    def _():
        zblk[...] = jnp.zeros(zblk.shape, zblk.dtype)
        used = 0
        for c in range(N_CLASSES):
            cnt = cnt_ref[c, 0]
            off = off_ref[c, 0]
            padded = ((cnt + moe_tile - 1) // moe_tile) * moe_tile
            used = off + padded

            def pad_copy(r, off=off):
                return pltpu.make_async_copy(zblk.at[pl.ds(0, 1)], dst_ref.at[pl.ds(off + r, 1)], zsem)

            def pad_start(r, carry, copy=pad_copy):
                copy(r).start()
                return carry

            def pad_wait(r, carry, copy=pad_copy):
                copy(r).wait()
                return carry

            lax.fori_loop(cnt, padded, pad_start, 0)
            lax.fori_loop(cnt, padded, pad_wait, 0)

        def tile_copy(p):
            return pltpu.make_async_copy(zblk, dst_ref.at[pl.ds(p * moe_tile, moe_tile)], zsem)

        def tile_start(p, carry):
            tile_copy(p).start()
            return carry

        def tile_wait(p, carry):
            tile_copy(p).wait()
            return carry

        first, last = used // moe_tile, dst_ref.shape[0] // moe_tile
        lax.fori_loop(first, last, tile_start, 0)
        lax.fori_loop(first, last, tile_wait, 0)

    for r in range(tm):
        row_copy(r).wait()


def _permute(pos, cnt, off, src, n_rows_out, moe_tile):
    nt, _, tm = pos.shape
    width = src.shape[1]
    smem = lambda shape, imap: pl.BlockSpec(shape, imap, memory_space=pltpu.SMEM)
    return pl.pallas_call(
        functools.partial(_permute_kernel, moe_tile=moe_tile),
        out_shape=jax.ShapeDtypeStruct((n_rows_out, width), src.dtype),
        grid=(nt,),
        in_specs=[smem((1, 1, tm), lambda t: (t, 0, 0)),
                  smem(cnt.shape, lambda t: (0, 0)),
                  smem(off.shape, lambda t: (0, 0)),
                  pl.BlockSpec((tm, width), lambda t: (t, 0))],
        out_specs=pl.BlockSpec(memory_space=pl.ANY),
        scratch_shapes=[pltpu.VMEM((moe_tile, width), src.dtype),
                        pltpu.SemaphoreType.DMA, pltpu.SemaphoreType.DMA],
        compiler_params=_cparams(("arbitrary",)),
        name="permute_rows",
    )(pos, cnt, off, src)


def _unpermute_kernel(pos_ref, src_ref, dst_ref, sem):
    tm = pos_ref.shape[2]

    def row_copy(r):
        return pltpu.make_async_copy(src_ref.at[pl.ds(pos_ref[0, 0, r], 1)],
                                     dst_ref.at[pl.ds(r, 1)], sem)

    for r in range(tm):
        row_copy(r).start(priority=r % N_DMA_PRIORITIES)
    for r in range(tm):
        row_copy(r).wait()


def _unpermute(pos, src, n_rows_out):
    nt, _, tm = pos.shape
    width = src.shape[1]
    return pl.pallas_call(
        _unpermute_kernel,
        out_shape=jax.ShapeDtypeStruct((n_rows_out, width), src.dtype),
        grid=(nt,),
        in_specs=[pl.BlockSpec((1, 1, tm), lambda t: (t, 0, 0), memory_space=pltpu.SMEM),
                  pl.BlockSpec(memory_space=pl.ANY)],
        out_specs=pl.BlockSpec((tm, width), lambda t: (t, 0)),
        scratch_shapes=[pltpu.SemaphoreType.DMA],
        compiler_params=_cparams(("arbitrary",)),
        name="unpermute_rows",
    )(pos, src)


def _moe_kernel(ea_ref, eb_ref, nu_ref, xs_ref, gf_ref, wgu_a, wdn_a, wgu_b, wdn_b, y_ref):
    p = pl.program_id(0)
    d = y_ref.shape[1]

    @pl.when(p < nu_ref[0])
    def _():
        x = xs_ref[:, :d]
        wa = xs_ref[:, d:d + 1]
        wb = xs_ref[:, d + 1:d + 2]
        hn = (_rms(x) * gf_ref[...]).astype(BF16)

        def expert(wgu, wdn):
            gu = _dot(hn, wgu[0])
            gate = gu[:, :D_FF]
            he = gate * _sigmoid(gate) * gu[:, D_FF:]
            return _dot(he.astype(BF16), wdn[0])

        y_ref[...] = x + wa * expert(wgu_a, wdn_a) + wb * expert(wgu_b, wdn_b)

    @pl.when(p >= nu_ref[0])
    def _():
        y_ref[...] = jnp.zeros(y_ref.shape, y_ref.dtype)


def _moe(ea, eb, nu, xs, g_ffn, w_gu, w_dn, moe_tile):
    n_rows, width = xs.shape
    d = width - PAY
    n_tiles = n_rows // moe_tile
    used = lambda p, ea, eb, nu: (jnp.minimum(p, nu[0] - 1), 0)
    return pl.pallas_call(
        _moe_kernel,
        out_shape=jax.ShapeDtypeStruct((n_rows, d), F32),
        grid_spec=pltpu.PrefetchScalarGridSpec(
            num_scalar_prefetch=3,
            grid=(n_tiles,),
            in_specs=[pl.BlockSpec((moe_tile, width), used),
                      pl.BlockSpec((1, d), lambda p, ea, eb, nu: (0, 0)),
                      pl.BlockSpec((1, d, 2 * D_FF), lambda p, ea, eb, nu: (ea[p], 0, 0)),
                      pl.BlockSpec((1, D_FF, d), lambda p, ea, eb, nu: (ea[p], 0, 0)),
                      pl.BlockSpec((1, d, 2 * D_FF), lambda p, ea, eb, nu: (eb[p], 0, 0)),
                      pl.BlockSpec((1, D_FF, d), lambda p, ea, eb, nu: (eb[p], 0, 0))],
            out_specs=pl.BlockSpec((moe_tile, d), lambda p, ea, eb, nu: (p, 0)),
        ),
        compiler_params=_cparams(("arbitrary",), 48),
        name="moe_sorted",
    )(ea, eb, nu, xs, g_ffn.reshape(1, d), w_gu, w_dn, w_gu, w_dn)


def _hier_moe_residual(x1e, cls, g_ffn, w_gu, w_dn):
    n = x1e.shape[0]
    moe_tile = min(MOE_TILE, max(MOE_TILE_MIN, pl.next_power_of_2(n // N_CLASSES) // 2))
    n_tiles = n // moe_tile + N_CLASSES
    n_tiles_pad = -(-n_tiles // V7X_LANES) * V7X_LANES
    pos, cnt, off, tiles = _sort(cls, moe_tile, n_tiles_pad)
    xs = _permute(pos, cnt, off, x1e, n_tiles * moe_tile, moe_tile)
    ys = _moe(tiles[0, :n_tiles], tiles[1, :n_tiles], tiles[2, :1], xs, g_ffn, w_gu, w_dn, moe_tile)
    return _unpermute(pos, ys, n)


def _rotary_tables(pos):
    half = R_DK // 2
    inv = 1.0 / (ROPE_BASE ** jnp.linspace(0.0, 1.0, half, dtype=F32))
    ang = pos.astype(F32)[:, None] * inv[None, :]
    cos = jnp.cos(ang)
    sin = jnp.sin(ang)
    return jnp.concatenate([cos, cos], axis=-1), jnp.concatenate([-sin, sin], axis=-1)


def _token_group(x, pos, lw, lam_init, log_gamma, state0, cache):
    bsz, t_len, d = x.shape
    x2d = x.reshape(bsz * t_len, d)
    cos_tab, sin_tab = _rotary_tables(pos)
    tm = _row_tile(bsz * t_len)
    if t_len < tm:
        reps = tm // t_len
        cos_tab = jnp.tile(cos_tab, (reps, 1))
        sin_tab = jnp.tile(sin_tab, (reps, 1))
    rqk, rv, rgs, dq, kb, vt, dk_rows, dv_rows, sgr, sgd = _in_proj(
        x2d, lw["g_mix"], lw["w_in"], cos_tab, sin_tab, lw["g_q"], lw["g_k"])
    ogr, s_fin = _retention(rqk, rv, rgs, state0, lw["g_ret"], log_gamma, bsz, t_len)
    if cache is None:
        od = _attn_prompt(dq, kb, vt, lw["g_q"], lw["g_k"], lw["lam_p"], lw["g_sub"], lam_init,
                          bsz, t_len)
    else:
        od = _attn_sample(dq, dk_rows, dv_rows, _to_cache_rows(cache[0]), _to_cache_rows(cache[1]),
                          lw["lam_p"], lw["g_sub"], lam_init, bsz, t_len)
        od = od.reshape(-1, tm, SEG).transpose(0, 2, 1)
    x1e, cls = _merge(ogr, od, sgr, sgd, x2d, lw["w_ret_o"], lw["w_diff_o"], lw["w_out"],
                      lw["g_ffn"], lw["rw_hi"], lw["rw_lo"], lw["rb"])
    y = _hier_moe_residual(x1e, cls, lw["g_ffn"], lw["w_gu"], lw["w_dn"])
    return (y.reshape(bsz, t_len, d), _from_cache_rows(dk_rows, bsz, t_len),
            _from_cache_rows(dv_rows, bsz, t_len), s_fin)


def _to_cache_rows(c):
    b, p, h, w = c.shape
    halves = w // V7X_LANES
    return c.reshape(b * p, h, halves, V7X_LANES).transpose(0, 2, 1, 3).reshape(b * p * h * halves, V7X_LANES)


def _from_cache_rows(rows, bsz, t_len):
    halves = D_DV // V7X_LANES
    r = rows.reshape(bsz * t_len, halves, D_HEADS, V7X_LANES).transpose(0, 2, 1, 3)
    return r.reshape(bsz, t_len, D_HEADS, D_DV)


def _layer_weights(l, g_mix, w_in, g_q, g_k, lambda_q1, lambda_k1, lambda_q2, lambda_k2, g_ret,
                   w_ret_o, g_sub, w_diff_o, w_out, g_ffn, w_group, b_group, w_expert, b_expert,
                   w_gate, w_up, w_down):
    d = w_in.shape[1]
    n_r = N_GROUPS + N_EXPERTS
    rw = jnp.concatenate([w_group[l], w_expert[l]], axis=1).astype(F32).T
    rw = jnp.zeros((CLS_PAD, d), F32).at[:n_r].set(rw)
    rw_hi = rw.astype(BF16)
    rw_lo = (rw - rw_hi.astype(F32)).astype(BF16)
    rb = jnp.concatenate([b_group[l], b_expert[l]]).astype(F32)
    rb = jnp.zeros((CLS_PAD,), F32).at[:n_r].set(rb)
    rb = jnp.broadcast_to(rb[:, None], (CLS_PAD, V7X_LANES))
    return dict(
        g_mix=g_mix[l], w_in=w_in[l].astype(BF16), g_q=g_q[l], g_k=g_k[l],
        lam_p=jnp.stack([lambda_q1[l], lambda_k1[l], lambda_q2[l], lambda_k2[l]]).astype(F32),
        g_ret=g_ret[l], w_ret_o=w_ret_o[l].astype(BF16), g_sub=g_sub[l],
        w_diff_o=w_diff_o[l].astype(BF16), w_out=w_out[l].astype(BF16), g_ffn=g_ffn[l],
        rw_hi=rw_hi, rw_lo=rw_lo, rb=rb,
        w_gu=jnp.concatenate([w_gate[l], w_up[l]], axis=-1).astype(BF16),
        w_dn=w_down[l].astype(BF16))


def kernel(x_prompt, x_sample, cache_k, cache_v, state_ret, g_mix, w_in, g_q, g_k, lambda_q1, lambda_k1, lambda_q2, lambda_k2, g_ret, w_ret_o, g_sub, w_diff_o, w_out, g_ffn, w_group, b_group, w_expert, b_expert, w_gate, w_up, w_down):
    depth = w_in.shape[0]
    bp, tp, _ = x_prompt.shape
    bs, ts, _ = x_sample.shape
    past = cache_k.shape[2]
    log_gamma = jnp.log1p(-jnp.exp2(-5.0 - jnp.arange(R_HEADS, dtype=F32)))
    pos_p = jnp.arange(tp, dtype=jnp.int32)
    pos_s = past + jnp.arange(ts, dtype=jnp.int32)
    zero_state = jnp.zeros((bp, R_HEADS, R_DK, R_DV), F32)
    xp, xs = x_prompt, x_sample
    outs = [[] for _ in range(6)]
    for l in range(depth):
        lam_init = 0.8 - 0.6 * math.exp(-0.3 * l)
        lw = _layer_weights(l, g_mix, w_in, g_q, g_k, lambda_q1, lambda_k1, lambda_q2, lambda_k2,
                            g_ret, w_ret_o, g_sub, w_diff_o, w_out, g_ffn, w_group, b_group,
                            w_expert, b_expert, w_gate, w_up, w_down)
        xp, kp, vp, sp = _token_group(xp, pos_p, lw, lam_init, log_gamma, zero_state, None)
        cache = (cache_k[l], cache_v[l])
        xs, ks, vs, ss = _token_group(xs, pos_s, lw, lam_init, log_gamma,
                                      state_ret[l].astype(F32), cache)
        for lst, val in zip(outs, (kp, vp, sp, ks, vs, ss)):
            lst.append(val)
    return (xp, xs) + tuple(jnp.stack(o) for o in outs)
```

```python
import functools
import math

import jax
import jax.numpy as jnp
from jax import lax
from jax.experimental import pallas as pl
from jax.experimental.pallas import tpu as pltpu

F32 = jnp.float32
BF16 = jnp.bfloat16
I32 = jnp.int32

CHUNK = 64
EPS = 1e-6
R_HEADS = 4
R_DK = 128
R_DV = 256
ROPE_BASE = 10000.0
D_HEADS = 4
D_DH = 128
D_DV = 256
N_GROUPS = 4
EXP_PER_GROUP = 4
N_EXPERTS = N_GROUPS * EXP_PER_GROUP
N_PAIRS = EXP_PER_GROUP * (EXP_PER_GROUP - 1) // 2
N_CLASSES = N_GROUPS * N_PAIRS
D_FF = 512
SEG = 1024
N_SEG = 8

V7X_LANES = 128
V7X_SUBLANES = 8
V7X_VMEM_LIMIT_BYTES = 56 * 1024 * 1024

ROW_TILE = 512
RET_SUPER = 512
ATT_HEADS_PER_STEP = 2
MERGE_TILES_PER_STEP = 2
SORT_TILES_PER_STEP = 8
PERM_TILES_PER_STEP = 2
MOE_TILE_MIN = 32
SAMPLE_CHUNK = 1024
MOE_TILE = 512
N_DMA_PRIORITIES = 2
CLS_PAD = 32
NEG_BIG = -1e30
LOG2E = math.log2(math.e)
SCORE_BOUND_MARGIN = 1.01
MAX_SHIFT_LOG2 = 100.0
PAY = V7X_LANES


def _row_tile(n):
    return min(ROW_TILE, n)


def _cparams(sem, vmem_mb=None):
    kw = dict(dimension_semantics=sem)
    if vmem_mb is not None:
        kw["vmem_limit_bytes"] = min(vmem_mb * 1024 * 1024, V7X_VMEM_LIMIT_BYTES)
    return pltpu.CompilerParams(**kw)


def _resident(shape):
    nd = len(shape)
    return pl.BlockSpec(shape, lambda *_: (0,) * nd, pipeline_mode=pl.Buffered(1))


def _rms(x, eps=EPS):
    return x * lax.rsqrt(jnp.mean(x * x, axis=-1, keepdims=True) + eps)


def _sigmoid(x):
    return 1.0 / (1.0 + jnp.exp(-x))


def _dot(a, b):
    return jnp.dot(a, b, preferred_element_type=F32)


def _dot_nt(a, b):
    return lax.dot_general(a, b, (((1,), (1,)), ((), ())), preferred_element_type=F32)


def _dot_tn(a, b):
    return lax.dot_general(a, b, (((0,), (0,)), ((), ())), preferred_element_type=F32)


def _store_cache_rows(ref, z):
    rows = z.shape[0]
    for half in range(2):
        for h in range(D_HEADS):
            lo = h * D_DV + half * V7X_LANES
            ref[pl.ds(half * D_HEADS + h, rows, stride=2 * D_HEADS), :] = z[:, lo:lo + V7X_LANES]


def _inproj_kernel(x_ref, g_ref, w_ref, cos_ref, sin_ref, gq_ref, gk_ref,
                   rqk_ref, rv_ref, rgs_ref, dq_ref, kb_ref, vt_ref, dk_ref, dv_ref, sgr_ref, sgd_ref):
    x = x_ref[...]
    hb = (_rms(x) * g_ref[...]).astype(BF16)

    def seg(s):
        return _dot(hb, w_ref[:, s * SEG:(s + 1) * SEG])

    cos = cos_ref[...]
    sin = sin_ref[...]
    z = seg(0)
    for j in range(2 * R_HEADS):
        v = z[:, j * R_DK:(j + 1) * R_DK]
        r = v * cos + pltpu.roll(v, R_DK // 2, 1) * sin
        if j >= R_HEADS:
            r = r * (R_DK ** -0.5)
        rqk_ref[:, j * R_DK:(j + 1) * R_DK] = r.astype(BF16)

    z = seg(2)
    rgs_ref[...] = (z * _sigmoid(z)).astype(BF16)

    z = seg(3)
    gq = gq_ref[...] * (D_DH ** -0.5 * LOG2E)
    for j in range(2 * D_HEADS):
        v = z[:, j * D_DH:(j + 1) * D_DH]
        dq_ref[:, j * D_DH:(j + 1) * D_DH] = (_rms(v) * gq).astype(BF16)
    z = seg(4)
    gk = gk_ref[...]
    kn = jnp.concatenate([_rms(z[:, j * D_DH:(j + 1) * D_DH]) * gk for j in range(2 * D_HEADS)], axis=1)
    kb_ref[...] = kn.astype(BF16)
    _store_cache_rows(dk_ref, kn)
    z = seg(5)
    _store_cache_rows(dv_ref, z)
    vt_ref[0] = z.T.astype(BF16)
    sgr_ref[...] = _sigmoid(seg(6)).astype(BF16)
    sgd_ref[...] = _sigmoid(seg(7)).astype(BF16)
    rv_ref[...] = seg(1).astype(BF16)


def _in_proj(x2d, g_mix, w_in_bf, cos_tab, sin_tab, g_q, g_k):
    n, d = x2d.shape
    tm = _row_tile(n)
    nt = n // tm
    ntab = cos_tab.shape[0] // tm
    row = lambda w: pl.BlockSpec((tm, w), lambda i: (i, 0))
    tab = pl.BlockSpec((tm, R_DK), lambda i: (i % ntab, 0))
    flat = jax.ShapeDtypeStruct((n, SEG), BF16)
    cache_rows = jax.ShapeDtypeStruct((n * 2 * D_HEADS, V7X_LANES), F32)
    cache_spec = pl.BlockSpec((tm * 2 * D_HEADS, V7X_LANES), lambda i: (i, 0))
    outs = [flat, flat, flat, flat, flat, jax.ShapeDtypeStruct((nt, SEG, tm), BF16),
            cache_rows, cache_rows, flat, flat]
    return pl.pallas_call(
        _inproj_kernel,
        out_shape=outs,
        grid=(nt,),
        in_specs=[row(d), _resident((1, d)), _resident(w_in_bf.shape), tab, tab,
                  _resident((1, D_DH)), _resident((1, D_DH))],
        out_specs=[row(SEG)] * 5 + [pl.BlockSpec((1, SEG, tm), lambda i: (i, 0, 0)),
                                    cache_spec, cache_spec, row(SEG), row(SEG)],
        compiler_params=_cparams(("arbitrary",), 56),
        name="in_proj",
    )(x2d, g_mix.reshape(1, d), w_in_bf, cos_tab, sin_tab,
      g_q.reshape(1, D_DH), g_k.reshape(1, D_DH))


def _ret_kernel(lg_ref, qk_ref, v_ref, rgs_ref, s0_ref, g_ref,
                o_ref, sfin_ref, s_scr, d_scr, lam_scr, wk_scr, dec_scr, *, c_len):
    c = pl.program_id(1)

    @pl.when((pl.program_id(0) == 0) & (c == 0))
    def _():
        t = lax.broadcasted_iota(I32, (c_len, c_len), 0)
        s = lax.broadcasted_iota(I32, (c_len, c_len), 1)
        dist = jnp.abs(t - s).astype(F32)
        vis = (s // CHUNK) <= (t // CHUNK)
        pos_v = lax.broadcasted_iota(I32, (c_len, R_DV), 0).astype(F32)
        pos_k = lax.broadcasted_iota(I32, (c_len, R_DK), 0).astype(F32)
        for h in range(R_HEADS):
            lg = lg_ref[h]
            d_scr[h] = jnp.where(vis, jnp.exp(lg * dist), 0.0)
            lam_scr[h] = jnp.exp(lg * (pos_v + 1.0))
            wk_scr[h] = jnp.exp(lg * (c_len - 1.0 - pos_k))
            dec_scr[h] = jnp.exp(lg * jnp.full((V7X_SUBLANES, R_DV), float(c_len), F32))

    @pl.when(c == 0)
    def _():
        s_scr[...] = s0_ref[0]

    for h in range(R_HEADS):
        q = qk_ref[0, :, h * R_DK:(h + 1) * R_DK]
        k = qk_ref[0, :, (R_HEADS + h) * R_DK:(R_HEADS + h + 1) * R_DK]
        v = v_ref[0, :, h * R_DV:(h + 1) * R_DV]
        state = s_scr[h]
        s = _dot_nt(q, k) * d_scr[h]
        o = _dot(s.astype(BF16), v) + lam_scr[h] * _dot(q, state.astype(BF16))
        kw = (k.astype(F32) * wk_scr[h]).astype(BF16)
        s_scr[h] = state * dec_scr[h][0:1, :] + _dot_tn(kw, v)
        gate = rgs_ref[0, :, h * R_DV:(h + 1) * R_DV].astype(F32)
        o_ref[0, :, h * R_DV:(h + 1) * R_DV] = (_rms(o) * g_ref[...] * gate).astype(BF16)

    @pl.when(c == pl.num_programs(1) - 1)
    def _():
        sfin_ref[0] = s_scr[...]


def _retention(rqk, rv, rgs, state0, g_ret, log_gamma, bsz, t_len):
    c_len = min(RET_SUPER, t_len)
    nc = t_len // c_len
    rqk3 = rqk.reshape(bsz, t_len, SEG)
    rv3 = rv.reshape(bsz, t_len, SEG)
    rgs3 = rgs.reshape(bsz, t_len, SEG)
    rows = pl.BlockSpec((1, c_len, SEG), lambda b, c, lg: (b, c, 0))
    sspec = pl.BlockSpec((1, R_HEADS, R_DK, R_DV), lambda b, c, lg: (b, 0, 0, 0))
    gspec = pl.BlockSpec((1, R_DV), lambda b, c, lg: (0, 0))
    o, sfin = pl.pallas_call(
        functools.partial(_ret_kernel, c_len=c_len),
        out_shape=[jax.ShapeDtypeStruct((bsz, t_len, SEG), BF16),
                   jax.ShapeDtypeStruct((bsz, R_HEADS, R_DK, R_DV), F32)],
        grid_spec=pltpu.PrefetchScalarGridSpec(
            num_scalar_prefetch=1,
            grid=(bsz, nc),
            in_specs=[rows, rows, rows, sspec, gspec],
            out_specs=[rows, sspec],
            scratch_shapes=[pltpu.VMEM((R_HEADS, R_DK, R_DV), F32),
                            pltpu.VMEM((R_HEADS, c_len, c_len), F32),
                            pltpu.VMEM((R_HEADS, c_len, R_DV), F32),
                            pltpu.VMEM((R_HEADS, c_len, R_DK), F32),
                            pltpu.VMEM((R_HEADS, V7X_SUBLANES, R_DV), F32)],
        ),
        compiler_params=_cparams(("arbitrary",) * 2, 32),
        name="retention",
    )(log_gamma, rqk3, rv3, rgs3, state0, g_ret.reshape(1, R_DV))
    return o.reshape(bsz * t_len, SEG), sfin


def _lambda_full(lp_ref, lam_init):
    lp = lp_ref[...]
    a = jnp.sum(lp[0:1] * lp[1:2], axis=-1, keepdims=True)
    b = jnp.sum(lp[2:3] * lp[3:4], axis=-1, keepdims=True)
    return jnp.exp(a) - jnp.exp(b) + lam_init


def _attn_finish(o1, l1, o2, l2, lam, g, lam_init):
    o = o1 / l1 - lam * (o2 / l2)
    return (_rms(o) * g * (1.0 - lam_init)).astype(BF16)


def _attn_prompt_kernel(bound_ref, q_ref, k_ref, vt_ref, lp_ref, g_ref, o_ref,
                        m_scr, l_scr, a_scr, bias, *, tile, heads, lam_init, fixed_shift):
    i = pl.program_id(2)
    w = 2 * D_DH

    @pl.when((pl.program_id(0) == 0) & (pl.program_id(1) == 0) & (i == 0))
    def _():
        kk = lax.broadcasted_iota(I32, (tile, tile), 0)
        qq = lax.broadcasted_iota(I32, (tile, tile), 1)
        bias[...] = jnp.where((kk // CHUNK) <= (qq // CHUNK), 0.0, NEG_BIG)

    def update(s, vts, idx, first):
        if fixed_shift:
            p = jnp.exp2(s - bound_ref[0])
            psum = jnp.sum(p, axis=0, keepdims=True)
            pv = _dot(vts, p.astype(BF16))
            l_scr[idx] = psum if first else l_scr[idx] + psum
            a_scr[idx] = pv if first else a_scr[idx] + pv
            return
        smax = jnp.max(s, axis=0, keepdims=True)
        m_new = smax if first else jnp.maximum(m_scr[idx], smax)
        p = jnp.exp2(s - m_new)
        psum = jnp.sum(p, axis=0, keepdims=True)
        pv = _dot(vts, p.astype(BF16))
        if first:
            l_scr[idx] = psum
            a_scr[idx] = pv
        else:
            alpha = jnp.exp2(m_scr[idx] - m_new)
            l_scr[idx] = alpha * l_scr[idx] + psum
            a_scr[idx] = alpha * a_scr[idx] + pv
        m_scr[idx] = m_new

    def block(hh, j, mask=None, first=False):
        ks = k_ref[0, pl.ds(pl.multiple_of(j * tile, tile), tile), hh * w:(hh + 1) * w]
        vts = vt_ref[0, j, hh * D_DV:(hh + 1) * D_DV, :]
        q = q_ref[0, :, hh * w:(hh + 1) * w]
        s1 = _dot_nt(ks[:, :D_DH], q[:, :D_DH])
        s2 = _dot_nt(ks[:, D_DH:], q[:, D_DH:])
        if mask is not None:
            s1 = s1 + mask
            s2 = s2 + mask
        update(s1, vts, 2 * hh, first)
        update(s2, vts, 2 * hh + 1, first)

    for hh in range(heads):
        block(hh, i, bias[...], first=True)

    def body(jj, carry):
        for hh in range(heads):
            block(hh, 2 * jj)
            block(hh, 2 * jj + 1)
        return carry

    lax.fori_loop(0, i // 2, body, 0)

    @pl.when(i % 2 == 1)
    def _():
        for hh in range(heads):
            block(hh, i - 1)

    lam = _lambda_full(lp_ref, lam_init)
    for hh in range(heads):
        a1, l1, a2, l2 = a_scr[2 * hh], l_scr[2 * hh], a_scr[2 * hh + 1], l_scr[2 * hh + 1]
        ot = a1 * (1.0 / l1) - lam * (a2 * (1.0 / l2))
        scale = lax.rsqrt(jnp.mean(ot * ot, axis=0, keepdims=True) + EPS) * (1.0 - lam_init)
        o_ref[0, 0, hh * D_DV:(hh + 1) * D_DV, :] = (ot * scale * g_ref[...]).astype(BF16)


def _attn_prompt(dq, kb, vt, g_q, g_k, lam_p, g_sub, lam_init, bsz, t_len):
    tile = vt.shape[2]
    assert t_len % tile == 0 and tile % CHUNK == 0
    nq = t_len // tile
    w = 2 * D_DH
    q3 = dq.reshape(bsz, t_len, SEG)
    k3 = kb.reshape(bsz, t_len, SEG)
    vt4 = vt.reshape(bsz, nq, SEG, tile)
    hps = ATT_HEADS_PER_STEP
    qspec = pl.BlockSpec((1, tile, hps * w), lambda b, h, i: (b, i, h))
    kspec = pl.BlockSpec((1, t_len, hps * w), lambda b, h, i: (b, 0, h))
    vspec = pl.BlockSpec((1, nq, hps * D_DV, tile), lambda b, h, i: (b, 0, h, 0))
    n_stat = 2 * hps
    col = lambda: pltpu.VMEM((n_stat, 1, tile), F32)

    def call(fixed_shift):
        return pl.pallas_call(
            functools.partial(_attn_prompt_kernel, tile=tile, heads=hps, lam_init=lam_init,
                              fixed_shift=fixed_shift),
            out_shape=jax.ShapeDtypeStruct((bsz, nq, SEG, tile), BF16),
            grid=(bsz, D_HEADS // hps, nq),
            in_specs=[pl.BlockSpec(memory_space=pltpu.SMEM), qspec, kspec, vspec,
                      pl.BlockSpec((4, D_DH), lambda b, h, i: (0, 0)),
                      pl.BlockSpec((D_DV, tile), lambda b, h, i: (0, 0))],
            out_specs=pl.BlockSpec((1, 1, hps * D_DV, tile), lambda b, h, i: (b, i, h, 0)),
            scratch_shapes=[col(), col(), pltpu.VMEM((n_stat, D_DV, tile), F32),
                            pltpu.VMEM((tile, tile), F32)],
            compiler_params=_cparams(("arbitrary",) * 3, 40),
            name="attn_prompt_fixed" if fixed_shift else "attn_prompt_online",
        )

    bound = SCORE_BOUND_MARGIN * D_DH * (D_DH ** -0.5 * LOG2E) * jnp.max(jnp.abs(g_q * g_k))
    g_col = jnp.broadcast_to(g_sub.astype(F32)[:, None], (D_DV, tile))
    args = (bound.reshape(1).astype(F32), q3, k3, vt4, lam_p, g_col)
    out = lax.cond(2.0 * bound < MAX_SHIFT_LOG2,
                   lambda *a: call(True)(*a), lambda *a: call(False)(*a), *args)
    return out.reshape(bsz * nq, SEG, tile)


def _attn_sample_kernel(q_ref, kc_ref, vc_ref, kn_ref, vn_ref, lp_ref, g_ref, o_ref,
                        m_scr, l_scr, a_scr, *, lam_init, chunk, t_new):
    c = pl.program_id(1)

    @pl.when(c == 0)
    def _():
        m_scr[...] = jnp.full(m_scr.shape, NEG_BIG, F32)
        l_scr[...] = jnp.zeros(l_scr.shape, F32)
        a_scr[...] = jnp.zeros(a_scr.shape, F32)

    q = q_ref[0]

    def head_rows(ref, h, half, n):
        return ref[pl.ds(half * D_HEADS + h, n, stride=2 * D_HEADS), :].astype(BF16)

    def absorb(k_src, v_src, n):
        for h in range(D_HEADS):
            v = jnp.concatenate([head_rows(v_src, h, 0, n), head_rows(v_src, h, 1, n)], axis=1)
            for mp in range(2):
                idx = 2 * h + mp
                lo = h * 2 * D_DH + mp * D_DH
                s = _dot_nt(q[:, lo:lo + D_DH], head_rows(k_src, h, mp, n))
                m_old = m_scr[idx]
                m_new = jnp.maximum(m_old, jnp.max(s, axis=-1, keepdims=True))
                alpha = jnp.exp2(m_old - m_new)
                p = jnp.exp2(s - m_new)
                l_scr[idx] = alpha * l_scr[idx] + jnp.sum(p, axis=-1, keepdims=True)
                a_scr[idx] = alpha * a_scr[idx] + _dot(p.astype(BF16), v)
                m_scr[idx] = m_new

    absorb(kc_ref, vc_ref, chunk)

    @pl.when(c == pl.num_programs(1) - 1)
    def _():
        absorb(kn_ref, vn_ref, t_new)
        lam = _lambda_full(lp_ref, lam_init)
        for h in range(D_HEADS):
            o_ref[0, :, h * D_DV:(h + 1) * D_DV] = _attn_finish(
                a_scr[2 * h], l_scr[2 * h], a_scr[2 * h + 1], l_scr[2 * h + 1], lam, g_ref[...], lam_init)


def _attn_sample(dq, dk_rows, dv_rows, cache_k_rows, cache_v_rows, lam_p, g_sub, lam_init, bsz, t_len):
    rpf = 2 * D_HEADS
    past = cache_k_rows.shape[0] // (bsz * rpf)
    chunk = min(SAMPLE_CHUNK, past)
    nc = past // chunk
    q3 = dq.reshape(bsz, t_len, SEG)
    qspec = pl.BlockSpec((1, t_len, SEG), lambda b, c: (b, 0, 0))
    cspec = pl.BlockSpec((chunk * rpf, V7X_LANES), lambda b, c: (b * nc + c, 0))
    nspec = pl.BlockSpec((t_len * rpf, V7X_LANES), lambda b, c: (b, 0))
    n_stat = 2 * D_HEADS
    out = pl.pallas_call(
        functools.partial(_attn_sample_kernel, lam_init=lam_init, chunk=chunk, t_new=t_len),
        out_shape=jax.ShapeDtypeStruct((bsz, t_len, SEG), BF16),
        grid=(bsz, nc),
        in_specs=[qspec, cspec, cspec, nspec, nspec,
                  pl.BlockSpec((4, D_DH), lambda b, c: (0, 0)),
                  pl.BlockSpec((1, D_DV), lambda b, c: (0, 0))],
        out_specs=qspec,
        scratch_shapes=[pltpu.VMEM((n_stat, t_len, 1), F32), pltpu.VMEM((n_stat, t_len, 1), F32),
                        pltpu.VMEM((n_stat, t_len, D_DV), F32)],
        compiler_params=_cparams(("arbitrary",) * 2, 32),
        name="attn_sample",
    )(q3, cache_k_rows, cache_v_rows, dk_rows, dv_rows, lam_p, g_sub.reshape(1, D_DV))
    return out.reshape(bsz * t_len, SEG)


def _split_hi_lo(x):
    hi = x.astype(BF16)
    lo = (x - hi.astype(F32)).astype(BF16)
    return hi, lo


def _merge_kernel(*refs):
    cls_ref = refs[-1]
    for sub in range(cls_ref.shape[0]):
        _merge_tile(sub, cls_ref.shape[2], *refs)


def _merge_tile(sub, tm, ogr_ref, od_ref, sgr_ref, sgd_ref, x_ref, wr_ref, wd_ref, wo_ref,
                gf_ref, rw_hi_ref, rw_lo_ref, rb_ref, x1e_ref, cls_ref):
    rs = slice(sub * tm, (sub + 1) * tm)
    y_r = _dot(ogr_ref[rs, :], wr_ref[...])
    y_d = _dot_tn(od_ref[sub], wd_ref[...])
    m = sgr_ref[rs, :].astype(F32) * y_r + sgd_ref[rs, :].astype(F32) * y_d
    x1 = x_ref[rs, :] + _dot(m.astype(BF16), wo_ref[...])

    hn = _rms(x1) * gf_ref[...]
    h_hi, h_lo = _split_hi_lo(hn)
    w_hi = rw_hi_ref[...]
    both = _dot_nt(jnp.concatenate([w_hi, rw_lo_ref[...]], axis=0), h_hi)
    lt = both[:CLS_PAD] + both[CLS_PAD:] + _dot_nt(w_hi, h_lo)
    lt = lt + rb_ref[...][:, 0:1]
    g = [lt[i:i + 1, :] for i in range(N_GROUPS)]
    e = [lt[N_GROUPS + i:N_GROUPS + i + 1, :] for i in range(N_EXPERTS)]

    gmax = functools.reduce(jnp.maximum, g)
    gid = jnp.full(g[0].shape, N_GROUPS - 1, I32)
    for i in range(N_GROUPS - 2, -1, -1):
        gid = jnp.where(g[i] == gmax, i, gid)
    g_w = 1.0 / functools.reduce(lambda a, b: a + b, [jnp.exp(v - gmax) for v in g])

    es = []
    for j in range(EXP_PER_GROUP):
        v = e[(N_GROUPS - 1) * EXP_PER_GROUP + j]
        for i in range(N_GROUPS - 2, -1, -1):
            v = jnp.where(gid == i, e[i * EXP_PER_GROUP + j], v)
        es.append(v)

    def first_argmax(vals):
        mx = functools.reduce(jnp.maximum, vals)
        idx = jnp.full(mx.shape, len(vals) - 1, I32)
        for i in range(len(vals) - 2, -1, -1):
            idx = jnp.where(vals[i] == mx, i, idx)
        return mx, idx

    l1, i1 = first_argmax(es)
    rest = [jnp.where(i1 == j, -jnp.inf, es[j]) for j in range(EXP_PER_GROUP)]
    l2, i2 = first_argmax(rest)
    t = jnp.exp(l2 - l1)
    c1 = g_w / (1.0 + t)
    c2 = g_w * t / (1.0 + t)
    lo = jnp.minimum(i1, i2)
    hi = jnp.maximum(i1, i2)
    base = jnp.where(lo == 0, 0, jnp.where(lo == 1, EXP_PER_GROUP - 1, 2 * EXP_PER_GROUP - 3))
    cls = gid * N_PAIRS + base + hi - lo - 1
    wa = jnp.where(i1 < i2, c1, c2)
    wb = jnp.where(i1 < i2, c2, c1)

    cls_ref[sub] = cls
    rid = lax.broadcasted_iota(I32, (PAY, tm), 0)
    pay = jnp.where(rid == 0, wa, jnp.where(rid == 1, wb, 0.0))
    d = x_ref.shape[1]
    x1e_ref[rs, :d] = x1
    x1e_ref[rs, d:] = pay.T


def _merge(ogr, od, sgr, sgd, x2d, w_ret_o, w_diff_o, w_out, g_ffn, rw_hi, rw_lo, rb):
    n, d = x2d.shape
    tm = _row_tile(n)
    nt = n // tm
    n_sub = MERGE_TILES_PER_STEP if nt % MERGE_TILES_PER_STEP == 0 else 1
    row = lambda w: pl.BlockSpec((n_sub * tm, w), lambda i: (i, 0))
    return pl.pallas_call(
        _merge_kernel,
        out_shape=[jax.ShapeDtypeStruct((n, d + PAY), F32),
                   jax.ShapeDtypeStruct((nt, 1, tm), I32)],
        grid=(nt // n_sub,),
        in_specs=[row(SEG), pl.BlockSpec((n_sub, SEG, tm), lambda i: (i, 0, 0)), row(SEG), row(SEG), row(d),
                  _resident(w_ret_o.shape), _resident(w_diff_o.shape), _resident(w_out.shape),
                  _resident((1, d)), _resident(rw_hi.shape), _resident(rw_lo.shape),
                  _resident(rb.shape)],
        out_specs=[row(d + PAY), pl.BlockSpec((n_sub, 1, tm), lambda i: (i, 0, 0))],
        compiler_params=_cparams(("arbitrary",), 48),
        name="merge_route",
    )(ogr, od, sgr, sgd, x2d, w_ret_o, w_diff_o, w_out, g_ffn.reshape(1, d), rw_hi, rw_lo, rb)


def _prefix_excl(x):
    rid = lax.broadcasted_iota(I32, x.shape, 0)
    inc = x
    s = 1
    while s < x.shape[0]:
        inc = inc + jnp.where(rid >= s, pltpu.roll(inc, s, 0), 0.0)
        s *= 2
    return inc - x


def _sort_kernel(cls_ref, pos_ref, cnt_ref, off_ref, tiles_ref, cnt_scr, run_scr, off_scr,
                 *, moe_tile, n_tiles_pad):
    ph = pl.program_id(0)
    t = pl.program_id(1)
    k_tiles, _, tm = cls_ref.shape
    cid = lax.broadcasted_iota(I32, (CLS_PAD, tm), 0)

    def onehot(j):
        return jnp.where(cid == cls_ref[j], 1.0, 0.0)

    @pl.when(jnp.logical_and(ph == 0, t == 0))
    def _():
        cnt_scr[...] = jnp.zeros(cnt_scr.shape, F32)

    @pl.when(ph == 0)
    def _():
        for j in range(k_tiles):
            cnt_scr[...] += jnp.sum(onehot(j), axis=-1, keepdims=True)

    @pl.when(jnp.logical_and(ph == 1, t == 0))
    def _():
        cnt = cnt_scr[...]
        padded = jnp.ceil(cnt / moe_tile) * moe_tile
        off = _prefix_excl(padded)
        off_scr[...] = off
        run_scr[...] = jnp.zeros(run_scr.shape, F32)
        cnt_ref[...] = cnt.astype(I32)
        off_ref[...] = off.astype(I32)
        end = (off + padded)[:, 0:1]
        total = jnp.max(end, axis=0, keepdims=True)
        n_used = total / moe_tile
        p = lax.broadcasted_iota(I32, (1, n_tiles_pad), 1).astype(F32)
        start = jnp.minimum(p, n_used - 1.0) * moe_tile
        cid2 = lax.broadcasted_iota(I32, (CLS_PAD, n_tiles_pad), 0)
        before = jnp.logical_and(end <= start, cid2 < N_CLASSES)
        tcls = jnp.sum(before.astype(F32), axis=0, keepdims=True).astype(I32)
        grp = tcls // N_PAIRS
        pr = tcls - grp * N_PAIRS
        lo = (pr >= EXP_PER_GROUP - 1).astype(I32) + (pr >= 2 * EXP_PER_GROUP - 3).astype(I32)
        base = jnp.where(lo == 0, 0, jnp.where(lo == 1, EXP_PER_GROUP - 1, 2 * EXP_PER_GROUP - 3))
        hi = pr - base + lo + 1
        rid = lax.broadcasted_iota(I32, (V7X_SUBLANES, n_tiles_pad), 0)
        ea = grp * EXP_PER_GROUP + lo
        eb = grp * EXP_PER_GROUP + hi
        nu = jnp.broadcast_to(n_used.astype(I32), (1, n_tiles_pad))
        tiles_ref[...] = jnp.where(rid == 0, ea, jnp.where(rid == 1, eb, jnp.where(rid == 2, nu, 0)))

    @pl.when(ph == 1)
    def _():
        r = lax.broadcasted_iota(I32, (tm, tm), 0)
        c = lax.broadcasted_iota(I32, (tm, tm), 1)
        upper = jnp.where(r <= c, 1.0, 0.0).astype(BF16)
        for j in range(k_tiles):
            oh = onehot(j)
            incl = _dot(oh.astype(BF16), upper)
            slot = off_scr[...][:, 0:1] + run_scr[...][:, 0:1] + incl - 1.0
            pos_ref[j] = jnp.sum(oh * slot, axis=0, keepdims=True).astype(I32)
            run_scr[...] += jnp.sum(oh, axis=-1, keepdims=True)


def _sort(cls, moe_tile, n_tiles_pad):
    nt, _, tm = cls.shape
    k_tiles = SORT_TILES_PER_STEP if nt % SORT_TILES_PER_STEP == 0 else 1
    blk = pl.BlockSpec((k_tiles, 1, tm), lambda ph, t: (t, 0, 0))
    oblk = pl.BlockSpec((k_tiles, 1, tm), lambda ph, t: (t * ph, 0, 0))
    whole = lambda shape: pl.BlockSpec(shape, lambda ph, t: (0, 0))
    return pl.pallas_call(
        functools.partial(_sort_kernel, moe_tile=moe_tile, n_tiles_pad=n_tiles_pad),
        out_shape=[jax.ShapeDtypeStruct((nt, 1, tm), I32),
                   jax.ShapeDtypeStruct((CLS_PAD, V7X_LANES), I32),
                   jax.ShapeDtypeStruct((CLS_PAD, V7X_LANES), I32),
                   jax.ShapeDtypeStruct((V7X_SUBLANES, n_tiles_pad), I32)],
        grid=(2, nt // k_tiles),
        in_specs=[blk],
        out_specs=[oblk, whole((CLS_PAD, V7X_LANES)), whole((CLS_PAD, V7X_LANES)),
                   whole((V7X_SUBLANES, n_tiles_pad))],
        scratch_shapes=[pltpu.VMEM((CLS_PAD, V7X_LANES), F32)] * 3,
        compiler_params=_cparams(("arbitrary",) * 2),
        name="class_sort",
    )(cls)


def _permute_kernel(pos_ref, cnt_ref, off_ref, src_ref, dst_ref, zblk, sem, zsem, *, moe_tile):
    t = pl.program_id(0)
    k_tiles, _, tm = pos_ref.shape
    rows = k_tiles * tm

    def row_copy(r):
        return pltpu.make_async_copy(src_ref.at[pl.ds(r, 1)],
                                     dst_ref.at[pl.ds(pos_ref[r // tm, 0, r % tm], 1)], sem)

    for r in range(rows):
        row_copy(r).start(priority=r % N_DMA_PRIORITIES)

    @pl.when(t == 0)
    def _():
        zblk[...] = jnp.zeros(zblk.shape, zblk.dtype)
        used = 0
        for c in range(N_CLASSES):
            cnt = cnt_ref[c, 0]
            off = off_ref[c, 0]
            padded = ((cnt + moe_tile - 1) // moe_tile) * moe_tile
            used = off + padded

            def pad_copy(r, off=off):
                return pltpu.make_async_copy(zblk.at[pl.ds(0, 1)], dst_ref.at[pl.ds(off + r, 1)], zsem)

            def pad_start(r, carry, copy=pad_copy):
                copy(r).start()
                return carry

            def pad_wait(r, carry, copy=pad_copy):
                copy(r).wait()
                return carry

            lax.fori_loop(cnt, padded, pad_start, 0)
            lax.fori_loop(cnt, padded, pad_wait, 0)

        def tile_copy(p):
            return pltpu.make_async_copy(zblk, dst_ref.at[pl.ds(p * moe_tile, moe_tile)], zsem)

        def tile_start(p, carry):
            tile_copy(p).start()
            return carry

        def tile_wait(p, carry):
            tile_copy(p).wait()
            return carry

        first, last = used // moe_tile, dst_ref.shape[0] // moe_tile
        lax.fori_loop(first, last, tile_start, 0)
        lax.fori_loop(first, last, tile_wait, 0)

    for r in range(rows):
        row_copy(r).wait()


def _perm_tiles_per_step(nt):
    return PERM_TILES_PER_STEP if nt % PERM_TILES_PER_STEP == 0 else 1


def _permute(pos, cnt, off, src, n_rows_out, moe_tile):
    nt, _, tm = pos.shape
    width = src.shape[1]
    k = _perm_tiles_per_step(nt)
    smem = lambda shape, imap: pl.BlockSpec(shape, imap, memory_space=pltpu.SMEM)
    return pl.pallas_call(
        functools.partial(_permute_kernel, moe_tile=moe_tile),
        out_shape=jax.ShapeDtypeStruct((n_rows_out, width), src.dtype),
        grid=(nt // k,),
        in_specs=[smem((k, 1, tm), lambda t: (t, 0, 0)),
                  smem(cnt.shape, lambda t: (0, 0)),
                  smem(off.shape, lambda t: (0, 0)),
                  pl.BlockSpec((k * tm, width), lambda t: (t, 0))],
        out_specs=pl.BlockSpec(memory_space=pl.ANY),
        scratch_shapes=[pltpu.VMEM((moe_tile, width), src.dtype),
                        pltpu.SemaphoreType.DMA, pltpu.SemaphoreType.DMA],
        compiler_params=_cparams(("arbitrary",), 32),
        name="permute_rows",
    )(pos, cnt, off, src)


def _unpermute_kernel(pos_ref, src_ref, dst_ref, sem):
    k_tiles, _, tm = pos_ref.shape
    rows = k_tiles * tm

    def row_copy(r):
        return pltpu.make_async_copy(src_ref.at[pl.ds(pos_ref[r // tm, 0, r % tm], 1)],
                                     dst_ref.at[pl.ds(r, 1)], sem)

    for r in range(rows):
        row_copy(r).start(priority=r % N_DMA_PRIORITIES)
    for r in range(rows):
        row_copy(r).wait()


def _unpermute(pos, src, n_rows_out):
    nt, _, tm = pos.shape
    width = src.shape[1]
    k = _perm_tiles_per_step(nt)
    return pl.pallas_call(
        _unpermute_kernel,
        out_shape=jax.ShapeDtypeStruct((n_rows_out, width), src.dtype),
        grid=(nt // k,),
        in_specs=[pl.BlockSpec((k, 1, tm), lambda t: (t, 0, 0), memory_space=pltpu.SMEM),
                  pl.BlockSpec(memory_space=pl.ANY)],
        out_specs=pl.BlockSpec((k * tm, width), lambda t: (t, 0)),
        scratch_shapes=[pltpu.SemaphoreType.DMA],
        compiler_params=_cparams(("arbitrary",), 32),
        name="unpermute_rows",
    )(pos, src)


def _moe_kernel(ea_ref, eb_ref, nu_ref, xs_ref, gf_ref, wgu_a, wdn_a, wgu_b, wdn_b, y_ref):
    p = pl.program_id(0)
    d = y_ref.shape[1]

    @pl.when(p < nu_ref[0])
    def _():
        x = xs_ref[:, :d]
        wa = xs_ref[:, d:d + 1]
        wb = xs_ref[:, d + 1:d + 2]
        hn = (_rms(x) * gf_ref[...]).astype(BF16)

        def expert(wgu, wdn):
            gu = _dot(hn, wgu[0])
            gate = gu[:, :D_FF]
            he = gate * _sigmoid(gate) * gu[:, D_FF:]
            return _dot(he.astype(BF16), wdn[0])

        y_ref[...] = x + wa * expert(wgu_a, wdn_a) + wb * expert(wgu_b, wdn_b)

    @pl.when(p >= nu_ref[0])
    def _():
        y_ref[...] = jnp.zeros(y_ref.shape, y_ref.dtype)


def _moe(ea, eb, nu, xs, g_ffn, w_gu, w_dn, moe_tile):
    n_rows, width = xs.shape
    d = width - PAY
    n_tiles = n_rows // moe_tile
    used = lambda p, ea, eb, nu: (jnp.minimum(p, nu[0] - 1), 0)
    return pl.pallas_call(
        _moe_kernel,
        out_shape=jax.ShapeDtypeStruct((n_rows, d), F32),
        grid_spec=pltpu.PrefetchScalarGridSpec(
            num_scalar_prefetch=3,
            grid=(n_tiles,),
            in_specs=[pl.BlockSpec((moe_tile, width), used),
                      pl.BlockSpec((1, d), lambda p, ea, eb, nu: (0, 0)),
                      pl.BlockSpec((1, d, 2 * D_FF), lambda p, ea, eb, nu: (ea[p], 0, 0)),
                      pl.BlockSpec((1, D_FF, d), lambda p, ea, eb, nu: (ea[p], 0, 0)),
                      pl.BlockSpec((1, d, 2 * D_FF), lambda p, ea, eb, nu: (eb[p], 0, 0)),
                      pl.BlockSpec((1, D_FF, d), lambda p, ea, eb, nu: (eb[p], 0, 0))],
            out_specs=pl.BlockSpec((moe_tile, d), lambda p, ea, eb, nu: (p, 0)),
        ),
        compiler_params=_cparams(("arbitrary",), 48),
        name="moe_sorted",
    )(ea, eb, nu, xs, g_ffn.reshape(1, d), w_gu, w_dn, w_gu, w_dn)


def _hier_moe_residual(x1e, cls, g_ffn, w_gu, w_dn):
    n = x1e.shape[0]
    moe_tile = min(MOE_TILE, max(MOE_TILE_MIN, pl.next_power_of_2(n // N_CLASSES) // 2))
    n_tiles = n // moe_tile + N_CLASSES
    n_tiles_pad = -(-n_tiles // V7X_LANES) * V7X_LANES
    pos, cnt, off, tiles = _sort(cls, moe_tile, n_tiles_pad)
    xs = _permute(pos, cnt, off, x1e, n_tiles * moe_tile, moe_tile)
    ys = _moe(tiles[0, :n_tiles], tiles[1, :n_tiles], tiles[2, :1], xs, g_ffn, w_gu, w_dn, moe_tile)
    return _unpermute(pos, ys, n)


def _rotary_tables(pos):
    half = R_DK // 2
    inv = 1.0 / (ROPE_BASE ** jnp.linspace(0.0, 1.0, half, dtype=F32))
    ang = pos.astype(F32)[:, None] * inv[None, :]
    cos = jnp.cos(ang)
    sin = jnp.sin(ang)
    return jnp.concatenate([cos, cos], axis=-1), jnp.concatenate([-sin, sin], axis=-1)


def _token_group(x, pos, lw, lam_init, log_gamma, state0, cache):
    bsz, t_len, d = x.shape
    x2d = x.reshape(bsz * t_len, d)
    cos_tab, sin_tab = _rotary_tables(pos)
    tm = _row_tile(bsz * t_len)
    if t_len < tm:
        reps = tm // t_len
        cos_tab = jnp.tile(cos_tab, (reps, 1))
        sin_tab = jnp.tile(sin_tab, (reps, 1))
    rqk, rv, rgs, dq, kb, vt, dk_rows, dv_rows, sgr, sgd = _in_proj(
        x2d, lw["g_mix"], lw["w_in"], cos_tab, sin_tab, lw["g_q"], lw["g_k"])
    ogr, s_fin = _retention(rqk, rv, rgs, state0, lw["g_ret"], log_gamma, bsz, t_len)
    if cache is None:
        od = _attn_prompt(dq, kb, vt, lw["g_q"], lw["g_k"], lw["lam_p"], lw["g_sub"], lam_init,
                          bsz, t_len)
    else:
        od = _attn_sample(dq, dk_rows, dv_rows, _to_cache_rows(cache[0]), _to_cache_rows(cache[1]),
                          lw["lam_p"], lw["g_sub"], lam_init, bsz, t_len)
        od = od.reshape(-1, tm, SEG).transpose(0, 2, 1)
    x1e, cls = _merge(ogr, od, sgr, sgd, x2d, lw["w_ret_o"], lw["w_diff_o"], lw["w_out"],
                      lw["g_ffn"], lw["rw_hi"], lw["rw_lo"], lw["rb"])
    y = _hier_moe_residual(x1e, cls, lw["g_ffn"], lw["w_gu"], lw["w_dn"])
    return (y.reshape(bsz, t_len, d), _from_cache_rows(dk_rows, bsz, t_len),
            _from_cache_rows(dv_rows, bsz, t_len), s_fin)


def _to_cache_rows(c):
    b, p, h, w = c.shape
    halves = w // V7X_LANES
    return c.reshape(b * p, h, halves, V7X_LANES).transpose(0, 2, 1, 3).reshape(b * p * h * halves, V7X_LANES)


def _from_cache_rows(rows, bsz, t_len):
    halves = D_DV // V7X_LANES
    r = rows.reshape(bsz * t_len, halves, D_HEADS, V7X_LANES).transpose(0, 2, 1, 3)
    return r.reshape(bsz, t_len, D_HEADS, D_DV)


def _layer_weights(l, g_mix, w_in, g_q, g_k, lambda_q1, lambda_k1, lambda_q2, lambda_k2, g_ret,
                   w_ret_o, g_sub, w_diff_o, w_out, g_ffn, w_group, b_group, w_expert, b_expert,
                   w_gate, w_up, w_down):
    d = w_in.shape[1]
    n_r = N_GROUPS + N_EXPERTS
    rw = jnp.concatenate([w_group[l], w_expert[l]], axis=1).astype(F32).T
    rw = jnp.zeros((CLS_PAD, d), F32).at[:n_r].set(rw)
    rw_hi = rw.astype(BF16)
    rw_lo = (rw - rw_hi.astype(F32)).astype(BF16)
    rb = jnp.concatenate([b_group[l], b_expert[l]]).astype(F32)
    rb = jnp.zeros((CLS_PAD,), F32).at[:n_r].set(rb)
    rb = jnp.broadcast_to(rb[:, None], (CLS_PAD, V7X_LANES))
    return dict(
        g_mix=g_mix[l], w_in=w_in[l].astype(BF16), g_q=g_q[l], g_k=g_k[l],
        lam_p=jnp.stack([lambda_q1[l], lambda_k1[l], lambda_q2[l], lambda_k2[l]]).astype(F32),
        g_ret=g_ret[l], w_ret_o=w_ret_o[l].astype(BF16), g_sub=g_sub[l],
        w_diff_o=w_diff_o[l].astype(BF16), w_out=w_out[l].astype(BF16), g_ffn=g_ffn[l],
        rw_hi=rw_hi, rw_lo=rw_lo, rb=rb,
        w_gu=jnp.concatenate([w_gate[l], w_up[l]], axis=-1).astype(BF16),
        w_dn=w_down[l].astype(BF16))


def kernel(x_prompt, x_sample, cache_k, cache_v, state_ret, g_mix, w_in, g_q, g_k, lambda_q1, lambda_k1, lambda_q2, lambda_k2, g_ret, w_ret_o, g_sub, w_diff_o, w_out, g_ffn, w_group, b_group, w_expert, b_expert, w_gate, w_up, w_down):
    depth = w_in.shape[0]
    bp, tp, _ = x_prompt.shape
    bs, ts, _ = x_sample.shape
    past = cache_k.shape[2]
    log_gamma = jnp.log1p(-jnp.exp2(-5.0 - jnp.arange(R_HEADS, dtype=F32)))
    pos_p = jnp.arange(tp, dtype=jnp.int32)
    pos_s = past + jnp.arange(ts, dtype=jnp.int32)
    zero_state = jnp.zeros((bp, R_HEADS, R_DK, R_DV), F32)
    xp, xs = x_prompt, x_sample
    outs = [[] for _ in range(6)]
    for l in range(depth):
        lam_init = 0.8 - 0.6 * math.exp(-0.3 * l)
        lw = _layer_weights(l, g_mix, w_in, g_q, g_k, lambda_q1, lambda_k1, lambda_q2, lambda_k2,
                            g_ret, w_ret_o, g_sub, w_diff_o, w_out, g_ffn, w_group, b_group,
                            w_expert, b_expert, w_gate, w_up, w_down)
        xp, kp, vp, sp = _token_group(xp, pos_p, lw, lam_init, log_gamma, zero_state, None)
        cache = (cache_k[l], cache_v[l])
        xs, ks, vs, ss = _token_group(xs, pos_s, lw, lam_init, log_gamma,
                                      state_ret[l].astype(F32), cache)
        for lst, val in zip(outs, (kp, vp, sp, ks, vs, ss)):
            lst.append(val)
    return (xp, xs) + tuple(jnp.stack(o) for o in outs)
```

```python
import functools
import math

import jax
import jax.numpy as jnp
from jax import lax
from jax.experimental import pallas as pl
from jax.experimental.pallas import tpu as pltpu

F32 = jnp.float32
BF16 = jnp.bfloat16
I32 = jnp.int32

CHUNK = 64
EPS = 1e-6
R_HEADS = 4
R_DK = 128
R_DV = 256
ROPE_BASE = 10000.0
D_HEADS = 4
D_DH = 128
D_DV = 256
N_GROUPS = 4
EXP_PER_GROUP = 4
N_EXPERTS = N_GROUPS * EXP_PER_GROUP
N_PAIRS = EXP_PER_GROUP * (EXP_PER_GROUP - 1) // 2
N_CLASSES = N_GROUPS * N_PAIRS
D_FF = 512
SEG = 1024
N_SEG = 8

V7X_LANES = 128
V7X_SUBLANES = 8
V7X_VMEM_LIMIT_BYTES = 56 * 1024 * 1024

ROW_TILE = 512
RET_SUPER = 512
ATT_HEADS_PER_STEP = 2
MERGE_TILES_PER_STEP = 2
SORT_TILES_PER_STEP = 8
PERM_TILES_PER_STEP = 4
MOE_TILE_MIN = 32
SAMPLE_CHUNK = 1024
MOE_TILE = 512
N_DMA_PRIORITIES = 2
CLS_PAD = 32
NEG_BIG = -1e30
LOG2E = math.log2(math.e)
SCORE_BOUND_MARGIN = 1.01
MAX_SHIFT_LOG2 = 100.0
PAY = V7X_LANES


def _row_tile(n):
    return min(ROW_TILE, n)


def _cparams(sem, vmem_mb=None):
    kw = dict(dimension_semantics=sem)
    if vmem_mb is not None:
        kw["vmem_limit_bytes"] = min(vmem_mb * 1024 * 1024, V7X_VMEM_LIMIT_BYTES)
    return pltpu.CompilerParams(**kw)


def _resident(shape):
    nd = len(shape)
    return pl.BlockSpec(shape, lambda *_: (0,) * nd, pipeline_mode=pl.Buffered(1))


def _rms(x, eps=EPS):
    return x * lax.rsqrt(jnp.mean(x * x, axis=-1, keepdims=True) + eps)


def _sigmoid(x):
    return 1.0 / (1.0 + jnp.exp(-x))


def _dot(a, b):
    return jnp.dot(a, b, preferred_element_type=F32)


def _dot_nt(a, b):
    return lax.dot_general(a, b, (((1,), (1,)), ((), ())), preferred_element_type=F32)


def _dot_tn(a, b):
    return lax.dot_general(a, b, (((0,), (0,)), ((), ())), preferred_element_type=F32)


def _store_cache_rows(ref, z):
    rows = z.shape[0]
    for half in range(2):
        for h in range(D_HEADS):
            lo = h * D_DV + half * V7X_LANES
            ref[pl.ds(half * D_HEADS + h, rows, stride=2 * D_HEADS), :] = z[:, lo:lo + V7X_LANES]


def _inproj_kernel(x_ref, g_ref, w_ref, cos_ref, sin_ref, gq_ref, gk_ref,
                   rqk_ref, rv_ref, rgs_ref, dq_ref, kb_ref, vt_ref, dk_ref, dv_ref, sgr_ref, sgd_ref):
    x = x_ref[...]
    hb = (_rms(x) * g_ref[...]).astype(BF16)

    def seg(s):
        return _dot(hb, w_ref[:, s * SEG:(s + 1) * SEG])

    cos = cos_ref[...]
    sin = sin_ref[...]
    z = seg(0)
    for j in range(2 * R_HEADS):
        v = z[:, j * R_DK:(j + 1) * R_DK]
        r = v * cos + pltpu.roll(v, R_DK // 2, 1) * sin
        if j >= R_HEADS:
            r = r * (R_DK ** -0.5)
        rqk_ref[:, j * R_DK:(j + 1) * R_DK] = r.astype(BF16)

    z = seg(2)
    rgs_ref[...] = (z * _sigmoid(z)).astype(BF16)

    z = seg(3)
    gq = gq_ref[...] * (D_DH ** -0.5 * LOG2E)
    for j in range(2 * D_HEADS):
        v = z[:, j * D_DH:(j + 1) * D_DH]
        dq_ref[:, j * D_DH:(j + 1) * D_DH] = (_rms(v) * gq).astype(BF16)
    z = seg(4)
    gk = gk_ref[...]
    kn = jnp.concatenate([_rms(z[:, j * D_DH:(j + 1) * D_DH]) * gk for j in range(2 * D_HEADS)], axis=1)
    kb_ref[...] = kn.astype(BF16)
    _store_cache_rows(dk_ref, kn)
    z = seg(5)
    _store_cache_rows(dv_ref, z)
    vt_ref[0] = z.T.astype(BF16)
    sgr_ref[...] = _sigmoid(seg(6)).astype(BF16)
    sgd_ref[...] = _sigmoid(seg(7)).astype(BF16)
    rv_ref[...] = seg(1).astype(BF16)


def _in_proj(x2d, g_mix, w_in_bf, cos_tab, sin_tab, g_q, g_k):
    n, d = x2d.shape
    tm = _row_tile(n)
    nt = n // tm
    ntab = cos_tab.shape[0] // tm
    row = lambda w: pl.BlockSpec((tm, w), lambda i: (i, 0))
    tab = pl.BlockSpec((tm, R_DK), lambda i: (i % ntab, 0))
    flat = jax.ShapeDtypeStruct((n, SEG), BF16)
    cache_rows = jax.ShapeDtypeStruct((n * 2 * D_HEADS, V7X_LANES), F32)
    cache_spec = pl.BlockSpec((tm * 2 * D_HEADS, V7X_LANES), lambda i: (i, 0))
    outs = [flat, flat, flat, flat, flat, jax.ShapeDtypeStruct((nt, SEG, tm), BF16),
            cache_rows, cache_rows, flat, flat]
    return pl.pallas_call(
        _inproj_kernel,
        out_shape=outs,
        grid=(nt,),
        in_specs=[row(d), _resident((1, d)), _resident(w_in_bf.shape), tab, tab,
                  _resident((1, D_DH)), _resident((1, D_DH))],
        out_specs=[row(SEG)] * 5 + [pl.BlockSpec((1, SEG, tm), lambda i: (i, 0, 0)),
                                    cache_spec, cache_spec, row(SEG), row(SEG)],
        compiler_params=_cparams(("arbitrary",), 56),
        name="in_proj",
    )(x2d, g_mix.reshape(1, d), w_in_bf, cos_tab, sin_tab,
      g_q.reshape(1, D_DH), g_k.reshape(1, D_DH))


def _ret_kernel(lg_ref, qk_ref, v_ref, rgs_ref, s0_ref, g_ref,
                o_ref, sfin_ref, s_scr, d_scr, lam_scr, wk_scr, dec_scr, *, c_len):
    c = pl.program_id(1)

    @pl.when((pl.program_id(0) == 0) & (c == 0))
    def _():
        t = lax.broadcasted_iota(I32, (c_len, c_len), 0)
        s = lax.broadcasted_iota(I32, (c_len, c_len), 1)
        dist = jnp.abs(t - s).astype(F32)
        vis = (s // CHUNK) <= (t // CHUNK)
        pos_v = lax.broadcasted_iota(I32, (c_len, R_DV), 0).astype(F32)
        pos_k = lax.broadcasted_iota(I32, (c_len, R_DK), 0).astype(F32)
        for h in range(R_HEADS):
            lg = lg_ref[h]
            d_scr[h] = jnp.where(vis, jnp.exp(lg * dist), 0.0)
            lam_scr[h] = jnp.exp(lg * (pos_v + 1.0))
            wk_scr[h] = jnp.exp(lg * (c_len - 1.0 - pos_k))
            dec_scr[h] = jnp.exp(lg * jnp.full((V7X_SUBLANES, R_DV), float(c_len), F32))

    @pl.when(c == 0)
    def _():
        s_scr[...] = s0_ref[0]

    for h in range(R_HEADS):
        q = qk_ref[0, :, h * R_DK:(h + 1) * R_DK]
        k = qk_ref[0, :, (R_HEADS + h) * R_DK:(R_HEADS + h + 1) * R_DK]
        v = v_ref[0, :, h * R_DV:(h + 1) * R_DV]
        state = s_scr[h]
        s = _dot_nt(q, k) * d_scr[h]
        o = _dot(s.astype(BF16), v) + lam_scr[h] * _dot(q, state.astype(BF16))
        kw = (k.astype(F32) * wk_scr[h]).astype(BF16)
        s_scr[h] = state * dec_scr[h][0:1, :] + _dot_tn(kw, v)
        gate = rgs_ref[0, :, h * R_DV:(h + 1) * R_DV].astype(F32)
        o_ref[0, :, h * R_DV:(h + 1) * R_DV] = (_rms(o) * g_ref[...] * gate).astype(BF16)

    @pl.when(c == pl.num_programs(1) - 1)
    def _():
        sfin_ref[0] = s_scr[...]


def _retention(rqk, rv, rgs, state0, g_ret, log_gamma, bsz, t_len):
    c_len = min(RET_SUPER, t_len)
    nc = t_len // c_len
    rqk3 = rqk.reshape(bsz, t_len, SEG)
    rv3 = rv.reshape(bsz, t_len, SEG)
    rgs3 = rgs.reshape(bsz, t_len, SEG)
    rows = pl.BlockSpec((1, c_len, SEG), lambda b, c, lg: (b, c, 0))
    sspec = pl.BlockSpec((1, R_HEADS, R_DK, R_DV), lambda b, c, lg: (b, 0, 0, 0))
    gspec = pl.BlockSpec((1, R_DV), lambda b, c, lg: (0, 0))
    o, sfin = pl.pallas_call(
        functools.partial(_ret_kernel, c_len=c_len),
        out_shape=[jax.ShapeDtypeStruct((bsz, t_len, SEG), BF16),
                   jax.ShapeDtypeStruct((bsz, R_HEADS, R_DK, R_DV), F32)],
        grid_spec=pltpu.PrefetchScalarGridSpec(
            num_scalar_prefetch=1,
            grid=(bsz, nc),
            in_specs=[rows, rows, rows, sspec, gspec],
            out_specs=[rows, sspec],
            scratch_shapes=[pltpu.VMEM((R_HEADS, R_DK, R_DV), F32),
                            pltpu.VMEM((R_HEADS, c_len, c_len), F32),
                            pltpu.VMEM((R_HEADS, c_len, R_DV), F32),
                            pltpu.VMEM((R_HEADS, c_len, R_DK), F32),
                            pltpu.VMEM((R_HEADS, V7X_SUBLANES, R_DV), F32)],
        ),
        compiler_params=_cparams(("arbitrary",) * 2, 32),
        name="retention",
    )(log_gamma, rqk3, rv3, rgs3, state0, g_ret.reshape(1, R_DV))
    return o.reshape(bsz * t_len, SEG), sfin


def _lambda_full(lp_ref, lam_init):
    lp = lp_ref[...]
    a = jnp.sum(lp[0:1] * lp[1:2], axis=-1, keepdims=True)
    b = jnp.sum(lp[2:3] * lp[3:4], axis=-1, keepdims=True)
    return jnp.exp(a) - jnp.exp(b) + lam_init


def _attn_finish(o1, l1, o2, l2, lam, g, lam_init):
    o = o1 / l1 - lam * (o2 / l2)
    return (_rms(o) * g * (1.0 - lam_init)).astype(BF16)


def _attn_prompt_kernel(bound_ref, q_ref, k_ref, vt_ref, lp_ref, g_ref, o_ref,
                        m_scr, l_scr, a_scr, bias, *, tile, heads, lam_init, fixed_shift):
    i = pl.program_id(2)
    w = 2 * D_DH

    @pl.when((pl.program_id(0) == 0) & (pl.program_id(1) == 0) & (i == 0))
    def _():
        kk = lax.broadcasted_iota(I32, (tile, tile), 0)
        qq = lax.broadcasted_iota(I32, (tile, tile), 1)
        bias[...] = jnp.where((kk // CHUNK) <= (qq // CHUNK), 0.0, NEG_BIG)

    def update(s, vts, idx, first):
        if fixed_shift:
            p = jnp.exp2(s - bound_ref[0])
            psum = jnp.sum(p, axis=0, keepdims=True)
            pv = _dot(vts, p.astype(BF16))
            l_scr[idx] = psum if first else l_scr[idx] + psum
            a_scr[idx] = pv if first else a_scr[idx] + pv
            return
        smax = jnp.max(s, axis=0, keepdims=True)
        m_new = smax if first else jnp.maximum(m_scr[idx], smax)
        p = jnp.exp2(s - m_new)
        psum = jnp.sum(p, axis=0, keepdims=True)
        pv = _dot(vts, p.astype(BF16))
        if first:
            l_scr[idx] = psum
            a_scr[idx] = pv
        else:
            alpha = jnp.exp2(m_scr[idx] - m_new)
            l_scr[idx] = alpha * l_scr[idx] + psum
            a_scr[idx] = alpha * a_scr[idx] + pv
        m_scr[idx] = m_new

    def block(hh, j, mask=None, first=False):
        ks = k_ref[0, pl.ds(pl.multiple_of(j * tile, tile), tile), hh * w:(hh + 1) * w]
        vts = vt_ref[0, j, hh * D_DV:(hh + 1) * D_DV, :]
        q = q_ref[0, :, hh * w:(hh + 1) * w]
        s1 = _dot_nt(ks[:, :D_DH], q[:, :D_DH])
        s2 = _dot_nt(ks[:, D_DH:], q[:, D_DH:])
        if mask is not None:
            s1 = s1 + mask
            s2 = s2 + mask
        update(s1, vts, 2 * hh, first)
        update(s2, vts, 2 * hh + 1, first)

    for hh in range(heads):
        block(hh, i, bias[...], first=True)

    def body(jj, carry):
        for hh in range(heads):
            block(hh, 2 * jj)
            block(hh, 2 * jj + 1)
        return carry

    lax.fori_loop(0, i // 2, body, 0)

    @pl.when(i % 2 == 1)
    def _():
        for hh in range(heads):
            block(hh, i - 1)

    lam = _lambda_full(lp_ref, lam_init)
    for hh in range(heads):
        a1, l1, a2, l2 = a_scr[2 * hh], l_scr[2 * hh], a_scr[2 * hh + 1], l_scr[2 * hh + 1]
        ot = a1 * (1.0 / l1) - lam * (a2 * (1.0 / l2))
        scale = lax.rsqrt(jnp.mean(ot * ot, axis=0, keepdims=True) + EPS) * (1.0 - lam_init)
        o_ref[0, 0, hh * D_DV:(hh + 1) * D_DV, :] = (ot * scale * g_ref[...]).astype(BF16)


def _attn_prompt(dq, kb, vt, g_q, g_k, lam_p, g_sub, lam_init, bsz, t_len):
    tile = vt.shape[2]
    assert t_len % tile == 0 and tile % CHUNK == 0
    nq = t_len // tile
    w = 2 * D_DH
    q3 = dq.reshape(bsz, t_len, SEG)
    k3 = kb.reshape(bsz, t_len, SEG)
    vt4 = vt.reshape(bsz, nq, SEG, tile)
    hps = ATT_HEADS_PER_STEP
    qspec = pl.BlockSpec((1, tile, hps * w), lambda b, h, i: (b, i, h))
    kspec = pl.BlockSpec((1, t_len, hps * w), lambda b, h, i: (b, 0, h))
    vspec = pl.BlockSpec((1, nq, hps * D_DV, tile), lambda b, h, i: (b, 0, h, 0))
    n_stat = 2 * hps
    col = lambda: pltpu.VMEM((n_stat, 1, tile), F32)

    def call(fixed_shift):
        return pl.pallas_call(
            functools.partial(_attn_prompt_kernel, tile=tile, heads=hps, lam_init=lam_init,
                              fixed_shift=fixed_shift),
            out_shape=jax.ShapeDtypeStruct((bsz, nq, SEG, tile), BF16),
            grid=(bsz, D_HEADS // hps, nq),
            in_specs=[pl.BlockSpec(memory_space=pltpu.SMEM), qspec, kspec, vspec,
                      pl.BlockSpec((4, D_DH), lambda b, h, i: (0, 0)),
                      pl.BlockSpec((D_DV, tile), lambda b, h, i: (0, 0))],
            out_specs=pl.BlockSpec((1, 1, hps * D_DV, tile), lambda b, h, i: (b, i, h, 0)),
            scratch_shapes=[col(), col(), pltpu.VMEM((n_stat, D_DV, tile), F32),
                            pltpu.VMEM((tile, tile), F32)],
            compiler_params=_cparams(("arbitrary",) * 3, 40),
            name="attn_prompt_fixed" if fixed_shift else "attn_prompt_online",
        )

    bound = SCORE_BOUND_MARGIN * D_DH * (D_DH ** -0.5 * LOG2E) * jnp.max(jnp.abs(g_q * g_k))
    g_col = jnp.broadcast_to(g_sub.astype(F32)[:, None], (D_DV, tile))
    args = (bound.reshape(1).astype(F32), q3, k3, vt4, lam_p, g_col)
    out = lax.cond(2.0 * bound < MAX_SHIFT_LOG2,
                   lambda *a: call(True)(*a), lambda *a: call(False)(*a), *args)
    return out.reshape(bsz * nq, SEG, tile)


def _attn_sample_kernel(q_ref, kc_ref, vc_ref, kn_ref, vn_ref, lp_ref, g_ref, o_ref,
                        m_scr, l_scr, a_scr, *, lam_init, chunk, t_new):
    c = pl.program_id(1)

    @pl.when(c == 0)
    def _():
        m_scr[...] = jnp.full(m_scr.shape, NEG_BIG, F32)
        l_scr[...] = jnp.zeros(l_scr.shape, F32)
        a_scr[...] = jnp.zeros(a_scr.shape, F32)

    q = q_ref[0]

    def head_rows(ref, h, half, n):
        return ref[pl.ds(half * D_HEADS + h, n, stride=2 * D_HEADS), :].astype(BF16)

    def absorb(k_src, v_src, n):
        for h in range(D_HEADS):
            v = jnp.concatenate([head_rows(v_src, h, 0, n), head_rows(v_src, h, 1, n)], axis=1)
            for mp in range(2):
                idx = 2 * h + mp
                lo = h * 2 * D_DH + mp * D_DH
                s = _dot_nt(q[:, lo:lo + D_DH], head_rows(k_src, h, mp, n))
                m_old = m_scr[idx]
                m_new = jnp.maximum(m_old, jnp.max(s, axis=-1, keepdims=True))
                alpha = jnp.exp2(m_old - m_new)
                p = jnp.exp2(s - m_new)
                l_scr[idx] = alpha * l_scr[idx] + jnp.sum(p, axis=-1, keepdims=True)
                a_scr[idx] = alpha * a_scr[idx] + _dot(p.astype(BF16), v)
                m_scr[idx] = m_new

    absorb(kc_ref, vc_ref, chunk)

    @pl.when(c == pl.num_programs(1) - 1)
    def _():
        absorb(kn_ref, vn_ref, t_new)
        lam = _lambda_full(lp_ref, lam_init)
        for h in range(D_HEADS):
            o_ref[0, :, h * D_DV:(h + 1) * D_DV] = _attn_finish(
                a_scr[2 * h], l_scr[2 * h], a_scr[2 * h + 1], l_scr[2 * h + 1], lam, g_ref[...], lam_init)


def _attn_sample(dq, dk_rows, dv_rows, cache_k_rows, cache_v_rows, lam_p, g_sub, lam_init, bsz, t_len):
    rpf = 2 * D_HEADS
    past = cache_k_rows.shape[0] // (bsz * rpf)
    chunk = min(SAMPLE_CHUNK, past)
    nc = past // chunk
    q3 = dq.reshape(bsz, t_len, SEG)
    qspec = pl.BlockSpec((1, t_len, SEG), lambda b, c: (b, 0, 0))
    cspec = pl.BlockSpec((chunk * rpf, V7X_LANES), lambda b, c: (b * nc + c, 0))
    nspec = pl.BlockSpec((t_len * rpf, V7X_LANES), lambda b, c: (b, 0))
    n_stat = 2 * D_HEADS
    out = pl.pallas_call(
        functools.partial(_attn_sample_kernel, lam_init=lam_init, chunk=chunk, t_new=t_len),
        out_shape=jax.ShapeDtypeStruct((bsz, t_len, SEG), BF16),
        grid=(bsz, nc),
        in_specs=[qspec, cspec, cspec, nspec, nspec,
                  pl.BlockSpec((4, D_DH), lambda b, c: (0, 0)),
                  pl.BlockSpec((1, D_DV), lambda b, c: (0, 0))],
        out_specs=qspec,
        scratch_shapes=[pltpu.VMEM((n_stat, t_len, 1), F32), pltpu.VMEM((n_stat, t_len, 1), F32),
                        pltpu.VMEM((n_stat, t_len, D_DV), F32)],
        compiler_params=_cparams(("arbitrary",) * 2, 32),
        name="attn_sample",
    )(q3, cache_k_rows, cache_v_rows, dk_rows, dv_rows, lam_p, g_sub.reshape(1, D_DV))
    return out.reshape(bsz * t_len, SEG)


def _split_hi_lo(x):
    hi = x.astype(BF16)
    lo = (x - hi.astype(F32)).astype(BF16)
    return hi, lo


def _merge_kernel(*refs):
    cls_ref = refs[-1]
    for sub in range(cls_ref.shape[0]):
        _merge_tile(sub, cls_ref.shape[2], *refs)


def _merge_tile(sub, tm, ogr_ref, od_ref, sgr_ref, sgd_ref, x_ref, wr_ref, wd_ref, wo_ref,
                gf_ref, rw_hi_ref, rw_lo_ref, rb_ref, x1e_ref, cls_ref):
    rs = slice(sub * tm, (sub + 1) * tm)
    y_r = _dot(ogr_ref[rs, :], wr_ref[...])
    y_d = _dot_tn(od_ref[sub], wd_ref[...])
    m = sgr_ref[rs, :].astype(F32) * y_r + sgd_ref[rs, :].astype(F32) * y_d
    x1 = x_ref[rs, :] + _dot(m.astype(BF16), wo_ref[...])

    hn = _rms(x1) * gf_ref[...]
    h_hi, h_lo = _split_hi_lo(hn)
    w_hi = rw_hi_ref[...]
    both = _dot_nt(jnp.concatenate([w_hi, rw_lo_ref[...]], axis=0), h_hi)
    lt = both[:CLS_PAD] + both[CLS_PAD:] + _dot_nt(w_hi, h_lo)
    lt = lt + rb_ref[...][:, 0:1]
    g = [lt[i:i + 1, :] for i in range(N_GROUPS)]
    e = [lt[N_GROUPS + i:N_GROUPS + i + 1, :] for i in range(N_EXPERTS)]

    gmax = functools.reduce(jnp.maximum, g)
    gid = jnp.full(g[0].shape, N_GROUPS - 1, I32)
    for i in range(N_GROUPS - 2, -1, -1):
        gid = jnp.where(g[i] == gmax, i, gid)
    g_w = 1.0 / functools.reduce(lambda a, b: a + b, [jnp.exp(v - gmax) for v in g])

    es = []
    for j in range(EXP_PER_GROUP):
        v = e[(N_GROUPS - 1) * EXP_PER_GROUP + j]
        for i in range(N_GROUPS - 2, -1, -1):
            v = jnp.where(gid == i, e[i * EXP_PER_GROUP + j], v)
        es.append(v)

    def first_argmax(vals):
        mx = functools.reduce(jnp.maximum, vals)
        idx = jnp.full(mx.shape, len(vals) - 1, I32)
        for i in range(len(vals) - 2, -1, -1):
            idx = jnp.where(vals[i] == mx, i, idx)
        return mx, idx

    l1, i1 = first_argmax(es)
    rest = [jnp.where(i1 == j, -jnp.inf, es[j]) for j in range(EXP_PER_GROUP)]
    l2, i2 = first_argmax(rest)
    t = jnp.exp(l2 - l1)
    c1 = g_w / (1.0 + t)
    c2 = g_w * t / (1.0 + t)
    lo = jnp.minimum(i1, i2)
    hi = jnp.maximum(i1, i2)
    base = jnp.where(lo == 0, 0, jnp.where(lo == 1, EXP_PER_GROUP - 1, 2 * EXP_PER_GROUP - 3))
    cls = gid * N_PAIRS + base + hi - lo - 1
    wa = jnp.where(i1 < i2, c1, c2)
    wb = jnp.where(i1 < i2, c2, c1)

    cls_ref[sub] = cls
    rid = lax.broadcasted_iota(I32, (PAY, tm), 0)
    pay = jnp.where(rid == 0, wa, jnp.where(rid == 1, wb, 0.0))
    d = x_ref.shape[1]
    x1e_ref[rs, :d] = x1
    x1e_ref[rs, d:] = pay.T


def _merge(ogr, od, sgr, sgd, x2d, w_ret_o, w_diff_o, w_out, g_ffn, rw_hi, rw_lo, rb):
    n, d = x2d.shape
    tm = _row_tile(n)
    nt = n // tm
    n_sub = MERGE_TILES_PER_STEP if nt % MERGE_TILES_PER_STEP == 0 else 1
    row = lambda w: pl.BlockSpec((n_sub * tm, w), lambda i: (i, 0))
    return pl.pallas_call(
        _merge_kernel,
        out_shape=[jax.ShapeDtypeStruct((n, d + PAY), F32),
                   jax.ShapeDtypeStruct((nt, 1, tm), I32)],
        grid=(nt // n_sub,),
        in_specs=[row(SEG), pl.BlockSpec((n_sub, SEG, tm), lambda i: (i, 0, 0)), row(SEG), row(SEG), row(d),
                  _resident(w_ret_o.shape), _resident(w_diff_o.shape), _resident(w_out.shape),
                  _resident((1, d)), _resident(rw_hi.shape), _resident(rw_lo.shape),
                  _resident(rb.shape)],
        out_specs=[row(d + PAY), pl.BlockSpec((n_sub, 1, tm), lambda i: (i, 0, 0))],
        compiler_params=_cparams(("arbitrary",), 48),
        name="merge_route",
    )(ogr, od, sgr, sgd, x2d, w_ret_o, w_diff_o, w_out, g_ffn.reshape(1, d), rw_hi, rw_lo, rb)


def _prefix_excl(x):
    rid = lax.broadcasted_iota(I32, x.shape, 0)
    inc = x
    s = 1
    while s < x.shape[0]:
        inc = inc + jnp.where(rid >= s, pltpu.roll(inc, s, 0), 0.0)
        s *= 2
    return inc - x


def _sort_kernel(cls_ref, pos_ref, cnt_ref, off_ref, tiles_ref, cnt_scr, run_scr, off_scr,
                 *, moe_tile, n_tiles_pad):
    ph = pl.program_id(0)
    t = pl.program_id(1)
    k_tiles, _, tm = cls_ref.shape
    cid = lax.broadcasted_iota(I32, (CLS_PAD, tm), 0)

    def onehot(j):
        return jnp.where(cid == cls_ref[j], 1.0, 0.0)

    @pl.when(jnp.logical_and(ph == 0, t == 0))
    def _():
        cnt_scr[...] = jnp.zeros(cnt_scr.shape, F32)

    @pl.when(ph == 0)
    def _():
        for j in range(k_tiles):
            cnt_scr[...] += jnp.sum(onehot(j), axis=-1, keepdims=True)

    @pl.when(jnp.logical_and(ph == 1, t == 0))
    def _():
        cnt = cnt_scr[...]
        padded = jnp.ceil(cnt / moe_tile) * moe_tile
        off = _prefix_excl(padded)
        off_scr[...] = off
        run_scr[...] = jnp.zeros(run_scr.shape, F32)
        cnt_ref[...] = cnt.astype(I32)
        off_ref[...] = off.astype(I32)
        end = (off + padded)[:, 0:1]
        total = jnp.max(end, axis=0, keepdims=True)
        n_used = total / moe_tile
        p = lax.broadcasted_iota(I32, (1, n_tiles_pad), 1).astype(F32)
        start = jnp.minimum(p, n_used - 1.0) * moe_tile
        cid2 = lax.broadcasted_iota(I32, (CLS_PAD, n_tiles_pad), 0)
        before = jnp.logical_and(end <= start, cid2 < N_CLASSES)
        tcls = jnp.sum(before.astype(F32), axis=0, keepdims=True).astype(I32)
        grp = tcls // N_PAIRS
        pr = tcls - grp * N_PAIRS
        lo = (pr >= EXP_PER_GROUP - 1).astype(I32) + (pr >= 2 * EXP_PER_GROUP - 3).astype(I32)
        base = jnp.where(lo == 0, 0, jnp.where(lo == 1, EXP_PER_GROUP - 1, 2 * EXP_PER_GROUP - 3))
        hi = pr - base + lo + 1
        rid = lax.broadcasted_iota(I32, (V7X_SUBLANES, n_tiles_pad), 0)
        ea = grp * EXP_PER_GROUP + lo
        eb = grp * EXP_PER_GROUP + hi
        nu = jnp.broadcast_to(n_used.astype(I32), (1, n_tiles_pad))
        tiles_ref[...] = jnp.where(rid == 0, ea, jnp.where(rid == 1, eb, jnp.where(rid == 2, nu, 0)))

    @pl.when(ph == 1)
    def _():
        r = lax.broadcasted_iota(I32, (tm, tm), 0)
        c = lax.broadcasted_iota(I32, (tm, tm), 1)
        upper = jnp.where(r <= c, 1.0, 0.0).astype(BF16)
        for j in range(k_tiles):
            oh = onehot(j)
            incl = _dot(oh.astype(BF16), upper)
            slot = off_scr[...][:, 0:1] + run_scr[...][:, 0:1] + incl - 1.0
            pos_ref[j] = jnp.sum(oh * slot, axis=0, keepdims=True).astype(I32)
            run_scr[...] += jnp.sum(oh, axis=-1, keepdims=True)


def _sort(cls, moe_tile, n_tiles_pad):
    nt, _, tm = cls.shape
    k_tiles = SORT_TILES_PER_STEP if nt % SORT_TILES_PER_STEP == 0 else 1
    blk = pl.BlockSpec((k_tiles, 1, tm), lambda ph, t: (t, 0, 0))
    oblk = pl.BlockSpec((k_tiles, 1, tm), lambda ph, t: (t * ph, 0, 0))
    whole = lambda shape: pl.BlockSpec(shape, lambda ph, t: (0, 0))
    return pl.pallas_call(
        functools.partial(_sort_kernel, moe_tile=moe_tile, n_tiles_pad=n_tiles_pad),
        out_shape=[jax.ShapeDtypeStruct((nt, 1, tm), I32),
                   jax.ShapeDtypeStruct((CLS_PAD, V7X_LANES), I32),
                   jax.ShapeDtypeStruct((CLS_PAD, V7X_LANES), I32),
                   jax.ShapeDtypeStruct((V7X_SUBLANES, n_tiles_pad), I32)],
        grid=(2, nt // k_tiles),
        in_specs=[blk],
        out_specs=[oblk, whole((CLS_PAD, V7X_LANES)), whole((CLS_PAD, V7X_LANES)),
                   whole((V7X_SUBLANES, n_tiles_pad))],
        scratch_shapes=[pltpu.VMEM((CLS_PAD, V7X_LANES), F32)] * 3,
        compiler_params=_cparams(("arbitrary",) * 2),
        name="class_sort",
    )(cls)


def _permute_kernel(pos_ref, cnt_ref, off_ref, src_ref, dst_ref, zblk, sem, zsem, *, moe_tile):
    t = pl.program_id(0)
    k_tiles, _, tm = pos_ref.shape
    rows = k_tiles * tm

    def row_copy(r):
        return pltpu.make_async_copy(src_ref.at[pl.ds(r, 1)],
                                     dst_ref.at[pl.ds(pos_ref[r // tm, 0, r % tm], 1)], sem)

    for r in range(rows):
        row_copy(r).start(priority=r % N_DMA_PRIORITIES)

    @pl.when(t == 0)
    def _():
        zblk[...] = jnp.zeros(zblk.shape, zblk.dtype)
        used = 0
        pads = []
        for c in range(N_CLASSES):
            cnt = cnt_ref[c, 0]
            off = off_ref[c, 0]
            padded = ((cnt + moe_tile - 1) // moe_tile) * moe_tile
            used = off + padded
            pads.append((off + cnt, padded - cnt))

        def pad_dmas(do):
            for first_row, n_pad in pads:
                size = 1
                while size < moe_tile:
                    @pl.when((n_pad & size) != 0)
                    def _(first_row=first_row, size=size):
                        if size < V7X_SUBLANES:
                            for j in range(size):
                                do(pltpu.make_async_copy(zblk.at[pl.ds(0, 1)],
                                                         dst_ref.at[pl.ds(first_row + j, 1)], zsem))
                        else:
                            row0 = pl.multiple_of(first_row, V7X_SUBLANES)
                            do(pltpu.make_async_copy(zblk.at[pl.ds(0, size)],
                                                     dst_ref.at[pl.ds(row0, size)], zsem))
                    first_row = first_row + (n_pad & size)
                    size *= 2

        pad_dmas(lambda cp: cp.start())
        pad_dmas(lambda cp: cp.wait())

        def tile_copy(p):
            return pltpu.make_async_copy(zblk, dst_ref.at[pl.ds(p * moe_tile, moe_tile)], zsem)

        def tile_start(p, carry):
            tile_copy(p).start()
            return carry

        def tile_wait(p, carry):
            tile_copy(p).wait()
            return carry

        first, last = used // moe_tile, dst_ref.shape[0] // moe_tile
        lax.fori_loop(first, last, tile_start, 0)
        lax.fori_loop(first, last, tile_wait, 0)

    for r in range(rows):
        row_copy(r).wait()


def _perm_tiles_per_step(nt):
    return PERM_TILES_PER_STEP if nt % PERM_TILES_PER_STEP == 0 else 1


def _permute(pos, cnt, off, src, n_rows_out, moe_tile):
    nt, _, tm = pos.shape
    width = src.shape[1]
    k = _perm_tiles_per_step(nt)
    smem = lambda shape, imap: pl.BlockSpec(shape, imap, memory_space=pltpu.SMEM)
    return pl.pallas_call(
        functools.partial(_permute_kernel, moe_tile=moe_tile),
        out_shape=jax.ShapeDtypeStruct((n_rows_out, width), src.dtype),
        grid=(nt // k,),
        in_specs=[smem((k, 1, tm), lambda t: (t, 0, 0)),
                  smem(cnt.shape, lambda t: (0, 0)),
                  smem(off.shape, lambda t: (0, 0)),
                  pl.BlockSpec((k * tm, width), lambda t: (t, 0))],
        out_specs=pl.BlockSpec(memory_space=pl.ANY),
        scratch_shapes=[pltpu.VMEM((moe_tile, width), src.dtype),
                        pltpu.SemaphoreType.DMA, pltpu.SemaphoreType.DMA],
        compiler_params=_cparams(("arbitrary",), 32),
        name="permute_rows",
    )(pos, cnt, off, src)


def _unpermute_kernel(pos_ref, src_ref, dst_ref, sem):
    k_tiles, _, tm = pos_ref.shape
    rows = k_tiles * tm

    def row_copy(r):
        return pltpu.make_async_copy(src_ref.at[pl.ds(pos_ref[r // tm, 0, r % tm], 1)],
                                     dst_ref.at[pl.ds(r, 1)], sem)

    for r in range(rows):
        row_copy(r).start(priority=r % N_DMA_PRIORITIES)
    for r in range(rows):
        row_copy(r).wait()


def _unpermute(pos, src, n_rows_out):
    nt, _, tm = pos.shape
    width = src.shape[1]
    k = _perm_tiles_per_step(nt)
    return pl.pallas_call(
        _unpermute_kernel,
        out_shape=jax.ShapeDtypeStruct((n_rows_out, width), src.dtype),
        grid=(nt // k,),
        in_specs=[pl.BlockSpec((k, 1, tm), lambda t: (t, 0, 0), memory_space=pltpu.SMEM),
                  pl.BlockSpec(memory_space=pl.ANY)],
        out_specs=pl.BlockSpec((k * tm, width), lambda t: (t, 0)),
        scratch_shapes=[pltpu.SemaphoreType.DMA],
        compiler_params=_cparams(("arbitrary",), 32),
        name="unpermute_rows",
    )(pos, src)


def _moe_kernel(ea_ref, eb_ref, nu_ref, xs_ref, gf_ref, wgu_a, wdn_a, wgu_b, wdn_b, y_ref):
    p = pl.program_id(0)
    d = y_ref.shape[1]

    @pl.when(p < nu_ref[0])
    def _():
        x = xs_ref[:, :d]
        wa = xs_ref[:, d:d + 1]
        wb = xs_ref[:, d + 1:d + 2]
        hn = (_rms(x) * gf_ref[...]).astype(BF16)

        def expert(wgu, wdn):
            gu = _dot(hn, wgu[0])
            gate = gu[:, :D_FF]
            he = gate * _sigmoid(gate) * gu[:, D_FF:]
            return _dot(he.astype(BF16), wdn[0])

        y_ref[...] = x + wa * expert(wgu_a, wdn_a) + wb * expert(wgu_b, wdn_b)

    @pl.when(p >= nu_ref[0])
    def _():
        y_ref[...] = jnp.zeros(y_ref.shape, y_ref.dtype)


def _moe(ea, eb, nu, xs, g_ffn, w_gu, w_dn, moe_tile):
    n_rows, width = xs.shape
    d = width - PAY
    n_tiles = n_rows // moe_tile
    used = lambda p, ea, eb, nu: (jnp.minimum(p, nu[0] - 1), 0)
    return pl.pallas_call(
        _moe_kernel,
        out_shape=jax.ShapeDtypeStruct((n_rows, d), F32),
        grid_spec=pltpu.PrefetchScalarGridSpec(
            num_scalar_prefetch=3,
            grid=(n_tiles,),
            in_specs=[pl.BlockSpec((moe_tile, width), used),
                      pl.BlockSpec((1, d), lambda p, ea, eb, nu: (0, 0)),
                      pl.BlockSpec((1, d, 2 * D_FF), lambda p, ea, eb, nu: (ea[p], 0, 0)),
                      pl.BlockSpec((1, D_FF, d), lambda p, ea, eb, nu: (ea[p], 0, 0)),
                      pl.BlockSpec((1, d, 2 * D_FF), lambda p, ea, eb, nu: (eb[p], 0, 0)),
                      pl.BlockSpec((1, D_FF, d), lambda p, ea, eb, nu: (eb[p], 0, 0))],
            out_specs=pl.BlockSpec((moe_tile, d), lambda p, ea, eb, nu: (p, 0)),
        ),
        compiler_params=_cparams(("arbitrary",), 48),
        name="moe_sorted",
    )(ea, eb, nu, xs, g_ffn.reshape(1, d), w_gu, w_dn, w_gu, w_dn)


def _hier_moe_residual(x1e, cls, g_ffn, w_gu, w_dn):
    n = x1e.shape[0]
    moe_tile = min(MOE_TILE, max(MOE_TILE_MIN, pl.next_power_of_2(n // N_CLASSES) // 2))
    n_tiles = n // moe_tile + N_CLASSES
    n_tiles_pad = -(-n_tiles // V7X_LANES) * V7X_LANES
    pos, cnt, off, tiles = _sort(cls, moe_tile, n_tiles_pad)
    xs = _permute(pos, cnt, off, x1e, n_tiles * moe_tile, moe_tile)
    ys = _moe(tiles[0, :n_tiles], tiles[1, :n_tiles], tiles[2, :1], xs, g_ffn, w_gu, w_dn, moe_tile)
    return _unpermute(pos, ys, n)


def _rotary_tables(pos):
    half = R_DK // 2
    inv = 1.0 / (ROPE_BASE ** jnp.linspace(0.0, 1.0, half, dtype=F32))
    ang = pos.astype(F32)[:, None] * inv[None, :]
    cos = jnp.cos(ang)
    sin = jnp.sin(ang)
    return jnp.concatenate([cos, cos], axis=-1), jnp.concatenate([-sin, sin], axis=-1)


def _token_group(x, pos, lw, lam_init, log_gamma, state0, cache):
    bsz, t_len, d = x.shape
    x2d = x.reshape(bsz * t_len, d)
    cos_tab, sin_tab = _rotary_tables(pos)
    tm = _row_tile(bsz * t_len)
    if t_len < tm:
        reps = tm // t_len
        cos_tab = jnp.tile(cos_tab, (reps, 1))
        sin_tab = jnp.tile(sin_tab, (reps, 1))
    rqk, rv, rgs, dq, kb, vt, dk_rows, dv_rows, sgr, sgd = _in_proj(
        x2d, lw["g_mix"], lw["w_in"], cos_tab, sin_tab, lw["g_q"], lw["g_k"])
    ogr, s_fin = _retention(rqk, rv, rgs, state0, lw["g_ret"], log_gamma, bsz, t_len)
    if cache is None:
        od = _attn_prompt(dq, kb, vt, lw["g_q"], lw["g_k"], lw["lam_p"], lw["g_sub"], lam_init,
                          bsz, t_len)
    else:
        od = _attn_sample(dq, dk_rows, dv_rows, _to_cache_rows(cache[0]), _to_cache_rows(cache[1]),
                          lw["lam_p"], lw["g_sub"], lam_init, bsz, t_len)
        od = od.reshape(-1, tm, SEG).transpose(0, 2, 1)
    x1e, cls = _merge(ogr, od, sgr, sgd, x2d, lw["w_ret_o"], lw["w_diff_o"], lw["w_out"],
                      lw["g_ffn"], lw["rw_hi"], lw["rw_lo"], lw["rb"])
    y = _hier_moe_residual(x1e, cls, lw["g_ffn"], lw["w_gu"], lw["w_dn"])
    return (y.reshape(bsz, t_len, d), _from_cache_rows(dk_rows, bsz, t_len),
            _from_cache_rows(dv_rows, bsz, t_len), s_fin)


def _to_cache_rows(c):
    b, p, h, w = c.shape
    halves = w // V7X_LANES
    return c.reshape(b * p, h, halves, V7X_LANES).transpose(0, 2, 1, 3).reshape(b * p * h * halves, V7X_LANES)


def _from_cache_rows(rows, bsz, t_len):
    halves = D_DV // V7X_LANES
    r = rows.reshape(bsz * t_len, halves, D_HEADS, V7X_LANES).transpose(0, 2, 1, 3)
    return r.reshape(bsz, t_len, D_HEADS, D_DV)


def _layer_weights(l, g_mix, w_in, g_q, g_k, lambda_q1, lambda_k1, lambda_q2, lambda_k2, g_ret,
                   w_ret_o, g_sub, w_diff_o, w_out, g_ffn, w_group, b_group, w_expert, b_expert,
                   w_gate, w_up, w_down):
    d = w_in.shape[1]
    n_r = N_GROUPS + N_EXPERTS
    rw = jnp.concatenate([w_group[l], w_expert[l]], axis=1).astype(F32).T
    rw = jnp.zeros((CLS_PAD, d), F32).at[:n_r].set(rw)
    rw_hi = rw.astype(BF16)
    rw_lo = (rw - rw_hi.astype(F32)).astype(BF16)
    rb = jnp.concatenate([b_group[l], b_expert[l]]).astype(F32)
    rb = jnp.zeros((CLS_PAD,), F32).at[:n_r].set(rb)
    rb = jnp.broadcast_to(rb[:, None], (CLS_PAD, V7X_LANES))
    return dict(
        g_mix=g_mix[l], w_in=w_in[l].astype(BF16), g_q=g_q[l], g_k=g_k[l],
        lam_p=jnp.stack([lambda_q1[l], lambda_k1[l], lambda_q2[l], lambda_k2[l]]).astype(F32),
        g_ret=g_ret[l], w_ret_o=w_ret_o[l].astype(BF16), g_sub=g_sub[l],
        w_diff_o=w_diff_o[l].astype(BF16), w_out=w_out[l].astype(BF16), g_ffn=g_ffn[l],
        rw_hi=rw_hi, rw_lo=rw_lo, rb=rb,
        w_gu=jnp.concatenate([w_gate[l], w_up[l]], axis=-1).astype(BF16),
        w_dn=w_down[l].astype(BF16))


def kernel(x_prompt, x_sample, cache_k, cache_v, state_ret, g_mix, w_in, g_q, g_k, lambda_q1, lambda_k1, lambda_q2, lambda_k2, g_ret, w_ret_o, g_sub, w_diff_o, w_out, g_ffn, w_group, b_group, w_expert, b_expert, w_gate, w_up, w_down):
    depth = w_in.shape[0]
    bp, tp, _ = x_prompt.shape
    bs, ts, _ = x_sample.shape
    past = cache_k.shape[2]
    log_gamma = jnp.log1p(-jnp.exp2(-5.0 - jnp.arange(R_HEADS, dtype=F32)))
    pos_p = jnp.arange(tp, dtype=jnp.int32)
    pos_s = past + jnp.arange(ts, dtype=jnp.int32)
    zero_state = jnp.zeros((bp, R_HEADS, R_DK, R_DV), F32)
    xp, xs = x_prompt, x_sample
    outs = [[] for _ in range(6)]
    for l in range(depth):
        lam_init = 0.8 - 0.6 * math.exp(-0.3 * l)
        lw = _layer_weights(l, g_mix, w_in, g_q, g_k, lambda_q1, lambda_k1, lambda_q2, lambda_k2,
                            g_ret, w_ret_o, g_sub, w_diff_o, w_out, g_ffn, w_group, b_group,
                            w_expert, b_expert, w_gate, w_up, w_down)
        xp, kp, vp, sp = _token_group(xp, pos_p, lw, lam_init, log_gamma, zero_state, None)
        cache = (cache_k[l], cache_v[l])
        xs, ks, vs, ss = _token_group(xs, pos_s, lw, lam_init, log_gamma,
                                      state_ret[l].astype(F32), cache)
        for lst, val in zip(outs, (kp, vp, sp, ks, vs, ss)):
            lst.append(val)
    return (xp, xs) + tuple(jnp.stack(o) for o in outs)
```

```python
import functools
import math

import jax
import jax.numpy as jnp
from jax import lax
from jax.experimental import pallas as pl
from jax.experimental.pallas import tpu as pltpu

F32 = jnp.float32
BF16 = jnp.bfloat16
I32 = jnp.int32

CHUNK = 64
EPS = 1e-6
R_HEADS = 4
R_DK = 128
R_DV = 256
ROPE_BASE = 10000.0
D_HEADS = 4
D_DH = 128
D_DV = 256
N_GROUPS = 4
EXP_PER_GROUP = 4
N_EXPERTS = N_GROUPS * EXP_PER_GROUP
N_PAIRS = EXP_PER_GROUP * (EXP_PER_GROUP - 1) // 2
N_CLASSES = N_GROUPS * N_PAIRS
D_FF = 512
SEG = 1024

V7X_LANES = 128
V7X_SUBLANES = 8
V7X_VMEM_LIMIT_BYTES = 56 * 1024 * 1024
N_DMA_PRIORITIES = 2

ROW_TILE = 512
RET_SUPER = 512
ATT_HEADS_PER_STEP = 4
MERGE_TILES_PER_STEP = 2
SORT_TILES_PER_STEP = 8
PERM_TILES_PER_STEP = 4
SAMPLE_CHUNK = 1024
MOE_TILE = 512
MOE_TILE_MIN = 32
CLS_PAD = 32
PAY = V7X_LANES

NEG_BIG = -1e30
LOG2E = math.log2(math.e)
SCORE_BOUND_MARGIN = 1.01
MAX_SHIFT_LOG2 = 100.0


def _row_tile(n):
    return min(ROW_TILE, n)


def _cparams(sem, vmem_mb=None):
    kw = dict(dimension_semantics=sem)
    if vmem_mb is not None:
        kw["vmem_limit_bytes"] = min(vmem_mb * 1024 * 1024, V7X_VMEM_LIMIT_BYTES)
    return pltpu.CompilerParams(**kw)


def _resident(shape):
    nd = len(shape)
    return pl.BlockSpec(shape, lambda *_: (0,) * nd, pipeline_mode=pl.Buffered(1))


def _rms(x, eps=EPS):
    return x * lax.rsqrt(jnp.mean(x * x, axis=-1, keepdims=True) + eps)


def _sigmoid(x):
    return 1.0 / (1.0 + jnp.exp(-x))


def _dot(a, b):
    return jnp.dot(a, b, preferred_element_type=F32)


def _dot_nt(a, b):
    return lax.dot_general(a, b, (((1,), (1,)), ((), ())), preferred_element_type=F32)


def _dot_tn(a, b):
    return lax.dot_general(a, b, (((0,), (0,)), ((), ())), preferred_element_type=F32)


def _store_cache_rows(ref, z):
    rows = z.shape[0]
    for half in range(2):
        for h in range(D_HEADS):
            lo = h * D_DV + half * V7X_LANES
            ref[pl.ds(half * D_HEADS + h, rows, stride=2 * D_HEADS), :] = z[:, lo:lo + V7X_LANES]


def _inproj_kernel(x_ref, g_ref, w_ref, cos_ref, sin_ref, gq_ref, gk_ref,
                   rqk_ref, rv_ref, rgs_ref, dq_ref, kb_ref, vt_ref, dk_ref, dv_ref, sgr_ref, sgd_ref):
    x = x_ref[...]
    hb = (_rms(x) * g_ref[...]).astype(BF16)

    def seg(s):
        return _dot(hb, w_ref[:, s * SEG:(s + 1) * SEG])

    cos = cos_ref[...]
    sin = sin_ref[...]
    z = seg(0)
    for j in range(2 * R_HEADS):
        v = z[:, j * R_DK:(j + 1) * R_DK]
        r = v * cos + pltpu.roll(v, R_DK // 2, 1) * sin
        if j >= R_HEADS:
            r = r * (R_DK ** -0.5)
        rqk_ref[:, j * R_DK:(j + 1) * R_DK] = r.astype(BF16)

    z = seg(2)
    rgs_ref[...] = (z * _sigmoid(z)).astype(BF16)

    z = seg(3)
    gq = gq_ref[...] * (D_DH ** -0.5 * LOG2E)
    for j in range(2 * D_HEADS):
        v = z[:, j * D_DH:(j + 1) * D_DH]
        dq_ref[:, j * D_DH:(j + 1) * D_DH] = (_rms(v) * gq).astype(BF16)
    z = seg(4)
    gk = gk_ref[...]
    kn = jnp.concatenate([_rms(z[:, j * D_DH:(j + 1) * D_DH]) * gk for j in range(2 * D_HEADS)], axis=1)
    kb_ref[...] = kn.astype(BF16)
    _store_cache_rows(dk_ref, kn)
    z = seg(5)
    _store_cache_rows(dv_ref, z)
    vt_ref[0] = z.T.astype(BF16)
    sgr_ref[...] = _sigmoid(seg(6)).astype(BF16)
    sgd_ref[...] = _sigmoid(seg(7)).astype(BF16)
    rv_ref[...] = seg(1).astype(BF16)


def _in_proj(x2d, g_mix, w_in_bf, cos_tab, sin_tab, g_q, g_k):
    n, d = x2d.shape
    tm = _row_tile(n)
    nt = n // tm
    ntab = cos_tab.shape[0] // tm
    row = lambda w: pl.BlockSpec((tm, w), lambda i: (i, 0))
    tab = pl.BlockSpec((tm, R_DK), lambda i: (i % ntab, 0))
    flat = jax.ShapeDtypeStruct((n, SEG), BF16)
    cache_rows = jax.ShapeDtypeStruct((n * 2 * D_HEADS, V7X_LANES), F32)
    cache_spec = pl.BlockSpec((tm * 2 * D_HEADS, V7X_LANES), lambda i: (i, 0))
    outs = [flat, flat, flat, flat, flat, jax.ShapeDtypeStruct((nt, SEG, tm), BF16),
            cache_rows, cache_rows, flat, flat]
    return pl.pallas_call(
        _inproj_kernel,
        out_shape=outs,
        grid=(nt,),
        in_specs=[row(d), _resident((1, d)), _resident(w_in_bf.shape), tab, tab,
                  _resident((1, D_DH)), _resident((1, D_DH))],
        out_specs=[row(SEG)] * 5 + [pl.BlockSpec((1, SEG, tm), lambda i: (i, 0, 0)),
                                    cache_spec, cache_spec, row(SEG), row(SEG)],
        compiler_params=_cparams(("arbitrary",), 56),
        name="in_proj",
    )(x2d, g_mix.reshape(1, d), w_in_bf, cos_tab, sin_tab,
      g_q.reshape(1, D_DH), g_k.reshape(1, D_DH))


def _ret_kernel(lg_ref, qk_ref, v_ref, rgs_ref, s0_ref, g_ref,
                o_ref, sfin_ref, s_scr, d_scr, lam_scr, wk_scr, dec_scr, *, c_len):
    c = pl.program_id(1)

    @pl.when((pl.program_id(0) == 0) & (c == 0))
    def _():
        t = lax.broadcasted_iota(I32, (c_len, c_len), 0)
        s = lax.broadcasted_iota(I32, (c_len, c_len), 1)
        dist = jnp.abs(t - s).astype(F32)
        vis = (s // CHUNK) <= (t // CHUNK)
        pos_v = lax.broadcasted_iota(I32, (c_len, R_DV), 0).astype(F32)
        pos_k = lax.broadcasted_iota(I32, (c_len, R_DK), 0).astype(F32)
        for h in range(R_HEADS):
            lg = lg_ref[h]
            d_scr[h] = jnp.where(vis, jnp.exp(lg * dist), 0.0)
            lam_scr[h] = jnp.exp(lg * (pos_v + 1.0))
            wk_scr[h] = jnp.exp(lg * (c_len - 1.0 - pos_k))
            dec_scr[h] = jnp.exp(lg * jnp.full((V7X_SUBLANES, R_DV), float(c_len), F32))

    @pl.when(c == 0)
    def _():
        s_scr[...] = s0_ref[0]

    for h in range(R_HEADS):
        q = qk_ref[0, :, h * R_DK:(h + 1) * R_DK]
        k = qk_ref[0, :, (R_HEADS + h) * R_DK:(R_HEADS + h + 1) * R_DK]
        v = v_ref[0, :, h * R_DV:(h + 1) * R_DV]
        state = s_scr[h]
        s = _dot_nt(q, k) * d_scr[h]
        o = _dot(s.astype(BF16), v) + lam_scr[h] * _dot(q, state.astype(BF16))
        kw = (k.astype(F32) * wk_scr[h]).astype(BF16)
        s_scr[h] = state * dec_scr[h][0:1, :] + _dot_tn(kw, v)
        gate = rgs_ref[0, :, h * R_DV:(h + 1) * R_DV].astype(F32)
        o_ref[0, :, h * R_DV:(h + 1) * R_DV] = (_rms(o) * g_ref[...] * gate).astype(BF16)

    @pl.when(c == pl.num_programs(1) - 1)
    def _():
        sfin_ref[0] = s_scr[...]


def _retention(rqk, rv, rgs, state0, g_ret, log_gamma, bsz, t_len):
    c_len = min(RET_SUPER, t_len)
    nc = t_len // c_len
    rqk3 = rqk.reshape(bsz, t_len, SEG)
    rv3 = rv.reshape(bsz, t_len, SEG)
    rgs3 = rgs.reshape(bsz, t_len, SEG)
    rows = pl.BlockSpec((1, c_len, SEG), lambda b, c, lg: (b, c, 0))
    sspec = pl.BlockSpec((1, R_HEADS, R_DK, R_DV), lambda b, c, lg: (b, 0, 0, 0))
    gspec = pl.BlockSpec((1, R_DV), lambda b, c, lg: (0, 0))
    o, sfin = pl.pallas_call(
        functools.partial(_ret_kernel, c_len=c_len),
        out_shape=[jax.ShapeDtypeStruct((bsz, t_len, SEG), BF16),
                   jax.ShapeDtypeStruct((bsz, R_HEADS, R_DK, R_DV), F32)],
        grid_spec=pltpu.PrefetchScalarGridSpec(
            num_scalar_prefetch=1,
            grid=(bsz, nc),
            in_specs=[rows, rows, rows, sspec, gspec],
            out_specs=[rows, sspec],
            scratch_shapes=[pltpu.VMEM((R_HEADS, R_DK, R_DV), F32),
                            pltpu.VMEM((R_HEADS, c_len, c_len), F32),
                            pltpu.VMEM((R_HEADS, c_len, R_DV), F32),
                            pltpu.VMEM((R_HEADS, c_len, R_DK), F32),
                            pltpu.VMEM((R_HEADS, V7X_SUBLANES, R_DV), F32)],
        ),
        compiler_params=_cparams(("arbitrary",) * 2, 32),
        name="retention",
    )(log_gamma, rqk3, rv3, rgs3, state0, g_ret.reshape(1, R_DV))
    return o.reshape(bsz * t_len, SEG), sfin


def _lambda_full(lp_ref, lam_init):
    lp = lp_ref[...]
    a = jnp.sum(lp[0:1] * lp[1:2], axis=-1, keepdims=True)
    b = jnp.sum(lp[2:3] * lp[3:4], axis=-1, keepdims=True)
    return jnp.exp(a) - jnp.exp(b) + lam_init


def _attn_finish(o1, l1, o2, l2, lam, g, lam_init):
    o = o1 / l1 - lam * (o2 / l2)
    return (_rms(o) * g * (1.0 - lam_init)).astype(BF16)


def _attn_prompt_kernel(bound_ref, q_ref, k_ref, vt_ref, lp_ref, g_ref, o_ref,
                        m_scr, l_scr, a_scr, bias, *, tile, heads, lam_init, fixed_shift):
    i = pl.program_id(2)
    w = 2 * D_DH

    @pl.when((pl.program_id(0) == 0) & (pl.program_id(1) == 0) & (i == 0))
    def _():
        kk = lax.broadcasted_iota(I32, (tile, tile), 0)
        qq = lax.broadcasted_iota(I32, (tile, tile), 1)
        bias[...] = jnp.where((kk // CHUNK) <= (qq // CHUNK), 0.0, NEG_BIG)

    def update(s, vts, idx, first):
        if fixed_shift:
            p = jnp.exp2(s - bound_ref[0])
            psum = jnp.sum(p, axis=0, keepdims=True)
            pv = _dot(vts, p.astype(BF16))
            l_scr[idx] = psum if first else l_scr[idx] + psum
            a_scr[idx] = pv if first else a_scr[idx] + pv
            return
        smax = jnp.max(s, axis=0, keepdims=True)
        m_new = smax if first else jnp.maximum(m_scr[idx], smax)
        p = jnp.exp2(s - m_new)
        psum = jnp.sum(p, axis=0, keepdims=True)
        pv = _dot(vts, p.astype(BF16))
        if first:
            l_scr[idx] = psum
            a_scr[idx] = pv
        else:
            alpha = jnp.exp2(m_scr[idx] - m_new)
            l_scr[idx] = alpha * l_scr[idx] + psum
            a_scr[idx] = alpha * a_scr[idx] + pv
        m_scr[idx] = m_new

    def block(hh, j, mask=None, first=False):
        ks = k_ref[0, pl.ds(pl.multiple_of(j * tile, tile), tile), hh * w:(hh + 1) * w]
        vts = vt_ref[0, j, hh * D_DV:(hh + 1) * D_DV, :]
        q = q_ref[0, :, hh * w:(hh + 1) * w]
        s1 = _dot_nt(ks[:, :D_DH], q[:, :D_DH])
        s2 = _dot_nt(ks[:, D_DH:], q[:, D_DH:])
        if mask is not None:
            s1 = s1 + mask
            s2 = s2 + mask
        update(s1, vts, 2 * hh, first)
        update(s2, vts, 2 * hh + 1, first)

    for hh in range(heads):
        block(hh, i, bias[...], first=True)

    def body(jj, carry):
        for hh in range(heads):
            block(hh, 2 * jj)
            block(hh, 2 * jj + 1)
        return carry

    lax.fori_loop(0, i // 2, body, 0)

    @pl.when(i % 2 == 1)
    def _():
        for hh in range(heads):
            block(hh, i - 1)

    lam = _lambda_full(lp_ref, lam_init)
    for hh in range(heads):
        a1, l1, a2, l2 = a_scr[2 * hh], l_scr[2 * hh], a_scr[2 * hh + 1], l_scr[2 * hh + 1]
        ot = a1 * (1.0 / l1) - lam * (a2 * (1.0 / l2))
        scale = lax.rsqrt(jnp.mean(ot * ot, axis=0, keepdims=True) + EPS) * (1.0 - lam_init)
        o_ref[0, 0, hh * D_DV:(hh + 1) * D_DV, :] = (ot * scale * g_ref[...]).astype(BF16)


def _attn_prompt(dq, kb, vt, g_q, g_k, lam_p, g_sub, lam_init, bsz, t_len):
    tile = vt.shape[2]
    assert t_len % tile == 0 and tile % CHUNK == 0
    nq = t_len // tile
    w = 2 * D_DH
    q3 = dq.reshape(bsz, t_len, SEG)
    k3 = kb.reshape(bsz, t_len, SEG)
    vt4 = vt.reshape(bsz, nq, SEG, tile)
    hps = ATT_HEADS_PER_STEP
    qspec = pl.BlockSpec((1, tile, hps * w), lambda b, h, i: (b, i, h))
    kspec = pl.BlockSpec((1, t_len, hps * w), lambda b, h, i: (b, 0, h))
    vspec = pl.BlockSpec((1, nq, hps * D_DV, tile), lambda b, h, i: (b, 0, h, 0))
    n_stat = 2 * hps
    col = lambda: pltpu.VMEM((n_stat, 1, tile), F32)

    def call(fixed_shift):
        return pl.pallas_call(
            functools.partial(_attn_prompt_kernel, tile=tile, heads=hps, lam_init=lam_init,
                              fixed_shift=fixed_shift),
            out_shape=jax.ShapeDtypeStruct((bsz, nq, SEG, tile), BF16),
            grid=(bsz, D_HEADS // hps, nq),
            in_specs=[pl.BlockSpec(memory_space=pltpu.SMEM), qspec, kspec, vspec,
                      pl.BlockSpec((4, D_DH), lambda b, h, i: (0, 0)),
                      pl.BlockSpec((D_DV, tile), lambda b, h, i: (0, 0))],
            out_specs=pl.BlockSpec((1, 1, hps * D_DV, tile), lambda b, h, i: (b, i, h, 0)),
            scratch_shapes=[col(), col(), pltpu.VMEM((n_stat, D_DV, tile), F32),
                            pltpu.VMEM((tile, tile), F32)],
            compiler_params=_cparams(("arbitrary",) * 3, 56),
            name="attn_prompt_fixed" if fixed_shift else "attn_prompt_online",
        )

    bound = SCORE_BOUND_MARGIN * D_DH * (D_DH ** -0.5 * LOG2E) * jnp.max(jnp.abs(g_q * g_k))
    g_col = jnp.broadcast_to(g_sub.astype(F32)[:, None], (D_DV, tile))
    args = (bound.reshape(1).astype(F32), q3, k3, vt4, lam_p, g_col)
    out = lax.cond(2.0 * bound < MAX_SHIFT_LOG2,
                   lambda *a: call(True)(*a), lambda *a: call(False)(*a), *args)
    return out.reshape(bsz * nq, SEG, tile)


def _attn_sample_kernel(q_ref, kc_ref, vc_ref, kn_ref, vn_ref, lp_ref, g_ref, o_ref,
                        m_scr, l_scr, a_scr, *, lam_init, chunk, t_new):
    c = pl.program_id(1)

    @pl.when(c == 0)
    def _():
        m_scr[...] = jnp.full(m_scr.shape, NEG_BIG, F32)
        l_scr[...] = jnp.zeros(l_scr.shape, F32)
        a_scr[...] = jnp.zeros(a_scr.shape, F32)

    q = q_ref[0]

    def head_rows(ref, h, half, n):
        return ref[pl.ds(half * D_HEADS + h, n, stride=2 * D_HEADS), :].astype(BF16)

    def absorb(k_src, v_src, n):
        for h in range(D_HEADS):
            v = jnp.concatenate([head_rows(v_src, h, 0, n), head_rows(v_src, h, 1, n)], axis=1)
            for mp in range(2):
                idx = 2 * h + mp
                lo = h * 2 * D_DH + mp * D_DH
                s = _dot_nt(q[:, lo:lo + D_DH], head_rows(k_src, h, mp, n))
                m_old = m_scr[idx]
                m_new = jnp.maximum(m_old, jnp.max(s, axis=-1, keepdims=True))
                alpha = jnp.exp2(m_old - m_new)
                p = jnp.exp2(s - m_new)
                l_scr[idx] = alpha * l_scr[idx] + jnp.sum(p, axis=-1, keepdims=True)
                a_scr[idx] = alpha * a_scr[idx] + _dot(p.astype(BF16), v)
                m_scr[idx] = m_new

    absorb(kc_ref, vc_ref, chunk)

    @pl.when(c == pl.num_programs(1) - 1)
    def _():
        absorb(kn_ref, vn_ref, t_new)
        lam = _lambda_full(lp_ref, lam_init)
        for h in range(D_HEADS):
            o_ref[0, :, h * D_DV:(h + 1) * D_DV] = _attn_finish(
                a_scr[2 * h], l_scr[2 * h], a_scr[2 * h + 1], l_scr[2 * h + 1], lam, g_ref[...], lam_init)


def _attn_sample(dq, dk_rows, dv_rows, cache_k_rows, cache_v_rows, lam_p, g_sub, lam_init, bsz, t_len):
    rpf = 2 * D_HEADS
    past = cache_k_rows.shape[0] // (bsz * rpf)
    chunk = min(SAMPLE_CHUNK, past)
    nc = past // chunk
    q3 = dq.reshape(bsz, t_len, SEG)
    qspec = pl.BlockSpec((1, t_len, SEG), lambda b, c: (b, 0, 0))
    cspec = pl.BlockSpec((chunk * rpf, V7X_LANES), lambda b, c: (b * nc + c, 0))
    nspec = pl.BlockSpec((t_len * rpf, V7X_LANES), lambda b, c: (b, 0))
    n_stat = 2 * D_HEADS
    out = pl.pallas_call(
        functools.partial(_attn_sample_kernel, lam_init=lam_init, chunk=chunk, t_new=t_len),
        out_shape=jax.ShapeDtypeStruct((bsz, t_len, SEG), BF16),
        grid=(bsz, nc),
        in_specs=[qspec, cspec, cspec, nspec, nspec,
                  pl.BlockSpec((4, D_DH), lambda b, c: (0, 0)),
                  pl.BlockSpec((1, D_DV), lambda b, c: (0, 0))],
        out_specs=qspec,
        scratch_shapes=[pltpu.VMEM((n_stat, t_len, 1), F32), pltpu.VMEM((n_stat, t_len, 1), F32),
                        pltpu.VMEM((n_stat, t_len, D_DV), F32)],
        compiler_params=_cparams(("arbitrary",) * 2, 32),
        name="attn_sample",
    )(q3, cache_k_rows, cache_v_rows, dk_rows, dv_rows, lam_p, g_sub.reshape(1, D_DV))
    return out.reshape(bsz * t_len, SEG)


def _split_hi_lo(x):
    hi = x.astype(BF16)
    lo = (x - hi.astype(F32)).astype(BF16)
    return hi, lo


def _merge_kernel(*refs):
    cls_ref = refs[-1]
    for sub in range(cls_ref.shape[0]):
        _merge_tile(sub, cls_ref.shape[2], *refs)


def _merge_tile(sub, tm, ogr_ref, od_ref, sgr_ref, sgd_ref, x_ref, wr_ref, wd_ref, wo_ref,
                gf_ref, rw_hi_ref, rw_lo_ref, rb_ref, x1e_ref, cls_ref):
    rs = slice(sub * tm, (sub + 1) * tm)
    y_r = _dot(ogr_ref[rs, :], wr_ref[...])
    y_d = _dot_tn(od_ref[sub], wd_ref[...])
    m = sgr_ref[rs, :].astype(F32) * y_r + sgd_ref[rs, :].astype(F32) * y_d
    x1 = x_ref[rs, :] + _dot(m.astype(BF16), wo_ref[...])

    hn = _rms(x1) * gf_ref[...]
    h_hi, h_lo = _split_hi_lo(hn)
    w_hi = rw_hi_ref[...]
    both = _dot_nt(jnp.concatenate([w_hi, rw_lo_ref[...]], axis=0), h_hi)
    lt = both[:CLS_PAD] + both[CLS_PAD:] + _dot_nt(w_hi, h_lo)
    lt = lt + rb_ref[...][:, 0:1]
    g = [lt[i:i + 1, :] for i in range(N_GROUPS)]
    e = [lt[N_GROUPS + i:N_GROUPS + i + 1, :] for i in range(N_EXPERTS)]

    gmax = functools.reduce(jnp.maximum, g)
    gid = jnp.full(g[0].shape, N_GROUPS - 1, I32)
    for i in range(N_GROUPS - 2, -1, -1):
        gid = jnp.where(g[i] == gmax, i, gid)
    g_w = 1.0 / functools.reduce(lambda a, b: a + b, [jnp.exp(v - gmax) for v in g])

    es = []
    for j in range(EXP_PER_GROUP):
        v = e[(N_GROUPS - 1) * EXP_PER_GROUP + j]
        for i in range(N_GROUPS - 2, -1, -1):
            v = jnp.where(gid == i, e[i * EXP_PER_GROUP + j], v)
        es.append(v)

    def first_argmax(vals):
        mx = functools.reduce(jnp.maximum, vals)
        idx = jnp.full(mx.shape, len(vals) - 1, I32)
        for i in range(len(vals) - 2, -1, -1):
            idx = jnp.where(vals[i] == mx, i, idx)
        return mx, idx

    l1, i1 = first_argmax(es)
    rest = [jnp.where(i1 == j, -jnp.inf, es[j]) for j in range(EXP_PER_GROUP)]
    l2, i2 = first_argmax(rest)
    t = jnp.exp(l2 - l1)
    c1 = g_w / (1.0 + t)
    c2 = g_w * t / (1.0 + t)
    lo = jnp.minimum(i1, i2)
    hi = jnp.maximum(i1, i2)
    base = jnp.where(lo == 0, 0, jnp.where(lo == 1, EXP_PER_GROUP - 1, 2 * EXP_PER_GROUP - 3))
    cls = gid * N_PAIRS + base + hi - lo - 1
    wa = jnp.where(i1 < i2, c1, c2)
    wb = jnp.where(i1 < i2, c2, c1)

    cls_ref[sub] = cls
    rid = lax.broadcasted_iota(I32, (PAY, tm), 0)
    pay = jnp.where(rid == 0, wa, jnp.where(rid == 1, wb, 0.0))
    d = x_ref.shape[1]
    x1e_ref[rs, :d] = x1
    x1e_ref[rs, d:] = pay.T


def _merge(ogr, od, sgr, sgd, x2d, w_ret_o, w_diff_o, w_out, g_ffn, rw_hi, rw_lo, rb):
    n, d = x2d.shape
    tm = _row_tile(n)
    nt = n // tm
    n_sub = MERGE_TILES_PER_STEP if nt % MERGE_TILES_PER_STEP == 0 else 1
    row = lambda w: pl.BlockSpec((n_sub * tm, w), lambda i: (i, 0))
    return pl.pallas_call(
        _merge_kernel,
        out_shape=[jax.ShapeDtypeStruct((n, d + PAY), F32),
                   jax.ShapeDtypeStruct((nt, 1, tm), I32)],
        grid=(nt // n_sub,),
        in_specs=[row(SEG), pl.BlockSpec((n_sub, SEG, tm), lambda i: (i, 0, 0)), row(SEG), row(SEG), row(d),
                  _resident(w_ret_o.shape), _resident(w_diff_o.shape), _resident(w_out.shape),
                  _resident((1, d)), _resident(rw_hi.shape), _resident(rw_lo.shape),
                  _resident(rb.shape)],
        out_specs=[row(d + PAY), pl.BlockSpec((n_sub, 1, tm), lambda i: (i, 0, 0))],
        compiler_params=_cparams(("arbitrary",), 48),
        name="merge_route",
    )(ogr, od, sgr, sgd, x2d, w_ret_o, w_diff_o, w_out, g_ffn.reshape(1, d), rw_hi, rw_lo, rb)


def _prefix_excl(x):
    rid = lax.broadcasted_iota(I32, x.shape, 0)
    inc = x
    s = 1
    while s < x.shape[0]:
        inc = inc + jnp.where(rid >= s, pltpu.roll(inc, s, 0), 0.0)
        s *= 2
    return inc - x


def _sort_kernel(cls_ref, pos_ref, cnt_ref, off_ref, tiles_ref, cnt_scr, run_scr, off_scr,
                 *, moe_tile, n_tiles_pad):
    ph = pl.program_id(0)
    t = pl.program_id(1)
    k_tiles, _, tm = cls_ref.shape
    cid = lax.broadcasted_iota(I32, (CLS_PAD, tm), 0)

    def onehot(j):
        return jnp.where(cid == cls_ref[j], 1.0, 0.0)

    @pl.when(jnp.logical_and(ph == 0, t == 0))
    def _():
        cnt_scr[...] = jnp.zeros(cnt_scr.shape, F32)

    @pl.when(ph == 0)
    def _():
        for j in range(k_tiles):
            cnt_scr[...] += jnp.sum(onehot(j), axis=-1, keepdims=True)

    @pl.when(jnp.logical_and(ph == 1, t == 0))
    def _():
        cnt = cnt_scr[...]
        padded = jnp.ceil(cnt / moe_tile) * moe_tile
        off = _prefix_excl(padded)
        off_scr[...] = off
        run_scr[...] = jnp.zeros(run_scr.shape, F32)
        cnt_ref[...] = cnt.astype(I32)
        off_ref[...] = off.astype(I32)
        end = (off + padded)[:, 0:1]
        total = jnp.max(end, axis=0, keepdims=True)
        n_used = total / moe_tile
        p = lax.broadcasted_iota(I32, (1, n_tiles_pad), 1).astype(F32)
        start = jnp.minimum(p, n_used - 1.0) * moe_tile
        cid2 = lax.broadcasted_iota(I32, (CLS_PAD, n_tiles_pad), 0)
        before = jnp.logical_and(end <= start, cid2 < N_CLASSES)
        tcls = jnp.sum(before.astype(F32), axis=0, keepdims=True).astype(I32)
        grp = tcls // N_PAIRS
        pr = tcls - grp * N_PAIRS
        lo = (pr >= EXP_PER_GROUP - 1).astype(I32) + (pr >= 2 * EXP_PER_GROUP - 3).astype(I32)
        base = jnp.where(lo == 0, 0, jnp.where(lo == 1, EXP_PER_GROUP - 1, 2 * EXP_PER_GROUP - 3))
        hi = pr - base + lo + 1
        rid = lax.broadcasted_iota(I32, (V7X_SUBLANES, n_tiles_pad), 0)
        ea = grp * EXP_PER_GROUP + lo
        eb = grp * EXP_PER_GROUP + hi
        nu = jnp.broadcast_to(n_used.astype(I32), (1, n_tiles_pad))
        tiles_ref[...] = jnp.where(rid == 0, ea, jnp.where(rid == 1, eb, jnp.where(rid == 2, nu, 0)))

    @pl.when(ph == 1)
    def _():
        r = lax.broadcasted_iota(I32, (tm, tm), 0)
        c = lax.broadcasted_iota(I32, (tm, tm), 1)
        upper = jnp.where(r <= c, 1.0, 0.0).astype(BF16)
        for j in range(k_tiles):
            oh = onehot(j)
            incl = _dot(oh.astype(BF16), upper)
            slot = off_scr[...][:, 0:1] + run_scr[...][:, 0:1] + incl - 1.0
            pos_ref[j] = jnp.sum(oh * slot, axis=0, keepdims=True).astype(I32)
            run_scr[...] += jnp.sum(oh, axis=-1, keepdims=True)


def _sort(cls, moe_tile, n_tiles_pad):
    nt, _, tm = cls.shape
    k_tiles = SORT_TILES_PER_STEP if nt % SORT_TILES_PER_STEP == 0 else 1
    blk = pl.BlockSpec((k_tiles, 1, tm), lambda ph, t: (t, 0, 0))
    oblk = pl.BlockSpec((k_tiles, 1, tm), lambda ph, t: (t * ph, 0, 0))
    whole = lambda shape: pl.BlockSpec(shape, lambda ph, t: (0, 0))
    return pl.pallas_call(
        functools.partial(_sort_kernel, moe_tile=moe_tile, n_tiles_pad=n_tiles_pad),
        out_shape=[jax.ShapeDtypeStruct((nt, 1, tm), I32),
                   jax.ShapeDtypeStruct((CLS_PAD, V7X_LANES), I32),
                   jax.ShapeDtypeStruct((CLS_PAD, V7X_LANES), I32),
                   jax.ShapeDtypeStruct((V7X_SUBLANES, n_tiles_pad), I32)],
        grid=(2, nt // k_tiles),
        in_specs=[blk],
        out_specs=[oblk, whole((CLS_PAD, V7X_LANES)), whole((CLS_PAD, V7X_LANES)),
                   whole((V7X_SUBLANES, n_tiles_pad))],
        scratch_shapes=[pltpu.VMEM((CLS_PAD, V7X_LANES), F32)] * 3,
        compiler_params=_cparams(("arbitrary",) * 2),
        name="class_sort",
    )(cls)


def _permute_kernel(pos_ref, cnt_ref, off_ref, src_ref, dst_ref, zblk, sem, zsem, *, moe_tile):
    t = pl.program_id(0)
    k_tiles, _, tm = pos_ref.shape
    rows = k_tiles * tm

    def row_copy(r):
        return pltpu.make_async_copy(src_ref.at[pl.ds(r, 1)],
                                     dst_ref.at[pl.ds(pos_ref[r // tm, 0, r % tm], 1)], sem)

    for r in range(rows):
        row_copy(r).start(priority=r % N_DMA_PRIORITIES)

    @pl.when(t == 0)
    def _():
        zblk[...] = jnp.zeros(zblk.shape, zblk.dtype)
        used = 0
        pads = []
        for c in range(N_CLASSES):
            cnt = cnt_ref[c, 0]
            off = off_ref[c, 0]
            padded = ((cnt + moe_tile - 1) // moe_tile) * moe_tile
            used = off + padded
            pads.append((off + cnt, padded - cnt))

        def pad_dmas(do):
            for first_row, n_pad in pads:
                size = 1
                while size < moe_tile:
                    @pl.when((n_pad & size) != 0)
                    def _(first_row=first_row, size=size):
                        if size < V7X_SUBLANES:
                            for j in range(size):
                                do(pltpu.make_async_copy(zblk.at[pl.ds(0, 1)],
                                                         dst_ref.at[pl.ds(first_row + j, 1)], zsem))
                        else:
                            row0 = pl.multiple_of(first_row, V7X_SUBLANES)
                            do(pltpu.make_async_copy(zblk.at[pl.ds(0, size)],
                                                     dst_ref.at[pl.ds(row0, size)], zsem))
                    first_row = first_row + (n_pad & size)
                    size *= 2

        pad_dmas(lambda cp: cp.start())
        pad_dmas(lambda cp: cp.wait())

        def tile_copy(p):
            return pltpu.make_async_copy(zblk, dst_ref.at[pl.ds(p * moe_tile, moe_tile)], zsem)

        def tile_start(p, carry):
            tile_copy(p).start()
            return carry

        def tile_wait(p, carry):
            tile_copy(p).wait()
            return carry

        first, last = used // moe_tile, dst_ref.shape[0] // moe_tile
        lax.fori_loop(first, last, tile_start, 0)
        lax.fori_loop(first, last, tile_wait, 0)

    for r in range(rows):
        row_copy(r).wait()


def _perm_tiles_per_step(nt):
    return PERM_TILES_PER_STEP if nt % PERM_TILES_PER_STEP == 0 else 1


def _permute(pos, cnt, off, src, n_rows_out, moe_tile):
    nt, _, tm = pos.shape
    width = src.shape[1]
    k = _perm_tiles_per_step(nt)
    smem = lambda shape, imap: pl.BlockSpec(shape, imap, memory_space=pltpu.SMEM)
    return pl.pallas_call(
        functools.partial(_permute_kernel, moe_tile=moe_tile),
        out_shape=jax.ShapeDtypeStruct((n_rows_out, width), src.dtype),
        grid=(nt // k,),
        in_specs=[smem((k, 1, tm), lambda t: (t, 0, 0)),
                  smem(cnt.shape, lambda t: (0, 0)),
                  smem(off.shape, lambda t: (0, 0)),
                  pl.BlockSpec((k * tm, width), lambda t: (t, 0))],
        out_specs=pl.BlockSpec(memory_space=pl.ANY),
        scratch_shapes=[pltpu.VMEM((moe_tile, width), src.dtype),
                        pltpu.SemaphoreType.DMA, pltpu.SemaphoreType.DMA],
        compiler_params=_cparams(("arbitrary",), 32),
        name="permute_rows",
    )(pos, cnt, off, src)


def _unpermute_kernel(pos_ref, src_ref, dst_ref, sem):
    k_tiles, _, tm = pos_ref.shape
    rows = k_tiles * tm

    def row_copy(r):
        return pltpu.make_async_copy(src_ref.at[pl.ds(pos_ref[r // tm, 0, r % tm], 1)],
                                     dst_ref.at[pl.ds(r, 1)], sem)

    for r in range(rows):
        row_copy(r).start(priority=r % N_DMA_PRIORITIES)
    for r in range(rows):
        row_copy(r).wait()


def _unpermute(pos, src, n_rows_out):
    nt, _, tm = pos.shape
    width = src.shape[1]
    k = _perm_tiles_per_step(nt)
    return pl.pallas_call(
        _unpermute_kernel,
        out_shape=jax.ShapeDtypeStruct((n_rows_out, width), src.dtype),
        grid=(nt // k,),
        in_specs=[pl.BlockSpec((k, 1, tm), lambda t: (t, 0, 0), memory_space=pltpu.SMEM),
                  pl.BlockSpec(memory_space=pl.ANY)],
        out_specs=pl.BlockSpec((k * tm, width), lambda t: (t, 0)),
        scratch_shapes=[pltpu.SemaphoreType.DMA],
        compiler_params=_cparams(("arbitrary",), 32),
        name="unpermute_rows",
    )(pos, src)


def _moe_kernel(ea_ref, eb_ref, nu_ref, xs_ref, gf_ref, wgu_a, wdn_a, wgu_b, wdn_b, y_ref):
    p = pl.program_id(0)
    d = y_ref.shape[1]

    @pl.when(p < nu_ref[0])
    def _():
        x = xs_ref[:, :d]
        wa = xs_ref[:, d:d + 1]
        wb = xs_ref[:, d + 1:d + 2]
        hn = (_rms(x) * gf_ref[...]).astype(BF16)

        def expert(wgu, wdn):
            gu = _dot(hn, wgu[0])
            gate = gu[:, :D_FF]
            he = gate * _sigmoid(gate) * gu[:, D_FF:]
            return _dot(he.astype(BF16), wdn[0])

        y_ref[...] = x + wa * expert(wgu_a, wdn_a) + wb * expert(wgu_b, wdn_b)

    @pl.when(p >= nu_ref[0])
    def _():
        y_ref[...] = jnp.zeros(y_ref.shape, y_ref.dtype)


def _moe(ea, eb, nu, xs, g_ffn, w_gu, w_dn, moe_tile):
    n_rows, width = xs.shape
    d = width - PAY
    n_tiles = n_rows // moe_tile
    used = lambda p, ea, eb, nu: (jnp.minimum(p, nu[0] - 1), 0)
    return pl.pallas_call(
        _moe_kernel,
        out_shape=jax.ShapeDtypeStruct((n_rows, d), F32),
        grid_spec=pltpu.PrefetchScalarGridSpec(
            num_scalar_prefetch=3,
            grid=(n_tiles,),
            in_specs=[pl.BlockSpec((moe_tile, width), used),
                      pl.BlockSpec((1, d), lambda p, ea, eb, nu: (0, 0)),
                      pl.BlockSpec((1, d, 2 * D_FF), lambda p, ea, eb, nu: (ea[p], 0, 0)),
                      pl.BlockSpec((1, D_FF, d), lambda p, ea, eb, nu: (ea[p], 0, 0)),
                      pl.BlockSpec((1, d, 2 * D_FF), lambda p, ea, eb, nu: (eb[p], 0, 0)),
                      pl.BlockSpec((1, D_FF, d), lambda p, ea, eb, nu: (eb[p], 0, 0))],
            out_specs=pl.BlockSpec((moe_tile, d), lambda p, ea, eb, nu: (p, 0)),
        ),
        compiler_params=_cparams(("arbitrary",), 48),
        name="moe_sorted",
    )(ea, eb, nu, xs, g_ffn.reshape(1, d), w_gu, w_dn, w_gu, w_dn)


def _hier_moe_residual(x1e, cls, g_ffn, w_gu, w_dn):
    n = x1e.shape[0]
    moe_tile = min(MOE_TILE, max(MOE_TILE_MIN, pl.next_power_of_2(n // N_CLASSES) // 2))
    n_tiles = n // moe_tile + N_CLASSES
    n_tiles_pad = -(-n_tiles // V7X_LANES) * V7X_LANES
    pos, cnt, off, tiles = _sort(cls, moe_tile, n_tiles_pad)
    xs = _permute(pos, cnt, off, x1e, n_tiles * moe_tile, moe_tile)
    ys = _moe(tiles[0, :n_tiles], tiles[1, :n_tiles], tiles[2, :1], xs, g_ffn, w_gu, w_dn, moe_tile)
    return _unpermute(pos, ys, n)


def _rotary_tables(pos):
    half = R_DK // 2
    inv = 1.0 / (ROPE_BASE ** jnp.linspace(0.0, 1.0, half, dtype=F32))
    ang = pos.astype(F32)[:, None] * inv[None, :]
    cos = jnp.cos(ang)
    sin = jnp.sin(ang)
    return jnp.concatenate([cos, cos], axis=-1), jnp.concatenate([-sin, sin], axis=-1)


def _token_group(x, pos, lw, lam_init, log_gamma, state0, cache):
    bsz, t_len, d = x.shape
    x2d = x.reshape(bsz * t_len, d)
    cos_tab, sin_tab = _rotary_tables(pos)
    tm = _row_tile(bsz * t_len)
    if t_len < tm:
        reps = tm // t_len
        cos_tab = jnp.tile(cos_tab, (reps, 1))
        sin_tab = jnp.tile(sin_tab, (reps, 1))
    rqk, rv, rgs, dq, kb, vt, dk_rows, dv_rows, sgr, sgd = _in_proj(
        x2d, lw["g_mix"], lw["w_in"], cos_tab, sin_tab, lw["g_q"], lw["g_k"])
    ogr, s_fin = _retention(rqk, rv, rgs, state0, lw["g_ret"], log_gamma, bsz, t_len)
    if cache is None:
        od = _attn_prompt(dq, kb, vt, lw["g_q"], lw["g_k"], lw["lam_p"], lw["g_sub"], lam_init,
                          bsz, t_len)
    else:
        od = _attn_sample(dq, dk_rows, dv_rows, _to_cache_rows(cache[0]), _to_cache_rows(cache[1]),
                          lw["lam_p"], lw["g_sub"], lam_init, bsz, t_len)
        od = od.reshape(-1, tm, SEG).transpose(0, 2, 1)
    x1e, cls = _merge(ogr, od, sgr, sgd, x2d, lw["w_ret_o"], lw["w_diff_o"], lw["w_out"],
                      lw["g_ffn"], lw["rw_hi"], lw["rw_lo"], lw["rb"])
    y = _hier_moe_residual(x1e, cls, lw["g_ffn"], lw["w_gu"], lw["w_dn"])
    return (y.reshape(bsz, t_len, d), _from_cache_rows(dk_rows, bsz, t_len),
            _from_cache_rows(dv_rows, bsz, t_len), s_fin)


def _to_cache_rows(c):
    b, p, h, w = c.shape
    halves = w // V7X_LANES
    return c.reshape(b * p, h, halves, V7X_LANES).transpose(0, 2, 1, 3).reshape(b * p * h * halves, V7X_LANES)


def _from_cache_rows(rows, bsz, t_len):
    halves = D_DV // V7X_LANES
    r = rows.reshape(bsz * t_len, halves, D_HEADS, V7X_LANES).transpose(0, 2, 1, 3)
    return r.reshape(bsz, t_len, D_HEADS, D_DV)


def _layer_weights(l, g_mix, w_in, g_q, g_k, lambda_q1, lambda_k1, lambda_q2, lambda_k2, g_ret,
                   w_ret_o, g_sub, w_diff_o, w_out, g_ffn, w_group, b_group, w_expert, b_expert,
                   w_gate, w_up, w_down):
    d = w_in.shape[1]
    n_r = N_GROUPS + N_EXPERTS
    rw = jnp.concatenate([w_group[l], w_expert[l]], axis=1).astype(F32).T
    rw = jnp.zeros((CLS_PAD, d), F32).at[:n_r].set(rw)
    rw_hi = rw.astype(BF16)
    rw_lo = (rw - rw_hi.astype(F32)).astype(BF16)
    rb = jnp.concatenate([b_group[l], b_expert[l]]).astype(F32)
    rb = jnp.zeros((CLS_PAD,), F32).at[:n_r].set(rb)
    rb = jnp.broadcast_to(rb[:, None], (CLS_PAD, V7X_LANES))
    return dict(
        g_mix=g_mix[l], w_in=w_in[l].astype(BF16), g_q=g_q[l], g_k=g_k[l],
        lam_p=jnp.stack([lambda_q1[l], lambda_k1[l], lambda_q2[l], lambda_k2[l]]).astype(F32),
        g_ret=g_ret[l], w_ret_o=w_ret_o[l].astype(BF16), g_sub=g_sub[l],
        w_diff_o=w_diff_o[l].astype(BF16), w_out=w_out[l].astype(BF16), g_ffn=g_ffn[l],
        rw_hi=rw_hi, rw_lo=rw_lo, rb=rb,
        w_gu=jnp.concatenate([w_gate[l], w_up[l]], axis=-1).astype(BF16),
        w_dn=w_down[l].astype(BF16))


def kernel(x_prompt, x_sample, cache_k, cache_v, state_ret, g_mix, w_in, g_q, g_k, lambda_q1, lambda_k1, lambda_q2, lambda_k2, g_ret, w_ret_o, g_sub, w_diff_o, w_out, g_ffn, w_group, b_group, w_expert, b_expert, w_gate, w_up, w_down):
    depth = w_in.shape[0]
    bp, tp, _ = x_prompt.shape
    bs, ts, _ = x_sample.shape
    past = cache_k.shape[2]
    log_gamma = jnp.log1p(-jnp.exp2(-5.0 - jnp.arange(R_HEADS, dtype=F32)))
    pos_p = jnp.arange(tp, dtype=jnp.int32)
    pos_s = past + jnp.arange(ts, dtype=jnp.int32)
    zero_state = jnp.zeros((bp, R_HEADS, R_DK, R_DV), F32)
    xp, xs = x_prompt, x_sample
    outs = [[] for _ in range(6)]
    for l in range(depth):
        lam_init = 0.8 - 0.6 * math.exp(-0.3 * l)
        lw = _layer_weights(l, g_mix, w_in, g_q, g_k, lambda_q1, lambda_k1, lambda_q2, lambda_k2,
                            g_ret, w_ret_o, g_sub, w_diff_o, w_out, g_ffn, w_group, b_group,
                            w_expert, b_expert, w_gate, w_up, w_down)
        xp, kp, vp, sp = _token_group(xp, pos_p, lw, lam_init, log_gamma, zero_state, None)
        cache = (cache_k[l], cache_v[l])
        xs, ks, vs, ss = _token_group(xs, pos_s, lw, lam_init, log_gamma,
                                      state_ret[l].astype(F32), cache)
        for lst, val in zip(outs, (kp, vp, sp, ks, vs, ss)):
            lst.append(val)
    return (xp, xs) + tuple(jnp.stack(o) for o in outs)
```

```python
import functools
import math

import jax
import jax.numpy as jnp
from jax import lax
from jax.experimental import pallas as pl
from jax.experimental.pallas import tpu as pltpu

F32 = jnp.float32
BF16 = jnp.bfloat16
I32 = jnp.int32

CHUNK = 64
EPS = 1e-6
R_HEADS = 4
R_DK = 128
R_DV = 256
ROPE_BASE = 10000.0
D_HEADS = 4
D_DH = 128
D_DV = 256
N_GROUPS = 4
EXP_PER_GROUP = 4
N_EXPERTS = N_GROUPS * EXP_PER_GROUP
N_PAIRS = EXP_PER_GROUP * (EXP_PER_GROUP - 1) // 2
N_CLASSES = N_GROUPS * N_PAIRS
D_FF = 512
SEG = 1024

V7X_LANES = 128
V7X_SUBLANES = 8
V7X_VMEM_LIMIT_BYTES = 56 * 1024 * 1024
N_DMA_PRIORITIES = 2

ROW_TILE = 512
RET_SUPER = 512
RET_STREAMS_PER_STEP = 2
ATT_HEADS_PER_STEP = 4
MERGE_TILES_PER_STEP = 2
SORT_TILES_PER_STEP = 8
PERM_TILES_PER_STEP = 4
SAMPLE_CHUNK = 1024
MOE_TILE = 512
MOE_TILE_MIN = 32
CLS_PAD = 32
PAY = V7X_LANES

NEG_BIG = -1e30
LOG2E = math.log2(math.e)
SCORE_BOUND_MARGIN = 1.01
MAX_SHIFT_LOG2 = 100.0


def _row_tile(n):
    return min(ROW_TILE, n)


def _cparams(sem, vmem_mb=None):
    kw = dict(dimension_semantics=sem)
    if vmem_mb is not None:
        kw["vmem_limit_bytes"] = min(vmem_mb * 1024 * 1024, V7X_VMEM_LIMIT_BYTES)
    return pltpu.CompilerParams(**kw)


def _resident(shape):
    nd = len(shape)
    return pl.BlockSpec(shape, lambda *_: (0,) * nd, pipeline_mode=pl.Buffered(1))


def _rms(x, eps=EPS):
    return x * lax.rsqrt(jnp.mean(x * x, axis=-1, keepdims=True) + eps)


def _sigmoid(x):
    return 1.0 / (1.0 + jnp.exp(-x))


def _dot(a, b):
    return jnp.dot(a, b, preferred_element_type=F32)


def _dot_nt(a, b):
    return lax.dot_general(a, b, (((1,), (1,)), ((), ())), preferred_element_type=F32)


def _dot_tn(a, b):
    return lax.dot_general(a, b, (((0,), (0,)), ((), ())), preferred_element_type=F32)


def _store_cache_rows(ref, z):
    rows = z.shape[0]
    for half in range(2):
        for h in range(D_HEADS):
            lo = h * D_DV + half * V7X_LANES
            ref[pl.ds(half * D_HEADS + h, rows, stride=2 * D_HEADS), :] = z[:, lo:lo + V7X_LANES]


def _inproj_kernel(x_ref, g_ref, w_ref, cos_ref, sin_ref, gq_ref, gk_ref,
                   rqk_ref, rv_ref, rgs_ref, dq_ref, kb_ref, vt_ref, dk_ref, dv_ref, sgr_ref, sgd_ref):
    x = x_ref[...]
    hb = (_rms(x) * g_ref[...]).astype(BF16)

    def seg(s):
        return _dot(hb, w_ref[:, s * SEG:(s + 1) * SEG])

    cos = cos_ref[...]
    sin = sin_ref[...]
    z = seg(0)
    for j in range(2 * R_HEADS):
        v = z[:, j * R_DK:(j + 1) * R_DK]
        r = v * cos + pltpu.roll(v, R_DK // 2, 1) * sin
        if j >= R_HEADS:
            r = r * (R_DK ** -0.5)
        rqk_ref[:, j * R_DK:(j + 1) * R_DK] = r.astype(BF16)

    z = seg(2)
    rgs_ref[...] = (z * _sigmoid(z)).astype(BF16)

    z = seg(3)
    gq = gq_ref[...] * (D_DH ** -0.5 * LOG2E)
    for j in range(2 * D_HEADS):
        v = z[:, j * D_DH:(j + 1) * D_DH]
        dq_ref[:, j * D_DH:(j + 1) * D_DH] = (_rms(v) * gq).astype(BF16)
    z = seg(4)
    gk = gk_ref[...]
    kn = jnp.concatenate([_rms(z[:, j * D_DH:(j + 1) * D_DH]) * gk for j in range(2 * D_HEADS)], axis=1)
    kb_ref[...] = kn.astype(BF16)
    _store_cache_rows(dk_ref, kn)
    z = seg(5)
    _store_cache_rows(dv_ref, z)
    vt_ref[0] = z.T.astype(BF16)
    sgr_ref[...] = _sigmoid(seg(6)).astype(BF16)
    sgd_ref[...] = _sigmoid(seg(7)).astype(BF16)
    rv_ref[...] = seg(1).astype(BF16)


def _in_proj(x2d, g_mix, w_in_bf, cos_tab, sin_tab, g_q, g_k):
    n, d = x2d.shape
    tm = _row_tile(n)
    nt = n // tm
    ntab = cos_tab.shape[0] // tm
    row = lambda w: pl.BlockSpec((tm, w), lambda i: (i, 0))
    tab = pl.BlockSpec((tm, R_DK), lambda i: (i % ntab, 0))
    flat = jax.ShapeDtypeStruct((n, SEG), BF16)
    cache_rows = jax.ShapeDtypeStruct((n * 2 * D_HEADS, V7X_LANES), F32)
    cache_spec = pl.BlockSpec((tm * 2 * D_HEADS, V7X_LANES), lambda i: (i, 0))
    outs = [flat, flat, flat, flat, flat, jax.ShapeDtypeStruct((nt, SEG, tm), BF16),
            cache_rows, cache_rows, flat, flat]
    return pl.pallas_call(
        _inproj_kernel,
        out_shape=outs,
        grid=(nt,),
        in_specs=[row(d), _resident((1, d)), _resident(w_in_bf.shape), tab, tab,
                  _resident((1, D_DH)), _resident((1, D_DH))],
        out_specs=[row(SEG)] * 5 + [pl.BlockSpec((1, SEG, tm), lambda i: (i, 0, 0)),
                                    cache_spec, cache_spec, row(SEG), row(SEG)],
        compiler_params=_cparams(("arbitrary",), 56),
        name="in_proj",
    )(x2d, g_mix.reshape(1, d), w_in_bf, cos_tab, sin_tab,
      g_q.reshape(1, D_DH), g_k.reshape(1, D_DH))


def _ret_kernel(lg_ref, qk_ref, v_ref, rgs_ref, s0_ref, g_ref,
                o_ref, sfin_ref, s_scr, d_scr, lam_scr, wk_scr, dec_scr, *, c_len):
    c = pl.program_id(1)

    @pl.when((pl.program_id(0) == 0) & (c == 0))
    def _():
        t = lax.broadcasted_iota(I32, (c_len, c_len), 0)
        s = lax.broadcasted_iota(I32, (c_len, c_len), 1)
        dist = jnp.abs(t - s).astype(F32)
        vis = (s // CHUNK) <= (t // CHUNK)
        pos_v = lax.broadcasted_iota(I32, (c_len, R_DV), 0).astype(F32)
        pos_k = lax.broadcasted_iota(I32, (c_len, R_DK), 0).astype(F32)
        for h in range(R_HEADS):
            lg = lg_ref[h]
            d_scr[h] = jnp.where(vis, jnp.exp(lg * dist), 0.0)
            lam_scr[h] = jnp.exp(lg * (pos_v + 1.0))
            wk_scr[h] = jnp.exp(lg * (c_len - 1.0 - pos_k))
            dec_scr[h] = jnp.exp(lg * jnp.full((V7X_SUBLANES, R_DV), float(c_len), F32))

    @pl.when(c == 0)
    def _():
        s_scr[...] = s0_ref[...]

    for sb in range(qk_ref.shape[0]):
        for h in range(R_HEADS):
            q = qk_ref[sb, :, h * R_DK:(h + 1) * R_DK]
            k = qk_ref[sb, :, (R_HEADS + h) * R_DK:(R_HEADS + h + 1) * R_DK]
            v = v_ref[sb, :, h * R_DV:(h + 1) * R_DV]
            state = s_scr[sb, h]
            s = _dot_nt(q, k) * d_scr[h]
            o = _dot(s.astype(BF16), v) + lam_scr[h] * _dot(q, state.astype(BF16))
            kw = (k.astype(F32) * wk_scr[h]).astype(BF16)
            s_scr[sb, h] = state * dec_scr[h][0:1, :] + _dot_tn(kw, v)
            gate = rgs_ref[sb, :, h * R_DV:(h + 1) * R_DV].astype(F32)
            o_ref[sb, :, h * R_DV:(h + 1) * R_DV] = (_rms(o) * g_ref[...] * gate).astype(BF16)

    @pl.when(c == pl.num_programs(1) - 1)
    def _():
        sfin_ref[...] = s_scr[...]


def _retention(rqk, rv, rgs, state0, g_ret, log_gamma, bsz, t_len):
    c_len = min(RET_SUPER, t_len)
    nc = t_len // c_len
    rqk3 = rqk.reshape(bsz, t_len, SEG)
    rv3 = rv.reshape(bsz, t_len, SEG)
    rgs3 = rgs.reshape(bsz, t_len, SEG)
    spb = RET_STREAMS_PER_STEP if bsz % RET_STREAMS_PER_STEP == 0 else 1
    rows = pl.BlockSpec((spb, c_len, SEG), lambda b, c, lg: (b, c, 0))
    sspec = pl.BlockSpec((spb, R_HEADS, R_DK, R_DV), lambda b, c, lg: (b, 0, 0, 0))
    gspec = pl.BlockSpec((1, R_DV), lambda b, c, lg: (0, 0))
    o, sfin = pl.pallas_call(
        functools.partial(_ret_kernel, c_len=c_len),
        out_shape=[jax.ShapeDtypeStruct((bsz, t_len, SEG), BF16),
                   jax.ShapeDtypeStruct((bsz, R_HEADS, R_DK, R_DV), F32)],
        grid_spec=pltpu.PrefetchScalarGridSpec(
            num_scalar_prefetch=1,
            grid=(bsz // spb, nc),
            in_specs=[rows, rows, rows, sspec, gspec],
            out_specs=[rows, sspec],
            scratch_shapes=[pltpu.VMEM((spb, R_HEADS, R_DK, R_DV), F32),
                            pltpu.VMEM((R_HEADS, c_len, c_len), F32),
                            pltpu.VMEM((R_HEADS, c_len, R_DV), F32),
                            pltpu.VMEM((R_HEADS, c_len, R_DK), F32),
                            pltpu.VMEM((R_HEADS, V7X_SUBLANES, R_DV), F32)],
        ),
        compiler_params=_cparams(("arbitrary",) * 2, 40),
        name="retention",
    )(log_gamma, rqk3, rv3, rgs3, state0, g_ret.reshape(1, R_DV))
    return o.reshape(bsz * t_len, SEG), sfin


def _lambda_full(lp_ref, lam_init):
    lp = lp_ref[...]
    a = jnp.sum(lp[0:1] * lp[1:2], axis=-1, keepdims=True)
    b = jnp.sum(lp[2:3] * lp[3:4], axis=-1, keepdims=True)
    return jnp.exp(a) - jnp.exp(b) + lam_init


def _attn_finish(o1, l1, o2, l2, lam, g, lam_init):
    o = o1 / l1 - lam * (o2 / l2)
    return (_rms(o) * g * (1.0 - lam_init)).astype(BF16)


def _attn_prompt_kernel(bound_ref, q_ref, k_ref, vt_ref, lp_ref, g_ref, o_ref,
                        m_scr, l_scr, a_scr, bias, *, tile, heads, lam_init, fixed_shift):
    i = pl.program_id(2)
    w = 2 * D_DH

    @pl.when((pl.program_id(0) == 0) & (pl.program_id(1) == 0) & (i == 0))
    def _():
        kk = lax.broadcasted_iota(I32, (tile, tile), 0)
        qq = lax.broadcasted_iota(I32, (tile, tile), 1)
        bias[...] = jnp.where((kk // CHUNK) <= (qq // CHUNK), 0.0, NEG_BIG)

    def update(s, vts, idx, first):
        if fixed_shift:
            p = jnp.exp2(s - bound_ref[0])
            psum = jnp.sum(p, axis=0, keepdims=True)
            pv = _dot(vts, p.astype(BF16))
            l_scr[idx] = psum if first else l_scr[idx] + psum
            a_scr[idx] = pv if first else a_scr[idx] + pv
            return
        smax = jnp.max(s, axis=0, keepdims=True)
        m_new = smax if first else jnp.maximum(m_scr[idx], smax)
        p = jnp.exp2(s - m_new)
        psum = jnp.sum(p, axis=0, keepdims=True)
        pv = _dot(vts, p.astype(BF16))
        if first:
            l_scr[idx] = psum
            a_scr[idx] = pv
        else:
            alpha = jnp.exp2(m_scr[idx] - m_new)
            l_scr[idx] = alpha * l_scr[idx] + psum
            a_scr[idx] = alpha * a_scr[idx] + pv
        m_scr[idx] = m_new

    def block(hh, j, mask=None, first=False):
        ks = k_ref[0, pl.ds(pl.multiple_of(j * tile, tile), tile), hh * w:(hh + 1) * w]
        vts = vt_ref[0, j, hh * D_DV:(hh + 1) * D_DV, :]
        q = q_ref[0, :, hh * w:(hh + 1) * w]
        s1 = _dot_nt(ks[:, :D_DH], q[:, :D_DH])
        s2 = _dot_nt(ks[:, D_DH:], q[:, D_DH:])
        if mask is not None:
            s1 = s1 + mask
            s2 = s2 + mask
        update(s1, vts, 2 * hh, first)
        update(s2, vts, 2 * hh + 1, first)

    for hh in range(heads):
        block(hh, i, bias[...], first=True)

    def body(jj, carry):
        for hh in range(heads):
            block(hh, 2 * jj)
            block(hh, 2 * jj + 1)
        return carry

    lax.fori_loop(0, i // 2, body, 0)

    @pl.when(i % 2 == 1)
    def _():
        for hh in range(heads):
            block(hh, i - 1)

    lam = _lambda_full(lp_ref, lam_init)
    for hh in range(heads):
        a1, l1, a2, l2 = a_scr[2 * hh], l_scr[2 * hh], a_scr[2 * hh + 1], l_scr[2 * hh + 1]
        ot = a1 * (1.0 / l1) - lam * (a2 * (1.0 / l2))
        scale = lax.rsqrt(jnp.mean(ot * ot, axis=0, keepdims=True) + EPS) * (1.0 - lam_init)
        o_ref[0, 0, hh * D_DV:(hh + 1) * D_DV, :] = (ot * scale * g_ref[...]).astype(BF16)


def _attn_prompt(dq, kb, vt, g_q, g_k, lam_p, g_sub, lam_init, bsz, t_len):
    tile = vt.shape[2]
    assert t_len % tile == 0 and tile % CHUNK == 0
    nq = t_len // tile
    w = 2 * D_DH
    q3 = dq.reshape(bsz, t_len, SEG)
    k3 = kb.reshape(bsz, t_len, SEG)
    vt4 = vt.reshape(bsz, nq, SEG, tile)
    hps = ATT_HEADS_PER_STEP
    qspec = pl.BlockSpec((1, tile, hps * w), lambda b, h, i: (b, i, h))
    kspec = pl.BlockSpec((1, t_len, hps * w), lambda b, h, i: (b, 0, h))
    vspec = pl.BlockSpec((1, nq, hps * D_DV, tile), lambda b, h, i: (b, 0, h, 0))
    n_stat = 2 * hps
    col = lambda: pltpu.VMEM((n_stat, 1, tile), F32)

    def call(fixed_shift):
        return pl.pallas_call(
            functools.partial(_attn_prompt_kernel, tile=tile, heads=hps, lam_init=lam_init,
                              fixed_shift=fixed_shift),
            out_shape=jax.ShapeDtypeStruct((bsz, nq, SEG, tile), BF16),
            grid=(bsz, D_HEADS // hps, nq),
            in_specs=[pl.BlockSpec(memory_space=pltpu.SMEM), qspec, kspec, vspec,
                      pl.BlockSpec((4, D_DH), lambda b, h, i: (0, 0)),
                      pl.BlockSpec((D_DV, tile), lambda b, h, i: (0, 0))],
            out_specs=pl.BlockSpec((1, 1, hps * D_DV, tile), lambda b, h, i: (b, i, h, 0)),
            scratch_shapes=[col(), col(), pltpu.VMEM((n_stat, D_DV, tile), F32),
                            pltpu.VMEM((tile, tile), F32)],
            compiler_params=_cparams(("arbitrary",) * 3, 56),
            name="attn_prompt_fixed" if fixed_shift else "attn_prompt_online",
        )

    bound = SCORE_BOUND_MARGIN * D_DH * (D_DH ** -0.5 * LOG2E) * jnp.max(jnp.abs(g_q * g_k))
    g_col = jnp.broadcast_to(g_sub.astype(F32)[:, None], (D_DV, tile))
    args = (bound.reshape(1).astype(F32), q3, k3, vt4, lam_p, g_col)
    out = lax.cond(2.0 * bound < MAX_SHIFT_LOG2,
                   lambda *a: call(True)(*a), lambda *a: call(False)(*a), *args)
    return out.reshape(bsz * nq, SEG, tile)


def _attn_sample_kernel(q_ref, kc_ref, vc_ref, kn_ref, vn_ref, lp_ref, g_ref, o_ref,
                        m_scr, l_scr, a_scr, *, lam_init, chunk, t_new):
    c = pl.program_id(1)

    @pl.when(c == 0)
    def _():
        m_scr[...] = jnp.full(m_scr.shape, NEG_BIG, F32)
        l_scr[...] = jnp.zeros(l_scr.shape, F32)
        a_scr[...] = jnp.zeros(a_scr.shape, F32)

    q = q_ref[0]

    def head_rows(ref, h, half, n):
        return ref[pl.ds(half * D_HEADS + h, n, stride=2 * D_HEADS), :].astype(BF16)

    def absorb(k_src, v_src, n):
        for h in range(D_HEADS):
            v = jnp.concatenate([head_rows(v_src, h, 0, n), head_rows(v_src, h, 1, n)], axis=1)
            for mp in range(2):
                idx = 2 * h + mp
                lo = h * 2 * D_DH + mp * D_DH
                s = _dot_nt(q[:, lo:lo + D_DH], head_rows(k_src, h, mp, n))
                m_old = m_scr[idx]
                m_new = jnp.maximum(m_old, jnp.max(s, axis=-1, keepdims=True))
                alpha = jnp.exp2(m_old - m_new)
                p = jnp.exp2(s - m_new)
                l_scr[idx] = alpha * l_scr[idx] + jnp.sum(p, axis=-1, keepdims=True)
                a_scr[idx] = alpha * a_scr[idx] + _dot(p.astype(BF16), v)
                m_scr[idx] = m_new

    absorb(kc_ref, vc_ref, chunk)

    @pl.when(c == pl.num_programs(1) - 1)
    def _():
        absorb(kn_ref, vn_ref, t_new)
        lam = _lambda_full(lp_ref, lam_init)
        for h in range(D_HEADS):
            o_ref[0, :, h * D_DV:(h + 1) * D_DV] = _attn_finish(
                a_scr[2 * h], l_scr[2 * h], a_scr[2 * h + 1], l_scr[2 * h + 1], lam, g_ref[...], lam_init)


def _attn_sample(dq, dk_rows, dv_rows, cache_k_rows, cache_v_rows, lam_p, g_sub, lam_init, bsz, t_len):
    rpf = 2 * D_HEADS
    past = cache_k_rows.shape[0] // (bsz * rpf)
    chunk = min(SAMPLE_CHUNK, past)
    nc = past // chunk
    q3 = dq.reshape(bsz, t_len, SEG)
    qspec = pl.BlockSpec((1, t_len, SEG), lambda b, c: (b, 0, 0))
    cspec = pl.BlockSpec((chunk * rpf, V7X_LANES), lambda b, c: (b * nc + c, 0))
    nspec = pl.BlockSpec((t_len * rpf, V7X_LANES), lambda b, c: (b, 0))
    n_stat = 2 * D_HEADS
    out = pl.pallas_call(
        functools.partial(_attn_sample_kernel, lam_init=lam_init, chunk=chunk, t_new=t_len),
        out_shape=jax.ShapeDtypeStruct((bsz, t_len, SEG), BF16),
        grid=(bsz, nc),
        in_specs=[qspec, cspec, cspec, nspec, nspec,
                  pl.BlockSpec((4, D_DH), lambda b, c: (0, 0)),
                  pl.BlockSpec((1, D_DV), lambda b, c: (0, 0))],
        out_specs=qspec,
        scratch_shapes=[pltpu.VMEM((n_stat, t_len, 1), F32), pltpu.VMEM((n_stat, t_len, 1), F32),
                        pltpu.VMEM((n_stat, t_len, D_DV), F32)],
        compiler_params=_cparams(("arbitrary",) * 2, 32),
        name="attn_sample",
    )(q3, cache_k_rows, cache_v_rows, dk_rows, dv_rows, lam_p, g_sub.reshape(1, D_DV))
    return out.reshape(bsz * t_len, SEG)


def _split_hi_lo(x):
    hi = x.astype(BF16)
    lo = (x - hi.astype(F32)).astype(BF16)
    return hi, lo


def _merge_kernel(*refs):
    cls_ref = refs[-1]
    for sub in range(cls_ref.shape[0]):
        _merge_tile(sub, cls_ref.shape[2], *refs)


def _merge_tile(sub, tm, ogr_ref, od_ref, sgr_ref, sgd_ref, x_ref, wr_ref, wd_ref, wo_ref,
                gf_ref, rw_hi_ref, rw_lo_ref, rb_ref, x1e_ref, cls_ref):
    rs = slice(sub * tm, (sub + 1) * tm)
    y_r = _dot(ogr_ref[rs, :], wr_ref[...])
    y_d = _dot_tn(od_ref[sub], wd_ref[...])
    m = sgr_ref[rs, :].astype(F32) * y_r + sgd_ref[rs, :].astype(F32) * y_d
    x1 = x_ref[rs, :] + _dot(m.astype(BF16), wo_ref[...])

    hn = _rms(x1) * gf_ref[...]
    h_hi, h_lo = _split_hi_lo(hn)
    w_hi = rw_hi_ref[...]
    both = _dot_nt(jnp.concatenate([w_hi, rw_lo_ref[...]], axis=0), h_hi)
    lt = both[:CLS_PAD] + both[CLS_PAD:] + _dot_nt(w_hi, h_lo)
    lt = lt + rb_ref[...][:, 0:1]
    g = [lt[i:i + 1, :] for i in range(N_GROUPS)]
    e = [lt[N_GROUPS + i:N_GROUPS + i + 1, :] for i in range(N_EXPERTS)]

    gmax = functools.reduce(jnp.maximum, g)
    gid = jnp.full(g[0].shape, N_GROUPS - 1, I32)
    for i in range(N_GROUPS - 2, -1, -1):
        gid = jnp.where(g[i] == gmax, i, gid)
    g_w = 1.0 / functools.reduce(lambda a, b: a + b, [jnp.exp(v - gmax) for v in g])

    es = []
    for j in range(EXP_PER_GROUP):
        v = e[(N_GROUPS - 1) * EXP_PER_GROUP + j]
        for i in range(N_GROUPS - 2, -1, -1):
            v = jnp.where(gid == i, e[i * EXP_PER_GROUP + j], v)
        es.append(v)

    def first_argmax(vals):
        mx = functools.reduce(jnp.maximum, vals)
        idx = jnp.full(mx.shape, len(vals) - 1, I32)
        for i in range(len(vals) - 2, -1, -1):
            idx = jnp.where(vals[i] == mx, i, idx)
        return mx, idx

    l1, i1 = first_argmax(es)
    rest = [jnp.where(i1 == j, -jnp.inf, es[j]) for j in range(EXP_PER_GROUP)]
    l2, i2 = first_argmax(rest)
    t = jnp.exp(l2 - l1)
    c1 = g_w / (1.0 + t)
    c2 = g_w * t / (1.0 + t)
    lo = jnp.minimum(i1, i2)
    hi = jnp.maximum(i1, i2)
    base = jnp.where(lo == 0, 0, jnp.where(lo == 1, EXP_PER_GROUP - 1, 2 * EXP_PER_GROUP - 3))
    cls = gid * N_PAIRS + base + hi - lo - 1
    wa = jnp.where(i1 < i2, c1, c2)
    wb = jnp.where(i1 < i2, c2, c1)

    cls_ref[sub] = cls
    rid = lax.broadcasted_iota(I32, (PAY, tm), 0)
    pay = jnp.where(rid == 0, wa, jnp.where(rid == 1, wb, 0.0))
    d = x_ref.shape[1]
    x1e_ref[rs, :d] = x1
    x1e_ref[rs, d:] = pay.T


def _merge(ogr, od, sgr, sgd, x2d, w_ret_o, w_diff_o, w_out, g_ffn, rw_hi, rw_lo, rb):
    n, d = x2d.shape
    tm = _row_tile(n)
    nt = n // tm
    n_sub = MERGE_TILES_PER_STEP if nt % MERGE_TILES_PER_STEP == 0 else 1
    row = lambda w: pl.BlockSpec((n_sub * tm, w), lambda i: (i, 0))
    return pl.pallas_call(
        _merge_kernel,
        out_shape=[jax.ShapeDtypeStruct((n, d + PAY), F32),
                   jax.ShapeDtypeStruct((nt, 1, tm), I32)],
        grid=(nt // n_sub,),
        in_specs=[row(SEG), pl.BlockSpec((n_sub, SEG, tm), lambda i: (i, 0, 0)), row(SEG), row(SEG), row(d),
                  _resident(w_ret_o.shape), _resident(w_diff_o.shape), _resident(w_out.shape),
                  _resident((1, d)), _resident(rw_hi.shape), _resident(rw_lo.shape),
                  _resident(rb.shape)],
        out_specs=[row(d + PAY), pl.BlockSpec((n_sub, 1, tm), lambda i: (i, 0, 0))],
        compiler_params=_cparams(("arbitrary",), 48),
        name="merge_route",
    )(ogr, od, sgr, sgd, x2d, w_ret_o, w_diff_o, w_out, g_ffn.reshape(1, d), rw_hi, rw_lo, rb)


def _prefix_excl(x):
    rid = lax.broadcasted_iota(I32, x.shape, 0)
    inc = x
    s = 1
    while s < x.shape[0]:
        inc = inc + jnp.where(rid >= s, pltpu.roll(inc, s, 0), 0.0)
        s *= 2
    return inc - x


def _sort_kernel(cls_ref, pos_ref, cnt_ref, off_ref, tiles_ref, cnt_scr, run_scr, off_scr,
                 *, moe_tile, n_tiles_pad):
    ph = pl.program_id(0)
    t = pl.program_id(1)
    k_tiles, _, tm = cls_ref.shape
    cid = lax.broadcasted_iota(I32, (CLS_PAD, tm), 0)

    def onehot(j):
        return jnp.where(cid == cls_ref[j], 1.0, 0.0)

    @pl.when(jnp.logical_and(ph == 0, t == 0))
    def _():
        cnt_scr[...] = jnp.zeros(cnt_scr.shape, F32)

    @pl.when(ph == 0)
    def _():
        for j in range(k_tiles):
            cnt_scr[...] += jnp.sum(onehot(j), axis=-1, keepdims=True)

    @pl.when(jnp.logical_and(ph == 1, t == 0))
    def _():
        cnt = cnt_scr[...]
        padded = jnp.ceil(cnt / moe_tile) * moe_tile
        off = _prefix_excl(padded)
        off_scr[...] = off
        run_scr[...] = jnp.zeros(run_scr.shape, F32)
        cnt_ref[...] = cnt.astype(I32)
        off_ref[...] = off.astype(I32)
        end = (off + padded)[:, 0:1]
        total = jnp.max(end, axis=0, keepdims=True)
        n_used = total / moe_tile
        p = lax.broadcasted_iota(I32, (1, n_tiles_pad), 1).astype(F32)
        start = jnp.minimum(p, n_used - 1.0) * moe_tile
        cid2 = lax.broadcasted_iota(I32, (CLS_PAD, n_tiles_pad), 0)
        before = jnp.logical_and(end <= start, cid2 < N_CLASSES)
        tcls = jnp.sum(before.astype(F32), axis=0, keepdims=True).astype(I32)
        grp = tcls // N_PAIRS
        pr = tcls - grp * N_PAIRS
        lo = (pr >= EXP_PER_GROUP - 1).astype(I32) + (pr >= 2 * EXP_PER_GROUP - 3).astype(I32)
        base = jnp.where(lo == 0, 0, jnp.where(lo == 1, EXP_PER_GROUP - 1, 2 * EXP_PER_GROUP - 3))
        hi = pr - base + lo + 1
        rid = lax.broadcasted_iota(I32, (V7X_SUBLANES, n_tiles_pad), 0)
        ea = grp * EXP_PER_GROUP + lo
        eb = grp * EXP_PER_GROUP + hi
        nu = jnp.broadcast_to(n_used.astype(I32), (1, n_tiles_pad))
        tiles_ref[...] = jnp.where(rid == 0, ea, jnp.where(rid == 1, eb, jnp.where(rid == 2, nu, 0)))

    @pl.when(ph == 1)
    def _():
        r = lax.broadcasted_iota(I32, (tm, tm), 0)
        c = lax.broadcasted_iota(I32, (tm, tm), 1)
        upper = jnp.where(r <= c, 1.0, 0.0).astype(BF16)
        for j in range(k_tiles):
            oh = onehot(j)
            incl = _dot(oh.astype(BF16), upper)
            slot = off_scr[...][:, 0:1] + run_scr[...][:, 0:1] + incl - 1.0
            pos_ref[j] = jnp.sum(oh * slot, axis=0, keepdims=True).astype(I32)
            run_scr[...] += jnp.sum(oh, axis=-1, keepdims=True)


def _sort(cls, moe_tile, n_tiles_pad):
    nt, _, tm = cls.shape
    k_tiles = SORT_TILES_PER_STEP if nt % SORT_TILES_PER_STEP == 0 else 1
    blk = pl.BlockSpec((k_tiles, 1, tm), lambda ph, t: (t, 0, 0))
    oblk = pl.BlockSpec((k_tiles, 1, tm), lambda ph, t: (t * ph, 0, 0))
    whole = lambda shape: pl.BlockSpec(shape, lambda ph, t: (0, 0))
    return pl.pallas_call(
        functools.partial(_sort_kernel, moe_tile=moe_tile, n_tiles_pad=n_tiles_pad),
        out_shape=[jax.ShapeDtypeStruct((nt, 1, tm), I32),
                   jax.ShapeDtypeStruct((CLS_PAD, V7X_LANES), I32),
                   jax.ShapeDtypeStruct((CLS_PAD, V7X_LANES), I32),
                   jax.ShapeDtypeStruct((V7X_SUBLANES, n_tiles_pad), I32)],
        grid=(2, nt // k_tiles),
        in_specs=[blk],
        out_specs=[oblk, whole((CLS_PAD, V7X_LANES)), whole((CLS_PAD, V7X_LANES)),
                   whole((V7X_SUBLANES, n_tiles_pad))],
        scratch_shapes=[pltpu.VMEM((CLS_PAD, V7X_LANES), F32)] * 3,
        compiler_params=_cparams(("arbitrary",) * 2),
        name="class_sort",
    )(cls)


def _permute_kernel(pos_ref, cnt_ref, off_ref, src_ref, dst_ref, zblk, sem, zsem, *, moe_tile):
    t = pl.program_id(0)
    k_tiles, _, tm = pos_ref.shape
    rows = k_tiles * tm

    def row_copy(r):
        return pltpu.make_async_copy(src_ref.at[pl.ds(r, 1)],
                                     dst_ref.at[pl.ds(pos_ref[r // tm, 0, r % tm], 1)], sem)

    for r in range(rows):
        row_copy(r).start(priority=r % N_DMA_PRIORITIES)

    @pl.when(t == 0)
    def _():
        zblk[...] = jnp.zeros(zblk.shape, zblk.dtype)
        used = 0
        pads = []
        for c in range(N_CLASSES):
            cnt = cnt_ref[c, 0]
            off = off_ref[c, 0]
            padded = ((cnt + moe_tile - 1) // moe_tile) * moe_tile
            used = off + padded
            pads.append((off + cnt, padded - cnt))

        def pad_dmas(do):
            for first_row, n_pad in pads:
                size = 1
                while size < moe_tile:
                    @pl.when((n_pad & size) != 0)
                    def _(first_row=first_row, size=size):
                        if size < V7X_SUBLANES:
                            for j in range(size):
                                do(pltpu.make_async_copy(zblk.at[pl.ds(0, 1)],
                                                         dst_ref.at[pl.ds(first_row + j, 1)], zsem))
                        else:
                            row0 = pl.multiple_of(first_row, V7X_SUBLANES)
                            do(pltpu.make_async_copy(zblk.at[pl.ds(0, size)],
                                                     dst_ref.at[pl.ds(row0, size)], zsem))
                    first_row = first_row + (n_pad & size)
                    size *= 2

        pad_dmas(lambda cp: cp.start())
        pad_dmas(lambda cp: cp.wait())

        def tile_copy(p):
            return pltpu.make_async_copy(zblk, dst_ref.at[pl.ds(p * moe_tile, moe_tile)], zsem)

        def tile_start(p, carry):
            tile_copy(p).start()
            return carry

        def tile_wait(p, carry):
            tile_copy(p).wait()
            return carry

        first, last = used // moe_tile, dst_ref.shape[0] // moe_tile
        lax.fori_loop(first, last, tile_start, 0)
        lax.fori_loop(first, last, tile_wait, 0)

    for r in range(rows):
        row_copy(r).wait()


def _perm_tiles_per_step(nt):
    return PERM_TILES_PER_STEP if nt % PERM_TILES_PER_STEP == 0 else 1


def _permute(pos, cnt, off, src, n_rows_out, moe_tile):
    nt, _, tm = pos.shape
    width = src.shape[1]
    k = _perm_tiles_per_step(nt)
    smem = lambda shape, imap: pl.BlockSpec(shape, imap, memory_space=pltpu.SMEM)
    return pl.pallas_call(
        functools.partial(_permute_kernel, moe_tile=moe_tile),
        out_shape=jax.ShapeDtypeStruct((n_rows_out, width), src.dtype),
        grid=(nt // k,),
        in_specs=[smem((k, 1, tm), lambda t: (t, 0, 0)),
                  smem(cnt.shape, lambda t: (0, 0)),
                  smem(off.shape, lambda t: (0, 0)),
                  pl.BlockSpec((k * tm, width), lambda t: (t, 0))],
        out_specs=pl.BlockSpec(memory_space=pl.ANY),
        scratch_shapes=[pltpu.VMEM((moe_tile, width), src.dtype),
                        pltpu.SemaphoreType.DMA, pltpu.SemaphoreType.DMA],
        compiler_params=_cparams(("arbitrary",), 32),
        name="permute_rows",
    )(pos, cnt, off, src)


def _unpermute_kernel(pos_ref, src_ref, dst_ref, sem):
    k_tiles, _, tm = pos_ref.shape
    rows = k_tiles * tm

    def row_copy(r):
        return pltpu.make_async_copy(src_ref.at[pl.ds(pos_ref[r // tm, 0, r % tm], 1)],
                                     dst_ref.at[pl.ds(r, 1)], sem)

    for r in range(rows):
        row_copy(r).start(priority=r % N_DMA_PRIORITIES)
    for r in range(rows):
        row_copy(r).wait()


def _unpermute(pos, src, n_rows_out):
    nt, _, tm = pos.shape
    width = src.shape[1]
    k = _perm_tiles_per_step(nt)
    return pl.pallas_call(
        _unpermute_kernel,
        out_shape=jax.ShapeDtypeStruct((n_rows_out, width), src.dtype),
        grid=(nt // k,),
        in_specs=[pl.BlockSpec((k, 1, tm), lambda t: (t, 0, 0), memory_space=pltpu.SMEM),
                  pl.BlockSpec(memory_space=pl.ANY)],
        out_specs=pl.BlockSpec((k * tm, width), lambda t: (t, 0)),
        scratch_shapes=[pltpu.SemaphoreType.DMA],
        compiler_params=_cparams(("arbitrary",), 32),
        name="unpermute_rows",
    )(pos, src)


def _moe_kernel(ea_ref, eb_ref, nu_ref, xs_ref, gf_ref, wgu_a, wdn_a, wgu_b, wdn_b, y_ref):
    p = pl.program_id(0)
    d = y_ref.shape[1]

    @pl.when(p < nu_ref[0])
    def _():
        x = xs_ref[:, :d]
        wa = xs_ref[:, d:d + 1]
        wb = xs_ref[:, d + 1:d + 2]
        hn = (_rms(x) * gf_ref[...]).astype(BF16)

        def expert(wgu, wdn):
            gu = _dot(hn, wgu[0])
            gate = gu[:, :D_FF]
            he = gate * _sigmoid(gate) * gu[:, D_FF:]
            return _dot(he.astype(BF16), wdn[0])

        y_ref[...] = x + wa * expert(wgu_a, wdn_a) + wb * expert(wgu_b, wdn_b)

    @pl.when(p >= nu_ref[0])
    def _():
        y_ref[...] = jnp.zeros(y_ref.shape, y_ref.dtype)


def _moe(ea, eb, nu, xs, g_ffn, w_gu, w_dn, moe_tile):
    n_rows, width = xs.shape
    d = width - PAY
    n_tiles = n_rows // moe_tile
    used = lambda p, ea, eb, nu: (jnp.minimum(p, nu[0] - 1), 0)
    return pl.pallas_call(
        _moe_kernel,
        out_shape=jax.ShapeDtypeStruct((n_rows, d), F32),
        grid_spec=pltpu.PrefetchScalarGridSpec(
            num_scalar_prefetch=3,
            grid=(n_tiles,),
            in_specs=[pl.BlockSpec((moe_tile, width), used),
                      pl.BlockSpec((1, d), lambda p, ea, eb, nu: (0, 0)),
                      pl.BlockSpec((1, d, 2 * D_FF), lambda p, ea, eb, nu: (ea[p], 0, 0)),
                      pl.BlockSpec((1, D_FF, d), lambda p, ea, eb, nu: (ea[p], 0, 0)),
                      pl.BlockSpec((1, d, 2 * D_FF), lambda p, ea, eb, nu: (eb[p], 0, 0)),
                      pl.BlockSpec((1, D_FF, d), lambda p, ea, eb, nu: (eb[p], 0, 0))],
            out_specs=pl.BlockSpec((moe_tile, d), lambda p, ea, eb, nu: (p, 0)),
        ),
        compiler_params=_cparams(("arbitrary",), 48),
        name="moe_sorted",
    )(ea, eb, nu, xs, g_ffn.reshape(1, d), w_gu, w_dn, w_gu, w_dn)


def _hier_moe_residual(x1e, cls, g_ffn, w_gu, w_dn):
    n = x1e.shape[0]
    moe_tile = min(MOE_TILE, max(MOE_TILE_MIN, pl.next_power_of_2(n // N_CLASSES) // 2))
    n_tiles = n // moe_tile + N_CLASSES
    n_tiles_pad = -(-n_tiles // V7X_LANES) * V7X_LANES
    pos, cnt, off, tiles = _sort(cls, moe_tile, n_tiles_pad)
    xs = _permute(pos, cnt, off, x1e, n_tiles * moe_tile, moe_tile)
    ys = _moe(tiles[0, :n_tiles], tiles[1, :n_tiles], tiles[2, :1], xs, g_ffn, w_gu, w_dn, moe_tile)
    return _unpermute(pos, ys, n)


def _rotary_tables(pos):
    half = R_DK // 2
    inv = 1.0 / (ROPE_BASE ** jnp.linspace(0.0, 1.0, half, dtype=F32))
    ang = pos.astype(F32)[:, None] * inv[None, :]
    cos = jnp.cos(ang)
    sin = jnp.sin(ang)
    return jnp.concatenate([cos, cos], axis=-1), jnp.concatenate([-sin, sin], axis=-1)


def _token_group(x, pos, lw, lam_init, log_gamma, state0, cache):
    bsz, t_len, d = x.shape
    x2d = x.reshape(bsz * t_len, d)
    cos_tab, sin_tab = _rotary_tables(pos)
    tm = _row_tile(bsz * t_len)
    if t_len < tm:
        reps = tm // t_len
        cos_tab = jnp.tile(cos_tab, (reps, 1))
        sin_tab = jnp.tile(sin_tab, (reps, 1))
    rqk, rv, rgs, dq, kb, vt, dk_rows, dv_rows, sgr, sgd = _in_proj(
        x2d, lw["g_mix"], lw["w_in"], cos_tab, sin_tab, lw["g_q"], lw["g_k"])
    ogr, s_fin = _retention(rqk, rv, rgs, state0, lw["g_ret"], log_gamma, bsz, t_len)
    if cache is None:
        od = _attn_prompt(dq, kb, vt, lw["g_q"], lw["g_k"], lw["lam_p"], lw["g_sub"], lam_init,
                          bsz, t_len)
    else:
        od = _attn_sample(dq, dk_rows, dv_rows, _to_cache_rows(cache[0]), _to_cache_rows(cache[1]),
                          lw["lam_p"], lw["g_sub"], lam_init, bsz, t_len)
        od = od.reshape(-1, tm, SEG).transpose(0, 2, 1)
    x1e, cls = _merge(ogr, od, sgr, sgd, x2d, lw["w_ret_o"], lw["w_diff_o"], lw["w_out"],
                      lw["g_ffn"], lw["rw_hi"], lw["rw_lo"], lw["rb"])
    y = _hier_moe_residual(x1e, cls, lw["g_ffn"], lw["w_gu"], lw["w_dn"])
    return (y.reshape(bsz, t_len, d), _from_cache_rows(dk_rows, bsz, t_len),
            _from_cache_rows(dv_rows, bsz, t_len), s_fin)


def _to_cache_rows(c):
    b, p, h, w = c.shape
    halves = w // V7X_LANES
    return c.reshape(b * p, h, halves, V7X_LANES).transpose(0, 2, 1, 3).reshape(b * p * h * halves, V7X_LANES)


def _from_cache_rows(rows, bsz, t_len):
    halves = D_DV // V7X_LANES
    r = rows.reshape(bsz * t_len, halves, D_HEADS, V7X_LANES).transpose(0, 2, 1, 3)
    return r.reshape(bsz, t_len, D_HEADS, D_DV)


def _layer_weights(l, g_mix, w_in, g_q, g_k, lambda_q1, lambda_k1, lambda_q2, lambda_k2, g_ret,
                   w_ret_o, g_sub, w_diff_o, w_out, g_ffn, w_group, b_group, w_expert, b_expert,
                   w_gate, w_up, w_down):
    d = w_in.shape[1]
    n_r = N_GROUPS + N_EXPERTS
    rw = jnp.concatenate([w_group[l], w_expert[l]], axis=1).astype(F32).T
    rw = jnp.zeros((CLS_PAD, d), F32).at[:n_r].set(rw)
    rw_hi = rw.astype(BF16)
    rw_lo = (rw - rw_hi.astype(F32)).astype(BF16)
    rb = jnp.concatenate([b_group[l], b_expert[l]]).astype(F32)
    rb = jnp.zeros((CLS_PAD,), F32).at[:n_r].set(rb)
    rb = jnp.broadcast_to(rb[:, None], (CLS_PAD, V7X_LANES))
    return dict(
        g_mix=g_mix[l], w_in=w_in[l].astype(BF16), g_q=g_q[l], g_k=g_k[l],
        lam_p=jnp.stack([lambda_q1[l], lambda_k1[l], lambda_q2[l], lambda_k2[l]]).astype(F32),
        g_ret=g_ret[l], w_ret_o=w_ret_o[l].astype(BF16), g_sub=g_sub[l],
        w_diff_o=w_diff_o[l].astype(BF16), w_out=w_out[l].astype(BF16), g_ffn=g_ffn[l],
        rw_hi=rw_hi, rw_lo=rw_lo, rb=rb,
        w_gu=jnp.concatenate([w_gate[l], w_up[l]], axis=-1).astype(BF16),
        w_dn=w_down[l].astype(BF16))


def kernel(x_prompt, x_sample, cache_k, cache_v, state_ret, g_mix, w_in, g_q, g_k, lambda_q1, lambda_k1, lambda_q2, lambda_k2, g_ret, w_ret_o, g_sub, w_diff_o, w_out, g_ffn, w_group, b_group, w_expert, b_expert, w_gate, w_up, w_down):
    depth = w_in.shape[0]
    bp, tp, _ = x_prompt.shape
    bs, ts, _ = x_sample.shape
    past = cache_k.shape[2]
    log_gamma = jnp.log1p(-jnp.exp2(-5.0 - jnp.arange(R_HEADS, dtype=F32)))
    pos_p = jnp.arange(tp, dtype=jnp.int32)
    pos_s = past + jnp.arange(ts, dtype=jnp.int32)
    zero_state = jnp.zeros((bp, R_HEADS, R_DK, R_DV), F32)
    xp, xs = x_prompt, x_sample
    outs = [[] for _ in range(6)]
    for l in range(depth):
        lam_init = 0.8 - 0.6 * math.exp(-0.3 * l)
        lw = _layer_weights(l, g_mix, w_in, g_q, g_k, lambda_q1, lambda_k1, lambda_q2, lambda_k2,
                            g_ret, w_ret_o, g_sub, w_diff_o, w_out, g_ffn, w_group, b_group,
                            w_expert, b_expert, w_gate, w_up, w_down)
        xp, kp, vp, sp = _token_group(xp, pos_p, lw, lam_init, log_gamma, zero_state, None)
        cache = (cache_k[l], cache_v[l])
        xs, ks, vs, ss = _token_group(xs, pos_s, lw, lam_init, log_gamma,
                                      state_ret[l].astype(F32), cache)
        for lst, val in zip(outs, (kp, vp, sp, ks, vs, ss)):
            lst.append(val)
    return (xp, xs) + tuple(jnp.stack(o) for o in outs)
```
